```python
import functools
import jax
import jax.numpy as jnp
from jax import lax
import numpy as np

D_MODEL = 1024
BATCH = 2
SEQ = 16384
DEPTH = 2
DEC_BATCH = 4
DEC_SEQ = 4096
PAST_LEN = 128

N_META = 16
CHUNK = 64
GROUP_W = D_MODEL // 2
MIX_W = 2 * GROUP_W
RET_HEADS = 4
RET_DK = GROUP_W // RET_HEADS
RET_DV = GROUP_W // RET_HEADS
ROPE_BASE = 10000.0
LRU_W = GROUP_W
LRU_BLOCKS = 8
LRU_BW = LRU_W // LRU_BLOCKS
LRU_C = 8.0
CONV_W = 4
CONV_PAD = (2, 1)
HG_HEADS = 4
HG_DK = GROUP_W // HG_HEADS
HG_DV = GROUP_W // HG_HEADS
ML_HEADS = 4
ML_DK = GROUP_W // ML_HEADS
ML_DV = GROUP_W // ML_HEADS
D_FF = 2816
N_EXP = 8
TOP_K = 2
E_FF = 3584
N_EVEN = (DEPTH + 1) // 2
N_ODD = DEPTH // 2
ALPHA = (2.0 * DEPTH) ** 0.25
BETA = (8.0 * DEPTH) ** -0.25
EPS = 1e-5
EVEN_SPLITS = (GROUP_W, GROUP_W, GROUP_W, GROUP_W, LRU_W, LRU_W)
EVEN_IN = sum(EVEN_SPLITS)
ODD_SPLITS = (GROUP_W,) * 9 + (4 * ML_HEADS,)
ODD_IN = sum(ODD_SPLITS)

kernel_name = 'hybrid_bidir_encoder_meta'


def _split(z, sizes):
    return jnp.split(z, np.cumsum(sizes)[:-1].tolist(), axis=-1)


def _layer_norm(x, g, b):
    xf = x.astype(jnp.float32)
    mu = jnp.mean(xf, axis=-1, keepdims=True)
    xc = xf - mu
    y = xc * lax.rsqrt(jnp.mean(jnp.square(xc), axis=-1, keepdims=True) + EPS)
    return (y * g + b).astype(x.dtype)


def _head_norm(t, center):
    if center:
        t = t - jnp.mean(t, axis=-1, keepdims=True)
    return t * lax.rsqrt(jnp.mean(jnp.square(t), axis=-1, keepdims=True) + EPS)


def _rope_tables(seq_len):
    inv = ROPE_BASE ** (-jnp.arange(0, RET_DK, 2, dtype=jnp.float32) / RET_DK)
    ang = jnp.arange(seq_len, dtype=jnp.float32)[:, None] * inv[None, :]
    return jnp.cos(ang)[:, None, :], jnp.sin(ang)[:, None, :]


def _rope(t, cos, sin):
    t1, t2 = jnp.split(t, 2, axis=-1)
    return jnp.concatenate([t1 * cos - t2 * sin, t2 * cos + t1 * sin], axis=-1)


def _to_chunks(a):
    b, n = a.shape[0], a.shape[1]
    return jnp.moveaxis(a.reshape(b, n // CHUNK, CHUNK, *a.shape[2:]), 1, 0)


def _from_chunks(a):
    a = jnp.moveaxis(a, 0, 1)
    return a.reshape(a.shape[0], a.shape[1] * a.shape[2], *a.shape[3:])


def _directional(step, carry0, seqs, reverse):
    meta = tuple(s[:, :N_META] for s in seqs)
    real = tuple(_to_chunks(s[:, N_META:]) for s in seqs)
    if reverse:
        meta = tuple(jnp.flip(s, axis=1) for s in meta)
        real = tuple(jnp.flip(s, axis=(0, 2)) for s in real)
        carry, y_real = lax.scan(step, carry0, real)
        _, y_meta = step(carry, meta)
        y_real = jnp.flip(y_real, axis=(0, 2))
        y_meta = jnp.flip(y_meta, axis=1)
    else:
        carry, y_meta = step(carry0, meta)
        _, y_real = lax.scan(step, carry, real)
    return jnp.concatenate([y_meta, _from_chunks(y_real)], axis=1)


def _retention_step(state, inp, log_gamma, include_diag):
    q, k, v = inp
    cl = q.shape[1]
    pos = jnp.arange(cl, dtype=jnp.float32)
    rel = pos[:, None] - pos[None, :]
    mask = rel >= 0 if include_diag else rel > 0
    dec = jnp.where(mask[:, :, None], jnp.exp(jnp.maximum(rel, 0.0)[:, :, None] * log_gamma), 0.0)
    scores = jnp.einsum('bihd,bjhd->bijh', q, k) * dec
    intra = jnp.einsum('bijh,bjhe->bihe', scores, v)
    inter = jnp.einsum('bihd,bhde->bihe', q, state) * jnp.exp((pos + 1.0)[:, None] * log_gamma)[:, :, None]
    kd = k * jnp.exp((cl - 1.0 - pos)[:, None] * log_gamma)[:, :, None]
    new_state = jnp.exp(cl * log_gamma)[:, None, None] * state + jnp.einsum('bjhd,bjhe->bhde', kd, v)
    return new_state, intra + inter


def _hgrn2_step(state, inp):
    q, k, v, lf = inp
    cl = q.shape[1]
    b = jnp.cumsum(lf, axis=1)
    tri = jnp.tril(jnp.ones((cl, cl), dtype=bool))
    diff = b[:, :, None] - b[:, None, :]
    dec = jnp.exp(jnp.where(tri[None, :, :, None, None], diff, -jnp.inf))
    scores = jnp.einsum('bihd,bjhd,bijhd->bijh', q, k, dec)
    intra = jnp.einsum('bijh,bjhe->bihe', scores, v)
    inter = jnp.einsum('bihd,bhde->bihe', q * jnp.exp(b), state)
    b_last = b[:, -1]
    kd = k * jnp.exp(b_last[:, None] - b)
    new_state = jnp.exp(b_last)[..., None] * state + jnp.einsum('bjhd,bjhe->bhde', kd, v)
    return new_state, intra + inter


def _mlstm_step(carry, inp):
    c_st, n_st, m_st = carry
    q, k, v, ig, lf = inp
    cl = q.shape[1]
    b = jnp.cumsum(lf, axis=1)
    tri = jnp.tril(jnp.ones((cl, cl), dtype=bool))
    logw = b[:, :, None, :] - b[:, None, :, :] + ig[:, None, :, :]
    logw = jnp.where(tri[None, :, :, None], logw, -jnp.inf)
    log_inter = b + m_st[:, None, :]
    m_i = jnp.maximum(log_inter, jnp.max(logw, axis=2))
    w = jnp.exp(logw - m_i[:, :, None, :])
    s_inter = jnp.exp(log_inter - m_i)
    qk = jnp.einsum('bihd,bjhd->bijh', q, k) * w
    num = jnp.einsum('bijh,bjhe->bihe', qk, v) + s_inter[..., None] * jnp.einsum('bihd,bhde->bihe', q, c_st)
    den = jnp.sum(qk, axis=2) + s_inter * jnp.einsum('bihd,bhd->bih', q, n_st)
    h = num / jnp.maximum(jnp.abs(den), jnp.exp(-m_i))[..., None]
    b_last = b[:, -1]
    logw_end = b_last[:, None] - b + ig
    m_new = jnp.maximum(b_last + m_st, jnp.max(logw_end, axis=1))
    we = jnp.exp(logw_end - m_new[:, None])
    decay = jnp.exp(b_last + m_st - m_new)
    c_new = decay[..., None, None] * c_st + jnp.einsum('bjh,bjhd,bjhe->bhde', we, k, v)
    n_new = decay[..., None] * n_st + jnp.einsum('bjh,bjhd->bhd', we, k)
    return (c_new, n_new, m_new), h


def _linear_scan(a, u):
    def combine(l, r):
        return l[0] * r[0], r[0] * l[1] + r[1]
    return lax.associative_scan(combine, (a, u), axis=1)[1]


def _block_diag(x, w):
    b, l, _ = x.shape
    y = jnp.einsum('blni,nij->blnj', x.reshape(b, l, LRU_BLOCKS, LRU_BW), w)
    return y.reshape(b, l, LRU_W)


def _swiglu(t, wi, wo):
    g, u = jnp.split(t @ wi, 2, axis=-1)
    return (jax.nn.silu(g) * u) @ wo


def _even_mixer(h, w_in, conv_w, conv_b, wa, ba, wx, bx, lam, w_out):
    bsz, seq_len, _ = h.shape
    z = jnp.einsum('bld,de->ble', h, w_in).astype(jnp.float32)
    rq, rk, rv, rg, lx, ly = _split(z, EVEN_SPLITS)

    cos, sin = _rope_tables(seq_len)
    q = _rope(rq.reshape(bsz, seq_len, RET_HEADS, RET_DK), cos, sin)
    k = _rope(rk.reshape(bsz, seq_len, RET_HEADS, RET_DK), cos, sin) * (RET_DK ** -0.5)
    v = rv.reshape(bsz, seq_len, RET_HEADS, RET_DV)
    log_gamma = jnp.log1p(-jnp.exp2(-5.0 - jnp.arange(RET_HEADS, dtype=jnp.float32)))
    s0 = jnp.zeros((bsz, RET_HEADS, RET_DK, RET_DV), jnp.float32)
    fwd = _directional(functools.partial(_retention_step, log_gamma=log_gamma, include_diag=True), s0, (q, k, v), False)
    bwd = _directional(functools.partial(_retention_step, log_gamma=log_gamma, include_diag=False), s0, (q, k, v), True)
    ret = _head_norm(fwd + bwd, False).reshape(bsz, seq_len, GROUP_W) * jax.nn.silu(rg)

    xc = lax.conv_general_dilated(lx, conv_w.astype(jnp.float32)[:, None, :], (1,), [CONV_PAD],
                                  dimension_numbers=('NWC', 'WIO', 'NWC'),
                                  feature_group_count=LRU_W) + conv_b

    def lru_direction(d, rev):
        r = jax.nn.sigmoid(_block_diag(xc, wa[d]) + ba[d])
        i = jax.nn.sigmoid(_block_diag(xc, wx[d]) + bx[d])
        log_a = -LRU_C * jax.nn.softplus(-lam[d].astype(jnp.float32)) * r
        u = jnp.sqrt(-jnp.expm1(2.0 * log_a)) * (i * xc)
        a = jnp.exp(log_a)
        if rev:
            return jnp.flip(_linear_scan(jnp.flip(a, 1), jnp.flip(u, 1)), 1)
        return _linear_scan(a, u)

    lru = (lru_direction(0, False) + lru_direction(1, True)) * jax.nn.gelu(ly)
    mixed = jnp.concatenate([ret, lru], axis=-1)
    return jnp.einsum('ble,ed->bld', mixed, w_out).astype(h.dtype)


def _odd_mixer(h, w_in, lower, layer, ml_bi, ml_bf, w_out):
    bsz, seq_len, _ = h.shape
    z = jnp.einsum('bld,de->ble', h, w_in).astype(jnp.float32)
    hq, hf_f, hf_b, hi, hg, mq, mk, mv, mo, mgate = _split(z, ODD_SPLITS)

    def heads(t, n):
        return t.reshape(bsz, seq_len, n, -1)

    lb_soft = jax.nn.softmax(lower.astype(jnp.float32), axis=0)
    lb = (jnp.cumsum(lb_soft, axis=0) - lb_soft[0])[layer]
    q = heads(jax.nn.silu(hq), HG_HEADS)
    v = heads(hi, HG_HEADS)
    s0 = jnp.zeros((bsz, HG_HEADS, HG_DK, HG_DV), jnp.float32)

    def hg_direction(fp, rev):
        f = lb + (1.0 - lb) * jax.nn.sigmoid(fp)
        return _directional(_hgrn2_step, s0, (q, heads(1.0 - f, HG_HEADS), v, heads(jnp.log(f), HG_HEADS)), rev)

    o = hg_direction(hf_f, False) + hg_direction(hf_b, True)
    hg_out = _head_norm(o, False).reshape(bsz, seq_len, GROUP_W) * jax.nn.silu(hg)

    qm = heads(mq, ML_HEADS)
    km = heads(mk, ML_HEADS) * (ML_DK ** -0.5)
    vm = heads(mv, ML_HEADS)
    gates = mgate.reshape(bsz, seq_len, 2, 2, ML_HEADS)
    carry0 = (jnp.zeros((bsz, ML_HEADS, ML_DK, ML_DV), jnp.float32),
              jnp.zeros((bsz, ML_HEADS, ML_DK), jnp.float32),
              jnp.full((bsz, ML_HEADS), -jnp.inf, jnp.float32))

    def ml_direction(d, rev):
        ig = gates[:, :, 0, d] + ml_bi[d]
        lf = jax.nn.log_sigmoid(gates[:, :, 1, d] + ml_bf[d])
        return _directional(_mlstm_step, carry0, (qm, km, vm, ig, lf), rev)

    hm = ml_direction(0, False) + ml_direction(1, True)
    ml_out = _head_norm(hm, True).reshape(bsz, seq_len, GROUP_W) * jax.nn.sigmoid(mo)

    mixed = jnp.concatenate([hg_out, ml_out], axis=-1)
    return jnp.einsum('ble,ed->bld', mixed, w_out).astype(h.dtype)


def _moe(h, router, wi, wo):
    bsz, seq_len, d = h.shape
    t = h.reshape(-1, d)
    logits = (t @ router).astype(jnp.float32)
    top_val, top_idx = lax.top_k(logits, TOP_K)
    gate = jax.nn.softmax(top_val, axis=-1)
    combine = jnp.sum(jax.nn.one_hot(top_idx, N_EXP, dtype=jnp.float32) * gate[..., None], axis=1)
    y = jnp.zeros(t.shape, jnp.float32)
    for e in range(N_EXP):
        y = y + combine[:, e:e + 1] * _swiglu(t, wi[e], wo[e])
    return y.reshape(bsz, seq_len, d).astype(h.dtype)


def _trunk(x, meta, e_w_in, e_conv_w, e_conv_b, e_lru_wa, e_lru_ba, e_lru_wx, e_lru_bx, e_lru_lambda,
           e_w_out, e_ffn_wi, e_ffn_wo, o_w_in, o_hg_lower, o_ml_bi, o_ml_bf, o_w_out, o_router,
           o_exp_wi, o_exp_wo, ln_g, ln_b):
    bsz = x.shape[0]
    h = jnp.concatenate([jnp.broadcast_to(meta.astype(x.dtype)[None], (bsz, N_META, D_MODEL)), x], axis=1)
    for layer in range(DEPTH):
        p = layer // 2
        if layer % 2 == 0:
            mix = _even_mixer(h, e_w_in[p], e_conv_w[p], e_conv_b[p], e_lru_wa[p], e_lru_ba[p],
                              e_lru_wx[p], e_lru_bx[p], e_lru_lambda[p], e_w_out[p])
            h = _layer_norm(ALPHA * h + mix, ln_g[layer, 0], ln_b[layer, 0])
            ffn = _swiglu(h, e_ffn_wi[p], e_ffn_wo[p]).astype(h.dtype)
            h = _layer_norm(ALPHA * h + ffn, ln_g[layer, 1], ln_b[layer, 1])
        else:
            mix = _odd_mixer(h, o_w_in[p], o_hg_lower, layer, o_ml_bi[p], o_ml_bf[p], o_w_out[p])
            h = _layer_norm(ALPHA * h + mix, ln_g[layer, 0], ln_b[layer, 0])
            ffn = _moe(h, o_router[p], o_exp_wi[p], o_exp_wo[p])
            h = _layer_norm(ALPHA * h + ffn, ln_g[layer, 1], ln_b[layer, 1])
    return h[:, N_META:]


def setup_inputs(seed: int = 0) -> dict:
    key = jax.random.key(seed)
    ks = jax.random.split(key, 24)

    def nrm(k, shape, scale):
        return jax.random.normal(k, shape, jnp.float32) * scale

    u = jax.random.uniform(ks[10], (N_EVEN, 2, LRU_W), jnp.float32, 0.9, 0.999)
    a0 = u ** (1.0 / LRU_C)
    return {
        'x_prompt': nrm(ks[0], (BATCH, SEQ, D_MODEL), 1.0),
        'x_sample': nrm(ks[1], (DEC_BATCH, DEC_SEQ, D_MODEL), 1.0),
        'meta': nrm(ks[2], (N_META, D_MODEL), 1.0),
        'e_w_in': nrm(ks[3], (N_EVEN, D_MODEL, EVEN_IN), D_MODEL ** -0.5),
        'e_conv_w': nrm(ks[4], (N_EVEN, CONV_W, LRU_W), CONV_W ** -0.5),
        'e_conv_b': nrm(ks[5], (N_EVEN, LRU_W), 0.01),
        'e_lru_wa': nrm(ks[6], (N_EVEN, 2, LRU_BLOCKS, LRU_BW, LRU_BW), LRU_BW ** -0.5),
        'e_lru_ba': nrm(ks[7], (N_EVEN, 2, LRU_W), 0.01),
        'e_lru_wx': nrm(ks[8], (N_EVEN, 2, LRU_BLOCKS, LRU_BW, LRU_BW), LRU_BW ** -0.5),
        'e_lru_bx': nrm(ks[9], (N_EVEN, 2, LRU_W), 0.01),
        'e_lru_lambda': jnp.log(a0) - jnp.log1p(-a0),
        'e_w_out': nrm(ks[11], (N_EVEN, MIX_W, D_MODEL), MIX_W ** -0.5 * BETA),
        'e_ffn_wi': nrm(ks[12], (N_EVEN, D_MODEL, 2 * D_FF), D_MODEL ** -0.5),
        'e_ffn_wo': nrm(ks[13], (N_EVEN, D_FF, D_MODEL), D_FF ** -0.5 * BETA),
        'o_w_in': nrm(ks[14], (N_ODD, D_MODEL, ODD_IN), D_MODEL ** -0.5),
        'o_hg_lower': nrm(ks[15], (DEPTH, GROUP_W), 0.1),
        'o_ml_bi': nrm(ks[16], (N_ODD, 2, ML_HEADS), 0.1),
        'o_ml_bf': jnp.linspace(3.0, 6.0, ML_HEADS, dtype=jnp.float32) + nrm(ks[17], (N_ODD, 2, ML_HEADS), 0.1),
        'o_w_out': nrm(ks[18], (N_ODD, MIX_W, D_MODEL), MIX_W ** -0.5 * BETA),
        'o_router': nrm(ks[19], (N_ODD, D_MODEL, N_EXP), D_MODEL ** -0.5),
        'o_exp_wi': nrm(ks[20], (N_ODD, N_EXP, D_MODEL, 2 * E_FF), D_MODEL ** -0.5),
        'o_exp_wo': nrm(ks[21], (N_ODD, N_EXP, E_FF, D_MODEL), E_FF ** -0.5 * BETA),
        'ln_g': 1.0 + nrm(ks[22], (DEPTH, 2, D_MODEL), 0.01),
        'ln_b': nrm(ks[23], (DEPTH, 2, D_MODEL), 0.01),
    }


def reference(x_prompt, x_sample, meta, e_w_in, e_conv_w, e_conv_b, e_lru_wa, e_lru_ba, e_lru_wx,
              e_lru_bx, e_lru_lambda, e_w_out, e_ffn_wi, e_ffn_wo, o_w_in, o_hg_lower, o_ml_bi, o_ml_bf,
              o_w_out, o_router, o_exp_wi, o_exp_wo, ln_g, ln_b):
    weights = (meta, e_w_in, e_conv_w, e_conv_b, e_lru_wa, e_lru_ba, e_lru_wx, e_lru_bx, e_lru_lambda,
               e_w_out, e_ffn_wi, e_ffn_wo, o_w_in, o_hg_lower, o_ml_bi, o_ml_bf, o_w_out, o_router,
               o_exp_wi, o_exp_wo, ln_g, ln_b)
    y_prompt = _trunk(x_prompt, *weights)
    y_sample = _trunk(x_sample, *weights)
    return (y_prompt, y_sample)
```

```python
import functools
import math

import numpy as np
import jax
import jax.numpy as jnp
from jax import lax
from jax.experimental import pallas as pl
from jax.experimental.pallas import tpu as pltpu

F32 = jnp.float32
BF16 = jnp.bfloat16
HIGHEST = lax.Precision.HIGHEST

D_MODEL = 1024
GROUP_W = D_MODEL // 2
N_HEADS = 4
HEAD_D = GROUP_W // N_HEADS
N_META = 16
ROPE_BASE = 10000.0
LRU_BLOCKS = 8
LRU_BW = GROUP_W // LRU_BLOCKS
LRU_C = 8.0
N_EXP = 8
EPS = 1e-5

LANES = 128
SUBLANES = 8
CH = 128
PAD_ROWS = CH - N_META
SUB = 16
NEG_BIG = -1e30
VMEM_LIMIT = 56 * 1024 * 1024


def _dot(a, b):
    return jnp.dot(a.astype(BF16), b.astype(BF16), preferred_element_type=F32)


def _dot_nt(a, b):
    return lax.dot_general(a.astype(BF16), b.astype(BF16), (((1,), (1,)), ((), ())),
                           preferred_element_type=F32)


def _dot_tn(a, b):
    return _dot(a.T, b)


def _dot_f32(a, b):
    return jnp.dot(a, b, preferred_element_type=F32, precision=HIGHEST)


def _sigmoid(x):
    return 1.0 / (1.0 + jnp.exp(-x))


def _silu(x):
    return x * _sigmoid(x)


def _gelu_tanh(x):
    return 0.5 * x * (1.0 + jnp.tanh(math.sqrt(2.0 / math.pi) * (x + 0.044715 * (x * x * x))))


def _softplus(x):
    return jnp.maximum(x, 0.0) + jnp.log(1.0 + jnp.exp(-jnp.abs(x)))


def _pick_tile(total, target):
    best = LANES
    for t in range(LANES, min(total, target) + 1, LANES):
        if total % t == 0:
            best = t
    return best


def _params(n_axes, sem="arbitrary"):
    return pltpu.CompilerParams(dimension_semantics=(sem,) * n_axes, vmem_limit_bytes=VMEM_LIMIT)


def _proj_kernel(x_ref, w_ref, o_ref):
    o_ref[...] = _dot(x_ref[...], w_ref[...])


def _proj_gates_kernel(x_ref, w_ref, wg_ref, o_ref, g_ref):
    @pl.when(pl.program_id(1) == 0)
    def _():
        g_ref[...] = _dot_f32(x_ref[...], wg_ref[...])
    o_ref[...] = _dot(x_ref[...], w_ref[...])


def _project(x, w, w_gates=None):
    t, k = x.shape
    n = w.shape[1]
    tm = _pick_tile(t, 1024)
    tn = _pick_tile(n, 1536)
    grid = (t // tm, n // tn)
    x_spec = pl.BlockSpec((tm, k), lambda i, j: (i, 0))
    w_spec = pl.BlockSpec((k, tn), lambda i, j: (0, j))
    o_spec = pl.BlockSpec((tm, tn), lambda i, j: (i, j))
    if w_gates is None:
        return pl.pallas_call(
            _proj_kernel, grid=grid, in_specs=[x_spec, w_spec], out_specs=o_spec,
            out_shape=jax.ShapeDtypeStruct((t, n), F32), compiler_params=_params(2),
            name="proj")(x, w)
    ng = w_gates.shape[1]
    return pl.pallas_call(
        _proj_gates_kernel, grid=grid,
        in_specs=[x_spec, w_spec, pl.BlockSpec((k, ng), lambda i, j: (0, 0))],
        out_specs=[o_spec, pl.BlockSpec((tm, ng), lambda i, j: (i, 0))],
        out_shape=[jax.ShapeDtypeStruct((t, n), F32), jax.ShapeDtypeStruct((t, ng), F32)],
        compiler_params=_params(2), name="proj_gates")(x, w, w_gates)


def _chunk_tables(seq_shapes):
    cpos, clast = [], []
    for b, l in seq_shapes:
        n = l // CH + 1
        for _ in range(b):
            cpos += list(range(n))
            clast += [0] * (n - 1) + [1]
    return np.asarray(cpos, np.int32), np.asarray(clast, np.int32)


def _fwd_map(col):
    return lambda g, cp, cl: (g, col)


def _bwd_map(n_chunks, col):
    return lambda g, cp, cl: (n_chunks - 1 - g, col)


def _row_valid(first):
    row = lax.broadcasted_iota(jnp.int32, (CH, 1), 0)
    return row >= jnp.where(first, PAD_ROWS, 0)


def _ret_log_gamma():
    return np.log1p(-np.exp2(-5.0 - np.arange(N_HEADS, dtype=np.float64)))


def _ret_consts():
    lg = _ret_log_gamma()
    pos = np.arange(CH, dtype=np.float64)
    rel = pos[:, None] - pos[None, :]
    dmat = np.zeros((2, N_HEADS, CH, CH), np.float64)
    rows = np.zeros((4, CH, GROUP_W), np.float64)
    for h in range(N_HEADS):
        dmat[0, h] = np.where(rel >= 0, np.exp(np.maximum(rel, 0.0) * lg[h]), 0.0)
        dmat[1, h] = np.where(rel < 0, np.exp(np.maximum(-rel, 0.0) * lg[h]), 0.0)
        sl = slice(h * HEAD_D, (h + 1) * HEAD_D)
        rows[0, :, sl] = np.exp((pos + 1.0) * lg[h])[:, None]
        rows[1, :, sl] = np.exp((CH - 1.0 - pos) * lg[h])[:, None]
        rows[2, :, sl] = np.exp((CH - pos) * lg[h])[:, None]
        rows[3, :, sl] = np.exp(pos * lg[h])[:, None]
    return jnp.asarray(dmat, F32), jnp.asarray(rows, F32)


def _ret_direction(q_ref, k_ref, v_ref, cos_ref, sin_ref, dmat_ref, rin_ref, rout_ref, s_ref, o_ref,
                   reset, first):
    @pl.when(reset)
    def _():
        s_ref[...] = jnp.zeros_like(s_ref)

    valid = _row_valid(first)
    cos = cos_ref[...]
    sin = sin_ref[...]
    chunk_decay = np.exp(CH * _ret_log_gamma())
    for h in range(N_HEADS):
        sl = slice(h * HEAD_D, (h + 1) * HEAD_D)
        q = q_ref[:, sl]
        k = k_ref[:, sl]
        v = v_ref[:, sl]
        q = q * cos + pltpu.roll(q, HEAD_D // 2, 1) * sin
        k = (k * cos + pltpu.roll(k, HEAD_D // 2, 1) * sin) * (HEAD_D ** -0.5)
        k = jnp.where(valid, k, 0.0)
        scores = _dot_nt(q, k) * dmat_ref[h]
        state = s_ref[h]
        o_ref[:, sl] = _dot(scores, v) + _dot(q * rin_ref[:, sl], state)
        s_ref[h] = float(chunk_decay[h]) * state + _dot_tn(k * rout_ref[:, sl], v)


def _ret_kernel(cpos_ref, clast_ref, qf, kf, vf, cosf, sinf, qb, kb, vb, cosb, sinb, dmat_ref, rows_ref,
                of_ref, ob_ref, sf_ref, sb_ref):
    g = pl.program_id(0)
    gb = pl.num_programs(0) - 1 - g
    _ret_direction(qf, kf, vf, cosf, sinf, dmat_ref.at[0], rows_ref.at[0], rows_ref.at[1], sf_ref, of_ref,
                   cpos_ref[g] == 0, cpos_ref[g] == 0)
    _ret_direction(qb, kb, vb, cosb, sinb, dmat_ref.at[1], rows_ref.at[2], rows_ref.at[3], sb_ref, ob_ref,
                   clast_ref[gb] == 1, cpos_ref[gb] == 0)


def _rope_tables(n_rows):
    inv = ROPE_BASE ** (-jnp.arange(0, HEAD_D, 2, dtype=jnp.float32) / HEAD_D)
    pos = jnp.maximum(jnp.arange(n_rows, dtype=jnp.float32) - PAD_ROWS, 0.0)
    ang = pos[:, None] * inv[None, :]
    cos, sin = jnp.cos(ang), jnp.sin(ang)
    return jnp.concatenate([cos, cos], axis=1), jnp.concatenate([-sin, sin], axis=1)


def _retention(z, cpos, clast, max_rows):
    t = z.shape[0]
    n = t // CH
    cos2, sin2 = _rope_tables(max_rows)
    dmat, rows = _ret_consts()
    blk = (CH, GROUP_W)
    in_specs = []
    for mk, pm in ((_fwd_map, lambda g, cp, cl: (cp[g], 0)),
                   (functools.partial(_bwd_map, n), lambda g, cp, cl: (cp[n - 1 - g], 0))):
        in_specs += [pl.BlockSpec(blk, mk(0)), pl.BlockSpec(blk, mk(1)), pl.BlockSpec(blk, mk(2)),
                     pl.BlockSpec((CH, HEAD_D), pm), pl.BlockSpec((CH, HEAD_D), pm)]
    in_specs += [pl.BlockSpec(dmat.shape, lambda g, cp, cl: (0, 0, 0, 0)),
                 pl.BlockSpec(rows.shape, lambda g, cp, cl: (0, 0, 0))]
    grid_spec = pltpu.PrefetchScalarGridSpec(
        num_scalar_prefetch=2, grid=(n,), in_specs=in_specs,
        out_specs=[pl.BlockSpec(blk, _fwd_map(0)), pl.BlockSpec(blk, _bwd_map(n, 0))],
        scratch_shapes=[pltpu.VMEM((N_HEADS, HEAD_D, HEAD_D), F32)] * 2)
    return pl.pallas_call(
        _ret_kernel, grid_spec=grid_spec,
        out_shape=[jax.ShapeDtypeStruct((t, GROUP_W), F32)] * 2,
        compiler_params=_params(1), name="retention")(
            cpos, clast, z, z, z, cos2, sin2, z, z, z, cos2, sin2, dmat, rows)


HALO = SUBLANES


def _shift_rows(x, s, fill, reverse):
    row = lax.broadcasted_iota(jnp.int32, (CH, 1), 0)
    if reverse:
        return jnp.where(row < CH - s, pltpu.roll(x, CH - s, 0), fill)
    return jnp.where(row >= s, pltpu.roll(x, s, 0), fill)


def _lru_direction(x_ref, prev_ref, next_ref, convw_ref, convb_ref, wg_ref, bias_ref, lam_ref, ext_ref,
                   carry_ref, o_ref, reset, first, last, reverse):
    @pl.when(reset)
    def _():
        carry_ref[...] = jnp.zeros_like(carry_ref)

    valid = _row_valid(first)
    ext_ref[HALO:HALO + CH, :] = jnp.where(valid, x_ref[...], 0.0)
    ext_ref[0:HALO, :] = jnp.where(first, 0.0, prev_ref[...])
    ext_ref[HALO + CH:, :] = jnp.where(last, 0.0, next_ref[...])
    xc = convb_ref[...] + ext_ref[HALO - 2:HALO - 2 + CH, :] * convw_ref[0:1, :]
    for tap in range(1, 4):
        xc = xc + ext_ref[HALO - 2 + tap:HALO - 2 + tap + CH, :] * convw_ref[tap:tap + 1, :]

    log_sig_lam = -_softplus(-lam_ref[...])
    parts_a, parts_u = [], []
    for grp in range(GROUP_W // LANES):
        sl = slice(grp * LANES, (grp + 1) * LANES)
        xg = xc[:, sl]
        pre = _dot(xg, wg_ref[grp])
        r = _sigmoid(pre[:, :LANES] + bias_ref[0:1, sl])
        i = _sigmoid(pre[:, LANES:] + bias_ref[1:2, sl])
        a = jnp.exp(LRU_C * log_sig_lam[:, sl] * r)
        u = jnp.sqrt(1.0 - a * a) * (i * xg)
        parts_a.append(a)
        parts_u.append(jnp.where(valid, u, 0.0))
    a = jnp.concatenate(parts_a, axis=1)
    u = jnp.concatenate(parts_u, axis=1)

    s = 1
    while s < CH:
        u = u + a * _shift_rows(u, s, 0.0, reverse)
        a = a * _shift_rows(a, s, 1.0, reverse)
        s *= 2
    hseq = u + a * carry_ref[...]
    o_ref[...] = hseq
    edge = 0 if reverse else CH - 1
    carry_ref[...] = hseq[edge:edge + 1, :]


def _lru_kernel(cpos_ref, clast_ref, xf, pf, nf, xb, pb, nb, convw_ref, convb_ref, wg_ref, bias_ref, lam_ref,
                of_ref, ob_ref, extf_ref, extb_ref, cf_ref, cb_ref):
    g = pl.program_id(0)
    gb = pl.num_programs(0) - 1 - g
    _lru_direction(xf, pf, nf, convw_ref, convb_ref, wg_ref.at[0], bias_ref.at[0], lam_ref.at[0], extf_ref,
                   cf_ref, of_ref, cpos_ref[g] == 0, cpos_ref[g] == 0, clast_ref[g] == 1, False)
    _lru_direction(xb, pb, nb, convw_ref, convb_ref, wg_ref.at[1], bias_ref.at[1], lam_ref.at[1], extb_ref,
                   cb_ref, ob_ref, clast_ref[gb] == 1, cpos_ref[gb] == 0, clast_ref[gb] == 1, True)


def _lru_gate_weights(wa, wx):
    per = LANES // LRU_BW

    def block_diag(w):
        w = w.reshape(2, LRU_BLOCKS // per, per, LRU_BW, LRU_BW)
        eye = jnp.eye(per, dtype=w.dtype)
        return jnp.einsum("dgpij,pq->dgpiqj", w, eye).reshape(2, LRU_BLOCKS // per, LANES, LANES)

    return jnp.concatenate([block_diag(wa), block_diag(wx)], axis=-1).astype(BF16)


def _rglru(z, col, cpos, clast, conv_w, conv_b, wa, ba, wx, bx, lam):
    t = z.shape[0]
    n = t // CH
    per = CH // HALO
    n_halo = t // HALO
    blk = (CH, GROUP_W)
    hblk = (HALO, GROUP_W)
    wg = _lru_gate_weights(wa, wx)
    bias = jnp.stack([ba, bx], axis=1)
    lam = lam.reshape(2, 1, GROUP_W)

    def prev_f(g, cp, cl):
        return (jnp.maximum(g * per - 1, 0), col)

    def next_f(g, cp, cl):
        return (jnp.minimum((g + 1) * per, n_halo - 1), col)

    def prev_b(g, cp, cl):
        return (jnp.maximum((n - 1 - g) * per - 1, 0), col)

    def next_b(g, cp, cl):
        return (jnp.minimum((n - g) * per, n_halo - 1), col)

    full = lambda a: pl.BlockSpec(a.shape, lambda g, cp, cl: (0,) * a.ndim)
    conv_b2 = conv_b.reshape(1, GROUP_W)
    in_specs = [pl.BlockSpec(blk, _fwd_map(col)), pl.BlockSpec(hblk, prev_f), pl.BlockSpec(hblk, next_f),
                pl.BlockSpec(blk, _bwd_map(n, col)), pl.BlockSpec(hblk, prev_b), pl.BlockSpec(hblk, next_b),
                full(conv_w), full(conv_b2), full(wg), full(bias), full(lam)]
    grid_spec = pltpu.PrefetchScalarGridSpec(
        num_scalar_prefetch=2, grid=(n,), in_specs=in_specs,
        out_specs=[pl.BlockSpec(blk, _fwd_map(0)), pl.BlockSpec(blk, _bwd_map(n, 0))],
        scratch_shapes=[pltpu.VMEM((CH + 2 * HALO, GROUP_W), F32)] * 2 + [pltpu.VMEM((1, GROUP_W), F32)] * 2)
    return pl.pallas_call(
        _lru_kernel, grid_spec=grid_spec,
        out_shape=[jax.ShapeDtypeStruct((t, GROUP_W), F32)] * 2,
        compiler_params=_params(1), name="rglru")(
            cpos, clast, z, z, z, z, z, z, conv_w, conv_b2, wg, bias, lam)


def _tri_consts():
    pos = np.arange(CH)
    lower = (pos[:, None] >= pos[None, :]).astype(np.float32)
    return jnp.asarray(np.stack([lower, lower.T]))


def _hg_direction(q_ref, f_ref, v_ref, lb, cum_ref, st_ref, o_ref, reset, first, reverse):
    @pl.when(reset)
    def _():
        st_ref[...] = jnp.zeros_like(st_ref)

    valid = _row_valid(first)
    col = lax.broadcasted_iota(jnp.int32, (1, CH), 1)
    sub_i = lax.broadcasted_iota(jnp.int32, (SUB, 1), 0)
    ones_w = jnp.ones((HEAD_D, HEAD_D), BF16)
    cum_op = cum_ref[...]
    for h in range(N_HEADS):
        sl = slice(h * HEAD_D, (h + 1) * HEAD_D)
        lbh = lb[:, sl]
        q = _silu(q_ref[:, sl])
        f = lbh + (1.0 - lbh) * _sigmoid(f_ref[:, sl])
        k = jnp.where(valid, 1.0 - f, 0.0)
        v = v_ref[:, sl]
        b = _dot_f32(cum_op, jnp.log(f))
        state = st_ref[h]
        outs = []
        for blk in range(CH // SUB):
            r0 = blk * SUB
            rs = slice(r0, r0 + SUB)
            if reverse:
                edge = r0 + SUB
                ref_row = b[edge:edge + 1, :] if edge < CH else jnp.zeros((1, HEAD_D), F32)
                cross = col >= edge
            else:
                ref_row = b[r0 - 1:r0, :] if r0 > 0 else jnp.zeros((1, HEAD_D), F32)
                cross = col < r0
            qb, kb, vb, bb = q[rs], k[rs], v[rs], b[rs]
            q_dec = qb * jnp.exp(bb - ref_row)
            k_dec = k * jnp.exp(jnp.minimum(ref_row - b, 0.0))
            a_cross = jnp.where(cross, _dot_nt(q_dec, k_dec), 0.0)
            o_blk = _dot(a_cross, v) + _dot_nt(qb * jnp.exp(bb), state)
            tiles = []
            for j in range(SUB):
                keep = (sub_i <= j) if reverse else (sub_i >= j)
                e = jnp.exp(jnp.where(keep, bb - bb[j:j + 1, :], -jnp.inf))
                tiles.append(qb * e * kb[j:j + 1, :])
            rowsum = _dot(jnp.concatenate(tiles, axis=0), ones_w)
            for j in range(SUB):
                o_blk = o_blk + rowsum[j * SUB:(j + 1) * SUB, :] * vb[j:j + 1, :]
            outs.append(o_blk)
        o_ref[:, sl] = jnp.concatenate(outs, axis=0)
        b_tot = b[0:1, :] if reverse else b[CH - 1:CH, :]
        st_ref[h] = state * jnp.exp(b_tot) + _dot_tn(v, k * jnp.exp(b_tot - b))


def _hg_lower_bound(lower_ref, layer):
    low = lower_ref[...]
    e = jnp.exp(low - jnp.max(low, axis=0, keepdims=True))
    soft = e / jnp.sum(e, axis=0, keepdims=True)
    lb = jnp.zeros((1, GROUP_W), F32)
    for l in range(1, layer + 1):
        lb = lb + soft[l:l + 1, :]
    return lb


def _hg_kernel(layer, cpos_ref, clast_ref, qf, ff, vf, qb, fb, vb, lower_ref, cum_ref,
               of_ref, ob_ref, sf_ref, sb_ref):
    g = pl.program_id(0)
    gb = pl.num_programs(0) - 1 - g
    lb = _hg_lower_bound(lower_ref, layer)
    _hg_direction(qf, ff, vf, lb, cum_ref.at[0], sf_ref, of_ref, cpos_ref[g] == 0, cpos_ref[g] == 0, False)
    _hg_direction(qb, fb, vb, lb, cum_ref.at[1], sb_ref, ob_ref, clast_ref[gb] == 1, cpos_ref[gb] == 0, True)


def _hgrn2(z, cpos, clast, lower, layer):
    t = z.shape[0]
    n = t // CH
    blk = (CH, GROUP_W)
    cum = _tri_consts()
    full = lambda a: pl.BlockSpec(a.shape, lambda g, cp, cl: (0,) * a.ndim)
    in_specs = [pl.BlockSpec(blk, _fwd_map(0)), pl.BlockSpec(blk, _fwd_map(1)), pl.BlockSpec(blk, _fwd_map(3)),
                pl.BlockSpec(blk, _bwd_map(n, 0)), pl.BlockSpec(blk, _bwd_map(n, 2)),
                pl.BlockSpec(blk, _bwd_map(n, 3)), full(lower), full(cum)]
    grid_spec = pltpu.PrefetchScalarGridSpec(
        num_scalar_prefetch=2, grid=(n,), in_specs=in_specs,
        out_specs=[pl.BlockSpec(blk, _fwd_map(0)), pl.BlockSpec(blk, _bwd_map(n, 0))],
        scratch_shapes=[pltpu.VMEM((N_HEADS, HEAD_D, HEAD_D), F32)] * 2)
    return pl.pallas_call(
        functools.partial(_hg_kernel, layer), grid_spec=grid_spec,
        out_shape=[jax.ShapeDtypeStruct((t, GROUP_W), F32)] * 2,
        compiler_params=_params(1), name="hgrn2")(cpos, clast, z, z, z, z, z, z, lower, cum)


def _ml_direction(q_ref, k_ref, v_ref, gz, cum, gz_t, cum_t, c_ref, n_ref, m_ref, o_ref, reset, first, d,
                  reverse):
    @pl.when(reset)
    def _():
        c_ref[...] = jnp.zeros_like(c_ref)
        n_ref[...] = jnp.zeros_like(n_ref)
        m_ref[...] = jnp.full(m_ref.shape, -jnp.inf, F32)

    valid = _row_valid(first)
    col = lax.broadcasted_iota(jnp.int32, (1, CH), 1)
    valid_t = col >= jnp.where(first, PAD_ROWS, 0)
    row = lax.broadcasted_iota(jnp.int32, (CH, CH), 0)
    colm = lax.broadcasted_iota(jnp.int32, (CH, CH), 1)
    causal = (colm >= row) if reverse else (colm <= row)
    edge = 0 if reverse else CH - 1
    for h in range(N_HEADS):
        sl = slice(h * HEAD_D, (h + 1) * HEAD_D)
        ci = d * N_HEADS + h
        cf = 2 * N_HEADS + d * N_HEADS + h
        q = q_ref[:, sl]
        k = jnp.where(valid, k_ref[:, sl] * (HEAD_D ** -0.5), 0.0)
        v = v_ref[:, sl]
        b_col = cum[:, cf:cf + 1]
        b_row = cum_t[cf:cf + 1, :]
        ig_col = jnp.where(valid, gz[:, ci:ci + 1], NEG_BIG)
        ig_row = jnp.where(valid_t, gz_t[ci:ci + 1, :], NEG_BIG)
        m_st = m_ref[h][:, 0:1]
        logw = jnp.where(causal, b_col - b_row + ig_row, -jnp.inf)
        log_inter = b_col + m_st
        m_i = jnp.maximum(log_inter, jnp.max(logw, axis=1, keepdims=True))
        w = jnp.exp(logw - m_i)
        s_inter = jnp.exp(log_inter - m_i)
        qk = _dot_nt(q, k) * w
        num = _dot(qk, v) + s_inter * _dot(q, c_ref[h])
        den = jnp.sum(qk, axis=1, keepdims=True) + s_inter * jnp.sum(q * n_ref[h], axis=1, keepdims=True)
        o_ref[:, sl] = num / jnp.maximum(jnp.abs(den), jnp.exp(-m_i))
        b_last = b_col[edge:edge + 1, :]
        logw_end = b_last - b_col + ig_col
        m_new = jnp.maximum(b_last + m_st, jnp.max(logw_end, axis=0, keepdims=True))
        kw = k * jnp.exp(logw_end - m_new)
        decay = jnp.exp(b_last + m_st - m_new)
        c_ref[h] = decay * c_ref[h] + _dot_tn(kw, v)
        n_ref[h] = decay * n_ref[h] + jnp.sum(kw, axis=0, keepdims=True)
        m_ref[h] = jnp.broadcast_to(m_new, (1, LANES))


def _log_sigmoid(x):
    return jnp.minimum(x, 0.0) - jnp.log(1.0 + jnp.exp(-jnp.abs(x)))


def _ml_gates(g_ref, bias_ref, cum_op):
    gz = g_ref[...] + bias_ref[...]
    cum = _dot_f32(cum_op, _log_sigmoid(gz))
    return gz, cum, gz.T, cum.T


def _ml_kernel(cpos_ref, clast_ref, qf, kf, vf, gf, qb, kb, vb, gb_ref, bias_ref, cum_ref,
               of_ref, ob_ref, cfs, nfs, mfs, cbs, nbs, mbs):
    g = pl.program_id(0)
    gb = pl.num_programs(0) - 1 - g
    _ml_direction(qf, kf, vf, *_ml_gates(gf, bias_ref, cum_ref[0]), cfs, nfs, mfs, of_ref,
                  cpos_ref[g] == 0, cpos_ref[g] == 0, 0, False)
    _ml_direction(qb, kb, vb, *_ml_gates(gb_ref, bias_ref, cum_ref[1]), cbs, nbs, mbs, ob_ref,
                  clast_ref[gb] == 1, cpos_ref[gb] == 0, 1, True)


def _mlstm(z, gates, cpos, clast, ml_bi, ml_bf):
    t = z.shape[0]
    n = t // CH
    blk = (CH, GROUP_W)
    gblk = (CH, LANES)
    cum = _tri_consts()
    bias = jnp.concatenate([ml_bi.reshape(-1), ml_bf.reshape(-1),
                            jnp.zeros((LANES - 4 * N_HEADS,), F32)]).reshape(1, LANES)
    full = lambda a: pl.BlockSpec(a.shape, lambda g, cp, cl: (0,) * a.ndim)
    in_specs = [pl.BlockSpec(blk, _fwd_map(5)), pl.BlockSpec(blk, _fwd_map(6)), pl.BlockSpec(blk, _fwd_map(7)),
                pl.BlockSpec(gblk, _fwd_map(0)),
                pl.BlockSpec(blk, _bwd_map(n, 5)), pl.BlockSpec(blk, _bwd_map(n, 6)),
                pl.BlockSpec(blk, _bwd_map(n, 7)), pl.BlockSpec(gblk, _bwd_map(n, 0)),
                full(bias), full(cum)]
    state = [pltpu.VMEM((N_HEADS, HEAD_D, HEAD_D), F32), pltpu.VMEM((N_HEADS, 1, HEAD_D), F32),
             pltpu.VMEM((N_HEADS, 1, LANES), F32)]
    grid_spec = pltpu.PrefetchScalarGridSpec(
        num_scalar_prefetch=2, grid=(n,), in_specs=in_specs,
        out_specs=[pl.BlockSpec(blk, _fwd_map(0)), pl.BlockSpec(blk, _bwd_map(n, 0))],
        scratch_shapes=state * 2)
    return pl.pallas_call(
        _ml_kernel, grid_spec=grid_spec,
        out_shape=[jax.ShapeDtypeStruct((t, GROUP_W), F32)] * 2,
        compiler_params=_params(1), name="mlstm")(cpos, clast, z, z, z, gates, z, z, z, gates, bias, cum)


def _head_norm(x, center):
    outs = []
    for h in range(N_HEADS):
        xh = x[:, h * HEAD_D:(h + 1) * HEAD_D]
        if center:
            xh = xh - jnp.mean(xh, axis=1, keepdims=True)
        outs.append(xh * lax.rsqrt(jnp.mean(xh * xh, axis=1, keepdims=True) + EPS))
    return jnp.concatenate(outs, axis=1)


def _layer_norm(x, g, b):
    xc = x - jnp.mean(x, axis=1, keepdims=True)
    return xc * lax.rsqrt(jnp.mean(xc * xc, axis=1, keepdims=True) + EPS) * g + b


def _mix_out_kernel(alpha, even, af, ab, ag, bf, bb, bg, h_ref, w_ref, lng_ref, lnb_ref, o_ref):
    a = af[...] + ab[...]
    b = bf[...] + bb[...]
    if even:
        a = _head_norm(a, False) * _silu(ag[...])
        b = b * _gelu_tanh(bg[...])
    else:
        a = _head_norm(a, False) * _silu(ag[...])
        b = _head_norm(b, True) * _sigmoid(bg[...])
    mix = _dot(a, w_ref[0:GROUP_W, :]) + _dot(b, w_ref[GROUP_W:, :])
    o_ref[...] = _layer_norm(alpha * h_ref[...] + mix, lng_ref[...], lnb_ref[...])


def _mix_out(alpha, even, af, ab, bf, bb, z, col_a, col_b, h, w_out, ln_g, ln_b):
    t = h.shape[0]
    tm = _pick_tile(t, 512)
    half = lambda c: pl.BlockSpec((tm, GROUP_W), lambda i: (i, c))
    rows = pl.BlockSpec((tm, D_MODEL), lambda i: (i, 0))
    vec = pl.BlockSpec((1, D_MODEL), lambda i: (0, 0))
    return pl.pallas_call(
        functools.partial(_mix_out_kernel, alpha, even), grid=(t // tm,),
        in_specs=[half(0), half(0), half(col_a), half(0), half(0), half(col_b), rows,
                  pl.BlockSpec((D_MODEL, D_MODEL), lambda i: (0, 0)), vec, vec],
        out_specs=rows, out_shape=jax.ShapeDtypeStruct((t, D_MODEL), F32),
        compiler_params=_params(1), name="mix_out")(
            af, ab, z, bf, bb, z, h, w_out, ln_g.reshape(1, -1), ln_b.reshape(1, -1))


def _ffn_kernel(alpha, h_ref, wg_ref, wu_ref, wo_ref, lng_ref, lnb_ref, o_ref, acc_ref):
    f = pl.program_id(1)

    @pl.when(f == 0)
    def _():
        acc_ref[...] = jnp.zeros_like(acc_ref)

    x = h_ref[...].astype(BF16)
    act = _silu(_dot(x, wg_ref[...])) * _dot(x, wu_ref[...])
    acc_ref[...] += _dot(act, wo_ref[...])

    @pl.when(f == pl.num_programs(1) - 1)
    def _():
        o_ref[...] = _layer_norm(alpha * h_ref[...] + acc_ref[...], lng_ref[...], lnb_ref[...])


def _ffn(alpha, h, wi, wo, ln_g, ln_b):
    t = h.shape[0]
    d_ff = wo.shape[0]
    tm = _pick_tile(t, 1024)
    tf = _pick_tile(d_ff, 512)
    nf = d_ff // tf
    rows = pl.BlockSpec((tm, D_MODEL), lambda i, f: (i, 0))
    vec = pl.BlockSpec((1, D_MODEL), lambda i, f: (0, 0))
    return pl.pallas_call(
        functools.partial(_ffn_kernel, alpha), grid=(t // tm, nf),
        in_specs=[rows, pl.BlockSpec((D_MODEL, tf), lambda i, f: (0, f)),
                  pl.BlockSpec((D_MODEL, tf), lambda i, f: (0, f + nf)),
                  pl.BlockSpec((tf, D_MODEL), lambda i, f: (f, 0)), vec, vec],
        out_specs=rows, out_shape=jax.ShapeDtypeStruct((t, D_MODEL), F32),
        scratch_shapes=[pltpu.VMEM((tm, D_MODEL), F32)],
        compiler_params=_params(2), name="ffn")(h, wi, wi, wo, ln_g.reshape(1, -1), ln_b.reshape(1, -1))


def _moe_kernel(alpha, h_ref, router_ref, wg_ref, wu_ref, wo_ref, lng_ref, lnb_ref, o_ref, acc_ref, comb_ref):
    e = pl.program_id(1)
    f = pl.program_id(2)

    @pl.when((e == 0) & (f == 0))
    def _():
        acc_ref[...] = jnp.zeros_like(acc_ref)
        logits = _dot_f32(h_ref[...], router_ref[...])
        lane = lax.broadcasted_iota(jnp.int32, logits.shape, 1)
        logits = jnp.where(lane < N_EXP, logits, -jnp.inf)
        top1 = jnp.max(logits, axis=1, keepdims=True)
        idx1 = jnp.min(jnp.where(logits == top1, lane, LANES), axis=1, keepdims=True)
        rest = jnp.where(lane == idx1, -jnp.inf, logits)
        top2 = jnp.max(rest, axis=1, keepdims=True)
        idx2 = jnp.min(jnp.where(rest == top2, lane, LANES), axis=1, keepdims=True)
        g2 = jnp.exp(top2 - top1)
        denom = 1.0 + g2
        comb_ref[...] = jnp.where(lane == idx1, 1.0 / denom, 0.0) + jnp.where(lane == idx2, g2 / denom, 0.0)

    lane = lax.broadcasted_iota(jnp.int32, comb_ref.shape, 1)
    gate = jnp.sum(jnp.where(lane == e, comb_ref[...], 0.0), axis=1, keepdims=True)
    x = h_ref[...].astype(BF16)
    act = _silu(_dot(x, wg_ref[0])) * _dot(x, wu_ref[0])
    acc_ref[...] += _dot(act * gate, wo_ref[0])

    @pl.when((e == pl.num_programs(1) - 1) & (f == pl.num_programs(2) - 1))
    def _():
        o_ref[...] = _layer_norm(alpha * h_ref[...] + acc_ref[...], lng_ref[...], lnb_ref[...])


def _moe(alpha, h, router, wi, wo, ln_g, ln_b):
    t = h.shape[0]
    n_exp, e_ff = wo.shape[0], wo.shape[1]
    tm = _pick_tile(t, 1024)
    tf = _pick_tile(e_ff, 512)
    nf = e_ff // tf
    router_p = jnp.pad(router, ((0, 0), (0, LANES - n_exp)))
    rows = pl.BlockSpec((tm, D_MODEL), lambda i, e, f: (i, 0))
    vec = pl.BlockSpec((1, D_MODEL), lambda i, e, f: (0, 0))
    return pl.pallas_call(
        functools.partial(_moe_kernel, alpha), grid=(t // tm, n_exp, nf),
        in_specs=[rows, pl.BlockSpec((D_MODEL, LANES), lambda i, e, f: (0, 0)),
                  pl.BlockSpec((1, D_MODEL, tf), lambda i, e, f: (e, 0, f)),
                  pl.BlockSpec((1, D_MODEL, tf), lambda i, e, f: (e, 0, f + nf)),
                  pl.BlockSpec((1, tf, D_MODEL), lambda i, e, f: (e, f, 0)), vec, vec],
        out_specs=rows, out_shape=jax.ShapeDtypeStruct((t, D_MODEL), F32),
        scratch_shapes=[pltpu.VMEM((tm, D_MODEL), F32), pltpu.VMEM((tm, LANES), F32)],
        compiler_params=_params(3), name="moe")(
            h, router_p, wi, wi, wo, ln_g.reshape(1, -1), ln_b.reshape(1, -1))


def kernel(x_prompt, x_sample, meta, e_w_in, e_conv_w, e_conv_b, e_lru_wa, e_lru_ba, e_lru_wx, e_lru_bx,
           e_lru_lambda, e_w_out, e_ffn_wi, e_ffn_wo, o_w_in, o_hg_lower, o_ml_bi, o_ml_bf, o_w_out, o_router,
           o_exp_wi, o_exp_wo, ln_g, ln_b):
    groups = (x_prompt, x_sample)
    depth = ln_g.shape[0]
    alpha = (2.0 * depth) ** 0.25
    seq_shapes = [(x.shape[0], x.shape[1]) for x in groups]
    for _, l in seq_shapes:
        assert l % CH == 0
    cpos_np, clast_np = _chunk_tables(seq_shapes)
    cpos, clast = jnp.asarray(cpos_np), jnp.asarray(clast_np)
    max_rows = max(l for _, l in seq_shapes) + CH

    parts = []
    for x in groups:
        bsz = x.shape[0]
        head = jnp.concatenate([jnp.zeros((PAD_ROWS, D_MODEL), x.dtype), meta.astype(x.dtype)], axis=0)
        full = jnp.concatenate([jnp.broadcast_to(head[None], (bsz, CH, D_MODEL)), x], axis=1)
        parts.append(full.reshape(-1, D_MODEL))
    h = jnp.concatenate(parts, axis=0)

    n_odd_cols = 9 * GROUP_W
    for layer in range(depth):
        p = layer // 2
        if layer % 2 == 0:
            z = _project(h, e_w_in[p].astype(BF16))
            ret_f, ret_b = _retention(z, cpos, clast, max_rows)
            lru_f, lru_b = _rglru(z, 4, cpos, clast, e_conv_w[p], e_conv_b[p], e_lru_wa[p], e_lru_ba[p],
                                  e_lru_wx[p], e_lru_bx[p], e_lru_lambda[p])
            h = _mix_out(alpha, True, ret_f, ret_b, lru_f, lru_b, z, 3, 5, h, e_w_out[p].astype(BF16),
                         ln_g[layer, 0], ln_b[layer, 0])
            h = _ffn(alpha, h, e_ffn_wi[p].astype(BF16), e_ffn_wo[p].astype(BF16),
                     ln_g[layer, 1], ln_b[layer, 1])
        else:
            w_in = o_w_in[p]
            w_gates = jnp.pad(w_in[:, n_odd_cols:], ((0, 0), (0, LANES - (w_in.shape[1] - n_odd_cols))))
            z, gates = _project(h, w_in[:, :n_odd_cols].astype(BF16), w_gates)
            hg_f, hg_b = _hgrn2(z, cpos, clast, o_hg_lower, layer)
            ml_f, ml_b = _mlstm(z, gates, cpos, clast, o_ml_bi[p], o_ml_bf[p])
            h = _mix_out(alpha, False, hg_f, hg_b, ml_f, ml_b, z, 4, 8, h, o_w_out[p].astype(BF16),
                         ln_g[layer, 0], ln_b[layer, 0])
            h = _moe(alpha, h, o_router[p], o_exp_wi[p].astype(BF16), o_exp_wo[p].astype(BF16),
                     ln_g[layer, 1], ln_b[layer, 1])

    outs = []
    row = 0
    for b, l in seq_shapes:
        n = b * (l + CH)
        outs.append(h[row:row + n].reshape(b, l + CH, D_MODEL)[:, CH:])
        row += n
    return tuple(outs)
```

```python
import functools
import math

import numpy as np
import jax
import jax.numpy as jnp
from jax import lax
from jax.experimental import pallas as pl
from jax.experimental.pallas import tpu as pltpu

F32 = jnp.float32
BF16 = jnp.bfloat16
HIGHEST = lax.Precision.HIGHEST

D_MODEL = 1024
GROUP_W = D_MODEL // 2
N_HEADS = 4
HEAD_D = GROUP_W // N_HEADS
N_META = 16
ROPE_BASE = 10000.0
LRU_BLOCKS = 8
LRU_BW = GROUP_W // LRU_BLOCKS
LRU_C = 8.0
N_EXP = 8
EPS = 1e-5

LANES = 128
SUBLANES = 8
CH = 128
PAD_ROWS = CH - N_META
SUB = SUBLANES
GRP = 4 * SUB
NEG_BIG = -1e30
LOG2_E = math.log2(math.e)
VMEM_LIMIT = 56 * 1024 * 1024


def _dot(a, b):
    return jnp.dot(a.astype(BF16), b.astype(BF16), preferred_element_type=F32)


def _dot_nt(a, b):
    return lax.dot_general(a.astype(BF16), b.astype(BF16), (((1,), (1,)), ((), ())),
                           preferred_element_type=F32)


def _dot_tn(a, b):
    return _dot(a.T, b)


def _dot_f32(a, b):
    return jnp.dot(a, b, preferred_element_type=F32, precision=HIGHEST)


def _sigmoid(x):
    return 1.0 / (1.0 + jnp.exp(-x))


def _silu(x):
    return x * _sigmoid(x)


def _gelu_tanh(x):
    return 0.5 * x * (1.0 + jnp.tanh(math.sqrt(2.0 / math.pi) * (x + 0.044715 * (x * x * x))))


def _softplus(x):
    return jnp.maximum(x, 0.0) + jnp.log(1.0 + jnp.exp(-jnp.abs(x)))


def _pick_tile(total, target):
    best = LANES
    for t in range(LANES, min(total, target) + 1, LANES):
        if total % t == 0:
            best = t
    return best


def _params(n_axes, sem="arbitrary"):
    return pltpu.CompilerParams(dimension_semantics=(sem,) * n_axes, vmem_limit_bytes=VMEM_LIMIT)


def _proj_kernel(x_ref, w_ref, o_ref):
    o_ref[...] = _dot(x_ref[...], w_ref[...])


def _proj_gates_kernel(x_ref, w_ref, wg_ref, o_ref, g_ref):
    @pl.when(pl.program_id(1) == 0)
    def _():
        g_ref[...] = _dot_f32(x_ref[...], wg_ref[...])
    o_ref[...] = _dot(x_ref[...], w_ref[...])


def _project(x, w, w_gates=None):
    t, k = x.shape
    n = w.shape[1]
    tm = _pick_tile(t, 1024)
    tn = _pick_tile(n, 1536)
    grid = (t // tm, n // tn)
    x_spec = pl.BlockSpec((tm, k), lambda i, j: (i, 0))
    w_spec = pl.BlockSpec((k, tn), lambda i, j: (0, j))
    o_spec = pl.BlockSpec((tm, tn), lambda i, j: (i, j))
    if w_gates is None:
        return pl.pallas_call(
            _proj_kernel, grid=grid, in_specs=[x_spec, w_spec], out_specs=o_spec,
            out_shape=jax.ShapeDtypeStruct((t, n), F32), compiler_params=_params(2),
            name="proj")(x, w)
    ng = w_gates.shape[1]
    return pl.pallas_call(
        _proj_gates_kernel, grid=grid,
        in_specs=[x_spec, w_spec, pl.BlockSpec((k, ng), lambda i, j: (0, 0))],
        out_specs=[o_spec, pl.BlockSpec((tm, ng), lambda i, j: (i, 0))],
        out_shape=[jax.ShapeDtypeStruct((t, n), F32), jax.ShapeDtypeStruct((t, ng), F32)],
        compiler_params=_params(2), name="proj_gates")(x, w, w_gates)


def _chunk_tables(seq_shapes):
    cpos, clast = [], []
    for b, l in seq_shapes:
        n = l // CH + 1
        for _ in range(b):
            cpos += list(range(n))
            clast += [0] * (n - 1) + [1]
    return np.asarray(cpos, np.int32), np.asarray(clast, np.int32)


def _fwd_map(col):
    return lambda g, cp, cl: (g, col)


def _bwd_map(n_chunks, col):
    return lambda g, cp, cl: (n_chunks - 1 - g, col)


def _row_valid(first):
    row = lax.broadcasted_iota(jnp.int32, (CH, 1), 0)
    return row >= jnp.where(first, PAD_ROWS, 0)


def _ret_log_gamma():
    return np.log1p(-np.exp2(-5.0 - np.arange(N_HEADS, dtype=np.float64)))


def _ret_consts():
    lg = _ret_log_gamma()
    pos = np.arange(CH, dtype=np.float64)
    rel = pos[:, None] - pos[None, :]
    dmat = np.zeros((2, N_HEADS, CH, CH), np.float64)
    rows = np.zeros((4, CH, GROUP_W), np.float64)
    for h in range(N_HEADS):
        dmat[0, h] = np.where(rel >= 0, np.exp(np.maximum(rel, 0.0) * lg[h]), 0.0)
        dmat[1, h] = np.where(rel < 0, np.exp(np.maximum(-rel, 0.0) * lg[h]), 0.0)
        sl = slice(h * HEAD_D, (h + 1) * HEAD_D)
        rows[0, :, sl] = np.exp((pos + 1.0) * lg[h])[:, None]
        rows[1, :, sl] = np.exp((CH - 1.0 - pos) * lg[h])[:, None]
        rows[2, :, sl] = np.exp((CH - pos) * lg[h])[:, None]
        rows[3, :, sl] = np.exp(pos * lg[h])[:, None]
    return jnp.asarray(dmat, F32), jnp.asarray(rows, F32)


def _ret_direction(q_ref, k_ref, v_ref, cos_ref, sin_ref, dmat_ref, rin_ref, rout_ref, s_ref, o_ref,
                   reset, first):
    @pl.when(reset)
    def _():
        s_ref[...] = jnp.zeros_like(s_ref)

    valid = _row_valid(first)
    cos = cos_ref[...]
    sin = sin_ref[...]
    chunk_decay = np.exp(CH * _ret_log_gamma())
    for h in range(N_HEADS):
        sl = slice(h * HEAD_D, (h + 1) * HEAD_D)
        q = q_ref[:, sl]
        k = k_ref[:, sl]
        v = v_ref[:, sl]
        q = q * cos + pltpu.roll(q, HEAD_D // 2, 1) * sin
        k = (k * cos + pltpu.roll(k, HEAD_D // 2, 1) * sin) * (HEAD_D ** -0.5)
        k = jnp.where(valid, k, 0.0)
        scores = _dot_nt(q, k) * dmat_ref[h]
        state = s_ref[h]
        o_ref[:, sl] = _dot(scores, v) + _dot(q * rin_ref[:, sl], state)
        s_ref[h] = float(chunk_decay[h]) * state + _dot_tn(k * rout_ref[:, sl], v)


def _ret_kernel(cpos_ref, clast_ref, qf, kf, vf, cosf, sinf, qb, kb, vb, cosb, sinb, dmat_ref, rows_ref,
                of_ref, ob_ref, sf_ref, sb_ref):
    g = pl.program_id(0)
    gb = pl.num_programs(0) - 1 - g
    _ret_direction(qf, kf, vf, cosf, sinf, dmat_ref.at[0], rows_ref.at[0], rows_ref.at[1], sf_ref, of_ref,
                   cpos_ref[g] == 0, cpos_ref[g] == 0)
    _ret_direction(qb, kb, vb, cosb, sinb, dmat_ref.at[1], rows_ref.at[2], rows_ref.at[3], sb_ref, ob_ref,
                   clast_ref[gb] == 1, cpos_ref[gb] == 0)


def _rope_tables(n_rows):
    inv = ROPE_BASE ** (-jnp.arange(0, HEAD_D, 2, dtype=jnp.float32) / HEAD_D)
    pos = jnp.maximum(jnp.arange(n_rows, dtype=jnp.float32) - PAD_ROWS, 0.0)
    ang = pos[:, None] * inv[None, :]
    cos, sin = jnp.cos(ang), jnp.sin(ang)
    return jnp.concatenate([cos, cos], axis=1), jnp.concatenate([-sin, sin], axis=1)


def _retention(z, cpos, clast, max_rows):
    t = z.shape[0]
    n = t // CH
    cos2, sin2 = _rope_tables(max_rows)
    dmat, rows = _ret_consts()
    blk = (CH, GROUP_W)
    in_specs = []
    for mk, pm in ((_fwd_map, lambda g, cp, cl: (cp[g], 0)),
                   (functools.partial(_bwd_map, n), lambda g, cp, cl: (cp[n - 1 - g], 0))):
        in_specs += [pl.BlockSpec(blk, mk(0)), pl.BlockSpec(blk, mk(1)), pl.BlockSpec(blk, mk(2)),
                     pl.BlockSpec((CH, HEAD_D), pm), pl.BlockSpec((CH, HEAD_D), pm)]
    in_specs += [pl.BlockSpec(dmat.shape, lambda g, cp, cl: (0, 0, 0, 0)),
                 pl.BlockSpec(rows.shape, lambda g, cp, cl: (0, 0, 0))]
    grid_spec = pltpu.PrefetchScalarGridSpec(
        num_scalar_prefetch=2, grid=(n,), in_specs=in_specs,
        out_specs=[pl.BlockSpec(blk, _fwd_map(0)), pl.BlockSpec(blk, _bwd_map(n, 0))],
        scratch_shapes=[pltpu.VMEM((N_HEADS, HEAD_D, HEAD_D), F32)] * 2)
    return pl.pallas_call(
        _ret_kernel, grid_spec=grid_spec,
        out_shape=[jax.ShapeDtypeStruct((t, GROUP_W), F32)] * 2,
        compiler_params=_params(1), name="retention")(
            cpos, clast, z, z, z, cos2, sin2, z, z, z, cos2, sin2, dmat, rows)


HALO = SUBLANES


def _shift_rows(x, s, fill, reverse):
    row = lax.broadcasted_iota(jnp.int32, (CH, 1), 0)
    if reverse:
        return jnp.where(row < CH - s, pltpu.roll(x, CH - s, 0), fill)
    return jnp.where(row >= s, pltpu.roll(x, s, 0), fill)


def _lru_direction(x_ref, prev_ref, next_ref, convw_ref, convb_ref, wg_ref, bias_ref, lam_ref, ext_ref,
                   carry_ref, o_ref, reset, first, last, reverse):
    @pl.when(reset)
    def _():
        carry_ref[...] = jnp.zeros_like(carry_ref)

    valid = _row_valid(first)
    ext_ref[HALO:HALO + CH, :] = jnp.where(valid, x_ref[...], 0.0)
    ext_ref[0:HALO, :] = jnp.where(first, 0.0, prev_ref[...])
    ext_ref[HALO + CH:, :] = jnp.where(last, 0.0, next_ref[...])
    xc = convb_ref[...] + ext_ref[HALO - 2:HALO - 2 + CH, :] * convw_ref[0:1, :]
    for tap in range(1, 4):
        xc = xc + ext_ref[HALO - 2 + tap:HALO - 2 + tap + CH, :] * convw_ref[tap:tap + 1, :]

    log_sig_lam = -_softplus(-lam_ref[...])
    parts_a, parts_u = [], []
    for grp in range(GROUP_W // LANES):
        sl = slice(grp * LANES, (grp + 1) * LANES)
        xg = xc[:, sl]
        pre = _dot(xg, wg_ref[grp])
        r = _sigmoid(pre[:, :LANES] + bias_ref[0:1, sl])
        i = _sigmoid(pre[:, LANES:] + bias_ref[1:2, sl])
        a = jnp.exp(LRU_C * log_sig_lam[:, sl] * r)
        u = jnp.sqrt(1.0 - a * a) * (i * xg)
        parts_a.append(a)
        parts_u.append(jnp.where(valid, u, 0.0))
    a = jnp.concatenate(parts_a, axis=1)
    u = jnp.concatenate(parts_u, axis=1)

    s = 1
    while s < CH:
        u = u + a * _shift_rows(u, s, 0.0, reverse)
        a = a * _shift_rows(a, s, 1.0, reverse)
        s *= 2
    hseq = u + a * carry_ref[...]
    o_ref[...] = hseq
    edge = 0 if reverse else CH - 1
    carry_ref[...] = hseq[edge:edge + 1, :]


def _lru_kernel(cpos_ref, clast_ref, xf, pf, nf, xb, pb, nb, convw_ref, convb_ref, wg_ref, bias_ref, lam_ref,
                of_ref, ob_ref, extf_ref, extb_ref, cf_ref, cb_ref):
    g = pl.program_id(0)
    gb = pl.num_programs(0) - 1 - g
    _lru_direction(xf, pf, nf, convw_ref, convb_ref, wg_ref.at[0], bias_ref.at[0], lam_ref.at[0], extf_ref,
                   cf_ref, of_ref, cpos_ref[g] == 0, cpos_ref[g] == 0, clast_ref[g] == 1, False)
    _lru_direction(xb, pb, nb, convw_ref, convb_ref, wg_ref.at[1], bias_ref.at[1], lam_ref.at[1], extb_ref,
                   cb_ref, ob_ref, clast_ref[gb] == 1, cpos_ref[gb] == 0, clast_ref[gb] == 1, True)


def _lru_gate_weights(wa, wx):
    per = LANES // LRU_BW

    def block_diag(w):
        w = w.reshape(2, LRU_BLOCKS // per, per, LRU_BW, LRU_BW)
        eye = jnp.eye(per, dtype=w.dtype)
        return jnp.einsum("dgpij,pq->dgpiqj", w, eye).reshape(2, LRU_BLOCKS // per, LANES, LANES)

    return jnp.concatenate([block_diag(wa), block_diag(wx)], axis=-1).astype(BF16)


def _rglru(z, col, cpos, clast, conv_w, conv_b, wa, ba, wx, bx, lam):
    t = z.shape[0]
    n = t // CH
    per = CH // HALO
    n_halo = t // HALO
    blk = (CH, GROUP_W)
    hblk = (HALO, GROUP_W)
    wg = _lru_gate_weights(wa, wx)
    bias = jnp.stack([ba, bx], axis=1)
    lam = lam.reshape(2, 1, GROUP_W)

    def prev_f(g, cp, cl):
        return (jnp.maximum(g * per - 1, 0), col)

    def next_f(g, cp, cl):
        return (jnp.minimum((g + 1) * per, n_halo - 1), col)

    def prev_b(g, cp, cl):
        return (jnp.maximum((n - 1 - g) * per - 1, 0), col)

    def next_b(g, cp, cl):
        return (jnp.minimum((n - g) * per, n_halo - 1), col)

    full = lambda a: pl.BlockSpec(a.shape, lambda g, cp, cl: (0,) * a.ndim)
    conv_b2 = conv_b.reshape(1, GROUP_W)
    in_specs = [pl.BlockSpec(blk, _fwd_map(col)), pl.BlockSpec(hblk, prev_f), pl.BlockSpec(hblk, next_f),
                pl.BlockSpec(blk, _bwd_map(n, col)), pl.BlockSpec(hblk, prev_b), pl.BlockSpec(hblk, next_b),
                full(conv_w), full(conv_b2), full(wg), full(bias), full(lam)]
    grid_spec = pltpu.PrefetchScalarGridSpec(
        num_scalar_prefetch=2, grid=(n,), in_specs=in_specs,
        out_specs=[pl.BlockSpec(blk, _fwd_map(0)), pl.BlockSpec(blk, _bwd_map(n, 0))],
        scratch_shapes=[pltpu.VMEM((CH + 2 * HALO, GROUP_W), F32)] * 2 + [pltpu.VMEM((1, GROUP_W), F32)] * 2)
    return pl.pallas_call(
        _lru_kernel, grid_spec=grid_spec,
        out_shape=[jax.ShapeDtypeStruct((t, GROUP_W), F32)] * 2,
        compiler_params=_params(1), name="rglru")(
            cpos, clast, z, z, z, z, z, z, conv_w, conv_b2, wg, bias, lam)


def _tri_consts():
    pos = np.arange(CH)
    lower = (pos[:, None] >= pos[None, :]).astype(np.float32)
    return jnp.asarray(np.stack([lower, lower.T]))


def _hg_masks():
    pos = np.arange(CH)
    same_blk = pos[:, None] // SUB == pos[None, :] // SUB
    same_grp = pos[:, None] // GRP == pos[None, :] // GRP
    return jnp.asarray(np.stack([same_blk, same_grp]).astype(np.float32))


def _hg_direction(q_ref, f_ref, v_ref, lb, cum_ref, mask_ref, st_ref, o_ref, reset, first, reverse):
    @pl.when(reset)
    def _():
        st_ref[...] = jnp.zeros_like(st_ref)

    valid = _row_valid(first)
    sub_i = lax.broadcasted_iota(jnp.int32, (SUB, HEAD_D), 0)
    lane_j = lax.broadcasted_iota(jnp.int32, (SUB, HEAD_D), 1) & (SUB - 1)
    causal = (sub_i <= lane_j) if reverse else (sub_i >= lane_j)
    pick = [causal & (lane_j == j) for j in range(SUB)]
    cum_op = cum_ref[...]
    same_blk = mask_ref[0]
    same_grp = mask_ref[1]
    zero = jnp.zeros((SUB, HEAD_D), F32)
    n_blk = CH // SUB
    per = GRP // SUB
    n_grp = CH // GRP

    def edge(unit, idx):
        return unit * idx if reverse else unit * (idx + 1) - 1

    def split_product(b, qs, ks, bs, pieces):
        q_slabs, k_slabs = [], []
        for ref, q_active, k_active in pieces:
            qp, kp = [], []
            for i in range(n_blk):
                r = ref(i)
                qp.append(qs[i] * jnp.exp2(bs[i] - b[r:r + 1, :]) if q_active(i) else zero)
                kp.append(ks[i] * jnp.exp2(b[r:r + 1, :] - bs[i]) if k_active(i) else zero)
            q_slabs.append(jnp.concatenate(qp, axis=0))
            k_slabs.append(jnp.concatenate(kp, axis=0))
        return _dot_nt(jnp.concatenate(q_slabs, axis=1), jnp.concatenate(k_slabs, axis=1))

    level1 = []
    for c in (range(1, per) if reverse else range(per - 1)):
        level1.append((lambda i, c=c: edge(SUB, (i // per) * per + c),
                       (lambda i, c=c: i % per < c) if reverse else (lambda i, c=c: i % per > c),
                       lambda i, c=c: i % per == c))
    level2 = []
    for gc in (range(1, n_grp) if reverse else range(n_grp - 1)):
        level2.append((lambda i, gc=gc: edge(GRP, gc),
                       (lambda i, gc=gc: i // per < gc) if reverse else (lambda i, gc=gc: i // per > gc),
                       lambda i, gc=gc: i // per == gc))

    for h in range(N_HEADS):
        sl = slice(h * HEAD_D, (h + 1) * HEAD_D)
        lbh = lb[:, sl]
        q = _silu(q_ref[:, sl])
        f = lbh + (1.0 - lbh) * _sigmoid(f_ref[:, sl])
        k = jnp.where(valid, 1.0 - f, 0.0)
        v = v_ref[:, sl]
        b = _dot_f32(cum_op, jnp.log(f)) * LOG2_E
        qs = [q[i * SUB:(i + 1) * SUB] for i in range(n_blk)]
        ks = [k[i * SUB:(i + 1) * SUB] for i in range(n_blk)]
        bs = [b[i * SUB:(i + 1) * SUB] for i in range(n_blk)]

        tiles = [qs[blk] * jnp.exp2(jnp.minimum(bs[blk] - bs[blk][j:j + 1, :], 0.0))
                 for blk in range(n_blk) for j in range(SUB)]
        pair = _dot_nt(jnp.concatenate(tiles, axis=0), k)
        rows = []
        for blk in range(n_blk):
            base = blk * SUB * SUB
            a_blk = jnp.where(pick[0], pair[base:base + SUB], 0.0)
            for j in range(1, SUB):
                a_blk = a_blk + jnp.where(pick[j], pair[base + j * SUB:base + (j + 1) * SUB], 0.0)
            rows.append(a_blk)
        scores = jnp.concatenate(rows, axis=0) * same_blk
        scores = scores + split_product(b, qs, ks, bs, level1) * same_grp
        scores = scores + split_product(b, qs, ks, bs, level2)

        state = st_ref[h]
        o_ref[:, sl] = _dot(scores, v) + _dot_nt(q * jnp.exp2(b), state)
        b_tot = b[0:1, :] if reverse else b[CH - 1:CH, :]
        st_ref[h] = state * jnp.exp2(b_tot) + _dot_tn(v, k * jnp.exp2(b_tot - b))


def _hg_lower_bound(lower_ref, layer):
    low = lower_ref[...]
    e = jnp.exp(low - jnp.max(low, axis=0, keepdims=True))
    soft = e / jnp.sum(e, axis=0, keepdims=True)
    lb = jnp.zeros((1, GROUP_W), F32)
    for l in range(1, layer + 1):
        lb = lb + soft[l:l + 1, :]
    return lb


def _hg_kernel(layer, cpos_ref, clast_ref, qf, ff, vf, qb, fb, vb, lower_ref, cum_ref, mask_ref,
               of_ref, ob_ref, sf_ref, sb_ref):
    g = pl.program_id(0)
    gb = pl.num_programs(0) - 1 - g
    lb = _hg_lower_bound(lower_ref, layer)
    _hg_direction(qf, ff, vf, lb, cum_ref.at[0], mask_ref, sf_ref, of_ref,
                  cpos_ref[g] == 0, cpos_ref[g] == 0, False)
    _hg_direction(qb, fb, vb, lb, cum_ref.at[1], mask_ref, sb_ref, ob_ref,
                  clast_ref[gb] == 1, cpos_ref[gb] == 0, True)


def _hgrn2(z, cpos, clast, lower, layer):
    t = z.shape[0]
    n = t // CH
    blk = (CH, GROUP_W)
    cum = _tri_consts()
    masks = _hg_masks()
    full = lambda a: pl.BlockSpec(a.shape, lambda g, cp, cl: (0,) * a.ndim)
    in_specs = [pl.BlockSpec(blk, _fwd_map(0)), pl.BlockSpec(blk, _fwd_map(1)), pl.BlockSpec(blk, _fwd_map(3)),
                pl.BlockSpec(blk, _bwd_map(n, 0)), pl.BlockSpec(blk, _bwd_map(n, 2)),
                pl.BlockSpec(blk, _bwd_map(n, 3)), full(lower), full(cum), full(masks)]
    grid_spec = pltpu.PrefetchScalarGridSpec(
        num_scalar_prefetch=2, grid=(n,), in_specs=in_specs,
        out_specs=[pl.BlockSpec(blk, _fwd_map(0)), pl.BlockSpec(blk, _bwd_map(n, 0))],
        scratch_shapes=[pltpu.VMEM((N_HEADS, HEAD_D, HEAD_D), F32)] * 2)
    return pl.pallas_call(
        functools.partial(_hg_kernel, layer), grid_spec=grid_spec,
        out_shape=[jax.ShapeDtypeStruct((t, GROUP_W), F32)] * 2,
        compiler_params=_params(1), name="hgrn2")(cpos, clast, z, z, z, z, z, z, lower, cum, masks)


def _ml_direction(q_ref, k_ref, v_ref, gz, cum, gz_t, cum_t, c_ref, n_ref, m_ref, o_ref, reset, first, d,
                  reverse):
    @pl.when(reset)
    def _():
        c_ref[...] = jnp.zeros_like(c_ref)
        n_ref[...] = jnp.zeros_like(n_ref)
        m_ref[...] = jnp.full(m_ref.shape, -jnp.inf, F32)

    valid = _row_valid(first)
    col = lax.broadcasted_iota(jnp.int32, (1, CH), 1)
    valid_t = col >= jnp.where(first, PAD_ROWS, 0)
    row = lax.broadcasted_iota(jnp.int32, (CH, CH), 0)
    colm = lax.broadcasted_iota(jnp.int32, (CH, CH), 1)
    causal = (colm >= row) if reverse else (colm <= row)
    edge = 0 if reverse else CH - 1
    for h in range(N_HEADS):
        sl = slice(h * HEAD_D, (h + 1) * HEAD_D)
        ci = d * N_HEADS + h
        cf = 2 * N_HEADS + d * N_HEADS + h
        q = q_ref[:, sl]
        k = jnp.where(valid, k_ref[:, sl] * (HEAD_D ** -0.5), 0.0)
        v = v_ref[:, sl]
        b_col = cum[:, cf:cf + 1]
        b_row = cum_t[cf:cf + 1, :]
        ig_col = jnp.where(valid, gz[:, ci:ci + 1], NEG_BIG)
        ig_row = jnp.where(valid_t, gz_t[ci:ci + 1, :], NEG_BIG)
        m_st = m_ref[h][:, 0:1]
        logw = jnp.where(causal, b_col - b_row + ig_row, -jnp.inf)
        log_inter = b_col + m_st
        m_i = jnp.maximum(log_inter, jnp.max(logw, axis=1, keepdims=True))
        w = jnp.exp(logw - m_i)
        s_inter = jnp.exp(log_inter - m_i)
        qk = _dot_nt(q, k) * w
        num = _dot(qk, v) + s_inter * _dot(q, c_ref[h])
        den = jnp.sum(qk, axis=1, keepdims=True) + s_inter * jnp.sum(q * n_ref[h], axis=1, keepdims=True)
        o_ref[:, sl] = num / jnp.maximum(jnp.abs(den), jnp.exp(-m_i))
        b_last = b_col[edge:edge + 1, :]
        logw_end = b_last - b_col + ig_col
        m_new = jnp.maximum(b_last + m_st, jnp.max(logw_end, axis=0, keepdims=True))
        kw = k * jnp.exp(logw_end - m_new)
        decay = jnp.exp(b_last + m_st - m_new)
        c_ref[h] = decay * c_ref[h] + _dot_tn(kw, v)
        n_ref[h] = decay * n_ref[h] + jnp.sum(kw, axis=0, keepdims=True)
        m_ref[h] = jnp.broadcast_to(m_new, (1, LANES))


def _log_sigmoid(x):
    return jnp.minimum(x, 0.0) - jnp.log(1.0 + jnp.exp(-jnp.abs(x)))


def _ml_gates(g_ref, bias_ref, cum_op):
    gz = g_ref[...] + bias_ref[...]
    cum = _dot_f32(cum_op, _log_sigmoid(gz))
    return gz, cum, gz.T, cum.T


def _ml_kernel(cpos_ref, clast_ref, qf, kf, vf, gf, qb, kb, vb, gb_ref, bias_ref, cum_ref,
               of_ref, ob_ref, cfs, nfs, mfs, cbs, nbs, mbs):
    g = pl.program_id(0)
    gb = pl.num_programs(0) - 1 - g
    _ml_direction(qf, kf, vf, *_ml_gates(gf, bias_ref, cum_ref[0]), cfs, nfs, mfs, of_ref,
                  cpos_ref[g] == 0, cpos_ref[g] == 0, 0, False)
    _ml_direction(qb, kb, vb, *_ml_gates(gb_ref, bias_ref, cum_ref[1]), cbs, nbs, mbs, ob_ref,
                  clast_ref[gb] == 1, cpos_ref[gb] == 0, 1, True)


def _mlstm(z, gates, cpos, clast, ml_bi, ml_bf):
    t = z.shape[0]
    n = t // CH
    blk = (CH, GROUP_W)
    gblk = (CH, LANES)
    cum = _tri_consts()
    bias = jnp.concatenate([ml_bi.reshape(-1), ml_bf.reshape(-1),
                            jnp.zeros((LANES - 4 * N_HEADS,), F32)]).reshape(1, LANES)
    full = lambda a: pl.BlockSpec(a.shape, lambda g, cp, cl: (0,) * a.ndim)
    in_specs = [pl.BlockSpec(blk, _fwd_map(5)), pl.BlockSpec(blk, _fwd_map(6)), pl.BlockSpec(blk, _fwd_map(7)),
                pl.BlockSpec(gblk, _fwd_map(0)),
                pl.BlockSpec(blk, _bwd_map(n, 5)), pl.BlockSpec(blk, _bwd_map(n, 6)),
                pl.BlockSpec(blk, _bwd_map(n, 7)), pl.BlockSpec(gblk, _bwd_map(n, 0)),
                full(bias), full(cum)]
    state = [pltpu.VMEM((N_HEADS, HEAD_D, HEAD_D), F32), pltpu.VMEM((N_HEADS, 1, HEAD_D), F32),
             pltpu.VMEM((N_HEADS, 1, LANES), F32)]
    grid_spec = pltpu.PrefetchScalarGridSpec(
        num_scalar_prefetch=2, grid=(n,), in_specs=in_specs,
        out_specs=[pl.BlockSpec(blk, _fwd_map(0)), pl.BlockSpec(blk, _bwd_map(n, 0))],
        scratch_shapes=state * 2)
    return pl.pallas_call(
        _ml_kernel, grid_spec=grid_spec,
        out_shape=[jax.ShapeDtypeStruct((t, GROUP_W), F32)] * 2,
        compiler_params=_params(1), name="mlstm")(cpos, clast, z, z, z, gates, z, z, z, gates, bias, cum)


def _head_norm(x, center):
    outs = []
    for h in range(N_HEADS):
        xh = x[:, h * HEAD_D:(h + 1) * HEAD_D]
        if center:
            xh = xh - jnp.mean(xh, axis=1, keepdims=True)
        outs.append(xh * lax.rsqrt(jnp.mean(xh * xh, axis=1, keepdims=True) + EPS))
    return jnp.concatenate(outs, axis=1)


def _layer_norm(x, g, b):
    xc = x - jnp.mean(x, axis=1, keepdims=True)
    return xc * lax.rsqrt(jnp.mean(xc * xc, axis=1, keepdims=True) + EPS) * g + b


def _mix_out_kernel(alpha, even, af, ab, ag, bf, bb, bg, h_ref, w_ref, lng_ref, lnb_ref, o_ref):
    a = af[...] + ab[...]
    b = bf[...] + bb[...]
    if even:
        a = _head_norm(a, False) * _silu(ag[...])
        b = b * _gelu_tanh(bg[...])
    else:
        a = _head_norm(a, False) * _silu(ag[...])
        b = _head_norm(b, True) * _sigmoid(bg[...])
    mix = _dot(a, w_ref[0:GROUP_W, :]) + _dot(b, w_ref[GROUP_W:, :])
    o_ref[...] = _layer_norm(alpha * h_ref[...] + mix, lng_ref[...], lnb_ref[...])


def _mix_out(alpha, even, af, ab, bf, bb, z, col_a, col_b, h, w_out, ln_g, ln_b):
    t = h.shape[0]
    tm = _pick_tile(t, 512)
    half = lambda c: pl.BlockSpec((tm, GROUP_W), lambda i: (i, c))
    rows = pl.BlockSpec((tm, D_MODEL), lambda i: (i, 0))
    vec = pl.BlockSpec((1, D_MODEL), lambda i: (0, 0))
    return pl.pallas_call(
        functools.partial(_mix_out_kernel, alpha, even), grid=(t // tm,),
        in_specs=[half(0), half(0), half(col_a), half(0), half(0), half(col_b), rows,
                  pl.BlockSpec((D_MODEL, D_MODEL), lambda i: (0, 0)), vec, vec],
        out_specs=rows, out_shape=jax.ShapeDtypeStruct((t, D_MODEL), F32),
        compiler_params=_params(1), name="mix_out")(
            af, ab, z, bf, bb, z, h, w_out, ln_g.reshape(1, -1), ln_b.reshape(1, -1))


def _ffn_kernel(alpha, h_ref, wg_ref, wu_ref, wo_ref, lng_ref, lnb_ref, o_ref, acc_ref):
    f = pl.program_id(1)

    @pl.when(f == 0)
    def _():
        acc_ref[...] = jnp.zeros_like(acc_ref)

    x = h_ref[...].astype(BF16)
    act = _silu(_dot(x, wg_ref[...])) * _dot(x, wu_ref[...])
    acc_ref[...] += _dot(act, wo_ref[...])

    @pl.when(f == pl.num_programs(1) - 1)
    def _():
        o_ref[...] = _layer_norm(alpha * h_ref[...] + acc_ref[...], lng_ref[...], lnb_ref[...])


def _ffn(alpha, h, wi, wo, ln_g, ln_b):
    t = h.shape[0]
    d_ff = wo.shape[0]
    tm = _pick_tile(t, 1024)
    tf = _pick_tile(d_ff, 512)
    nf = d_ff // tf
    rows = pl.BlockSpec((tm, D_MODEL), lambda i, f: (i, 0))
    vec = pl.BlockSpec((1, D_MODEL), lambda i, f: (0, 0))
    return pl.pallas_call(
        functools.partial(_ffn_kernel, alpha), grid=(t // tm, nf),
        in_specs=[rows, pl.BlockSpec((D_MODEL, tf), lambda i, f: (0, f)),
                  pl.BlockSpec((D_MODEL, tf), lambda i, f: (0, f + nf)),
                  pl.BlockSpec((tf, D_MODEL), lambda i, f: (f, 0)), vec, vec],
        out_specs=rows, out_shape=jax.ShapeDtypeStruct((t, D_MODEL), F32),
        scratch_shapes=[pltpu.VMEM((tm, D_MODEL), F32)],
        compiler_params=_params(2), name="ffn")(h, wi, wi, wo, ln_g.reshape(1, -1), ln_b.reshape(1, -1))


EXPERT_TILE = 1024


def _route_kernel(h_ref, router_ref, tri_ref, meta_ref, gate_ref, cnt_ref, carry_ref):
    @pl.when(pl.program_id(0) == 0)
    def _():
        carry_ref[...] = jnp.zeros_like(carry_ref)

    logits = _dot_f32(h_ref[...], router_ref[...])
    lane = lax.broadcasted_iota(jnp.int32, logits.shape, 1)
    logits = jnp.where(lane < N_EXP, logits, -jnp.inf)
    top1 = jnp.max(logits, axis=1, keepdims=True)
    idx1 = jnp.min(jnp.where(logits == top1, lane, LANES), axis=1, keepdims=True)
    rest = jnp.where(lane == idx1, -jnp.inf, logits)
    top2 = jnp.max(rest, axis=1, keepdims=True)
    idx2 = jnp.min(jnp.where(rest == top2, lane, LANES), axis=1, keepdims=True)
    g2 = jnp.exp(top2 - top1)
    denom = 1.0 + g2
    hit1 = lane == idx1
    hit2 = lane == idx2
    both = jnp.where(hit1, 1.0, 0.0) + jnp.where(hit2, 1.0, 0.0)
    prefix = _dot(tri_ref[...], both) + carry_ref[...]
    rank1 = jnp.sum(jnp.where(hit1, prefix, 0.0), axis=1, keepdims=True).astype(jnp.int32)
    rank2 = jnp.sum(jnp.where(hit2, prefix, 0.0), axis=1, keepdims=True).astype(jnp.int32)
    carry_ref[...] += jnp.sum(both, axis=0, keepdims=True)
    cnt_ref[...] = carry_ref[...]
    meta_ref[...] = jnp.where(lane == 0, idx1, jnp.where(lane == 1, idx2, jnp.where(
        lane == 2, rank1, jnp.where(lane == 3, rank2, 0))))
    gate_ref[...] = jnp.where(lane == 0, 1.0 / denom, jnp.where(lane == 1, g2 / denom, 0.0))


def _route(h, router):
    t = h.shape[0]
    tm = _pick_tile(t, 512)
    router_p = jnp.pad(router, ((0, 0), (0, LANES - router.shape[1])))
    pos = np.arange(tm)
    tri = jnp.asarray(pos[:, None] > pos[None, :], BF16)
    rows = lambda w: pl.BlockSpec((tm, w), lambda i: (i, 0))
    const = lambda a: pl.BlockSpec(a.shape, lambda i: (0, 0))
    return pl.pallas_call(
        _route_kernel, grid=(t // tm,),
        in_specs=[rows(D_MODEL), const(router_p), const(tri)],
        out_specs=[rows(LANES), rows(LANES), pl.BlockSpec((1, LANES), lambda i: (0, 0))],
        out_shape=[jax.ShapeDtypeStruct((t, LANES), jnp.int32), jax.ShapeDtypeStruct((t, LANES), F32),
                   jax.ShapeDtypeStruct((1, LANES), F32)],
        scratch_shapes=[pltpu.VMEM((1, LANES), F32)],
        compiler_params=_params(1), name="route")(h, router_p, tri)


def _dispatch_plan(meta, counts, t):
    cnt = counts[0, :N_EXP].astype(jnp.int32)
    padded = ((cnt + EXPERT_TILE - 1) // EXPERT_TILE) * EXPERT_TILE
    ends = jnp.cumsum(padded)
    off = ends - padded

    def base(e):
        return sum(jnp.where(e == i, off[i], 0) for i in range(N_EXP))

    pos1 = base(meta[:, 0]) + meta[:, 2]
    pos2 = base(meta[:, 1]) + meta[:, 3]
    n_tiles = -(-2 * t // EXPERT_TILE) + N_EXP
    starts = jnp.arange(n_tiles, dtype=jnp.int32) * EXPERT_TILE
    tile_expert = jnp.minimum(jnp.sum(starts[:, None] >= ends[None, :], axis=1), N_EXP - 1).astype(jnp.int32)
    n_active = (ends[-1] // EXPERT_TILE).astype(jnp.int32).reshape(1)
    return pos1.astype(jnp.int32), pos2.astype(jnp.int32), tile_expert, n_active, n_tiles


def _dispatch_kernel(pos1_ref, pos2_ref, h_ref, xs_in_ref, xs_ref, sem):
    del xs_in_ref
    n = h_ref.shape[0]

    def row_copy(r, p):
        return pltpu.make_async_copy(h_ref.at[pl.ds(r, 1)], xs_ref.at[pl.ds(p, 1)], sem)

    def start(r, c):
        row_copy(r, pos1_ref[0, 0, r]).start()
        row_copy(r, pos2_ref[0, 0, r]).start()
        return c

    def wait(r, c):
        row_copy(0, 0).wait()
        row_copy(0, 0).wait()
        return c

    lax.fori_loop(0, n, start, 0)
    lax.fori_loop(0, n, wait, 0)


def _dispatch(h, pos1, pos2, n_rows):
    t = h.shape[0]
    tm = _pick_tile(t, 512)
    idx = lambda: pl.BlockSpec((1, 1, tm), lambda i: (i, 0, 0), memory_space=pltpu.SMEM)
    return pl.pallas_call(
        _dispatch_kernel, grid=(t // tm,),
        in_specs=[idx(), idx(), pl.BlockSpec((tm, D_MODEL), lambda i: (i, 0)),
                  pl.BlockSpec(memory_space=pl.ANY)],
        out_specs=pl.BlockSpec(memory_space=pl.ANY),
        out_shape=jax.ShapeDtypeStruct((n_rows, D_MODEL), F32),
        scratch_shapes=[pltpu.SemaphoreType.DMA(())],
        input_output_aliases={3: 0},
        compiler_params=_params(1), name="dispatch")(
            pos1.reshape(t // tm, 1, tm), pos2.reshape(t // tm, 1, tm), h, jnp.zeros((n_rows, D_MODEL), F32))


def _experts_kernel(te_ref, na_ref, x_ref, wg_ref, wu_ref, wo_ref, y_ref, acc_ref):
    i = pl.program_id(0)
    f = pl.program_id(1)
    last = f == pl.num_programs(1) - 1
    active = i < na_ref[0]

    @pl.when(active)
    def _():
        @pl.when(f == 0)
        def _():
            acc_ref[...] = jnp.zeros_like(acc_ref)

        x = x_ref[...].astype(BF16)
        act = _silu(_dot(x, wg_ref[0])) * _dot(x, wu_ref[0])
        acc_ref[...] += _dot(act, wo_ref[0])

        @pl.when(last)
        def _():
            y_ref[...] = acc_ref[...]

    @pl.when(jnp.logical_not(active) & last)
    def _():
        y_ref[...] = jnp.zeros_like(y_ref)


def _experts(xs, tile_expert, n_active, wi, wo):
    n_rows = xs.shape[0]
    e_ff = wo.shape[1]
    tf = _pick_tile(e_ff, 512)
    nf = e_ff // tf
    rows = pl.BlockSpec((EXPERT_TILE, D_MODEL), lambda i, f, te, na: (i, 0))
    grid_spec = pltpu.PrefetchScalarGridSpec(
        num_scalar_prefetch=2, grid=(n_rows // EXPERT_TILE, nf),
        in_specs=[rows, pl.BlockSpec((1, D_MODEL, tf), lambda i, f, te, na: (te[i], 0, f)),
                  pl.BlockSpec((1, D_MODEL, tf), lambda i, f, te, na: (te[i], 0, f + nf)),
                  pl.BlockSpec((1, tf, D_MODEL), lambda i, f, te, na: (te[i], f, 0))],
        out_specs=rows, scratch_shapes=[pltpu.VMEM((EXPERT_TILE, D_MODEL), F32)])
    return pl.pallas_call(
        _experts_kernel, grid_spec=grid_spec, out_shape=jax.ShapeDtypeStruct((n_rows, D_MODEL), F32),
        compiler_params=_params(2), name="experts")(tile_expert, n_active, xs, wi, wi, wo)


def _combine_kernel(alpha, pos1_ref, pos2_ref, h_ref, gate_ref, ys_ref, lng_ref, lnb_ref, o_ref,
                    a_ref, b_ref, sems):
    n = h_ref.shape[0]

    def row_copy(p, buf, r, which):
        return pltpu.make_async_copy(ys_ref.at[pl.ds(p, 1)], buf.at[pl.ds(r, 1)], sems.at[which])

    def start(r, c):
        row_copy(pos1_ref[0, 0, r], a_ref, r, 0).start()
        row_copy(pos2_ref[0, 0, r], b_ref, r, 1).start()
        return c

    def wait(r, c):
        row_copy(0, a_ref, 0, 0).wait()
        row_copy(0, b_ref, 0, 1).wait()
        return c

    lax.fori_loop(0, n, start, 0)
    lax.fori_loop(0, n, wait, 0)
    gate = gate_ref[...]
    y = gate[:, 0:1] * a_ref[...] + gate[:, 1:2] * b_ref[...]
    o_ref[...] = _layer_norm(alpha * h_ref[...] + y, lng_ref[...], lnb_ref[...])


def _combine(alpha, h, gates, ys, pos1, pos2, ln_g, ln_b):
    t = h.shape[0]
    tm = _pick_tile(t, 512)
    idx = lambda: pl.BlockSpec((1, 1, tm), lambda i: (i, 0, 0), memory_space=pltpu.SMEM)
    rows = pl.BlockSpec((tm, D_MODEL), lambda i: (i, 0))
    vec = pl.BlockSpec((1, D_MODEL), lambda i: (0, 0))
    return pl.pallas_call(
        functools.partial(_combine_kernel, alpha), grid=(t // tm,),
        in_specs=[idx(), idx(), rows, pl.BlockSpec((tm, LANES), lambda i: (i, 0)),
                  pl.BlockSpec(memory_space=pl.ANY), vec, vec],
        out_specs=rows, out_shape=jax.ShapeDtypeStruct((t, D_MODEL), F32),
        scratch_shapes=[pltpu.VMEM((tm, D_MODEL), F32), pltpu.VMEM((tm, D_MODEL), F32),
                        pltpu.SemaphoreType.DMA((2,))],
        compiler_params=_params(1), name="combine")(
            pos1.reshape(t // tm, 1, tm), pos2.reshape(t // tm, 1, tm), h, gates, ys,
            ln_g.reshape(1, -1), ln_b.reshape(1, -1))


def _moe(alpha, h, router, wi, wo, ln_g, ln_b):
    t = h.shape[0]
    meta, gates, counts = _route(h, router)
    pos1, pos2, tile_expert, n_active, n_tiles = _dispatch_plan(meta, counts, t)
    xs = _dispatch(h, pos1, pos2, n_tiles * EXPERT_TILE)
    ys = _experts(xs, tile_expert, n_active, wi, wo)
    return _combine(alpha, h, gates, ys, pos1, pos2, ln_g, ln_b)


def kernel(x_prompt, x_sample, meta, e_w_in, e_conv_w, e_conv_b, e_lru_wa, e_lru_ba, e_lru_wx, e_lru_bx,
           e_lru_lambda, e_w_out, e_ffn_wi, e_ffn_wo, o_w_in, o_hg_lower, o_ml_bi, o_ml_bf, o_w_out, o_router,
           o_exp_wi, o_exp_wo, ln_g, ln_b):
    groups = (x_prompt, x_sample)
    depth = ln_g.shape[0]
    alpha = (2.0 * depth) ** 0.25
    seq_shapes = [(x.shape[0], x.shape[1]) for x in groups]
    for _, l in seq_shapes:
        assert l % CH == 0
    cpos_np, clast_np = _chunk_tables(seq_shapes)
    cpos, clast = jnp.asarray(cpos_np), jnp.asarray(clast_np)
    max_rows = max(l for _, l in seq_shapes) + CH

    parts = []
    for x in groups:
        bsz = x.shape[0]
        head = jnp.concatenate([jnp.zeros((PAD_ROWS, D_MODEL), x.dtype), meta.astype(x.dtype)], axis=0)
        full = jnp.concatenate([jnp.broadcast_to(head[None], (bsz, CH, D_MODEL)), x], axis=1)
        parts.append(full.reshape(-1, D_MODEL))
    h = jnp.concatenate(parts, axis=0)

    n_odd_cols = 9 * GROUP_W
    for layer in range(depth):
        p = layer // 2
        if layer % 2 == 0:
            z = _project(h, e_w_in[p].astype(BF16))
            ret_f, ret_b = _retention(z, cpos, clast, max_rows)
            lru_f, lru_b = _rglru(z, 4, cpos, clast, e_conv_w[p], e_conv_b[p], e_lru_wa[p], e_lru_ba[p],
                                  e_lru_wx[p], e_lru_bx[p], e_lru_lambda[p])
            h = _mix_out(alpha, True, ret_f, ret_b, lru_f, lru_b, z, 3, 5, h, e_w_out[p].astype(BF16),
                         ln_g[layer, 0], ln_b[layer, 0])
            h = _ffn(alpha, h, e_ffn_wi[p].astype(BF16), e_ffn_wo[p].astype(BF16),
                     ln_g[layer, 1], ln_b[layer, 1])
        else:
            w_in = o_w_in[p]
            w_gates = jnp.pad(w_in[:, n_odd_cols:], ((0, 0), (0, LANES - (w_in.shape[1] - n_odd_cols))))
            z, gates = _project(h, w_in[:, :n_odd_cols].astype(BF16), w_gates)
            hg_f, hg_b = _hgrn2(z, cpos, clast, o_hg_lower, layer)
            ml_f, ml_b = _mlstm(z, gates, cpos, clast, o_ml_bi[p], o_ml_bf[p])
            h = _mix_out(alpha, False, hg_f, hg_b, ml_f, ml_b, z, 4, 8, h, o_w_out[p].astype(BF16),
                         ln_g[layer, 0], ln_b[layer, 0])
            h = _moe(alpha, h, o_router[p], o_exp_wi[p].astype(BF16), o_exp_wo[p].astype(BF16),
                     ln_g[layer, 1], ln_b[layer, 1])

    outs = []
    row = 0
    for b, l in seq_shapes:
        n = b * (l + CH)
        outs.append(h[row:row + n].reshape(b, l + CH, D_MODEL)[:, CH:])
        row += n
    return tuple(outs)
```

```python
import functools
import math

import numpy as np
import jax
import jax.numpy as jnp
from jax import lax
from jax.experimental import pallas as pl
from jax.experimental.pallas import tpu as pltpu

F32 = jnp.float32
BF16 = jnp.bfloat16
HIGHEST = lax.Precision.HIGHEST

D_MODEL = 1024
GROUP_W = D_MODEL // 2
N_HEADS = 4
HEAD_D = GROUP_W // N_HEADS
N_META = 16
ROPE_BASE = 10000.0
LRU_BLOCKS = 8
LRU_BW = GROUP_W // LRU_BLOCKS
LRU_C = 8.0
N_EXP = 8
EPS = 1e-5

LANES = 128
SUBLANES = 8
CH = 128
PAD_ROWS = CH - N_META
SUB = SUBLANES
GRP = 4 * SUB
NEG_BIG = -1e30
LOG2_E = math.log2(math.e)
VMEM_LIMIT = 56 * 1024 * 1024


def _dot(a, b):
    return jnp.dot(a.astype(BF16), b.astype(BF16), preferred_element_type=F32)


def _dot_nt(a, b):
    return lax.dot_general(a.astype(BF16), b.astype(BF16), (((1,), (1,)), ((), ())),
                           preferred_element_type=F32)


def _dot_tn(a, b):
    return _dot(a.T, b)


def _dot_f32(a, b):
    return jnp.dot(a, b, preferred_element_type=F32, precision=HIGHEST)


def _bf16_terms(x, n_terms):
    terms = []
    for _ in range(n_terms):
        t = x.astype(BF16)
        terms.append(t)
        x = x - t.astype(F32)
    return terms


PREFIX_TERMS = 3
SPREAD_TERMS = 2


def _prefix_dot(op_tiled, x):
    return jnp.dot(op_tiled, jnp.concatenate(_bf16_terms(x, PREFIX_TERMS), axis=0),
                   preferred_element_type=F32)


def _spread_dot(x, op_tiled):
    return jnp.dot(jnp.concatenate(_bf16_terms(x, SPREAD_TERMS), axis=1), op_tiled,
                   preferred_element_type=F32)


def _sigmoid(x):
    return 1.0 / (1.0 + jnp.exp(-x))


def _silu(x):
    return x * _sigmoid(x)


def _gelu_tanh(x):
    return 0.5 * x * (1.0 + jnp.tanh(math.sqrt(2.0 / math.pi) * (x + 0.044715 * (x * x * x))))


def _softplus(x):
    return jnp.maximum(x, 0.0) + jnp.log(1.0 + jnp.exp(-jnp.abs(x)))


def _pick_tile(total, target):
    best = LANES
    for t in range(LANES, min(total, target) + 1, LANES):
        if total % t == 0:
            best = t
    return best


def _params(n_axes, sem="arbitrary"):
    return pltpu.CompilerParams(dimension_semantics=(sem,) * n_axes, vmem_limit_bytes=VMEM_LIMIT)


def _proj_kernel(x_ref, w_ref, o_ref):
    o_ref[...] = _dot(x_ref[...], w_ref[...])


def _proj_gates_kernel(x_ref, w_ref, wg_ref, o_ref, g_ref):
    @pl.when(pl.program_id(1) == 0)
    def _():
        g_ref[...] = _dot_f32(x_ref[...], wg_ref[...])
    o_ref[...] = _dot(x_ref[...], w_ref[...])


def _project(x, w, w_gates=None):
    t, k = x.shape
    n = w.shape[1]
    tm = _pick_tile(t, 1024)
    tn = _pick_tile(n, 1536)
    grid = (t // tm, n // tn)
    x_spec = pl.BlockSpec((tm, k), lambda i, j: (i, 0))
    w_spec = pl.BlockSpec((k, tn), lambda i, j: (0, j))
    o_spec = pl.BlockSpec((tm, tn), lambda i, j: (i, j))
    if w_gates is None:
        return pl.pallas_call(
            _proj_kernel, grid=grid, in_specs=[x_spec, w_spec], out_specs=o_spec,
            out_shape=jax.ShapeDtypeStruct((t, n), F32), compiler_params=_params(2),
            name="proj")(x, w)
    ng = w_gates.shape[1]
    return pl.pallas_call(
        _proj_gates_kernel, grid=grid,
        in_specs=[x_spec, w_spec, pl.BlockSpec((k, ng), lambda i, j: (0, 0))],
        out_specs=[o_spec, pl.BlockSpec((tm, ng), lambda i, j: (i, 0))],
        out_shape=[jax.ShapeDtypeStruct((t, n), F32), jax.ShapeDtypeStruct((t, ng), F32)],
        compiler_params=_params(2), name="proj_gates")(x, w, w_gates)


def _chunk_tables(seq_shapes):
    cpos, clast = [], []
    for b, l in seq_shapes:
        n = l // CH + 1
        for _ in range(b):
            cpos += list(range(n))
            clast += [0] * (n - 1) + [1]
    return np.asarray(cpos, np.int32), np.asarray(clast, np.int32)


def _fwd_map(col):
    return lambda g, cp, cl: (g, col)


def _bwd_map(n_chunks, col):
    return lambda g, cp, cl: (n_chunks - 1 - g, col)


def _row_valid(first):
    row = lax.broadcasted_iota(jnp.int32, (CH, 1), 0)
    return row >= jnp.where(first, PAD_ROWS, 0)


def _ret_log_gamma():
    return np.log1p(-np.exp2(-5.0 - np.arange(N_HEADS, dtype=np.float64)))


def _ret_consts():
    lg = _ret_log_gamma()
    pos = np.arange(CH, dtype=np.float64)
    rel = pos[:, None] - pos[None, :]
    dmat = np.zeros((2, N_HEADS, CH, CH), np.float64)
    rows = np.zeros((4, CH, GROUP_W), np.float64)
    for h in range(N_HEADS):
        dmat[0, h] = np.where(rel >= 0, np.exp(np.maximum(rel, 0.0) * lg[h]), 0.0)
        dmat[1, h] = np.where(rel < 0, np.exp(np.maximum(-rel, 0.0) * lg[h]), 0.0)
        sl = slice(h * HEAD_D, (h + 1) * HEAD_D)
        rows[0, :, sl] = np.exp((pos + 1.0) * lg[h])[:, None]
        rows[1, :, sl] = np.exp((CH - 1.0 - pos) * lg[h])[:, None]
        rows[2, :, sl] = np.exp((CH - pos) * lg[h])[:, None]
        rows[3, :, sl] = np.exp(pos * lg[h])[:, None]
    return jnp.asarray(dmat, F32), jnp.asarray(rows, F32)


def _ret_direction(q_ref, k_ref, v_ref, cos_ref, sin_ref, dmat_ref, rin_ref, rout_ref, s_ref, o_ref,
                   reset, first):
    @pl.when(reset)
    def _():
        s_ref[...] = jnp.zeros_like(s_ref)

    valid = _row_valid(first)
    cos = cos_ref[...]
    sin = sin_ref[...]
    chunk_decay = np.exp(CH * _ret_log_gamma())
    for h in range(N_HEADS):
        sl = slice(h * HEAD_D, (h + 1) * HEAD_D)
        q = q_ref[:, sl]
        k = k_ref[:, sl]
        v = v_ref[:, sl]
        q = q * cos + pltpu.roll(q, HEAD_D // 2, 1) * sin
        k = (k * cos + pltpu.roll(k, HEAD_D // 2, 1) * sin) * (HEAD_D ** -0.5)
        k = jnp.where(valid, k, 0.0)
        scores = _dot_nt(q, k) * dmat_ref[h]
        state = s_ref[h]
        o_ref[:, sl] = _dot(scores, v) + _dot(q * rin_ref[:, sl], state)
        s_ref[h] = float(chunk_decay[h]) * state + _dot_tn(k * rout_ref[:, sl], v)


def _ret_kernel(cpos_ref, clast_ref, qf, kf, vf, cosf, sinf, qb, kb, vb, cosb, sinb, dmat_ref, rows_ref,
                of_ref, ob_ref, sf_ref, sb_ref):
    g = pl.program_id(0)
    gb = pl.num_programs(0) - 1 - g
    _ret_direction(qf, kf, vf, cosf, sinf, dmat_ref.at[0], rows_ref.at[0], rows_ref.at[1], sf_ref, of_ref,
                   cpos_ref[g] == 0, cpos_ref[g] == 0)
    _ret_direction(qb, kb, vb, cosb, sinb, dmat_ref.at[1], rows_ref.at[2], rows_ref.at[3], sb_ref, ob_ref,
                   clast_ref[gb] == 1, cpos_ref[gb] == 0)


def _rope_tables(n_rows):
    inv = ROPE_BASE ** (-jnp.arange(0, HEAD_D, 2, dtype=jnp.float32) / HEAD_D)
    pos = jnp.maximum(jnp.arange(n_rows, dtype=jnp.float32) - PAD_ROWS, 0.0)
    ang = pos[:, None] * inv[None, :]
    cos, sin = jnp.cos(ang), jnp.sin(ang)
    return jnp.concatenate([cos, cos], axis=1), jnp.concatenate([-sin, sin], axis=1)


def _retention(z, cpos, clast, max_rows):
    t = z.shape[0]
    n = t // CH
    cos2, sin2 = _rope_tables(max_rows)
    dmat, rows = _ret_consts()
    blk = (CH, GROUP_W)
    in_specs = []
    for mk, pm in ((_fwd_map, lambda g, cp, cl: (cp[g], 0)),
                   (functools.partial(_bwd_map, n), lambda g, cp, cl: (cp[n - 1 - g], 0))):
        in_specs += [pl.BlockSpec(blk, mk(0)), pl.BlockSpec(blk, mk(1)), pl.BlockSpec(blk, mk(2)),
                     pl.BlockSpec((CH, HEAD_D), pm), pl.BlockSpec((CH, HEAD_D), pm)]
    in_specs += [pl.BlockSpec(dmat.shape, lambda g, cp, cl: (0, 0, 0, 0)),
                 pl.BlockSpec(rows.shape, lambda g, cp, cl: (0, 0, 0))]
    grid_spec = pltpu.PrefetchScalarGridSpec(
        num_scalar_prefetch=2, grid=(n,), in_specs=in_specs,
        out_specs=[pl.BlockSpec(blk, _fwd_map(0)), pl.BlockSpec(blk, _bwd_map(n, 0))],
        scratch_shapes=[pltpu.VMEM((N_HEADS, HEAD_D, HEAD_D), F32)] * 2)
    return pl.pallas_call(
        _ret_kernel, grid_spec=grid_spec,
        out_shape=[jax.ShapeDtypeStruct((t, GROUP_W), F32)] * 2,
        compiler_params=_params(1), name="retention")(
            cpos, clast, z, z, z, cos2, sin2, z, z, z, cos2, sin2, dmat, rows)


HALO = SUBLANES


def _shift_rows(x, s, fill, reverse):
    row = lax.broadcasted_iota(jnp.int32, (CH, 1), 0)
    if reverse:
        return jnp.where(row < CH - s, pltpu.roll(x, CH - s, 0), fill)
    return jnp.where(row >= s, pltpu.roll(x, s, 0), fill)


def _lru_direction(x_ref, prev_ref, next_ref, convw_ref, convb_ref, wg_ref, bias_ref, lam_ref, ext_ref,
                   carry_ref, o_ref, reset, first, last, reverse):
    @pl.when(reset)
    def _():
        carry_ref[...] = jnp.zeros_like(carry_ref)

    valid = _row_valid(first)
    ext_ref[HALO:HALO + CH, :] = jnp.where(valid, x_ref[...], 0.0)
    ext_ref[0:HALO, :] = jnp.where(first, 0.0, prev_ref[...])
    ext_ref[HALO + CH:, :] = jnp.where(last, 0.0, next_ref[...])
    xc = convb_ref[...] + ext_ref[HALO - 2:HALO - 2 + CH, :] * convw_ref[0:1, :]
    for tap in range(1, 4):
        xc = xc + ext_ref[HALO - 2 + tap:HALO - 2 + tap + CH, :] * convw_ref[tap:tap + 1, :]

    log_sig_lam = -_softplus(-lam_ref[...])
    parts_a, parts_u = [], []
    for grp in range(GROUP_W // LANES):
        sl = slice(grp * LANES, (grp + 1) * LANES)
        xg = xc[:, sl]
        pre = _dot(xg, wg_ref[grp])
        r = _sigmoid(pre[:, :LANES] + bias_ref[0:1, sl])
        i = _sigmoid(pre[:, LANES:] + bias_ref[1:2, sl])
        a = jnp.exp(LRU_C * log_sig_lam[:, sl] * r)
        u = jnp.sqrt(1.0 - a * a) * (i * xg)
        parts_a.append(a)
        parts_u.append(jnp.where(valid, u, 0.0))
    a = jnp.concatenate(parts_a, axis=1)
    u = jnp.concatenate(parts_u, axis=1)

    s = 1
    while s < CH:
        u = u + a * _shift_rows(u, s, 0.0, reverse)
        a = a * _shift_rows(a, s, 1.0, reverse)
        s *= 2
    hseq = u + a * carry_ref[...]
    o_ref[...] = hseq
    edge = 0 if reverse else CH - 1
    carry_ref[...] = hseq[edge:edge + 1, :]


def _lru_kernel(cpos_ref, clast_ref, xf, pf, nf, xb, pb, nb, convw_ref, convb_ref, wg_ref, bias_ref, lam_ref,
                of_ref, ob_ref, extf_ref, extb_ref, cf_ref, cb_ref):
    g = pl.program_id(0)
    gb = pl.num_programs(0) - 1 - g
    _lru_direction(xf, pf, nf, convw_ref, convb_ref, wg_ref.at[0], bias_ref.at[0], lam_ref.at[0], extf_ref,
                   cf_ref, of_ref, cpos_ref[g] == 0, cpos_ref[g] == 0, clast_ref[g] == 1, False)
    _lru_direction(xb, pb, nb, convw_ref, convb_ref, wg_ref.at[1], bias_ref.at[1], lam_ref.at[1], extb_ref,
                   cb_ref, ob_ref, clast_ref[gb] == 1, cpos_ref[gb] == 0, clast_ref[gb] == 1, True)


def _lru_gate_weights(wa, wx):
    per = LANES // LRU_BW

    def block_diag(w):
        w = w.reshape(2, LRU_BLOCKS // per, per, LRU_BW, LRU_BW)
        eye = jnp.eye(per, dtype=w.dtype)
        return jnp.einsum("dgpij,pq->dgpiqj", w, eye).reshape(2, LRU_BLOCKS // per, LANES, LANES)

    return jnp.concatenate([block_diag(wa), block_diag(wx)], axis=-1).astype(BF16)


def _rglru(z, col, cpos, clast, conv_w, conv_b, wa, ba, wx, bx, lam):
    t = z.shape[0]
    n = t // CH
    per = CH // HALO
    n_halo = t // HALO
    blk = (CH, GROUP_W)
    hblk = (HALO, GROUP_W)
    wg = _lru_gate_weights(wa, wx)
    bias = jnp.stack([ba, bx], axis=1)
    lam = lam.reshape(2, 1, GROUP_W)

    def prev_f(g, cp, cl):
        return (jnp.maximum(g * per - 1, 0), col)

    def next_f(g, cp, cl):
        return (jnp.minimum((g + 1) * per, n_halo - 1), col)

    def prev_b(g, cp, cl):
        return (jnp.maximum((n - 1 - g) * per - 1, 0), col)

    def next_b(g, cp, cl):
        return (jnp.minimum((n - g) * per, n_halo - 1), col)

    full = lambda a: pl.BlockSpec(a.shape, lambda g, cp, cl: (0,) * a.ndim)
    conv_b2 = conv_b.reshape(1, GROUP_W)
    in_specs = [pl.BlockSpec(blk, _fwd_map(col)), pl.BlockSpec(hblk, prev_f), pl.BlockSpec(hblk, next_f),
                pl.BlockSpec(blk, _bwd_map(n, col)), pl.BlockSpec(hblk, prev_b), pl.BlockSpec(hblk, next_b),
                full(conv_w), full(conv_b2), full(wg), full(bias), full(lam)]
    grid_spec = pltpu.PrefetchScalarGridSpec(
        num_scalar_prefetch=2, grid=(n,), in_specs=in_specs,
        out_specs=[pl.BlockSpec(blk, _fwd_map(0)), pl.BlockSpec(blk, _bwd_map(n, 0))],
        scratch_shapes=[pltpu.VMEM((CH + 2 * HALO, GROUP_W), F32)] * 2 + [pltpu.VMEM((1, GROUP_W), F32)] * 2)
    return pl.pallas_call(
        _lru_kernel, grid_spec=grid_spec,
        out_shape=[jax.ShapeDtypeStruct((t, GROUP_W), F32)] * 2,
        compiler_params=_params(1), name="rglru")(
            cpos, clast, z, z, z, z, z, z, conv_w, conv_b2, wg, bias, lam)


def _tri_consts():
    pos = np.arange(CH)
    lower = (pos[:, None] >= pos[None, :]).astype(np.float32)
    ops = np.stack([lower, lower.T])
    return jnp.asarray(np.tile(ops, (1, 1, PREFIX_TERMS)), BF16)


def _hg_masks():
    pos = np.arange(CH)
    same_blk = pos[:, None] // SUB == pos[None, :] // SUB
    same_grp = pos[:, None] // GRP == pos[None, :] // GRP
    return jnp.asarray(np.stack([same_blk, same_grp]).astype(np.float32))


def _hg_direction(q_ref, f_ref, v_ref, lb, cum_ref, mask_ref, st_ref, o_ref, reset, first, reverse):
    @pl.when(reset)
    def _():
        st_ref[...] = jnp.zeros_like(st_ref)

    valid = _row_valid(first)
    sub_i = lax.broadcasted_iota(jnp.int32, (SUB, HEAD_D), 0)
    lane_j = lax.broadcasted_iota(jnp.int32, (SUB, HEAD_D), 1) & (SUB - 1)
    causal = (sub_i <= lane_j) if reverse else (sub_i >= lane_j)
    pick = [causal & (lane_j == j) for j in range(SUB)]
    cum_op = cum_ref[...]
    same_blk = mask_ref[0]
    same_grp = mask_ref[1]
    zero = jnp.zeros((SUB, HEAD_D), F32)
    n_blk = CH // SUB
    per = GRP // SUB
    n_grp = CH // GRP

    def edge(unit, idx):
        return unit * idx if reverse else unit * (idx + 1) - 1

    def split_product(b, qs, ks, bs, pieces):
        q_slabs, k_slabs = [], []
        for ref, q_active, k_active in pieces:
            qp, kp = [], []
            for i in range(n_blk):
                r = ref(i)
                qp.append(qs[i] * jnp.exp2(bs[i] - b[r:r + 1, :]) if q_active(i) else zero)
                kp.append(ks[i] * jnp.exp2(b[r:r + 1, :] - bs[i]) if k_active(i) else zero)
            q_slabs.append(jnp.concatenate(qp, axis=0))
            k_slabs.append(jnp.concatenate(kp, axis=0))
        return _dot_nt(jnp.concatenate(q_slabs, axis=1), jnp.concatenate(k_slabs, axis=1))

    level1 = []
    for c in (range(1, per) if reverse else range(per - 1)):
        level1.append((lambda i, c=c: edge(SUB, (i // per) * per + c),
                       (lambda i, c=c: i % per < c) if reverse else (lambda i, c=c: i % per > c),
                       lambda i, c=c: i % per == c))
    level2 = []
    for gc in (range(1, n_grp) if reverse else range(n_grp - 1)):
        level2.append((lambda i, gc=gc: edge(GRP, gc),
                       (lambda i, gc=gc: i // per < gc) if reverse else (lambda i, gc=gc: i // per > gc),
                       lambda i, gc=gc: i // per == gc))

    for h in range(N_HEADS):
        sl = slice(h * HEAD_D, (h + 1) * HEAD_D)
        lbh = lb[:, sl]
        q = _silu(q_ref[:, sl])
        f = lbh + (1.0 - lbh) * _sigmoid(f_ref[:, sl])
        k = jnp.where(valid, 1.0 - f, 0.0)
        v = v_ref[:, sl]
        b = _prefix_dot(cum_op, jnp.log(f)) * LOG2_E
        qs = [q[i * SUB:(i + 1) * SUB] for i in range(n_blk)]
        ks = [k[i * SUB:(i + 1) * SUB] for i in range(n_blk)]
        bs = [b[i * SUB:(i + 1) * SUB] for i in range(n_blk)]

        tiles = [qs[blk] * jnp.exp2(jnp.minimum(bs[blk] - bs[blk][j:j + 1, :], 0.0))
                 for blk in range(n_blk) for j in range(SUB)]
        pair = _dot_nt(jnp.concatenate(tiles, axis=0), k)
        rows = []
        for blk in range(n_blk):
            base = blk * SUB * SUB
            a_blk = jnp.where(pick[0], pair[base:base + SUB], 0.0)
            for j in range(1, SUB):
                a_blk = a_blk + jnp.where(pick[j], pair[base + j * SUB:base + (j + 1) * SUB], 0.0)
            rows.append(a_blk)
        scores = jnp.concatenate(rows, axis=0) * same_blk
        scores = scores + split_product(b, qs, ks, bs, level1) * same_grp
        scores = scores + split_product(b, qs, ks, bs, level2)

        state = st_ref[h]
        o_ref[:, sl] = _dot(scores, v) + _dot_nt(q * jnp.exp2(b), state)
        b_tot = b[0:1, :] if reverse else b[CH - 1:CH, :]
        st_ref[h] = state * jnp.exp2(b_tot) + _dot_tn(v, k * jnp.exp2(b_tot - b))


def _hg_lower_bound(lower_ref, layer):
    low = lower_ref[...]
    e = jnp.exp(low - jnp.max(low, axis=0, keepdims=True))
    soft = e / jnp.sum(e, axis=0, keepdims=True)
    lb = jnp.zeros((1, GROUP_W), F32)
    for l in range(1, layer + 1):
        lb = lb + soft[l:l + 1, :]
    return lb


def _hg_kernel(layer, cpos_ref, clast_ref, qf, ff, vf, qb, fb, vb, lower_ref, cum_ref, mask_ref,
               of_ref, ob_ref, sf_ref, sb_ref):
    g = pl.program_id(0)
    gb = pl.num_programs(0) - 1 - g
    lb = _hg_lower_bound(lower_ref, layer)
    _hg_direction(qf, ff, vf, lb, cum_ref.at[0], mask_ref, sf_ref, of_ref,
                  cpos_ref[g] == 0, cpos_ref[g] == 0, False)
    _hg_direction(qb, fb, vb, lb, cum_ref.at[1], mask_ref, sb_ref, ob_ref,
                  clast_ref[gb] == 1, cpos_ref[gb] == 0, True)


def _hgrn2(z, cpos, clast, lower, layer):
    t = z.shape[0]
    n = t // CH
    blk = (CH, GROUP_W)
    cum = _tri_consts()
    masks = _hg_masks()
    full = lambda a: pl.BlockSpec(a.shape, lambda g, cp, cl: (0,) * a.ndim)
    in_specs = [pl.BlockSpec(blk, _fwd_map(0)), pl.BlockSpec(blk, _fwd_map(1)), pl.BlockSpec(blk, _fwd_map(3)),
                pl.BlockSpec(blk, _bwd_map(n, 0)), pl.BlockSpec(blk, _bwd_map(n, 2)),
                pl.BlockSpec(blk, _bwd_map(n, 3)), full(lower), full(cum), full(masks)]
    grid_spec = pltpu.PrefetchScalarGridSpec(
        num_scalar_prefetch=2, grid=(n,), in_specs=in_specs,
        out_specs=[pl.BlockSpec(blk, _fwd_map(0)), pl.BlockSpec(blk, _bwd_map(n, 0))],
        scratch_shapes=[pltpu.VMEM((N_HEADS, HEAD_D, HEAD_D), F32)] * 2)
    return pl.pallas_call(
        functools.partial(_hg_kernel, layer), grid_spec=grid_spec,
        out_shape=[jax.ShapeDtypeStruct((t, GROUP_W), F32)] * 2,
        compiler_params=_params(1), name="hgrn2")(cpos, clast, z, z, z, z, z, z, lower, cum, masks)


def _log_sigmoid(x):
    return jnp.minimum(x, 0.0) - jnp.log(1.0 + jnp.exp(-jnp.abs(x)))


def _ml_direction(q_ref, k_ref, v_ref, gi_ref, gf_ref, bias_ref, cum_op, spread_ref, s_ref, m_ref, o_ref,
                  reset, first, d, reverse):
    @pl.when(reset)
    def _():
        s_ref[...] = jnp.zeros_like(s_ref)
        m_ref[...] = jnp.full(m_ref.shape, NEG_BIG, F32)

    valid = _row_valid(first)
    cum = _prefix_dot(cum_op, _log_sigmoid(gf_ref[...] + bias_ref[1]))
    a = jnp.where(valid, gi_ref[...] + bias_ref[0] - cum, NEG_BIG)
    run = a
    s = 1
    while s < CH:
        run = jnp.maximum(run, _shift_rows(run, s, NEG_BIG, reverse))
        s *= 2
    m_st = m_ref[...]
    mx = jnp.maximum(m_st, run)
    edge = 0 if reverse else CH - 1
    mx_last = mx[edge:edge + 1, :]
    m_ref[...] = cum[edge:edge + 1, :] + mx_last
    decay = jnp.exp(m_st - mx_last)

    stacked = jnp.concatenate([mx, m_st - mx, -(cum + mx), a - mx_last], axis=0) * LOG2_E
    spread_b = _spread_dot(stacked, spread_ref[...])
    mx_b, inter_b, floor_b, end_b = (spread_b[i * CH:(i + 1) * CH] for i in range(4))
    a_t = (a * LOG2_E).T

    row = lax.broadcasted_iota(jnp.int32, (CH, CH), 0)
    col = lax.broadcasted_iota(jnp.int32, (CH, CH), 1)
    causal = (col >= row) if reverse else (col <= row)
    ones = jnp.ones((CH, HEAD_D), F32)
    for h in range(N_HEADS):
        sl = slice(h * HEAD_D, (h + 1) * HEAD_D)
        x = d * N_HEADS + h
        q = q_ref[:, sl]
        k = jnp.where(valid, k_ref[:, sl] * (HEAD_D ** -0.5), 0.0)
        v_ext = jnp.concatenate([v_ref[:, sl], ones], axis=1)
        w = jnp.where(causal, jnp.exp2(jnp.minimum(a_t[x:x + 1, :] - mx_b[:, sl], 0.0)), 0.0)
        qk = _dot_nt(q, k) * w
        s_inter = jnp.exp2(jnp.minimum(inter_b[:, sl], 0.0))
        ext = _dot(qk, v_ext) + jnp.concatenate([s_inter, s_inter], axis=1) * _dot(q, s_ref[h])
        o_ref[:, sl] = ext[:, :HEAD_D] / jnp.maximum(jnp.abs(ext[:, HEAD_D:]), jnp.exp2(floor_b[:, sl]))
        kw = k * jnp.exp2(jnp.minimum(end_b[:, sl], 0.0))
        s_ref[h] = decay[:, x:x + 1] * s_ref[h] + _dot_tn(kw, v_ext)


def _ml_kernel(cpos_ref, clast_ref, qf, kf, vf, gif, gff, qb, kb, vb, gib, gfb, bias_ref, cum_ref, spread_ref,
               of_ref, ob_ref, sfs, mfs, sbs, mbs):
    g = pl.program_id(0)
    gb = pl.num_programs(0) - 1 - g
    _ml_direction(qf, kf, vf, gif, gff, bias_ref, cum_ref[0], spread_ref.at[0], sfs, mfs, of_ref,
                  cpos_ref[g] == 0, cpos_ref[g] == 0, 0, False)
    _ml_direction(qb, kb, vb, gib, gfb, bias_ref, cum_ref[1], spread_ref.at[1], sbs, mbs, ob_ref,
                  clast_ref[gb] == 1, cpos_ref[gb] == 0, 1, True)


def _ml_spread():
    spread = np.zeros((2, LANES, GROUP_W), np.float32)
    for d in range(2):
        for h in range(N_HEADS):
            spread[d, d * N_HEADS + h, h * HEAD_D:(h + 1) * HEAD_D] = 1.0
    return jnp.asarray(np.tile(spread, (1, SPREAD_TERMS, 1)), BF16)


def _mlstm(z, gates, cpos, clast, ml_bi, ml_bf):
    t = z.shape[0]
    n = t // CH
    blk = (CH, GROUP_W)
    gblk = (CH, LANES)
    cum = _tri_consts()
    spread = _ml_spread()
    pad = jnp.zeros((LANES - 2 * N_HEADS,), F32)
    bias = jnp.stack([jnp.concatenate([ml_bi.reshape(-1), pad]),
                      jnp.concatenate([ml_bf.reshape(-1), pad])]).reshape(2, 1, LANES)
    full = lambda a: pl.BlockSpec(a.shape, lambda g, cp, cl: (0,) * a.ndim)
    in_specs = [pl.BlockSpec(blk, _fwd_map(5)), pl.BlockSpec(blk, _fwd_map(6)), pl.BlockSpec(blk, _fwd_map(7)),
                pl.BlockSpec(gblk, _fwd_map(0)), pl.BlockSpec(gblk, _fwd_map(1)),
                pl.BlockSpec(blk, _bwd_map(n, 5)), pl.BlockSpec(blk, _bwd_map(n, 6)),
                pl.BlockSpec(blk, _bwd_map(n, 7)),
                pl.BlockSpec(gblk, _bwd_map(n, 0)), pl.BlockSpec(gblk, _bwd_map(n, 1)),
                full(bias), full(cum), full(spread)]
    state = [pltpu.VMEM((N_HEADS, HEAD_D, 2 * HEAD_D), F32), pltpu.VMEM((1, LANES), F32)]
    grid_spec = pltpu.PrefetchScalarGridSpec(
        num_scalar_prefetch=2, grid=(n,), in_specs=in_specs,
        out_specs=[pl.BlockSpec(blk, _fwd_map(0)), pl.BlockSpec(blk, _bwd_map(n, 0))],
        scratch_shapes=state * 2)
    return pl.pallas_call(
        _ml_kernel, grid_spec=grid_spec,
        out_shape=[jax.ShapeDtypeStruct((t, GROUP_W), F32)] * 2,
        compiler_params=_params(1), name="mlstm")(
            cpos, clast, z, z, z, gates, gates, z, z, z, gates, gates, bias, cum, spread)


def _head_norm(x, center):
    outs = []
    for h in range(N_HEADS):
        xh = x[:, h * HEAD_D:(h + 1) * HEAD_D]
        if center:
            xh = xh - jnp.mean(xh, axis=1, keepdims=True)
        outs.append(xh * lax.rsqrt(jnp.mean(xh * xh, axis=1, keepdims=True) + EPS))
    return jnp.concatenate(outs, axis=1)


def _layer_norm(x, g, b):
    xc = x - jnp.mean(x, axis=1, keepdims=True)
    return xc * lax.rsqrt(jnp.mean(xc * xc, axis=1, keepdims=True) + EPS) * g + b


def _mix_out_kernel(alpha, even, af, ab, ag, bf, bb, bg, h_ref, w_ref, lng_ref, lnb_ref, o_ref):
    a = af[...] + ab[...]
    b = bf[...] + bb[...]
    if even:
        a = _head_norm(a, False) * _silu(ag[...])
        b = b * _gelu_tanh(bg[...])
    else:
        a = _head_norm(a, False) * _silu(ag[...])
        b = _head_norm(b, True) * _sigmoid(bg[...])
    mix = _dot(a, w_ref[0:GROUP_W, :]) + _dot(b, w_ref[GROUP_W:, :])
    o_ref[...] = _layer_norm(alpha * h_ref[...] + mix, lng_ref[...], lnb_ref[...])


def _mix_out(alpha, even, af, ab, bf, bb, z, col_a, col_b, h, w_out, ln_g, ln_b):
    t = h.shape[0]
    tm = _pick_tile(t, 512)
    half = lambda c: pl.BlockSpec((tm, GROUP_W), lambda i: (i, c))
    rows = pl.BlockSpec((tm, D_MODEL), lambda i: (i, 0))
    vec = pl.BlockSpec((1, D_MODEL), lambda i: (0, 0))
    return pl.pallas_call(
        functools.partial(_mix_out_kernel, alpha, even), grid=(t // tm,),
        in_specs=[half(0), half(0), half(col_a), half(0), half(0), half(col_b), rows,
                  pl.BlockSpec((D_MODEL, D_MODEL), lambda i: (0, 0)), vec, vec],
        out_specs=rows, out_shape=jax.ShapeDtypeStruct((t, D_MODEL), F32),
        compiler_params=_params(1), name="mix_out")(
            af, ab, z, bf, bb, z, h, w_out, ln_g.reshape(1, -1), ln_b.reshape(1, -1))


def _ffn_kernel(alpha, h_ref, wg_ref, wu_ref, wo_ref, lng_ref, lnb_ref, o_ref, acc_ref):
    f = pl.program_id(1)

    @pl.when(f == 0)
    def _():
        acc_ref[...] = jnp.zeros_like(acc_ref)

    x = h_ref[...].astype(BF16)
    act = _silu(_dot(x, wg_ref[...])) * _dot(x, wu_ref[...])
    acc_ref[...] += _dot(act, wo_ref[...])

    @pl.when(f == pl.num_programs(1) - 1)
    def _():
        o_ref[...] = _layer_norm(alpha * h_ref[...] + acc_ref[...], lng_ref[...], lnb_ref[...])


def _ffn(alpha, h, wi, wo, ln_g, ln_b):
    t = h.shape[0]
    d_ff = wo.shape[0]
    tm = _pick_tile(t, 1024)
    tf = _pick_tile(d_ff, 1536)
    nf = d_ff // tf
    rows = pl.BlockSpec((tm, D_MODEL), lambda i, f: (i, 0))
    vec = pl.BlockSpec((1, D_MODEL), lambda i, f: (0, 0))
    return pl.pallas_call(
        functools.partial(_ffn_kernel, alpha), grid=(t // tm, nf),
        in_specs=[rows, pl.BlockSpec((D_MODEL, tf), lambda i, f: (0, f)),
                  pl.BlockSpec((D_MODEL, tf), lambda i, f: (0, f + nf)),
                  pl.BlockSpec((tf, D_MODEL), lambda i, f: (f, 0)), vec, vec],
        out_specs=rows, out_shape=jax.ShapeDtypeStruct((t, D_MODEL), F32),
        scratch_shapes=[pltpu.VMEM((tm, D_MODEL), F32)],
        compiler_params=_params(2), name="ffn")(h, wi, wi, wo, ln_g.reshape(1, -1), ln_b.reshape(1, -1))


EXPERT_TILE = 1024


def _route_kernel(pad_starts, h_ref, router_ref, tri_ref, meta_ref, gate_ref, cnt_ref, carry_ref):
    @pl.when(pl.program_id(0) == 0)
    def _():
        carry_ref[...] = jnp.zeros_like(carry_ref)

    tm = h_ref.shape[0]
    row = pl.program_id(0) * tm + lax.broadcasted_iota(jnp.int32, (tm, 1), 0)
    is_pad = row < 0
    for s in pad_starts:
        is_pad = is_pad | ((row >= s) & (row < s + PAD_ROWS))
    is_token = jnp.logical_not(is_pad)

    logits = _dot_f32(h_ref[...], router_ref[...])
    lane = lax.broadcasted_iota(jnp.int32, logits.shape, 1)
    logits = jnp.where(lane < N_EXP, logits, -jnp.inf)
    top1 = jnp.max(logits, axis=1, keepdims=True)
    idx1 = jnp.min(jnp.where(logits == top1, lane, LANES), axis=1, keepdims=True)
    rest = jnp.where(lane == idx1, -jnp.inf, logits)
    top2 = jnp.max(rest, axis=1, keepdims=True)
    idx2 = jnp.min(jnp.where(rest == top2, lane, LANES), axis=1, keepdims=True)
    g2 = jnp.exp(top2 - top1)
    denom = 1.0 + g2
    hit1 = lane == idx1
    hit2 = lane == idx2
    both = jnp.where(is_token, jnp.where(hit1, 1.0, 0.0) + jnp.where(hit2, 1.0, 0.0), 0.0)
    prefix = _dot(tri_ref[...], both) + carry_ref[...]
    rank1 = jnp.sum(jnp.where(hit1, prefix, 0.0), axis=1, keepdims=True).astype(jnp.int32)
    rank2 = jnp.sum(jnp.where(hit2, prefix, 0.0), axis=1, keepdims=True).astype(jnp.int32)
    carry_ref[...] += jnp.sum(both, axis=0, keepdims=True)
    cnt_ref[...] = carry_ref[...]
    meta_ref[...] = jnp.where(lane == 0, idx1, jnp.where(lane == 1, idx2, jnp.where(
        lane == 2, rank1, jnp.where(lane == 3, rank2, jnp.where(is_token & (lane == 4), 1, 0)))))
    gate_ref[...] = jnp.where(lane == 0, 1.0 / denom, jnp.where(lane == 1, g2 / denom, 0.0))


def _route(h, router, pad_starts):
    t = h.shape[0]
    tm = _pick_tile(t, 512)
    router_p = jnp.pad(router, ((0, 0), (0, LANES - router.shape[1])))
    pos = np.arange(tm)
    tri = jnp.asarray(pos[:, None] > pos[None, :], BF16)
    rows = lambda w: pl.BlockSpec((tm, w), lambda i: (i, 0))
    const = lambda a: pl.BlockSpec(a.shape, lambda i: (0, 0))
    return pl.pallas_call(
        functools.partial(_route_kernel, pad_starts), grid=(t // tm,),
        in_specs=[rows(D_MODEL), const(router_p), const(tri)],
        out_specs=[rows(LANES), rows(LANES), pl.BlockSpec((1, LANES), lambda i: (0, 0))],
        out_shape=[jax.ShapeDtypeStruct((t, LANES), jnp.int32), jax.ShapeDtypeStruct((t, LANES), F32),
                   jax.ShapeDtypeStruct((1, LANES), F32)],
        scratch_shapes=[pltpu.VMEM((1, LANES), F32)],
        compiler_params=_params(1), name="route")(h, router_p, tri)


def _dispatch_plan(meta, counts, t):
    cnt = counts[0, :N_EXP].astype(jnp.int32)
    padded = ((cnt + EXPERT_TILE - 1) // EXPERT_TILE) * EXPERT_TILE
    ends = jnp.cumsum(padded)
    off = ends - padded

    def base(e):
        return sum(jnp.where(e == i, off[i], 0) for i in range(N_EXP))

    is_token = meta[:, 4] > 0
    pos1 = jnp.where(is_token, base(meta[:, 0]) + meta[:, 2], -1).astype(jnp.int32)
    pos2 = jnp.where(is_token, base(meta[:, 1]) + meta[:, 3], -1).astype(jnp.int32)
    n_tiles = -(-2 * t // EXPERT_TILE) + N_EXP
    starts = jnp.arange(n_tiles, dtype=jnp.int32) * EXPERT_TILE
    tile_expert = jnp.minimum(jnp.sum(starts[:, None] >= ends[None, :], axis=1), N_EXP - 1).astype(jnp.int32)
    n_active = (ends[-1] // EXPERT_TILE).astype(jnp.int32).reshape(1)
    return pos1, pos2, tile_expert, n_active, n_tiles


assert D_MODEL == SUBLANES * LANES


def _store_token_tiles(ref, x):
    n = x.shape[0]
    for s in range(SUBLANES):
        ref[pl.ds(s, n, stride=SUBLANES), :] = x[:, s * LANES:(s + 1) * LANES]


def _load_token_tiles(ref, n):
    return jnp.concatenate([ref[pl.ds(s, n, stride=SUBLANES), :] for s in range(SUBLANES)], axis=1)


def _tile_rows(i):
    return pl.ds(pl.multiple_of(i * SUBLANES, SUBLANES), SUBLANES)


def _dispatch_kernel(pos1_ref, pos2_ref, h_ref, xs_in_ref, xs_ref, tok_ref, sem):
    del xs_in_ref
    n = h_ref.shape[0]
    _store_token_tiles(tok_ref, h_ref[...])

    def tile_copy(r, p):
        return pltpu.make_async_copy(tok_ref.at[_tile_rows(r)], xs_ref.at[_tile_rows(p)], sem)

    def start(r, c):
        p1 = pos1_ref[0, 0, r]

        @pl.when(p1 >= 0)
        def _():
            tile_copy(r, p1).start()
            tile_copy(r, pos2_ref[0, 0, r]).start()
        return c

    def wait(r, c):
        @pl.when(pos1_ref[0, 0, r] >= 0)
        def _():
            tile_copy(0, 0).wait()
            tile_copy(0, 0).wait()
        return c

    lax.fori_loop(0, n, start, 0)
    lax.fori_loop(0, n, wait, 0)


def _dispatch(h, pos1, pos2, n_rows):
    t = h.shape[0]
    tm = _pick_tile(t, 512)
    idx = lambda: pl.BlockSpec((1, 1, tm), lambda i: (i, 0, 0), memory_space=pltpu.SMEM)
    return pl.pallas_call(
        _dispatch_kernel, grid=(t // tm,),
        in_specs=[idx(), idx(), pl.BlockSpec((tm, D_MODEL), lambda i: (i, 0)),
                  pl.BlockSpec(memory_space=pl.ANY)],
        out_specs=pl.BlockSpec(memory_space=pl.ANY),
        out_shape=jax.ShapeDtypeStruct((n_rows * SUBLANES, LANES), F32),
        scratch_shapes=[pltpu.VMEM((tm * SUBLANES, LANES), F32), pltpu.SemaphoreType.DMA(())],
        input_output_aliases={3: 0},
        compiler_params=_params(1), name="dispatch")(
            pos1.reshape(t // tm, 1, tm), pos2.reshape(t // tm, 1, tm), h,
            jnp.zeros((n_rows * SUBLANES, LANES), F32))


def _experts_kernel(te_ref, na_ref, x_ref, wg_ref, wu_ref, wo_ref, y_ref, xb_ref, acc_ref):
    i = pl.program_id(0)
    f = pl.program_id(1)
    last = f == pl.num_programs(1) - 1
    active = i < na_ref[0]

    @pl.when(active)
    def _():
        @pl.when(f == 0)
        def _():
            acc_ref[...] = jnp.zeros_like(acc_ref)
            xb_ref[...] = _load_token_tiles(x_ref, EXPERT_TILE).astype(BF16)

        x = xb_ref[...]
        act = _silu(_dot(x, wg_ref[0])) * _dot(x, wu_ref[0])
        acc_ref[...] += _dot(act, wo_ref[0])

        @pl.when(last)
        def _():
            _store_token_tiles(y_ref, acc_ref[...])

    @pl.when(jnp.logical_not(active) & last)
    def _():
        y_ref[...] = jnp.zeros_like(y_ref)


def _experts(xs, tile_expert, n_active, wi, wo):
    n_rows = xs.shape[0] // SUBLANES
    e_ff = wo.shape[1]
    tf = _pick_tile(e_ff, 512)
    nf = e_ff // tf
    rows = pl.BlockSpec((EXPERT_TILE * SUBLANES, LANES), lambda i, f, te, na: (i, 0))
    grid_spec = pltpu.PrefetchScalarGridSpec(
        num_scalar_prefetch=2, grid=(n_rows // EXPERT_TILE, nf),
        in_specs=[rows, pl.BlockSpec((1, D_MODEL, tf), lambda i, f, te, na: (te[i], 0, f)),
                  pl.BlockSpec((1, D_MODEL, tf), lambda i, f, te, na: (te[i], 0, f + nf)),
                  pl.BlockSpec((1, tf, D_MODEL), lambda i, f, te, na: (te[i], f, 0))],
        out_specs=rows,
        scratch_shapes=[pltpu.VMEM((EXPERT_TILE, D_MODEL), BF16), pltpu.VMEM((EXPERT_TILE, D_MODEL), F32)])
    return pl.pallas_call(
        _experts_kernel, grid_spec=grid_spec, out_shape=jax.ShapeDtypeStruct(xs.shape, F32),
        compiler_params=_params(2), name="experts")(tile_expert, n_active, xs, wi, wi, wo)


def _combine_kernel(alpha, pos1_ref, pos2_ref, h_ref, gate_ref, ys_ref, lng_ref, lnb_ref, o_ref,
                    a_ref, b_ref, sems):
    n = h_ref.shape[0]

    def tile_copy(p, buf, r, which):
        return pltpu.make_async_copy(ys_ref.at[_tile_rows(p)], buf.at[_tile_rows(r)], sems.at[which])

    def start(r, c):
        tile_copy(jnp.maximum(pos1_ref[0, 0, r], 0), a_ref, r, 0).start()
        tile_copy(jnp.maximum(pos2_ref[0, 0, r], 0), b_ref, r, 1).start()
        return c

    def wait(r, c):
        tile_copy(0, a_ref, 0, 0).wait()
        tile_copy(0, b_ref, 0, 1).wait()
        return c

    lax.fori_loop(0, n, start, 0)
    lax.fori_loop(0, n, wait, 0)
    gate = gate_ref[...]
    y = gate[:, 0:1] * _load_token_tiles(a_ref, n) + gate[:, 1:2] * _load_token_tiles(b_ref, n)
    o_ref[...] = _layer_norm(alpha * h_ref[...] + y, lng_ref[...], lnb_ref[...])


def _combine(alpha, h, gates, ys, pos1, pos2, ln_g, ln_b):
    t = h.shape[0]
    tm = _pick_tile(t, 512)
    idx = lambda: pl.BlockSpec((1, 1, tm), lambda i: (i, 0, 0), memory_space=pltpu.SMEM)
    rows = pl.BlockSpec((tm, D_MODEL), lambda i: (i, 0))
    vec = pl.BlockSpec((1, D_MODEL), lambda i: (0, 0))
    return pl.pallas_call(
        functools.partial(_combine_kernel, alpha), grid=(t // tm,),
        in_specs=[idx(), idx(), rows, pl.BlockSpec((tm, LANES), lambda i: (i, 0)),
                  pl.BlockSpec(memory_space=pl.ANY), vec, vec],
        out_specs=rows, out_shape=jax.ShapeDtypeStruct((t, D_MODEL), F32),
        scratch_shapes=[pltpu.VMEM((tm * SUBLANES, LANES), F32), pltpu.VMEM((tm * SUBLANES, LANES), F32),
                        pltpu.SemaphoreType.DMA((2,))],
        compiler_params=_params(1), name="combine")(
            pos1.reshape(t // tm, 1, tm), pos2.reshape(t // tm, 1, tm), h, gates, ys,
            ln_g.reshape(1, -1), ln_b.reshape(1, -1))


def _moe(alpha, h, router, wi, wo, ln_g, ln_b, pad_starts):
    t = h.shape[0]
    meta, gates, counts = _route(h, router, pad_starts)
    pos1, pos2, tile_expert, n_active, n_tiles = _dispatch_plan(meta, counts, t)
    xs = _dispatch(h, pos1, pos2, n_tiles * EXPERT_TILE)
    ys = _experts(xs, tile_expert, n_active, wi, wo)
    return _combine(alpha, h, gates, ys, pos1, pos2, ln_g, ln_b)


def kernel(x_prompt, x_sample, meta, e_w_in, e_conv_w, e_conv_b, e_lru_wa, e_lru_ba, e_lru_wx, e_lru_bx,
           e_lru_lambda, e_w_out, e_ffn_wi, e_ffn_wo, o_w_in, o_hg_lower, o_ml_bi, o_ml_bf, o_w_out, o_router,
           o_exp_wi, o_exp_wo, ln_g, ln_b):
    groups = (x_prompt, x_sample)
    depth = ln_g.shape[0]
    alpha = (2.0 * depth) ** 0.25
    seq_shapes = [(x.shape[0], x.shape[1]) for x in groups]
    for _, l in seq_shapes:
        assert l % CH == 0
    cpos_np, clast_np = _chunk_tables(seq_shapes)
    cpos, clast = jnp.asarray(cpos_np), jnp.asarray(clast_np)
    max_rows = max(l for _, l in seq_shapes) + CH
    pad_starts = tuple(int(i) * CH for i in np.flatnonzero(cpos_np == 0))

    parts = []
    for x in groups:
        bsz = x.shape[0]
        head = jnp.concatenate([jnp.zeros((PAD_ROWS, D_MODEL), x.dtype), meta.astype(x.dtype)], axis=0)
        full = jnp.concatenate([jnp.broadcast_to(head[None], (bsz, CH, D_MODEL)), x], axis=1)
        parts.append(full.reshape(-1, D_MODEL))
    h = jnp.concatenate(parts, axis=0)

    n_odd_cols = 9 * GROUP_W
    for layer in range(depth):
        p = layer // 2
        if layer % 2 == 0:
            z = _project(h, e_w_in[p].astype(BF16))
            ret_f, ret_b = _retention(z, cpos, clast, max_rows)
            lru_f, lru_b = _rglru(z, 4, cpos, clast, e_conv_w[p], e_conv_b[p], e_lru_wa[p], e_lru_ba[p],
                                  e_lru_wx[p], e_lru_bx[p], e_lru_lambda[p])
            h = _mix_out(alpha, True, ret_f, ret_b, lru_f, lru_b, z, 3, 5, h, e_w_out[p].astype(BF16),
                         ln_g[layer, 0], ln_b[layer, 0])
            h = _ffn(alpha, h, e_ffn_wi[p].astype(BF16), e_ffn_wo[p].astype(BF16),
                     ln_g[layer, 1], ln_b[layer, 1])
        else:
            w_in = o_w_in[p]
            n_gate = 2 * N_HEADS
            gate_pad = ((0, 0), (0, LANES - n_gate))
            w_gates = jnp.concatenate([jnp.pad(w_in[:, n_odd_cols:n_odd_cols + n_gate], gate_pad),
                                       jnp.pad(w_in[:, n_odd_cols + n_gate:], gate_pad)], axis=1)
            z, gates = _project(h, w_in[:, :n_odd_cols].astype(BF16), w_gates)
            hg_f, hg_b = _hgrn2(z, cpos, clast, o_hg_lower, layer)
            ml_f, ml_b = _mlstm(z, gates, cpos, clast, o_ml_bi[p], o_ml_bf[p])
            h = _mix_out(alpha, False, hg_f, hg_b, ml_f, ml_b, z, 4, 8, h, o_w_out[p].astype(BF16),
                         ln_g[layer, 0], ln_b[layer, 0])
            h = _moe(alpha, h, o_router[p], o_exp_wi[p].astype(BF16), o_exp_wo[p].astype(BF16),
                     ln_g[layer, 1], ln_b[layer, 1], pad_starts)

    outs = []
    row = 0
    for b, l in seq_shapes:
        n = b * (l + CH)
        outs.append(h[row:row + n].reshape(b, l + CH, D_MODEL)[:, CH:])
        row += n
    return tuple(outs)
```

```python
import functools
import math

import numpy as np
import jax
import jax.numpy as jnp
from jax import lax
from jax.experimental import pallas as pl
from jax.experimental.pallas import tpu as pltpu

F32 = jnp.float32
BF16 = jnp.bfloat16
HIGHEST = lax.Precision.HIGHEST

D_MODEL = 1024
GROUP_W = D_MODEL // 2
N_HEADS = 4
HEAD_D = GROUP_W // N_HEADS
N_META = 16
ROPE_BASE = 10000.0
LRU_BLOCKS = 8
LRU_BW = GROUP_W // LRU_BLOCKS
LRU_C = 8.0
N_EXP = 8
EPS = 1e-5

LANES = 128
SUBLANES = 8
CH = 128
PAD_ROWS = CH - N_META
SUB = SUBLANES
GRP = 4 * SUB
NEG_BIG = -1e30
LOG2_E = math.log2(math.e)
VMEM_LIMIT = 56 * 1024 * 1024


def _dot(a, b):
    return jnp.dot(a.astype(BF16), b.astype(BF16), preferred_element_type=F32)


def _dot_nt(a, b):
    return lax.dot_general(a.astype(BF16), b.astype(BF16), (((1,), (1,)), ((), ())),
                           preferred_element_type=F32)


def _dot_tn(a, b):
    return _dot(a.T, b)


def _dot_f32(a, b):
    return jnp.dot(a, b, preferred_element_type=F32, precision=HIGHEST)


def _bf16_terms(x, n_terms):
    terms = []
    for _ in range(n_terms):
        t = x.astype(BF16)
        terms.append(t)
        x = x - t.astype(F32)
    return terms


PREFIX_TERMS = 3
SPREAD_TERMS = 2


def _prefix_dot(op_tiled, x):
    return jnp.dot(op_tiled, jnp.concatenate(_bf16_terms(x, PREFIX_TERMS), axis=0),
                   preferred_element_type=F32)


def _spread_dot(x, op_tiled):
    return jnp.dot(jnp.concatenate(_bf16_terms(x, SPREAD_TERMS), axis=1), op_tiled,
                   preferred_element_type=F32)


def _sigmoid(x):
    return 1.0 / (1.0 + jnp.exp(-x))


def _silu(x):
    return x * _sigmoid(x)


def _gelu_tanh(x):
    return 0.5 * x * (1.0 + jnp.tanh(math.sqrt(2.0 / math.pi) * (x + 0.044715 * (x * x * x))))


def _softplus(x):
    return jnp.maximum(x, 0.0) + jnp.log(1.0 + jnp.exp(-jnp.abs(x)))


def _pick_tile(total, target):
    best = LANES
    for t in range(LANES, min(total, target) + 1, LANES):
        if total % t == 0:
            best = t
    return best


def _params(n_axes, sem="arbitrary"):
    return pltpu.CompilerParams(dimension_semantics=(sem,) * n_axes, vmem_limit_bytes=VMEM_LIMIT)


def _proj_kernel(x_ref, w_ref, o_ref):
    o_ref[...] = _dot(x_ref[...], w_ref[...])


def _proj_gates_kernel(x_ref, w_ref, wg_ref, o_ref, g_ref):
    g_ref[...] = _dot_f32(x_ref[...], wg_ref[...])
    o_ref[...] = _dot(x_ref[...], w_ref[...])


PROJ_OUT_BLOCK_BYTES = 8 * 1024 * 1024


def _project(x, w, w_gates=None):
    t, k = x.shape
    n = w.shape[1]
    tm = _pick_tile(t, PROJ_OUT_BLOCK_BYTES // (4 * n))
    x_spec = pl.BlockSpec((tm, k), lambda i: (i, 0))
    w_spec = pl.BlockSpec((k, n), lambda i: (0, 0))
    o_spec = pl.BlockSpec((tm, n), lambda i: (i, 0))
    if w_gates is None:
        return pl.pallas_call(
            _proj_kernel, grid=(t // tm,), in_specs=[x_spec, w_spec], out_specs=o_spec,
            out_shape=jax.ShapeDtypeStruct((t, n), F32), compiler_params=_params(1),
            name="proj")(x, w)
    ng = w_gates.shape[1]
    return pl.pallas_call(
        _proj_gates_kernel, grid=(t // tm,),
        in_specs=[x_spec, w_spec, pl.BlockSpec((k, ng), lambda i: (0, 0))],
        out_specs=[o_spec, pl.BlockSpec((tm, ng), lambda i: (i, 0))],
        out_shape=[jax.ShapeDtypeStruct((t, n), F32), jax.ShapeDtypeStruct((t, ng), F32)],
        compiler_params=_params(1), name="proj_gates")(x, w, w_gates)


def _chunk_tables(seq_shapes):
    cpos, clast = [], []
    for b, l in seq_shapes:
        n = l // CH + 1
        for _ in range(b):
            cpos += list(range(n))
            clast += [0] * (n - 1) + [1]
    return np.asarray(cpos, np.int32), np.asarray(clast, np.int32)


def _fwd_map(col):
    return lambda g, cp, cl: (g, col)


def _bwd_map(n_chunks, col):
    return lambda g, cp, cl: (n_chunks - 1 - g, col)


def _walk_kernel(bodies, n_inputs, n_scratch, cpos_ref, clast_ref, *refs):
    refs = list(refs)
    ins = [[refs.pop(0) for _ in range(k)] for k in n_inputs]
    outs = [[refs.pop(0) for _ in range(2)] for _ in bodies]

    @pl.when(pl.program_id(0) == 0)
    def _():
        for r in refs:
            r[...] = jnp.zeros_like(r)

    for body, i, o, k in zip(bodies, ins, outs, n_scratch):
        body(cpos_ref, clast_ref, *i, *o, *[refs.pop(0) for _ in range(k)])


def _chunk_walk(parts, cpos, clast, t, name):
    n = t // CH
    blk = (CH, GROUP_W)
    bodies = [p[0] for p in parts]
    grid_spec = pltpu.PrefetchScalarGridSpec(
        num_scalar_prefetch=2, grid=(n,), in_specs=[s for p in parts for s in p[1]],
        out_specs=[pl.BlockSpec(blk, _fwd_map(0)), pl.BlockSpec(blk, _bwd_map(n, 0))] * len(parts),
        scratch_shapes=[s for p in parts for s in p[3]])
    outs = pl.pallas_call(
        functools.partial(_walk_kernel, bodies, [len(p[1]) for p in parts], [len(p[3]) for p in parts]),
        grid_spec=grid_spec, out_shape=[jax.ShapeDtypeStruct((t, GROUP_W), F32)] * (2 * len(parts)),
        compiler_params=_params(1), name=name)(cpos, clast, *[a for p in parts for a in p[2]])
    return [outs[2 * i:2 * i + 2] for i in range(len(parts))]


def _row_valid(first):
    row = lax.broadcasted_iota(jnp.int32, (CH, 1), 0)
    return row >= jnp.where(first, PAD_ROWS, 0)


def _ret_log_gamma():
    return np.log1p(-np.exp2(-5.0 - np.arange(N_HEADS, dtype=np.float64)))


def _ret_consts():
    lg = _ret_log_gamma()
    pos = np.arange(CH, dtype=np.float64)
    rel = pos[:, None] - pos[None, :]
    dmat = np.zeros((2, N_HEADS, CH, CH), np.float64)
    rows = np.zeros((4, CH, GROUP_W), np.float64)
    for h in range(N_HEADS):
        dmat[0, h] = np.where(rel >= 0, np.exp(np.maximum(rel, 0.0) * lg[h]), 0.0)
        dmat[1, h] = np.where(rel < 0, np.exp(np.maximum(-rel, 0.0) * lg[h]), 0.0)
        sl = slice(h * HEAD_D, (h + 1) * HEAD_D)
        rows[0, :, sl] = np.exp((pos + 1.0) * lg[h])[:, None]
        rows[1, :, sl] = np.exp((CH - 1.0 - pos) * lg[h])[:, None]
        rows[2, :, sl] = np.exp((CH - pos) * lg[h])[:, None]
        rows[3, :, sl] = np.exp(pos * lg[h])[:, None]
    return jnp.asarray(dmat, F32), jnp.asarray(rows, F32)


def _ret_direction(q_ref, k_ref, v_ref, cos_ref, sin_ref, dmat_ref, rin_ref, rout_ref, s_ref, o_ref,
                   reset, first):
    valid = _row_valid(first)
    cos = cos_ref[...]
    sin = sin_ref[...]
    chunk_decay = np.exp(CH * _ret_log_gamma())
    for h in range(N_HEADS):
        sl = slice(h * HEAD_D, (h + 1) * HEAD_D)
        q = q_ref[:, sl]
        k = k_ref[:, sl]
        v = v_ref[:, sl]
        q = q * cos + pltpu.roll(q, HEAD_D // 2, 1) * sin
        k = (k * cos + pltpu.roll(k, HEAD_D // 2, 1) * sin) * (HEAD_D ** -0.5)
        k = jnp.where(valid, k, 0.0)
        scores = _dot_nt(q, k) * dmat_ref[h]
        state = jnp.where(reset, 0.0, s_ref[h])
        o_ref[:, sl] = _dot(jnp.concatenate([scores, q * rin_ref[:, sl]], axis=1),
                            jnp.concatenate([v, state], axis=0))
        s_ref[h] = float(chunk_decay[h]) * state + _dot_tn(k * rout_ref[:, sl], v)


def _ret_kernel(cpos_ref, clast_ref, qf, kf, vf, cosf, sinf, qb, kb, vb, cosb, sinb, dmat_ref, rows_ref,
                of_ref, ob_ref, sf_ref, sb_ref):
    g = pl.program_id(0)
    gb = pl.num_programs(0) - 1 - g
    _ret_direction(qf, kf, vf, cosf, sinf, dmat_ref.at[0], rows_ref.at[0], rows_ref.at[1], sf_ref, of_ref,
                   cpos_ref[g] == 0, cpos_ref[g] == 0)
    _ret_direction(qb, kb, vb, cosb, sinb, dmat_ref.at[1], rows_ref.at[2], rows_ref.at[3], sb_ref, ob_ref,
                   clast_ref[gb] == 1, cpos_ref[gb] == 0)


def _rope_tables(n_rows):
    inv = ROPE_BASE ** (-jnp.arange(0, HEAD_D, 2, dtype=jnp.float32) / HEAD_D)
    pos = jnp.maximum(jnp.arange(n_rows, dtype=jnp.float32) - PAD_ROWS, 0.0)
    ang = pos[:, None] * inv[None, :]
    cos, sin = jnp.cos(ang), jnp.sin(ang)
    return jnp.concatenate([cos, cos], axis=1), jnp.concatenate([-sin, sin], axis=1)


def _retention(z, cpos, clast, max_rows):
    t = z.shape[0]
    n = t // CH
    cos2, sin2 = _rope_tables(max_rows)
    dmat, rows = _ret_consts()
    blk = (CH, GROUP_W)
    in_specs = []
    for mk, pm in ((_fwd_map, lambda g, cp, cl: (cp[g], 0)),
                   (functools.partial(_bwd_map, n), lambda g, cp, cl: (cp[n - 1 - g], 0))):
        in_specs += [pl.BlockSpec(blk, mk(0)), pl.BlockSpec(blk, mk(1)), pl.BlockSpec(blk, mk(2)),
                     pl.BlockSpec((CH, HEAD_D), pm), pl.BlockSpec((CH, HEAD_D), pm)]
    in_specs += [pl.BlockSpec(dmat.shape, lambda g, cp, cl: (0, 0, 0, 0)),
                 pl.BlockSpec(rows.shape, lambda g, cp, cl: (0, 0, 0))]
    return (_ret_kernel, in_specs, [z, z, z, cos2, sin2, z, z, z, cos2, sin2, dmat, rows],
            [pltpu.VMEM((N_HEADS, HEAD_D, HEAD_D), F32)] * 2)


HALO = SUBLANES


def _shift_rows(x, s, fill, reverse):
    row = lax.broadcasted_iota(jnp.int32, (CH, 1), 0)
    if reverse:
        return jnp.where(row < CH - s, pltpu.roll(x, CH - s, 0), fill)
    return jnp.where(row >= s, pltpu.roll(x, s, 0), fill)


def _lru_direction(x_ref, prev_ref, next_ref, convw_ref, convb_ref, wg_ref, bias_ref, lam_ref, ext_ref,
                   carry_ref, o_ref, reset, first, last, reverse):
    valid = _row_valid(first)
    ext_ref[HALO:HALO + CH, :] = jnp.where(valid, x_ref[...], 0.0)
    ext_ref[0:HALO, :] = jnp.where(first, 0.0, prev_ref[...])
    ext_ref[HALO + CH:, :] = jnp.where(last, 0.0, next_ref[...])
    xc = convb_ref[...] + ext_ref[HALO - 2:HALO - 2 + CH, :] * convw_ref[0:1, :]
    for tap in range(1, 4):
        xc = xc + ext_ref[HALO - 2 + tap:HALO - 2 + tap + CH, :] * convw_ref[tap:tap + 1, :]

    log_sig_lam = -_softplus(-lam_ref[...])
    parts_a, parts_u = [], []
    for grp in range(GROUP_W // LANES):
        sl = slice(grp * LANES, (grp + 1) * LANES)
        xg = xc[:, sl]
        pre = _dot(xg, wg_ref[grp])
        r = _sigmoid(pre[:, :LANES] + bias_ref[0:1, sl])
        i = _sigmoid(pre[:, LANES:] + bias_ref[1:2, sl])
        a = jnp.exp(LRU_C * log_sig_lam[:, sl] * r)
        u = jnp.sqrt(1.0 - a * a) * (i * xg)
        parts_a.append(a)
        parts_u.append(jnp.where(valid, u, 0.0))
    a = jnp.concatenate(parts_a, axis=1)
    u = jnp.concatenate(parts_u, axis=1)

    n_blk = CH // SUBLANES
    a = a.reshape(n_blk, SUBLANES, GROUP_W)
    u = u.reshape(n_blk, SUBLANES, GROUP_W)
    row_in_blk = lax.broadcasted_iota(jnp.int32, (1, SUBLANES, 1), 1)
    s = 1
    while s < SUBLANES:
        keep = (row_in_blk < SUBLANES - s) if reverse else (row_in_blk >= s)
        shift = SUBLANES - s if reverse else s
        u = u + a * jnp.where(keep, pltpu.roll(u, shift, 1), 0.0)
        a = a * jnp.where(keep, pltpu.roll(a, shift, 1), 1.0)
        s *= 2
    edge = 0 if reverse else SUBLANES - 1
    carry = jnp.where(reset, 0.0, carry_ref[...])
    blocks = [None] * n_blk
    for blk in (reversed(range(n_blk)) if reverse else range(n_blk)):
        blocks[blk] = u[blk] + a[blk] * carry
        carry = blocks[blk][edge:edge + 1, :]
    o_ref[...] = jnp.concatenate(blocks, axis=0)
    carry_ref[...] = carry


def _lru_kernel(cpos_ref, clast_ref, xf, pf, nf, xb, pb, nb, convw_ref, convb_ref, wg_ref, bias_ref, lam_ref,
                of_ref, ob_ref, extf_ref, extb_ref, cf_ref, cb_ref):
    g = pl.program_id(0)
    gb = pl.num_programs(0) - 1 - g
    _lru_direction(xf, pf, nf, convw_ref, convb_ref, wg_ref.at[0], bias_ref.at[0], lam_ref.at[0], extf_ref,
                   cf_ref, of_ref, cpos_ref[g] == 0, cpos_ref[g] == 0, clast_ref[g] == 1, False)
    _lru_direction(xb, pb, nb, convw_ref, convb_ref, wg_ref.at[1], bias_ref.at[1], lam_ref.at[1], extb_ref,
                   cb_ref, ob_ref, clast_ref[gb] == 1, cpos_ref[gb] == 0, clast_ref[gb] == 1, True)


def _lru_gate_weights(wa, wx):
    per = LANES // LRU_BW

    def block_diag(w):
        w = w.reshape(2, LRU_BLOCKS // per, per, LRU_BW, LRU_BW)
        eye = jnp.eye(per, dtype=w.dtype)
        return jnp.einsum("dgpij,pq->dgpiqj", w, eye).reshape(2, LRU_BLOCKS // per, LANES, LANES)

    return jnp.concatenate([block_diag(wa), block_diag(wx)], axis=-1).astype(BF16)


def _rglru(z, col, cpos, clast, conv_w, conv_b, wa, ba, wx, bx, lam):
    t = z.shape[0]
    n = t // CH
    per = CH // HALO
    n_halo = t // HALO
    blk = (CH, GROUP_W)
    hblk = (HALO, GROUP_W)
    wg = _lru_gate_weights(wa, wx)
    bias = jnp.stack([ba, bx], axis=1)
    lam = lam.reshape(2, 1, GROUP_W)

    def prev_f(g, cp, cl):
        return (jnp.maximum(g * per - 1, 0), col)

    def next_f(g, cp, cl):
        return (jnp.minimum((g + 1) * per, n_halo - 1), col)

    def prev_b(g, cp, cl):
        return (jnp.maximum((n - 1 - g) * per - 1, 0), col)

    def next_b(g, cp, cl):
        return (jnp.minimum((n - g) * per, n_halo - 1), col)

    full = lambda a: pl.BlockSpec(a.shape, lambda g, cp, cl: (0,) * a.ndim)
    conv_b2 = conv_b.reshape(1, GROUP_W)
    in_specs = [pl.BlockSpec(blk, _fwd_map(col)), pl.BlockSpec(hblk, prev_f), pl.BlockSpec(hblk, next_f),
                pl.BlockSpec(blk, _bwd_map(n, col)), pl.BlockSpec(hblk, prev_b), pl.BlockSpec(hblk, next_b),
                full(conv_w), full(conv_b2), full(wg), full(bias), full(lam)]
    return (_lru_kernel, in_specs, [z, z, z, z, z, z, conv_w, conv_b2, wg, bias, lam],
            [pltpu.VMEM((CH + 2 * HALO, GROUP_W), F32)] * 2 + [pltpu.VMEM((1, GROUP_W), F32)] * 2)


def _tri_consts():
    pos = np.arange(CH)
    lower = (pos[:, None] >= pos[None, :]).astype(np.float32)
    ops = np.stack([lower, lower.T])
    return jnp.asarray(np.tile(ops, (1, 1, PREFIX_TERMS)), BF16)


def _hg_masks():
    pos = np.arange(CH)
    same_blk = pos[:, None] // SUB == pos[None, :] // SUB
    same_grp = pos[:, None] // GRP == pos[None, :] // GRP
    return jnp.asarray(np.stack([same_blk, same_grp]).astype(np.float32))


def _hg_direction(q_ref, f_ref, v_ref, lb, cum_ref, mask_ref, st_ref, o_ref, reset, first, reverse):
    valid = _row_valid(first)
    sub_i =lax.broadcasted_iota(jnp.int32, (SUB, HEAD_D), 0)
    lane_j = lax.broadcasted_iota(jnp.int32, (SUB, HEAD_D), 1) & (SUB - 1)
    causal = (sub_i <= lane_j) if reverse else (sub_i >= lane_j)
    pick = [causal & (lane_j == j) for j in range(SUB)]
    cum_op = cum_ref[...]
    same_blk = mask_ref[0]
    same_grp = mask_ref[1]
    zero = jnp.zeros((SUB, HEAD_D), F32)
    n_blk = CH // SUB
    per = GRP // SUB
    n_grp = CH // GRP

    def edge(unit, idx):
        return unit * idx if reverse else unit * (idx + 1) - 1

    def split_product(b, qs, ks, bs, pieces):
        q_slabs, k_slabs = [], []
        for ref, q_active, k_active in pieces:
            qp, kp = [], []
            for i in range(n_blk):
                r = ref(i)
                qp.append(qs[i] * jnp.exp2(bs[i] - b[r:r + 1, :]) if q_active(i) else zero)
                kp.append(ks[i] * jnp.exp2(b[r:r + 1, :] - bs[i]) if k_active(i) else zero)
            q_slabs.append(jnp.concatenate(qp, axis=0))
            k_slabs.append(jnp.concatenate(kp, axis=0))
        return _dot_nt(jnp.concatenate(q_slabs, axis=1), jnp.concatenate(k_slabs, axis=1))

    level1 = []
    for c in (range(1, per) if reverse else range(per - 1)):
        level1.append((lambda i, c=c: edge(SUB, (i // per) * per + c),
                       (lambda i, c=c: i % per < c) if reverse else (lambda i, c=c: i % per > c),
                       lambda i, c=c: i % per == c))
    level2 = []
    for gc in (range(1, n_grp) if reverse else range(n_grp - 1)):
        level2.append((lambda i, gc=gc: edge(GRP, gc),
                       (lambda i, gc=gc: i // per < gc) if reverse else (lambda i, gc=gc: i // per > gc),
                       lambda i, gc=gc: i // per == gc))

    for h in range(N_HEADS):
        sl = slice(h * HEAD_D, (h + 1) * HEAD_D)
        lbh = lb[:, sl]
        q = _silu(q_ref[:, sl])
        f = lbh + (1.0 - lbh) * _sigmoid(f_ref[:, sl])
        k = jnp.where(valid, 1.0 - f, 0.0)
        v = v_ref[:, sl]
        b = _prefix_dot(cum_op, jnp.log(f)) * LOG2_E
        qs = [q[i * SUB:(i + 1) * SUB] for i in range(n_blk)]
        ks = [k[i * SUB:(i + 1) * SUB] for i in range(n_blk)]
        bs = [b[i * SUB:(i + 1) * SUB] for i in range(n_blk)]

        tiles = [qs[blk] * jnp.exp2(jnp.minimum(bs[blk] - bs[blk][j:j + 1, :], 0.0))
                 for blk in range(n_blk) for j in range(SUB)]
        pair = _dot_nt(jnp.concatenate(tiles, axis=0), k)
        rows = []
        for blk in range(n_blk):
            base = blk * SUB * SUB
            a_blk = jnp.where(pick[0], pair[base:base + SUB], 0.0)
            for j in range(1, SUB):
                a_blk = a_blk + jnp.where(pick[j], pair[base + j * SUB:base + (j + 1) * SUB], 0.0)
            rows.append(a_blk)
        scores = jnp.concatenate(rows, axis=0) * same_blk
        scores = scores + split_product(b, qs, ks, bs, level1) * same_grp
        scores = scores + split_product(b, qs, ks, bs, level2)

        state = jnp.where(reset, 0.0, st_ref[h])
        o_ref[:, sl] = _dot(scores, v) + _dot_nt(q * jnp.exp2(b), state)
        b_tot = b[0:1, :] if reverse else b[CH - 1:CH, :]
        st_ref[h] = state * jnp.exp2(b_tot) + _dot_tn(v, k * jnp.exp2(b_tot - b))


def _hg_lower_bound(lower_ref, layer):
    low = lower_ref[...]
    e = jnp.exp(low - jnp.max(low, axis=0, keepdims=True))
    soft = e / jnp.sum(e, axis=0, keepdims=True)
    lb = jnp.zeros((1, GROUP_W), F32)
    for l in range(1, layer + 1):
        lb = lb + soft[l:l + 1, :]
    return lb


def _hg_kernel(layer, cpos_ref, clast_ref, qf, ff, vf, qb, fb, vb, lower_ref, cum_ref, mask_ref,
               of_ref, ob_ref, sf_ref, sb_ref):
    g = pl.program_id(0)
    gb = pl.num_programs(0) - 1 - g
    lb = _hg_lower_bound(lower_ref, layer)
    _hg_direction(qf, ff, vf, lb, cum_ref.at[0], mask_ref, sf_ref, of_ref,
                  cpos_ref[g] == 0, cpos_ref[g] == 0, False)
    _hg_direction(qb, fb, vb, lb, cum_ref.at[1], mask_ref, sb_ref, ob_ref,
                  clast_ref[gb] == 1, cpos_ref[gb] == 0, True)


def _hgrn2(z, cpos, clast, lower, layer):
    t = z.shape[0]
    n = t // CH
    blk = (CH, GROUP_W)
    cum = _tri_consts()
    masks = _hg_masks()
    full = lambda a: pl.BlockSpec(a.shape, lambda g, cp, cl: (0,) * a.ndim)
    in_specs = [pl.BlockSpec(blk, _fwd_map(0)), pl.BlockSpec(blk, _fwd_map(1)), pl.BlockSpec(blk, _fwd_map(3)),
                pl.BlockSpec(blk, _bwd_map(n, 0)), pl.BlockSpec(blk, _bwd_map(n, 2)),
                pl.BlockSpec(blk, _bwd_map(n, 3)), full(lower), full(cum), full(masks)]
    return (functools.partial(_hg_kernel, layer), in_specs, [z, z, z, z, z, z, lower, cum, masks],
            [pltpu.VMEM((N_HEADS, HEAD_D, HEAD_D), F32)] * 2)


def _log_sigmoid(x):
    return jnp.minimum(x, 0.0) - jnp.log(1.0 + jnp.exp(-jnp.abs(x)))


def _ml_direction(q_ref, k_ref, v_ref, gi_ref, gf_ref, bias_ref, cum_op, spread_ref, s_ref, m_ref, o_ref,
                  reset, first, d, reverse):
    valid = _row_valid(first)
    cum = _prefix_dot(cum_op, _log_sigmoid(gf_ref[...] + bias_ref[1]))
    a = jnp.where(valid, gi_ref[...] + bias_ref[0] - cum, NEG_BIG)
    run = a
    s = 1
    while s < CH:
        run = jnp.maximum(run, _shift_rows(run, s, NEG_BIG, reverse))
        s *= 2
    m_st = jnp.where(reset, NEG_BIG, m_ref[...])
    mx = jnp.maximum(m_st, run)
    edge = 0 if reverse else CH - 1
    mx_last = mx[edge:edge + 1, :]
    m_ref[...] = cum[edge:edge + 1, :] + mx_last
    decay = jnp.exp(m_st - mx_last)

    stacked = jnp.concatenate([mx, m_st - mx, -(cum + mx), a - mx_last], axis=0) * LOG2_E
    spread_b = _spread_dot(stacked, spread_ref[...])
    mx_b, inter_b, floor_b, end_b = (spread_b[i * CH:(i + 1) * CH] for i in range(4))
    a_t = (a * LOG2_E).T

    row = lax.broadcasted_iota(jnp.int32, (CH, CH), 0)
    col = lax.broadcasted_iota(jnp.int32, (CH, CH), 1)
    causal = (col >= row) if reverse else (col <= row)
    ones = jnp.ones((CH, HEAD_D), F32)
    for h in range(N_HEADS):
        sl = slice(h * HEAD_D, (h + 1) * HEAD_D)
        x = d * N_HEADS + h
        q = q_ref[:, sl]
        k = jnp.where(valid, k_ref[:, sl] * (HEAD_D ** -0.5), 0.0)
        v_ext = jnp.concatenate([v_ref[:, sl], ones], axis=1)
        w = jnp.where(causal, jnp.exp2(jnp.minimum(a_t[x:x + 1, :] - mx_b[:, sl], 0.0)), 0.0)
        qk = _dot_nt(q, k) * w
        s_inter = jnp.exp2(jnp.minimum(inter_b[:, sl], 0.0))
        state = jnp.where(reset, 0.0, s_ref[h])
        ext = _dot(jnp.concatenate([qk, q * s_inter], axis=1), jnp.concatenate([v_ext, state], axis=0))
        o_ref[:, sl] = ext[:, :HEAD_D] / jnp.maximum(jnp.abs(ext[:, HEAD_D:]), jnp.exp2(floor_b[:, sl]))
        kw = k * jnp.exp2(jnp.minimum(end_b[:, sl], 0.0))
        s_ref[h] = decay[:, x:x + 1] * state + _dot_tn(kw, v_ext)


def _ml_kernel(cpos_ref, clast_ref, qf, kf, vf, gif, gff, qb, kb, vb, gib, gfb, bias_ref, cum_ref, spread_ref,
               of_ref, ob_ref, sfs, mfs, sbs, mbs):
    g = pl.program_id(0)
    gb = pl.num_programs(0) - 1 - g
    _ml_direction(qf, kf, vf, gif, gff, bias_ref, cum_ref[0], spread_ref.at[0], sfs, mfs, of_ref,
                  cpos_ref[g] == 0, cpos_ref[g] == 0, 0, False)
    _ml_direction(qb, kb, vb, gib, gfb, bias_ref, cum_ref[1], spread_ref.at[1], sbs, mbs, ob_ref,
                  clast_ref[gb] == 1, cpos_ref[gb] == 0, 1, True)


def _ml_spread():
    spread = np.zeros((2, LANES, GROUP_W), np.float32)
    for d in range(2):
        for h in range(N_HEADS):
            spread[d, d * N_HEADS + h, h * HEAD_D:(h + 1) * HEAD_D] = 1.0
    return jnp.asarray(np.tile(spread, (1, SPREAD_TERMS, 1)), BF16)


def _mlstm(z, gates, cpos, clast, ml_bi, ml_bf):
    t = z.shape[0]
    n = t // CH
    blk = (CH, GROUP_W)
    gblk = (CH, LANES)
    cum = _tri_consts()
    spread = _ml_spread()
    pad = jnp.zeros((LANES - 2 * N_HEADS,), F32)
    bias = jnp.stack([jnp.concatenate([ml_bi.reshape(-1), pad]),
                      jnp.concatenate([ml_bf.reshape(-1), pad])]).reshape(2, 1, LANES)
    full = lambda a: pl.BlockSpec(a.shape, lambda g, cp, cl: (0,) * a.ndim)
    in_specs = [pl.BlockSpec(blk, _fwd_map(5)), pl.BlockSpec(blk, _fwd_map(6)), pl.BlockSpec(blk, _fwd_map(7)),
                pl.BlockSpec(gblk, _fwd_map(0)), pl.BlockSpec(gblk, _fwd_map(1)),
                pl.BlockSpec(blk, _bwd_map(n, 5)), pl.BlockSpec(blk, _bwd_map(n, 6)),
                pl.BlockSpec(blk, _bwd_map(n, 7)),
                pl.BlockSpec(gblk, _bwd_map(n, 0)), pl.BlockSpec(gblk, _bwd_map(n, 1)),
                full(bias), full(cum), full(spread)]
    state = [pltpu.VMEM((N_HEADS, HEAD_D, 2 * HEAD_D), F32), pltpu.VMEM((1, LANES), F32)]
    return (_ml_kernel, in_specs, [z, z, z, gates, gates, z, z, z, gates, gates, bias, cum, spread], state * 2)


def _head_norm(x, center):
    outs = []
    for h in range(N_HEADS):
        xh = x[:, h * HEAD_D:(h + 1) * HEAD_D]
        if center:
            xh = xh - jnp.mean(xh, axis=1, keepdims=True)
        outs.append(xh * lax.rsqrt(jnp.mean(xh * xh, axis=1, keepdims=True) + EPS))
    return jnp.concatenate(outs, axis=1)


def _layer_norm(x, g, b):
    xc = x - jnp.mean(x, axis=1, keepdims=True)
    return xc * lax.rsqrt(jnp.mean(xc * xc, axis=1, keepdims=True) + EPS) * g + b


def _mix_out_kernel(alpha, even, af, ab, ag, bf, bb, bg, h_ref, w_ref, lng_ref, lnb_ref, o_ref):
    a = af[...] + ab[...]
    b = bf[...] + bb[...]
    if even:
        a = _head_norm(a, False) * _silu(ag[...])
        b = b * _gelu_tanh(bg[...])
    else:
        a = _head_norm(a, False) * _silu(ag[...])
        b = _head_norm(b, True) * _sigmoid(bg[...])
    mix = _dot(a, w_ref[0:GROUP_W, :]) + _dot(b, w_ref[GROUP_W:, :])
    o_ref[...] = _layer_norm(alpha * h_ref[...] + mix, lng_ref[...], lnb_ref[...])


def _mix_out(alpha, even, af, ab, bf, bb, z, col_a, col_b, h, w_out, ln_g, ln_b):
    t = h.shape[0]
    tm = _pick_tile(t, 512)
    half = lambda c: pl.BlockSpec((tm, GROUP_W), lambda i: (i, c))
    rows = pl.BlockSpec((tm, D_MODEL), lambda i: (i, 0))
    vec = pl.BlockSpec((1, D_MODEL), lambda i: (0, 0))
    return pl.pallas_call(
        functools.partial(_mix_out_kernel, alpha, even), grid=(t // tm,),
        in_specs=[half(0), half(0), half(col_a), half(0), half(0), half(col_b), rows,
                  pl.BlockSpec((D_MODEL, D_MODEL), lambda i: (0, 0)), vec, vec],
        out_specs=rows, out_shape=jax.ShapeDtypeStruct((t, D_MODEL), F32),
        compiler_params=_params(1), name="mix_out")(
            af, ab, z, bf, bb, z, h, w_out, ln_g.reshape(1, -1), ln_b.reshape(1, -1))


def _ffn_kernel(alpha, h_ref, wg_ref, wu_ref, wo_ref, lng_ref, lnb_ref, o_ref, acc_ref):
    f = pl.program_id(1)

    @pl.when(f == 0)
    def _():
        acc_ref[...] = jnp.zeros_like(acc_ref)

    x = h_ref[...].astype(BF16)
    act = _silu(_dot(x, wg_ref[...])) * _dot(x, wu_ref[...])
    acc_ref[...] += _dot(act, wo_ref[...])

    @pl.when(f == pl.num_programs(1) - 1)
    def _():
        o_ref[...] = _layer_norm(alpha * h_ref[...] + acc_ref[...], lng_ref[...], lnb_ref[...])


def _ffn(alpha, h, wi, wo, ln_g, ln_b):
    t = h.shape[0]
    d_ff = wo.shape[0]
    tm = _pick_tile(t, 1024)
    tf = _pick_tile(d_ff, 1536)
    nf = d_ff // tf
    rows = pl.BlockSpec((tm, D_MODEL), lambda i, f: (i, 0))
    vec = pl.BlockSpec((1, D_MODEL), lambda i, f: (0, 0))
    return pl.pallas_call(
        functools.partial(_ffn_kernel, alpha), grid=(t // tm, nf),
        in_specs=[rows, pl.BlockSpec((D_MODEL, tf), lambda i, f: (0, f)),
                  pl.BlockSpec((D_MODEL, tf), lambda i, f: (0, f + nf)),
                  pl.BlockSpec((tf, D_MODEL), lambda i, f: (f, 0)), vec, vec],
        out_specs=rows, out_shape=jax.ShapeDtypeStruct((t, D_MODEL), F32),
        scratch_shapes=[pltpu.VMEM((tm, D_MODEL), F32)],
        compiler_params=_params(2), name="ffn")(h, wi, wi, wo, ln_g.reshape(1, -1), ln_b.reshape(1, -1))


EXPERT_TILE = 1024
DMA_UNROLL = 8


def _route_kernel(pad_starts, h_ref, router_ref, tri_ref, meta_ref, gate_ref, cnt_ref, carry_ref):
    @pl.when(pl.program_id(0) == 0)
    def _():
        carry_ref[...] = jnp.zeros_like(carry_ref)

    tm = h_ref.shape[0]
    row = pl.program_id(0) * tm + lax.broadcasted_iota(jnp.int32, (tm, 1), 0)
    is_pad = row < 0
    for s in pad_starts:
        is_pad = is_pad | ((row >= s) & (row < s + PAD_ROWS))
    is_token = jnp.logical_not(is_pad)

    logits = _dot_f32(h_ref[...], router_ref[...])
    lane = lax.broadcasted_iota(jnp.int32, logits.shape, 1)
    logits = jnp.where(lane < N_EXP, logits, -jnp.inf)
    top1 = jnp.max(logits, axis=1, keepdims=True)
    idx1 = jnp.min(jnp.where(logits == top1, lane, LANES), axis=1, keepdims=True)
    rest = jnp.where(lane == idx1, -jnp.inf, logits)
    top2 = jnp.max(rest, axis=1, keepdims=True)
    idx2 = jnp.min(jnp.where(rest == top2, lane, LANES), axis=1, keepdims=True)
    g2 = jnp.exp(top2 - top1)
    denom = 1.0 + g2
    hit1 = lane == idx1
    hit2 = lane == idx2
    both = jnp.where(is_token, jnp.where(hit1, 1.0, 0.0) + jnp.where(hit2, 1.0, 0.0), 0.0)
    prefix = _dot(tri_ref[...], both) + carry_ref[...]
    rank1 = jnp.sum(jnp.where(hit1, prefix, 0.0), axis=1, keepdims=True).astype(jnp.int32)
    rank2 = jnp.sum(jnp.where(hit2, prefix, 0.0), axis=1, keepdims=True).astype(jnp.int32)
    carry_ref[...] += jnp.sum(both, axis=0, keepdims=True)
    cnt_ref[...] = carry_ref[...]
    meta_ref[...] = jnp.where(lane == 0, idx1, jnp.where(lane == 1, idx2, jnp.where(
        lane == 2, rank1, jnp.where(lane == 3, rank2, jnp.where(is_token & (lane == 4), 1, 0)))))
    gate_ref[...] = jnp.where(lane == 0, 1.0 / denom, jnp.where(lane == 1, g2 / denom, 0.0))


def _route(h, router, pad_starts):
    t = h.shape[0]
    tm = _pick_tile(t, 512)
    router_p = jnp.pad(router, ((0, 0), (0, LANES - router.shape[1])))
    pos = np.arange(tm)
    tri = jnp.asarray(pos[:, None] > pos[None, :], BF16)
    rows = lambda w: pl.BlockSpec((tm, w), lambda i: (i, 0))
    const = lambda a: pl.BlockSpec(a.shape, lambda i: (0, 0))
    return pl.pallas_call(
        functools.partial(_route_kernel, pad_starts), grid=(t // tm,),
        in_specs=[rows(D_MODEL), const(router_p), const(tri)],
        out_specs=[rows(LANES), rows(LANES), pl.BlockSpec((1, LANES), lambda i: (0, 0))],
        out_shape=[jax.ShapeDtypeStruct((t, LANES), jnp.int32), jax.ShapeDtypeStruct((t, LANES), F32),
                   jax.ShapeDtypeStruct((1, LANES), F32)],
        scratch_shapes=[pltpu.VMEM((1, LANES), F32)],
        compiler_params=_params(1), name="route")(h, router_p, tri)


def _dispatch_plan(meta, counts, t):
    cnt = counts[0, :N_EXP].astype(jnp.int32)
    padded = ((cnt + EXPERT_TILE - 1) // EXPERT_TILE) * EXPERT_TILE
    ends = jnp.cumsum(padded)
    off = ends - padded

    def base(e):
        return sum(jnp.where(e == i, off[i], 0) for i in range(N_EXP))

    is_token = meta[:, 4] > 0
    n_tiles = -(-2 * t // EXPERT_TILE) + N_EXP
    spare = n_tiles * EXPERT_TILE + 2 * (jnp.cumsum(jnp.logical_not(is_token).astype(jnp.int32)) - 1)
    pos1 = (base(meta[:, 0]) + meta[:, 2]).astype(jnp.int32)
    pos2 = (base(meta[:, 1]) + meta[:, 3]).astype(jnp.int32)
    scatter = (jnp.where(is_token, pos1, spare), jnp.where(is_token, pos2, spare + 1))
    gather = (jnp.where(is_token, pos1, 0), jnp.where(is_token, pos2, 0))
    starts = jnp.arange(n_tiles, dtype=jnp.int32) * EXPERT_TILE
    tile_expert = jnp.minimum(jnp.sum(starts[:, None] >= ends[None, :], axis=1), N_EXP - 1).astype(jnp.int32)
    n_active = (ends[-1] // EXPERT_TILE).astype(jnp.int32).reshape(1)
    return scatter, gather, tile_expert, n_active, n_tiles


assert D_MODEL == SUBLANES * LANES


def _store_token_tiles(ref, x):
    n = x.shape[0]
    for s in range(SUBLANES):
        ref[pl.ds(s, n, stride=SUBLANES), :] = x[:, s * LANES:(s + 1) * LANES]


def _load_token_tiles(ref, n):
    return jnp.concatenate([ref[pl.ds(s, n, stride=SUBLANES), :] for s in range(SUBLANES)], axis=1)


def _tile_rows(i):
    return pl.ds(pl.multiple_of(i * SUBLANES, SUBLANES), SUBLANES)


def _dispatch_kernel(pos1_ref, pos2_ref, h_ref, xs_in_ref, xs_ref, tok_ref, sem):
    del xs_in_ref
    n = h_ref.shape[0]
    _store_token_tiles(tok_ref, h_ref[...])

    def tile_copy(r, p):
        return pltpu.make_async_copy(tok_ref.at[_tile_rows(r)], xs_ref.at[_tile_rows(p)], sem)

    def start(r, c):
        tile_copy(r, pos1_ref[0, 0, r]).start()
        tile_copy(r, pos2_ref[0, 0, r]).start()
        return c

    lax.fori_loop(0, n, start, 0, unroll=DMA_UNROLL)
    all_rows = pltpu.make_async_copy(tok_ref, xs_ref.at[pl.ds(0, n * SUBLANES)], sem)
    all_rows.wait()
    all_rows.wait()


def _dispatch(h, pos1, pos2, n_rows):
    t = h.shape[0]
    tm = _pick_tile(t, 512)
    idx = lambda: pl.BlockSpec((1, 1, tm), lambda i: (i, 0, 0), memory_space=pltpu.SMEM)
    return pl.pallas_call(
        _dispatch_kernel, grid=(t // tm,),
        in_specs=[idx(), idx(), pl.BlockSpec((tm, D_MODEL), lambda i: (i, 0)),
                  pl.BlockSpec(memory_space=pl.ANY)],
        out_specs=pl.BlockSpec(memory_space=pl.ANY),
        out_shape=jax.ShapeDtypeStruct((n_rows * SUBLANES, LANES), F32),
        scratch_shapes=[pltpu.VMEM((tm * SUBLANES, LANES), F32), pltpu.SemaphoreType.DMA(())],
        input_output_aliases={3: 0},
        compiler_params=_params(1), name="dispatch")(
            pos1.reshape(t // tm, 1, tm), pos2.reshape(t // tm, 1, tm), h,
            jnp.zeros((n_rows * SUBLANES, LANES), F32))


def _experts_kernel(te_ref, na_ref, x_ref, wg_ref, wu_ref, wo_ref, y_ref, xb_ref, acc_ref):
    i = pl.program_id(0)
    f = pl.program_id(1)
    last = f == pl.num_programs(1) - 1
    active = i < na_ref[0]

    @pl.when(active)
    def _():
        @pl.when(f == 0)
        def _():
            acc_ref[...] = jnp.zeros_like(acc_ref)
            xb_ref[...] = _load_token_tiles(x_ref, EXPERT_TILE).astype(BF16)

        x = xb_ref[...]
        act = _silu(_dot(x, wg_ref[0])) * _dot(x, wu_ref[0])
        acc_ref[...] += _dot(act, wo_ref[0])

        @pl.when(last)
        def _():
            _store_token_tiles(y_ref, acc_ref[...])

    @pl.when(jnp.logical_not(active) & last)
    def _():
        y_ref[...] = jnp.zeros_like(y_ref)


def _experts(xs, tile_expert, n_active, wi, wo, n_tiles):
    n_rows = n_tiles * EXPERT_TILE
    e_ff = wo.shape[1]
    tf = _pick_tile(e_ff, 512)
    nf = e_ff // tf
    rows = pl.BlockSpec((EXPERT_TILE * SUBLANES, LANES), lambda i, f, te, na: (i, 0))
    grid_spec = pltpu.PrefetchScalarGridSpec(
        num_scalar_prefetch=2, grid=(n_rows // EXPERT_TILE, nf),
        in_specs=[rows, pl.BlockSpec((1, D_MODEL, tf), lambda i, f, te, na: (te[i], 0, f)),
                  pl.BlockSpec((1, D_MODEL, tf), lambda i, f, te, na: (te[i], 0, f + nf)),
                  pl.BlockSpec((1, tf, D_MODEL), lambda i, f, te, na: (te[i], f, 0))],
        out_specs=rows,
        scratch_shapes=[pltpu.VMEM((EXPERT_TILE, D_MODEL), BF16), pltpu.VMEM((EXPERT_TILE, D_MODEL), F32)])
    return pl.pallas_call(
        _experts_kernel, grid_spec=grid_spec, out_shape=jax.ShapeDtypeStruct((n_rows * SUBLANES, LANES), F32),
        compiler_params=_params(2), name="experts")(tile_expert, n_active, xs, wi, wi, wo)


def _combine_kernel(alpha, pos1_ref, pos2_ref, h_ref, gate_ref, ys_ref, lng_ref, lnb_ref, o_ref,
                    a_ref, b_ref, sems):
    n = h_ref.shape[0]

    def tile_copy(p, buf, r, which):
        return pltpu.make_async_copy(ys_ref.at[_tile_rows(p)], buf.at[_tile_rows(r)], sems.at[which])

    def start(r, c):
        tile_copy(pos1_ref[0, 0, r], a_ref, r, 0).start()
        tile_copy(pos2_ref[0, 0, r], b_ref, r, 1).start()
        return c

    lax.fori_loop(0, n, start, 0, unroll=DMA_UNROLL)
    for which, buf in enumerate((a_ref, b_ref)):
        pltpu.make_async_copy(ys_ref.at[pl.ds(0, n * SUBLANES)], buf, sems.at[which]).wait()
    gate = gate_ref[...]
    y = gate[:, 0:1] * _load_token_tiles(a_ref, n) + gate[:, 1:2] * _load_token_tiles(b_ref, n)
    o_ref[...] = _layer_norm(alpha * h_ref[...] + y, lng_ref[...], lnb_ref[...])


def _combine(alpha, h, gates, ys, pos1, pos2, ln_g, ln_b):
    t = h.shape[0]
    tm = _pick_tile(t, 512)
    idx = lambda: pl.BlockSpec((1, 1, tm), lambda i: (i, 0, 0), memory_space=pltpu.SMEM)
    rows = pl.BlockSpec((tm, D_MODEL), lambda i: (i, 0))
    vec = pl.BlockSpec((1, D_MODEL), lambda i: (0, 0))
    return pl.pallas_call(
        functools.partial(_combine_kernel, alpha), grid=(t // tm,),
        in_specs=[idx(), idx(), rows, pl.BlockSpec((tm, LANES), lambda i: (i, 0)),
                  pl.BlockSpec(memory_space=pl.ANY), vec, vec],
        out_specs=rows, out_shape=jax.ShapeDtypeStruct((t, D_MODEL), F32),
        scratch_shapes=[pltpu.VMEM((tm * SUBLANES, LANES), F32), pltpu.VMEM((tm * SUBLANES, LANES), F32),
                        pltpu.SemaphoreType.DMA((2,))],
        compiler_params=_params(1), name="combine")(
            pos1.reshape(t // tm, 1, tm), pos2.reshape(t // tm, 1, tm), h, gates, ys,
            ln_g.reshape(1, -1), ln_b.reshape(1, -1))


def _moe(alpha, h, router, wi, wo, ln_g, ln_b, pad_starts):
    t = h.shape[0]
    meta, gates, counts = _route(h, router, pad_starts)
    scatter, gather, tile_expert, n_active, n_tiles = _dispatch_plan(meta, counts, t)
    n_spare = 2 * len(pad_starts) * PAD_ROWS
    xs = _dispatch(h, scatter[0], scatter[1], n_tiles * EXPERT_TILE + n_spare)
    ys = _experts(xs, tile_expert, n_active, wi, wo, n_tiles)
    return _combine(alpha, h, gates, ys, gather[0], gather[1], ln_g, ln_b)


def _sequence_rows(seq_shapes):
    seqs, row = [], 0
    for grp, (b, l) in enumerate(seq_shapes):
        for i in range(b):
            seqs.append((grp, i, row, l))
            row += l + CH
    return tuple(seqs)


def _run_copies(copies):
    for c in copies:
        c.start()
    for c in copies:
        c.wait()


def _assemble_kernel(seqs, head_ref, xa_ref, xb_ref, h_ref, sem):
    copies = []
    for grp, i, row, l in seqs:
        x_ref = (xa_ref, xb_ref)[grp]
        copies.append(pltpu.make_async_copy(head_ref, h_ref.at[pl.ds(row, CH)], sem))
        copies.append(pltpu.make_async_copy(x_ref.at[i], h_ref.at[pl.ds(row + CH, l)], sem))
    _run_copies(copies)


def _assemble(seqs, head, xa, xb):
    t = sum(l + CH for _, _, _, l in seqs)
    hbm = pl.BlockSpec(memory_space=pl.ANY)
    return pl.pallas_call(
        functools.partial(_assemble_kernel, seqs), in_specs=[hbm, hbm, hbm], out_specs=hbm,
        out_shape=jax.ShapeDtypeStruct((t, D_MODEL), F32),
        scratch_shapes=[pltpu.SemaphoreType.DMA(())], name="assemble")(head, xa, xb)


def _extract_kernel(seqs, h_ref, ya_ref, yb_ref, sem):
    copies = []
    for grp, i, row, l in seqs:
        y_ref = (ya_ref, yb_ref)[grp]
        copies.append(pltpu.make_async_copy(h_ref.at[pl.ds(row + CH, l)], y_ref.at[i], sem))
    _run_copies(copies)


def _extract(seqs, h, seq_shapes):
    hbm = pl.BlockSpec(memory_space=pl.ANY)
    return pl.pallas_call(
        functools.partial(_extract_kernel, seqs), in_specs=[hbm], out_specs=[hbm, hbm],
        out_shape=[jax.ShapeDtypeStruct((b, l, D_MODEL), F32) for b, l in seq_shapes],
        scratch_shapes=[pltpu.SemaphoreType.DMA(())], name="extract")(h)


def kernel(x_prompt, x_sample, meta, e_w_in, e_conv_w, e_conv_b, e_lru_wa, e_lru_ba, e_lru_wx, e_lru_bx,
           e_lru_lambda, e_w_out, e_ffn_wi, e_ffn_wo, o_w_in, o_hg_lower, o_ml_bi, o_ml_bf, o_w_out, o_router,
           o_exp_wi, o_exp_wo, ln_g, ln_b):
    groups = (x_prompt, x_sample)
    depth = ln_g.shape[0]
    alpha = (2.0 * depth) ** 0.25
    seq_shapes = [(x.shape[0], x.shape[1]) for x in groups]
    for _, l in seq_shapes:
        assert l % CH == 0
    cpos_np, clast_np = _chunk_tables(seq_shapes)
    cpos, clast = jnp.asarray(cpos_np), jnp.asarray(clast_np)
    max_rows = max(l for _, l in seq_shapes) + CH
    pad_starts = tuple(int(i) * CH for i in np.flatnonzero(cpos_np == 0))

    seqs = _sequence_rows(seq_shapes)
    head = jnp.concatenate([jnp.zeros((PAD_ROWS, D_MODEL), F32), meta.astype(F32)], axis=0)
    h = _assemble(seqs, head, x_prompt, x_sample)
    t = h.shape[0]

    n_odd_cols = 9 * GROUP_W
    for layer in range(depth):
        p = layer // 2
        if layer % 2 == 0:
            z = _project(h, e_w_in[p].astype(BF16))
            (ret_f, ret_b), (lru_f, lru_b) = _chunk_walk(
                [_retention(z, cpos, clast, max_rows),
                 _rglru(z, 4, cpos, clast, e_conv_w[p], e_conv_b[p], e_lru_wa[p], e_lru_ba[p],
                        e_lru_wx[p], e_lru_bx[p], e_lru_lambda[p])], cpos, clast, t, "even_mixers")
            h = _mix_out(alpha, True, ret_f, ret_b, lru_f, lru_b, z, 3, 5, h, e_w_out[p].astype(BF16),
                         ln_g[layer, 0], ln_b[layer, 0])
            h = _ffn(alpha, h, e_ffn_wi[p].astype(BF16), e_ffn_wo[p].astype(BF16),
                     ln_g[layer, 1], ln_b[layer, 1])
        else:
            w_in = o_w_in[p]
            n_gate = 2 * N_HEADS
            gate_pad = ((0, 0), (0, LANES - n_gate))
            w_gates = jnp.concatenate([jnp.pad(w_in[:, n_odd_cols:n_odd_cols + n_gate], gate_pad),
                                       jnp.pad(w_in[:, n_odd_cols + n_gate:], gate_pad)], axis=1)
            z, gates = _project(h, w_in[:, :n_odd_cols].astype(BF16), w_gates)
            (hg_f, hg_b), (ml_f, ml_b) = _chunk_walk(
                [_hgrn2(z, cpos, clast, o_hg_lower, layer),
                 _mlstm(z, gates, cpos, clast, o_ml_bi[p], o_ml_bf[p])], cpos, clast, t, "odd_mixers")
            h = _mix_out(alpha, False, hg_f, hg_b, ml_f, ml_b, z, 4, 8, h, o_w_out[p].astype(BF16),
                         ln_g[layer, 0], ln_b[layer, 0])
            h = _moe(alpha, h, o_router[p], o_exp_wi[p].astype(BF16), o_exp_wo[p].astype(BF16),
                     ln_g[layer, 1], ln_b[layer, 1], pad_starts)

    return tuple(_extract(seqs, h, seq_shapes))
```

```python
import functools
import math

import numpy as np
import jax
import jax.numpy as jnp
from jax import lax
from jax.experimental import pallas as pl
from jax.experimental.pallas import tpu as pltpu

F32 = jnp.float32
BF16 = jnp.bfloat16

D_MODEL = 1024
GROUP_W = D_MODEL // 2
N_HEADS = 4
HEAD_D = GROUP_W // N_HEADS
N_META = 16
ROPE_BASE = 10000.0
LRU_BLOCKS = 8
LRU_BW = GROUP_W // LRU_BLOCKS
LRU_C = 8.0
N_EXP = 8
EPS = 1e-5

LANES = 128
SUBLANES = 8
CH = 128
PAD_ROWS = CH - N_META
SUB = SUBLANES
GRP = 4 * SUB
NEG_BIG = -1e30
LOG2_E = math.log2(math.e)
VMEM_LIMIT = 56 * 1024 * 1024


def _dot(a, b):
    return jnp.dot(a.astype(BF16), b.astype(BF16), preferred_element_type=F32)


def _dot_nt(a, b):
    return lax.dot_general(a.astype(BF16), b.astype(BF16), (((1,), (1,)), ((), ())),
                           preferred_element_type=F32)


def _dot_tn(a, b):
    return _dot(a.T, b)


def _bf16_terms(x, n_terms):
    terms = []
    for _ in range(n_terms):
        t = x.astype(BF16)
        terms.append(t)
        x = x - t.astype(F32)
    return terms


def _stack_weight_terms(w):
    hi, lo = _bf16_terms(w, 2)
    return jnp.concatenate([hi, lo, hi], axis=0)


def _dot_16bit(x, w_terms):
    hi, lo = _bf16_terms(x, 2)
    return jnp.dot(jnp.concatenate([hi, hi, lo], axis=1), w_terms, preferred_element_type=F32)


PREFIX_TERMS = 3
SPREAD_TERMS = 2


def _prefix_dot(op_tiled, x):
    return jnp.dot(op_tiled, jnp.concatenate(_bf16_terms(x, PREFIX_TERMS), axis=0),
                   preferred_element_type=F32)


def _spread_dot(x, op_tiled):
    return jnp.dot(jnp.concatenate(_bf16_terms(x, SPREAD_TERMS), axis=1), op_tiled,
                   preferred_element_type=F32)


def _sigmoid(x):
    return 1.0 / (1.0 + jnp.exp(-x))


def _silu(x):
    return x * _sigmoid(x)


def _gelu_tanh(x):
    return 0.5 * x * (1.0 + jnp.tanh(math.sqrt(2.0 / math.pi) * (x + 0.044715 * (x * x * x))))


def _softplus(x):
    return jnp.maximum(x, 0.0) + jnp.log(1.0 + jnp.exp(-jnp.abs(x)))


def _pick_tile(total, target):
    best = LANES
    for t in range(LANES, min(total, target) + 1, LANES):
        if total % t == 0:
            best = t
    return best


def _params(n_axes, sem="arbitrary"):
    return pltpu.CompilerParams(dimension_semantics=(sem,) * n_axes, vmem_limit_bytes=VMEM_LIMIT)


def _proj_kernel(x_ref, w_ref, o_ref):
    o_ref[...] = _dot(x_ref[...], w_ref[...])


def _proj_gates_kernel(x_ref, w_ref, wg_ref, o_ref, g_ref):
    g_ref[...] = _dot_16bit(x_ref[...], wg_ref[...])
    o_ref[...] = _dot(x_ref[...], w_ref[...])


PROJ_OUT_BLOCK_BYTES = 8 * 1024 * 1024


def _project(x, w, w_gates=None):
    t, k = x.shape
    n = w.shape[1]
    tm = _pick_tile(t, PROJ_OUT_BLOCK_BYTES // (4 * n))
    x_spec = pl.BlockSpec((tm, k), lambda i: (i, 0))
    w_spec = pl.BlockSpec((k, n), lambda i: (0, 0))
    o_spec = pl.BlockSpec((tm, n), lambda i: (i, 0))
    if w_gates is None:
        return pl.pallas_call(
            _proj_kernel, grid=(t // tm,), in_specs=[x_spec, w_spec], out_specs=o_spec,
            out_shape=jax.ShapeDtypeStruct((t, n), F32), compiler_params=_params(1),
            name="proj")(x, w)
    ng = w_gates.shape[1]
    w_gates = _stack_weight_terms(w_gates)
    return pl.pallas_call(
        _proj_gates_kernel, grid=(t // tm,),
        in_specs=[x_spec, w_spec, pl.BlockSpec(w_gates.shape, lambda i: (0, 0))],
        out_specs=[o_spec, pl.BlockSpec((tm, ng), lambda i: (i, 0))],
        out_shape=[jax.ShapeDtypeStruct((t, n), F32), jax.ShapeDtypeStruct((t, ng), F32)],
        compiler_params=_params(1), name="proj_gates")(x, w, w_gates)


def _chunk_tables(seq_shapes):
    cpos, clast = [], []
    for b, l in seq_shapes:
        n = l // CH + 1
        for _ in range(b):
            cpos += list(range(n))
            clast += [0] * (n - 1) + [1]
    return np.asarray(cpos, np.int32), np.asarray(clast, np.int32)


def _fwd_map(col):
    return lambda g, cp, cl: (g, col)


def _bwd_map(n_chunks, col):
    return lambda g, cp, cl: (n_chunks - 1 - g, col)


def _walk_kernel(bodies, n_inputs, n_scratch, cpos_ref, clast_ref, *refs):
    refs = list(refs)
    ins = [[refs.pop(0) for _ in range(k)] for k in n_inputs]
    outs = [[refs.pop(0) for _ in range(2)] for _ in bodies]

    @pl.when(pl.program_id(0) == 0)
    def _():
        for r in refs:
            r[...] = jnp.zeros_like(r)

    for body, i, o, k in zip(bodies, ins, outs, n_scratch):
        body(cpos_ref, clast_ref, *i, *o, *[refs.pop(0) for _ in range(k)])


def _chunk_walk(parts, cpos, clast, t, name):
    n = t // CH
    blk = (CH, GROUP_W)
    bodies = [p[0] for p in parts]
    grid_spec = pltpu.PrefetchScalarGridSpec(
        num_scalar_prefetch=2, grid=(n,), in_specs=[s for p in parts for s in p[1]],
        out_specs=[pl.BlockSpec(blk, _fwd_map(0)), pl.BlockSpec(blk, _bwd_map(n, 0))] * len(parts),
        scratch_shapes=[s for p in parts for s in p[3]])
    outs = pl.pallas_call(
        functools.partial(_walk_kernel, bodies, [len(p[1]) for p in parts], [len(p[3]) for p in parts]),
        grid_spec=grid_spec, out_shape=[jax.ShapeDtypeStruct((t, GROUP_W), F32)] * (2 * len(parts)),
        compiler_params=_params(1), name=name)(cpos, clast, *[a for p in parts for a in p[2]])
    return [outs[2 * i:2 * i + 2] for i in range(len(parts))]


def _row_valid(first):
    row = lax.broadcasted_iota(jnp.int32, (CH, 1), 0)
    return row >= jnp.where(first, PAD_ROWS, 0)


def _ret_log_gamma():
    return np.log1p(-np.exp2(-5.0 - np.arange(N_HEADS, dtype=np.float64)))


def _ret_consts():
    lg = _ret_log_gamma()
    pos = np.arange(CH, dtype=np.float64)
    rel = pos[:, None] - pos[None, :]
    dmat = np.zeros((2, N_HEADS, CH, CH), np.float64)
    rows = np.zeros((4, CH, GROUP_W), np.float64)
    for h in range(N_HEADS):
        dmat[0, h] = np.where(rel >= 0, np.exp(np.maximum(rel, 0.0) * lg[h]), 0.0)
        dmat[1, h] = np.where(rel < 0, np.exp(np.maximum(-rel, 0.0) * lg[h]), 0.0)
        sl = slice(h * HEAD_D, (h + 1) * HEAD_D)
        rows[0, :, sl] = np.exp((pos + 1.0) * lg[h])[:, None]
        rows[1, :, sl] = np.exp((CH - 1.0 - pos) * lg[h])[:, None]
        rows[2, :, sl] = np.exp((CH - pos) * lg[h])[:, None]
        rows[3, :, sl] = np.exp(pos * lg[h])[:, None]
    return jnp.asarray(dmat, F32), jnp.asarray(rows, F32)


def _ret_direction(q_ref, k_ref, v_ref, cos_ref, sin_ref, dmat_ref, rin_ref, rout_ref, s_ref, o_ref,
                   reset, first):
    valid = _row_valid(first)
    cos = cos_ref[...]
    sin = sin_ref[...]
    chunk_decay = np.exp(CH * _ret_log_gamma())
    for h in range(N_HEADS):
        sl = slice(h * HEAD_D, (h + 1) * HEAD_D)
        q = q_ref[:, sl]
        k = k_ref[:, sl]
        v = v_ref[:, sl]
        q = q * cos + pltpu.roll(q, HEAD_D // 2, 1) * sin
        k = (k * cos + pltpu.roll(k, HEAD_D // 2, 1) * sin) * (HEAD_D ** -0.5)
        k = jnp.where(valid, k, 0.0)
        scores = _dot_nt(q, k) * dmat_ref[h]
        state = jnp.where(reset, 0.0, s_ref[h])
        o_ref[:, sl] = _dot(jnp.concatenate([scores, q * rin_ref[:, sl]], axis=1),
                            jnp.concatenate([v, state], axis=0))
        s_ref[h] = float(chunk_decay[h]) * state + _dot_tn(k * rout_ref[:, sl], v)


def _ret_kernel(cpos_ref, clast_ref, qf, kf, vf, cosf, sinf, qb, kb, vb, cosb, sinb, dmat_ref, rows_ref,
                of_ref, ob_ref, sf_ref, sb_ref):
    g = pl.program_id(0)
    gb = pl.num_programs(0) - 1 - g
    _ret_direction(qf, kf, vf, cosf, sinf, dmat_ref.at[0], rows_ref.at[0], rows_ref.at[1], sf_ref, of_ref,
                   cpos_ref[g] == 0, cpos_ref[g] == 0)
    _ret_direction(qb, kb, vb, cosb, sinb, dmat_ref.at[1], rows_ref.at[2], rows_ref.at[3], sb_ref, ob_ref,
                   clast_ref[gb] == 1, cpos_ref[gb] == 0)


def _rope_tables(n_rows):
    inv = ROPE_BASE ** (-jnp.arange(0, HEAD_D, 2, dtype=jnp.float32) / HEAD_D)
    pos = jnp.maximum(jnp.arange(n_rows, dtype=jnp.float32) - PAD_ROWS, 0.0)
    ang = pos[:, None] * inv[None, :]
    cos, sin = jnp.cos(ang), jnp.sin(ang)
    return jnp.concatenate([cos, cos], axis=1), jnp.concatenate([-sin, sin], axis=1)


def _retention(z, cpos, clast, max_rows):
    t = z.shape[0]
    n = t // CH
    cos2, sin2 = _rope_tables(max_rows)
    dmat, rows = _ret_consts()
    blk = (CH, GROUP_W)
    in_specs = []
    for mk, pm in ((_fwd_map, lambda g, cp, cl: (cp[g], 0)),
                   (functools.partial(_bwd_map, n), lambda g, cp, cl: (cp[n - 1 - g], 0))):
        in_specs += [pl.BlockSpec(blk, mk(0)), pl.BlockSpec(blk, mk(1)), pl.BlockSpec(blk, mk(2)),
                     pl.BlockSpec((CH, HEAD_D), pm), pl.BlockSpec((CH, HEAD_D), pm)]
    in_specs += [pl.BlockSpec(dmat.shape, lambda g, cp, cl: (0, 0, 0, 0)),
                 pl.BlockSpec(rows.shape, lambda g, cp, cl: (0, 0, 0))]
    return (_ret_kernel, in_specs, [z, z, z, cos2, sin2, z, z, z, cos2, sin2, dmat, rows],
            [pltpu.VMEM((N_HEADS, HEAD_D, HEAD_D), F32)] * 2)


HALO = SUBLANES


def _shift_rows(x, s, fill, reverse):
    row = lax.broadcasted_iota(jnp.int32, (CH, 1), 0)
    if reverse:
        return jnp.where(row < CH - s, pltpu.roll(x, CH - s, 0), fill)
    return jnp.where(row >= s, pltpu.roll(x, s, 0), fill)


def _lru_direction(x_ref, prev_ref, next_ref, convw_ref, convb_ref, wg_ref, bias_ref, lam_ref, ext_ref,
                   carry_ref, o_ref, reset, first, last, reverse):
    valid = _row_valid(first)
    ext_ref[HALO:HALO + CH, :] = jnp.where(valid, x_ref[...], 0.0)
    ext_ref[0:HALO, :] = jnp.where(first, 0.0, prev_ref[...])
    ext_ref[HALO + CH:, :] = jnp.where(last, 0.0, next_ref[...])
    xc = convb_ref[...] + ext_ref[HALO - 2:HALO - 2 + CH, :] * convw_ref[0:1, :]
    for tap in range(1, 4):
        xc = xc + ext_ref[HALO - 2 + tap:HALO - 2 + tap + CH, :] * convw_ref[tap:tap + 1, :]

    log_sig_lam = -_softplus(-lam_ref[...])
    parts_a, parts_u = [], []
    for grp in range(GROUP_W // LANES):
        sl = slice(grp * LANES, (grp + 1) * LANES)
        xg = xc[:, sl]
        pre = _dot(xg, wg_ref[grp])
        r = _sigmoid(pre[:, :LANES] + bias_ref[0:1, sl])
        i = _sigmoid(pre[:, LANES:] + bias_ref[1:2, sl])
        a = jnp.exp(LRU_C * log_sig_lam[:, sl] * r)
        u = jnp.sqrt(1.0 - a * a) * (i * xg)
        parts_a.append(a)
        parts_u.append(jnp.where(valid, u, 0.0))
    a = jnp.concatenate(parts_a, axis=1)
    u = jnp.concatenate(parts_u, axis=1)

    n_blk = CH // SUBLANES
    a = a.reshape(n_blk, SUBLANES, GROUP_W)
    u = u.reshape(n_blk, SUBLANES, GROUP_W)
    row_in_blk = lax.broadcasted_iota(jnp.int32, (1, SUBLANES, 1), 1)
    s = 1
    while s < SUBLANES:
        keep = (row_in_blk < SUBLANES - s) if reverse else (row_in_blk >= s)
        shift = SUBLANES - s if reverse else s
        u = u + a * jnp.where(keep, pltpu.roll(u, shift, 1), 0.0)
        a = a * jnp.where(keep, pltpu.roll(a, shift, 1), 1.0)
        s *= 2
    edge = 0 if reverse else SUBLANES - 1
    carry = jnp.where(reset, 0.0, carry_ref[...])
    blocks = [None] * n_blk
    for blk in (reversed(range(n_blk)) if reverse else range(n_blk)):
        blocks[blk] = u[blk] + a[blk] * carry
        carry = blocks[blk][edge:edge + 1, :]
    o_ref[...] = jnp.concatenate(blocks, axis=0)
    carry_ref[...] = carry


def _lru_kernel(cpos_ref, clast_ref, xf, pf, nf, xb, pb, nb, convw_ref, convb_ref, wg_ref, bias_ref, lam_ref,
                of_ref, ob_ref, extf_ref, extb_ref, cf_ref, cb_ref):
    g = pl.program_id(0)
    gb = pl.num_programs(0) - 1 - g
    _lru_direction(xf, pf, nf, convw_ref, convb_ref, wg_ref.at[0], bias_ref.at[0], lam_ref.at[0], extf_ref,
                   cf_ref, of_ref, cpos_ref[g] == 0, cpos_ref[g] == 0, clast_ref[g] == 1, False)
    _lru_direction(xb, pb, nb, convw_ref, convb_ref, wg_ref.at[1], bias_ref.at[1], lam_ref.at[1], extb_ref,
                   cb_ref, ob_ref, clast_ref[gb] == 1, cpos_ref[gb] == 0, clast_ref[gb] == 1, True)


def _lru_gate_weights(wa, wx):
    per = LANES // LRU_BW

    def block_diag(w):
        w = w.reshape(2, LRU_BLOCKS // per, per, LRU_BW, LRU_BW)
        eye = jnp.eye(per, dtype=w.dtype)
        return jnp.einsum("dgpij,pq->dgpiqj", w, eye).reshape(2, LRU_BLOCKS // per, LANES, LANES)

    return jnp.concatenate([block_diag(wa), block_diag(wx)], axis=-1).astype(BF16)


def _rglru(z, col, cpos, clast, conv_w, conv_b, wa, ba, wx, bx, lam):
    t = z.shape[0]
    n = t // CH
    per = CH // HALO
    n_halo = t // HALO
    blk = (CH, GROUP_W)
    hblk = (HALO, GROUP_W)
    wg = _lru_gate_weights(wa, wx)
    bias = jnp.stack([ba, bx], axis=1)
    lam = lam.reshape(2, 1, GROUP_W)

    def prev_f(g, cp, cl):
        return (jnp.maximum(g * per - 1, 0), col)

    def next_f(g, cp, cl):
        return (jnp.minimum((g + 1) * per, n_halo - 1), col)

    def prev_b(g, cp, cl):
        return (jnp.maximum((n - 1 - g) * per - 1, 0), col)

    def next_b(g, cp, cl):
        return (jnp.minimum((n - g) * per, n_halo - 1), col)

    full = lambda a: pl.BlockSpec(a.shape, lambda g, cp, cl: (0,) * a.ndim)
    conv_b2 = conv_b.reshape(1, GROUP_W)
    in_specs = [pl.BlockSpec(blk, _fwd_map(col)), pl.BlockSpec(hblk, prev_f), pl.BlockSpec(hblk, next_f),
                pl.BlockSpec(blk, _bwd_map(n, col)), pl.BlockSpec(hblk, prev_b), pl.BlockSpec(hblk, next_b),
                full(conv_w), full(conv_b2), full(wg), full(bias), full(lam)]
    return (_lru_kernel, in_specs, [z, z, z, z, z, z, conv_w, conv_b2, wg, bias, lam],
            [pltpu.VMEM((CH + 2 * HALO, GROUP_W), F32)] * 2 + [pltpu.VMEM((1, GROUP_W), F32)] * 2)


def _tri_consts():
    pos = np.arange(CH)
    lower = (pos[:, None] >= pos[None, :]).astype(np.float32)
    ops = np.stack([lower, lower.T])
    return jnp.asarray(np.tile(ops, (1, 1, PREFIX_TERMS)), BF16)


def _hg_masks():
    pos = np.arange(CH)
    same_blk = pos[:, None] // SUB == pos[None, :] // SUB
    same_grp = pos[:, None] // GRP == pos[None, :] // GRP
    return jnp.asarray(np.stack([same_blk, same_grp]).astype(np.float32))


def _hg_direction(q_ref, f_ref, v_ref, lb, cum_ref, mask_ref, st_ref, o_ref, reset, first, reverse):
    valid = _row_valid(first)
    sub_i =lax.broadcasted_iota(jnp.int32, (SUB, HEAD_D), 0)
    lane_j = lax.broadcasted_iota(jnp.int32, (SUB, HEAD_D), 1) & (SUB - 1)
    causal = (sub_i <= lane_j) if reverse else (sub_i >= lane_j)
    pick = [causal & (lane_j == j) for j in range(SUB)]
    cum_op = cum_ref[...]
    same_blk = mask_ref[0]
    same_grp = mask_ref[1]
    zero = jnp.zeros((SUB, HEAD_D), F32)
    n_blk = CH // SUB
    per = GRP // SUB
    n_grp = CH // GRP

    def edge(unit, idx):
        return unit * idx if reverse else unit * (idx + 1) - 1

    def split_product(b, qs, ks, bs, pieces):
        q_slabs, k_slabs = [], []
        for ref, q_active, k_active in pieces:
            qp, kp = [], []
            for i in range(n_blk):
                r = ref(i)
                qp.append(qs[i] * jnp.exp2(bs[i] - b[r:r + 1, :]) if q_active(i) else zero)
                kp.append(ks[i] * jnp.exp2(b[r:r + 1, :] - bs[i]) if k_active(i) else zero)
            q_slabs.append(jnp.concatenate(qp, axis=0))
            k_slabs.append(jnp.concatenate(kp, axis=0))
        return _dot_nt(jnp.concatenate(q_slabs, axis=1), jnp.concatenate(k_slabs, axis=1))

    level1 = []
    for c in (range(1, per) if reverse else range(per - 1)):
        level1.append((lambda i, c=c: edge(SUB, (i // per) * per + c),
                       (lambda i, c=c: i % per < c) if reverse else (lambda i, c=c: i % per > c),
                       lambda i, c=c: i % per == c))
    level2 = []
    for gc in (range(1, n_grp) if reverse else range(n_grp - 1)):
        level2.append((lambda i, gc=gc: edge(GRP, gc),
                       (lambda i, gc=gc: i // per < gc) if reverse else (lambda i, gc=gc: i // per > gc),
                       lambda i, gc=gc: i // per == gc))

    for h in range(N_HEADS):
        sl = slice(h * HEAD_D, (h + 1) * HEAD_D)
        lbh = lb[:, sl]
        q = _silu(q_ref[:, sl])
        f = lbh + (1.0 - lbh) * _sigmoid(f_ref[:, sl])
        k = jnp.where(valid, 1.0 - f, 0.0)
        v = v_ref[:, sl]
        b = _prefix_dot(cum_op, jnp.log(f)) * LOG2_E
        qs = [q[i * SUB:(i + 1) * SUB] for i in range(n_blk)]
        ks = [k[i * SUB:(i + 1) * SUB] for i in range(n_blk)]
        bs = [b[i * SUB:(i + 1) * SUB] for i in range(n_blk)]

        tiles = [qs[blk] * jnp.exp2(jnp.minimum(bs[blk] - bs[blk][j:j + 1, :], 0.0))
                 for blk in range(n_blk) for j in range(SUB)]
        pair = _dot_nt(jnp.concatenate(tiles, axis=0), k)
        rows = []
        for blk in range(n_blk):
            base = blk * SUB * SUB
            a_blk = jnp.where(pick[0], pair[base:base + SUB], 0.0)
            for j in range(1, SUB):
                a_blk = a_blk + jnp.where(pick[j], pair[base + j * SUB:base + (j + 1) * SUB], 0.0)
            rows.append(a_blk)
        scores = jnp.concatenate(rows, axis=0) * same_blk
        scores = scores + split_product(b, qs, ks, bs, level1) * same_grp
        scores = scores + split_product(b, qs, ks, bs, level2)

        state = jnp.where(reset, 0.0, st_ref[h])
        o_ref[:, sl] = _dot(scores, v) + _dot_nt(q * jnp.exp2(b), state)
        b_tot = b[0:1, :] if reverse else b[CH - 1:CH, :]
        st_ref[h] = state * jnp.exp2(b_tot) + _dot_tn(v, k * jnp.exp2(b_tot - b))


def _hg_lower_bound(lower_ref, layer):
    low = lower_ref[...]
    e = jnp.exp(low - jnp.max(low, axis=0, keepdims=True))
    soft = e / jnp.sum(e, axis=0, keepdims=True)
    lb = jnp.zeros((1, GROUP_W), F32)
    for l in range(1, layer + 1):
        lb = lb + soft[l:l + 1, :]
    return lb


def _hg_kernel(layer, cpos_ref, clast_ref, qf, ff, vf, qb, fb, vb, lower_ref, cum_ref, mask_ref,
               of_ref, ob_ref, sf_ref, sb_ref):
    g = pl.program_id(0)
    gb = pl.num_programs(0) - 1 - g
    lb = _hg_lower_bound(lower_ref, layer)
    _hg_direction(qf, ff, vf, lb, cum_ref.at[0], mask_ref, sf_ref, of_ref,
                  cpos_ref[g] == 0, cpos_ref[g] == 0, False)
    _hg_direction(qb, fb, vb, lb, cum_ref.at[1], mask_ref, sb_ref, ob_ref,
                  clast_ref[gb] == 1, cpos_ref[gb] == 0, True)


def _hgrn2(z, cpos, clast, lower, layer):
    t = z.shape[0]
    n = t // CH
    blk = (CH, GROUP_W)
    cum = _tri_consts()
    masks = _hg_masks()
    full = lambda a: pl.BlockSpec(a.shape, lambda g, cp, cl: (0,) * a.ndim)
    in_specs = [pl.BlockSpec(blk, _fwd_map(0)), pl.BlockSpec(blk, _fwd_map(1)), pl.BlockSpec(blk, _fwd_map(3)),
                pl.BlockSpec(blk, _bwd_map(n, 0)), pl.BlockSpec(blk, _bwd_map(n, 2)),
                pl.BlockSpec(blk, _bwd_map(n, 3)), full(lower), full(cum), full(masks)]
    return (functools.partial(_hg_kernel, layer), in_specs, [z, z, z, z, z, z, lower, cum, masks],
            [pltpu.VMEM((N_HEADS, HEAD_D, HEAD_D), F32)] * 2)


def _log_sigmoid(x):
    return jnp.minimum(x, 0.0) - jnp.log(1.0 + jnp.exp(-jnp.abs(x)))


def _ml_direction(q_ref, k_ref, v_ref, gi_ref, gf_ref, bias_ref, cum_op, spread_ref, s_ref, m_ref, o_ref,
                  reset, first, d, reverse):
    valid = _row_valid(first)
    cum = _prefix_dot(cum_op, _log_sigmoid(gf_ref[...] + bias_ref[1]))
    a = jnp.where(valid, gi_ref[...] + bias_ref[0] - cum, NEG_BIG)
    run = a
    s = 1
    while s < CH:
        run = jnp.maximum(run, _shift_rows(run, s, NEG_BIG, reverse))
        s *= 2
    m_st = jnp.where(reset, NEG_BIG, m_ref[...])
    mx = jnp.maximum(m_st, run)
    edge = 0 if reverse else CH - 1
    mx_last = mx[edge:edge + 1, :]
    m_ref[...] = cum[edge:edge + 1, :] + mx_last
    decay = jnp.exp(m_st - mx_last)

    stacked = jnp.concatenate([mx, m_st - mx, -(cum + mx), a - mx_last], axis=0) * LOG2_E
    spread_b = _spread_dot(stacked, spread_ref[...])
    mx_b, inter_b, floor_b, end_b = (spread_b[i * CH:(i + 1) * CH] for i in range(4))
    a_t = (a * LOG2_E).T

    row = lax.broadcasted_iota(jnp.int32, (CH, CH), 0)
    col = lax.broadcasted_iota(jnp.int32, (CH, CH), 1)
    causal = (col >= row) if reverse else (col <= row)
    ones = jnp.ones((CH, HEAD_D), F32)
    for h in range(N_HEADS):
        sl = slice(h * HEAD_D, (h + 1) * HEAD_D)
        x = d * N_HEADS + h
        q = q_ref[:, sl]
        k = jnp.where(valid, k_ref[:, sl] * (HEAD_D ** -0.5), 0.0)
        v_ext = jnp.concatenate([v_ref[:, sl], ones], axis=1)
        w = jnp.where(causal, jnp.exp2(jnp.minimum(a_t[x:x + 1, :] - mx_b[:, sl], 0.0)), 0.0)
        qk = _dot_nt(q, k) * w
        s_inter = jnp.exp2(jnp.minimum(inter_b[:, sl], 0.0))
        state = jnp.where(reset, 0.0, s_ref[h])
        ext = _dot(jnp.concatenate([qk, q * s_inter], axis=1), jnp.concatenate([v_ext, state], axis=0))
        o_ref[:, sl] = ext[:, :HEAD_D] / jnp.maximum(jnp.abs(ext[:, HEAD_D:]), jnp.exp2(floor_b[:, sl]))
        kw = k * jnp.exp2(jnp.minimum(end_b[:, sl], 0.0))
        s_ref[h] = decay[:, x:x + 1] * state + _dot_tn(kw, v_ext)


def _ml_kernel(cpos_ref, clast_ref, qf, kf, vf, gif, gff, qb, kb, vb, gib, gfb, bias_ref, cum_ref, spread_ref,
               of_ref, ob_ref, sfs, mfs, sbs, mbs):
    g = pl.program_id(0)
    gb = pl.num_programs(0) - 1 - g
    _ml_direction(qf, kf, vf, gif, gff, bias_ref, cum_ref[0], spread_ref.at[0], sfs, mfs, of_ref,
                  cpos_ref[g] == 0, cpos_ref[g] == 0, 0, False)
    _ml_direction(qb, kb, vb, gib, gfb, bias_ref, cum_ref[1], spread_ref.at[1], sbs, mbs, ob_ref,
                  clast_ref[gb] == 1, cpos_ref[gb] == 0, 1, True)


def _ml_spread():
    spread = np.zeros((2, LANES, GROUP_W), np.float32)
    for d in range(2):
        for h in range(N_HEADS):
            spread[d, d * N_HEADS + h, h * HEAD_D:(h + 1) * HEAD_D] = 1.0
    return jnp.asarray(np.tile(spread, (1, SPREAD_TERMS, 1)), BF16)


def _mlstm(z, gates, cpos, clast, ml_bi, ml_bf):
    t = z.shape[0]
    n = t // CH
    blk = (CH, GROUP_W)
    gblk = (CH, LANES)
    cum = _tri_consts()
    spread = _ml_spread()
    pad = jnp.zeros((LANES - 2 * N_HEADS,), F32)
    bias = jnp.stack([jnp.concatenate([ml_bi.reshape(-1), pad]),
                      jnp.concatenate([ml_bf.reshape(-1), pad])]).reshape(2, 1, LANES)
    full = lambda a: pl.BlockSpec(a.shape, lambda g, cp, cl: (0,) * a.ndim)
    in_specs = [pl.BlockSpec(blk, _fwd_map(5)), pl.BlockSpec(blk, _fwd_map(6)), pl.BlockSpec(blk, _fwd_map(7)),
                pl.BlockSpec(gblk, _fwd_map(0)), pl.BlockSpec(gblk, _fwd_map(1)),
                pl.BlockSpec(blk, _bwd_map(n, 5)), pl.BlockSpec(blk, _bwd_map(n, 6)),
                pl.BlockSpec(blk, _bwd_map(n, 7)),
                pl.BlockSpec(gblk, _bwd_map(n, 0)), pl.BlockSpec(gblk, _bwd_map(n, 1)),
                full(bias), full(cum), full(spread)]
    state = [pltpu.VMEM((N_HEADS, HEAD_D, 2 * HEAD_D), F32), pltpu.VMEM((1, LANES), F32)]
    return (_ml_kernel, in_specs, [z, z, z, gates, gates, z, z, z, gates, gates, bias, cum, spread], state * 2)


def _head_norm(x, center):
    outs = []
    for h in range(N_HEADS):
        xh = x[:, h * HEAD_D:(h + 1) * HEAD_D]
        if center:
            xh = xh - jnp.mean(xh, axis=1, keepdims=True)
        outs.append(xh * lax.rsqrt(jnp.mean(xh * xh, axis=1, keepdims=True) + EPS))
    return jnp.concatenate(outs, axis=1)


def _layer_norm(x, g, b):
    xc = x - jnp.mean(x, axis=1, keepdims=True)
    return xc * lax.rsqrt(jnp.mean(xc * xc, axis=1, keepdims=True) + EPS) * g + b


def _mix_out_kernel(alpha, even, af, ab, ag, bf, bb, bg, h_ref, w_ref, lng_ref, lnb_ref, o_ref):
    a = af[...] + ab[...]
    b = bf[...] + bb[...]
    if even:
        a = _head_norm(a, False) * _silu(ag[...])
        b = b * _gelu_tanh(bg[...])
    else:
        a = _head_norm(a, False) * _silu(ag[...])
        b = _head_norm(b, True) * _sigmoid(bg[...])
    mix = _dot(a, w_ref[0:GROUP_W, :]) + _dot(b, w_ref[GROUP_W:, :])
    o_ref[...] = _layer_norm(alpha * h_ref[...] + mix, lng_ref[...], lnb_ref[...])


def _mix_out(alpha, even, af, ab, bf, bb, z, col_a, col_b, h, w_out, ln_g, ln_b):
    t = h.shape[0]
    tm = _pick_tile(t, 512)
    half = lambda c: pl.BlockSpec((tm, GROUP_W), lambda i: (i, c))
    rows = pl.BlockSpec((tm, D_MODEL), lambda i: (i, 0))
    vec = pl.BlockSpec((1, D_MODEL), lambda i: (0, 0))
    return pl.pallas_call(
        functools.partial(_mix_out_kernel, alpha, even), grid=(t // tm,),
        in_specs=[half(0), half(0), half(col_a), half(0), half(0), half(col_b), rows,
                  pl.BlockSpec((D_MODEL, D_MODEL), lambda i: (0, 0)), vec, vec],
        out_specs=rows, out_shape=jax.ShapeDtypeStruct((t, D_MODEL), F32),
        compiler_params=_params(1), name="mix_out")(
            af, ab, z, bf, bb, z, h, w_out, ln_g.reshape(1, -1), ln_b.reshape(1, -1))


def _ffn_kernel(alpha, h_ref, wg_ref, wu_ref, wo_ref, lng_ref, lnb_ref, o_ref, acc_ref):
    f = pl.program_id(1)

    @pl.when(f == 0)
    def _():
        acc_ref[...] = jnp.zeros_like(acc_ref)

    x = h_ref[...].astype(BF16)
    act = _silu(_dot(x, wg_ref[...])) * _dot(x, wu_ref[...])
    acc_ref[...] += _dot(act, wo_ref[...])

    @pl.when(f == pl.num_programs(1) - 1)
    def _():
        o_ref[...] = _layer_norm(alpha * h_ref[...] + acc_ref[...], lng_ref[...], lnb_ref[...])


def _ffn(alpha, h, wi, wo, ln_g, ln_b):
    t = h.shape[0]
    d_ff = wo.shape[0]
    tm = _pick_tile(t, 1024)
    tf = _pick_tile(d_ff, 1536)
    nf = d_ff // tf
    rows = pl.BlockSpec((tm, D_MODEL), lambda i, f: (i, 0))
    vec = pl.BlockSpec((1, D_MODEL), lambda i, f: (0, 0))
    return pl.pallas_call(
        functools.partial(_ffn_kernel, alpha), grid=(t // tm, nf),
        in_specs=[rows, pl.BlockSpec((D_MODEL, tf), lambda i, f: (0, f)),
                  pl.BlockSpec((D_MODEL, tf), lambda i, f: (0, f + nf)),
                  pl.BlockSpec((tf, D_MODEL), lambda i, f: (f, 0)), vec, vec],
        out_specs=rows, out_shape=jax.ShapeDtypeStruct((t, D_MODEL), F32),
        scratch_shapes=[pltpu.VMEM((tm, D_MODEL), F32)],
        compiler_params=_params(2), name="ffn")(h, wi, wi, wo, ln_g.reshape(1, -1), ln_b.reshape(1, -1))


EXPERT_TILE = 1024
DMA_UNROLL = 8
DMA_PRIORITIES = 2


def _route_kernel(pad_starts, h_ref, router_ref, tri_ref, meta_ref, gate_ref, cnt_ref, carry_ref):
    @pl.when(pl.program_id(0) == 0)
    def _():
        carry_ref[...] = jnp.zeros_like(carry_ref)

    tm = h_ref.shape[0]
    row = pl.program_id(0) * tm + lax.broadcasted_iota(jnp.int32, (tm, 1), 0)
    is_pad = row < 0
    for s in pad_starts:
        is_pad = is_pad | ((row >= s) & (row < s + PAD_ROWS))
    is_token = jnp.logical_not(is_pad)

    logits = _dot_16bit(h_ref[...], router_ref[...])
    lane = lax.broadcasted_iota(jnp.int32, logits.shape, 1)
    logits = jnp.where(lane < N_EXP, logits, -jnp.inf)
    top1 = jnp.max(logits, axis=1, keepdims=True)
    idx1 = jnp.min(jnp.where(logits == top1, lane, LANES), axis=1, keepdims=True)
    rest = jnp.where(lane == idx1, -jnp.inf, logits)
    top2 = jnp.max(rest, axis=1, keepdims=True)
    idx2 = jnp.min(jnp.where(rest == top2, lane, LANES), axis=1, keepdims=True)
    g2 = jnp.exp(top2 - top1)
    denom = 1.0 + g2
    hit1 = lane == idx1
    hit2 = lane == idx2
    both = jnp.where(is_token, jnp.where(hit1, 1.0, 0.0) + jnp.where(hit2, 1.0, 0.0), 0.0)
    prefix = _dot(tri_ref[...], both) + carry_ref[...]
    rank1 = jnp.sum(jnp.where(hit1, prefix, 0.0), axis=1, keepdims=True).astype(jnp.int32)
    rank2 = jnp.sum(jnp.where(hit2, prefix, 0.0), axis=1, keepdims=True).astype(jnp.int32)
    carry_ref[...] += jnp.sum(both, axis=0, keepdims=True)
    cnt_ref[...] = carry_ref[...]
    meta_ref[...] = jnp.where(lane == 0, idx1, jnp.where(lane == 1, idx2, jnp.where(
        lane == 2, rank1, jnp.where(lane == 3, rank2, jnp.where(is_token & (lane == 4), 1, 0)))))
    gate_ref[...] = jnp.where(lane == 0, 1.0 / denom, jnp.where(lane == 1, g2 / denom, 0.0))


def _route(h, router, pad_starts):
    t = h.shape[0]
    tm = _pick_tile(t, 512)
    router_p = _stack_weight_terms(jnp.pad(router, ((0, 0), (0, LANES - router.shape[1]))))
    pos = np.arange(tm)
    tri = jnp.asarray(pos[:, None] > pos[None, :], BF16)
    rows = lambda w: pl.BlockSpec((tm, w), lambda i: (i, 0))
    const = lambda a: pl.BlockSpec(a.shape, lambda i: (0, 0))
    return pl.pallas_call(
        functools.partial(_route_kernel, pad_starts), grid=(t // tm,),
        in_specs=[rows(D_MODEL), const(router_p), const(tri)],
        out_specs=[rows(LANES), rows(LANES), pl.BlockSpec((1, LANES), lambda i: (0, 0))],
        out_shape=[jax.ShapeDtypeStruct((t, LANES), jnp.int32), jax.ShapeDtypeStruct((t, LANES), F32),
                   jax.ShapeDtypeStruct((1, LANES), F32)],
        scratch_shapes=[pltpu.VMEM((1, LANES), F32)],
        compiler_params=_params(1), name="route")(h, router_p, tri)


def _dispatch_plan(meta, counts, t):
    cnt = counts[0, :N_EXP].astype(jnp.int32)
    padded = ((cnt + EXPERT_TILE - 1) // EXPERT_TILE) * EXPERT_TILE
    ends = jnp.cumsum(padded)
    off = ends - padded

    def base(e):
        return sum(jnp.where(e == i, off[i], 0) for i in range(N_EXP))

    is_token = meta[:, 4] > 0
    n_tiles = -(-2 * t // EXPERT_TILE) + N_EXP
    spare = n_tiles * EXPERT_TILE + 2 * (jnp.cumsum(jnp.logical_not(is_token).astype(jnp.int32)) - 1)
    pos1 = (base(meta[:, 0]) + meta[:, 2]).astype(jnp.int32)
    pos2 = (base(meta[:, 1]) + meta[:, 3]).astype(jnp.int32)
    scatter = (jnp.where(is_token, pos1, spare), jnp.where(is_token, pos2, spare + 1))
    gather = (jnp.where(is_token, pos1, 0), jnp.where(is_token, pos2, 0))
    starts = jnp.arange(n_tiles, dtype=jnp.int32) * EXPERT_TILE
    tile_expert = jnp.minimum(jnp.sum(starts[:, None] >= ends[None, :], axis=1), N_EXP - 1).astype(jnp.int32)
    n_active = (ends[-1] // EXPERT_TILE).astype(jnp.int32).reshape(1)
    return scatter, gather, tile_expert, n_active, n_tiles


assert D_MODEL == SUBLANES * LANES


def _store_token_tiles(ref, x):
    n = x.shape[0]
    for s in range(SUBLANES):
        ref[pl.ds(s, n, stride=SUBLANES), :] = x[:, s * LANES:(s + 1) * LANES]


def _load_token_tiles(ref, n):
    return jnp.concatenate([ref[pl.ds(s, n, stride=SUBLANES), :] for s in range(SUBLANES)], axis=1)


def _tile_rows(i):
    return pl.ds(pl.multiple_of(i * SUBLANES, SUBLANES), SUBLANES)


def _dispatch_kernel(pos1_ref, pos2_ref, h_ref, xs_in_ref, xs_ref, tok_ref, sem):
    del xs_in_ref
    n = h_ref.shape[0]
    _store_token_tiles(tok_ref, h_ref[...])

    def tile_copy(r, p):
        return pltpu.make_async_copy(tok_ref.at[_tile_rows(r)], xs_ref.at[_tile_rows(p)], sem)

    def start(pair, c):
        for prio in range(DMA_PRIORITIES):
            r = pair * DMA_PRIORITIES + prio
            tile_copy(r, pos1_ref[0, 0, r]).start(priority=prio)
            tile_copy(r, pos2_ref[0, 0, r]).start(priority=prio)
        return c

    lax.fori_loop(0, n // DMA_PRIORITIES, start, 0, unroll=DMA_UNROLL // DMA_PRIORITIES)
    all_rows = pltpu.make_async_copy(tok_ref, xs_ref.at[pl.ds(0, n * SUBLANES)], sem)
    all_rows.wait()
    all_rows.wait()


def _dispatch(h, pos1, pos2, n_rows):
    t = h.shape[0]
    tm = _pick_tile(t, 512)
    idx = lambda: pl.BlockSpec((1, 1, tm), lambda i: (i, 0, 0), memory_space=pltpu.SMEM)
    return pl.pallas_call(
        _dispatch_kernel, grid=(t // tm,),
        in_specs=[idx(), idx(), pl.BlockSpec((tm, D_MODEL), lambda i: (i, 0)),
                  pl.BlockSpec(memory_space=pl.ANY)],
        out_specs=pl.BlockSpec(memory_space=pl.ANY),
        out_shape=jax.ShapeDtypeStruct((n_rows * SUBLANES, LANES), F32),
        scratch_shapes=[pltpu.VMEM((tm * SUBLANES, LANES), F32), pltpu.SemaphoreType.DMA(())],
        input_output_aliases={3: 0},
        compiler_params=_params(1), name="dispatch")(
            pos1.reshape(t // tm, 1, tm), pos2.reshape(t // tm, 1, tm), h,
            jnp.zeros((n_rows * SUBLANES, LANES), F32))


def _experts_kernel(te_ref, na_ref, x_ref, wg_ref, wu_ref, wo_ref, y_ref, xb_ref, acc_ref):
    i = pl.program_id(0)
    f = pl.program_id(1)
    last = f == pl.num_programs(1) - 1
    active = i < na_ref[0]

    @pl.when(active)
    def _():
        @pl.when(f == 0)
        def _():
            acc_ref[...] = jnp.zeros_like(acc_ref)
            xb_ref[...] = _load_token_tiles(x_ref, EXPERT_TILE).astype(BF16)

        x = xb_ref[...]
        act = _silu(_dot(x, wg_ref[0])) * _dot(x, wu_ref[0])
        acc_ref[...] += _dot(act, wo_ref[0])

        @pl.when(last)
        def _():
            _store_token_tiles(y_ref, acc_ref[...])

    @pl.when(jnp.logical_not(active) & last)
    def _():
        y_ref[...] = jnp.zeros_like(y_ref)


def _experts(xs, tile_expert, n_active, wi, wo, n_tiles):
    n_rows = n_tiles * EXPERT_TILE
    e_ff = wo.shape[1]
    tf = _pick_tile(e_ff, 512)
    nf = e_ff // tf
    rows = pl.BlockSpec((EXPERT_TILE * SUBLANES, LANES), lambda i, f, te, na: (i, 0))
    grid_spec = pltpu.PrefetchScalarGridSpec(
        num_scalar_prefetch=2, grid=(n_rows // EXPERT_TILE, nf),
        in_specs=[rows, pl.BlockSpec((1, D_MODEL, tf), lambda i, f, te, na: (te[i], 0, f)),
                  pl.BlockSpec((1, D_MODEL, tf), lambda i, f, te, na: (te[i], 0, f + nf)),
                  pl.BlockSpec((1, tf, D_MODEL), lambda i, f, te, na: (te[i], f, 0))],
        out_specs=rows,
        scratch_shapes=[pltpu.VMEM((EXPERT_TILE, D_MODEL), BF16), pltpu.VMEM((EXPERT_TILE, D_MODEL), F32)])
    return pl.pallas_call(
        _experts_kernel, grid_spec=grid_spec, out_shape=jax.ShapeDtypeStruct((n_rows * SUBLANES, LANES), F32),
        compiler_params=_params(2), name="experts")(tile_expert, n_active, xs, wi, wi, wo)


def _combine_kernel(alpha, pos1_ref, pos2_ref, h_ref, gate_ref, ys_ref, lng_ref, lnb_ref, o_ref,
                    a_ref, b_ref, sems):
    n = h_ref.shape[0]

    def tile_copy(p, buf, r, which):
        return pltpu.make_async_copy(ys_ref.at[_tile_rows(p)], buf.at[_tile_rows(r)], sems.at[which])

    def start(pair, c):
        for prio in range(DMA_PRIORITIES):
            r = pair * DMA_PRIORITIES + prio
            tile_copy(pos1_ref[0, 0, r], a_ref, r, 0).start(priority=prio)
            tile_copy(pos2_ref[0, 0, r], b_ref, r, 1).start(priority=prio)
        return c

    lax.fori_loop(0, n // DMA_PRIORITIES, start, 0, unroll=DMA_UNROLL // DMA_PRIORITIES)
    for which, buf in enumerate((a_ref, b_ref)):
        pltpu.make_async_copy(ys_ref.at[pl.ds(0, n * SUBLANES)], buf, sems.at[which]).wait()
    gate = gate_ref[...]
    y = gate[:, 0:1] * _load_token_tiles(a_ref, n) + gate[:, 1:2] * _load_token_tiles(b_ref, n)
    o_ref[...] = _layer_norm(alpha * h_ref[...] + y, lng_ref[...], lnb_ref[...])


def _combine(alpha, h, gates, ys, pos1, pos2, ln_g, ln_b):
    t = h.shape[0]
    tm = _pick_tile(t, 512)
    idx = lambda: pl.BlockSpec((1, 1, tm), lambda i: (i, 0, 0), memory_space=pltpu.SMEM)
    rows = pl.BlockSpec((tm, D_MODEL), lambda i: (i, 0))
    vec = pl.BlockSpec((1, D_MODEL), lambda i: (0, 0))
    return pl.pallas_call(
        functools.partial(_combine_kernel, alpha), grid=(t // tm,),
        in_specs=[idx(), idx(), rows, pl.BlockSpec((tm, LANES), lambda i: (i, 0)),
                  pl.BlockSpec(memory_space=pl.ANY), vec, vec],
        out_specs=rows, out_shape=jax.ShapeDtypeStruct((t, D_MODEL), F32),
        scratch_shapes=[pltpu.VMEM((tm * SUBLANES, LANES), F32), pltpu.VMEM((tm * SUBLANES, LANES), F32),
                        pltpu.SemaphoreType.DMA((2,))],
        compiler_params=_params(1), name="combine")(
            pos1.reshape(t // tm, 1, tm), pos2.reshape(t // tm, 1, tm), h, gates, ys,
            ln_g.reshape(1, -1), ln_b.reshape(1, -1))


def _moe(alpha, h, router, wi, wo, ln_g, ln_b, pad_starts):
    t = h.shape[0]
    meta, gates, counts = _route(h, router, pad_starts)
    scatter, gather, tile_expert, n_active, n_tiles = _dispatch_plan(meta, counts, t)
    n_spare = 2 * len(pad_starts) * PAD_ROWS
    xs = _dispatch(h, scatter[0], scatter[1], n_tiles * EXPERT_TILE + n_spare)
    ys = _experts(xs, tile_expert, n_active, wi, wo, n_tiles)
    return _combine(alpha, h, gates, ys, gather[0], gather[1], ln_g, ln_b)


def kernel(x_prompt, x_sample, meta, e_w_in, e_conv_w, e_conv_b, e_lru_wa, e_lru_ba, e_lru_wx, e_lru_bx,
           e_lru_lambda, e_w_out, e_ffn_wi, e_ffn_wo, o_w_in, o_hg_lower, o_ml_bi, o_ml_bf, o_w_out, o_router,
           o_exp_wi, o_exp_wo, ln_g, ln_b):
    groups = (x_prompt, x_sample)
    depth = ln_g.shape[0]
    alpha = (2.0 * depth) ** 0.25
    seq_shapes = [(x.shape[0], x.shape[1]) for x in groups]
    for _, l in seq_shapes:
        assert l % CH == 0
    cpos_np, clast_np = _chunk_tables(seq_shapes)
    cpos, clast = jnp.asarray(cpos_np), jnp.asarray(clast_np)
    max_rows = max(l for _, l in seq_shapes) + CH
    pad_starts = tuple(int(i) * CH for i in np.flatnonzero(cpos_np == 0))

    head = jnp.concatenate([jnp.zeros((PAD_ROWS, D_MODEL), F32), meta.astype(F32)], axis=0)
    parts = []
    for x in groups:
        full = jnp.concatenate([jnp.broadcast_to(head[None], (x.shape[0], CH, D_MODEL)), x], axis=1)
        parts.append(full.reshape(-1, D_MODEL))
    h = jnp.concatenate(parts, axis=0)
    t = h.shape[0]

    n_odd_cols = 9 * GROUP_W
    for layer in range(depth):
        p = layer // 2
        if layer % 2 == 0:
            z = _project(h, e_w_in[p].astype(BF16))
            (ret_f, ret_b), (lru_f, lru_b) = _chunk_walk(
                [_retention(z, cpos, clast, max_rows),
                 _rglru(z, 4, cpos, clast, e_conv_w[p], e_conv_b[p], e_lru_wa[p], e_lru_ba[p],
                        e_lru_wx[p], e_lru_bx[p], e_lru_lambda[p])], cpos, clast, t, "even_mixers")
            h = _mix_out(alpha, True, ret_f, ret_b, lru_f, lru_b, z, 3, 5, h, e_w_out[p].astype(BF16),
                         ln_g[layer, 0], ln_b[layer, 0])
            h = _ffn(alpha, h, e_ffn_wi[p].astype(BF16), e_ffn_wo[p].astype(BF16),
                     ln_g[layer, 1], ln_b[layer, 1])
        else:
            w_in = o_w_in[p]
            n_gate = 2 * N_HEADS
            gate_pad = ((0, 0), (0, LANES - n_gate))
            w_gates = jnp.concatenate([jnp.pad(w_in[:, n_odd_cols:n_odd_cols + n_gate], gate_pad),
                                       jnp.pad(w_in[:, n_odd_cols + n_gate:], gate_pad)], axis=1)
            z, gates = _project(h, w_in[:, :n_odd_cols].astype(BF16), w_gates)
            (hg_f, hg_b), (ml_f, ml_b) = _chunk_walk(
                [_hgrn2(z, cpos, clast, o_hg_lower, layer),
                 _mlstm(z, gates, cpos, clast, o_ml_bi[p], o_ml_bf[p])], cpos, clast, t, "odd_mixers")
            h = _mix_out(alpha, False, hg_f, hg_b, ml_f, ml_b, z, 4, 8, h, o_w_out[p].astype(BF16),
                         ln_g[layer, 0], ln_b[layer, 0])
            h = _moe(alpha, h, o_router[p], o_exp_wi[p].astype(BF16), o_exp_wo[p].astype(BF16),
                     ln_g[layer, 1], ln_b[layer, 1], pad_starts)

    outs = []
    row = 0
    for b, l in seq_shapes:
        n = b * (l + CH)
        outs.append(h[row:row + n].reshape(b, l + CH, D_MODEL)[:, CH:])
        row += n
    return tuple(outs)
```

```python
import functools
import math

import numpy as np
import jax
import jax.numpy as jnp
from jax import lax
from jax.experimental import pallas as pl
from jax.experimental.pallas import tpu as pltpu

F32 = jnp.float32
BF16 = jnp.bfloat16

D_MODEL = 1024
GROUP_W = D_MODEL // 2
N_HEADS = 4
HEAD_D = GROUP_W // N_HEADS
N_META = 16
ROPE_BASE = 10000.0
LRU_BLOCKS = 8
LRU_BW = GROUP_W // LRU_BLOCKS
LRU_C = 8.0
N_EXP = 8
EPS = 1e-5

LANES = 128
SUBLANES = 8
CH = 128
PAD_ROWS = CH - N_META
SUB = SUBLANES
GRP = 4 * SUB
NEG_BIG = -1e30
LOG2_E = math.log2(math.e)
VMEM_LIMIT = 56 * 1024 * 1024


def _dot(a, b):
    return jnp.dot(a.astype(BF16), b.astype(BF16), preferred_element_type=F32)


def _dot_nt(a, b):
    return lax.dot_general(a.astype(BF16), b.astype(BF16), (((1,), (1,)), ((), ())),
                           preferred_element_type=F32)


def _dot_tn(a, b):
    return _dot(a.T, b)


def _bf16_terms(x, n_terms):
    terms = []
    for _ in range(n_terms):
        t = x.astype(BF16)
        terms.append(t)
        x = x - t.astype(F32)
    return terms


def _stack_weight_terms(w):
    hi, lo = _bf16_terms(w, 2)
    return jnp.concatenate([hi, lo, hi], axis=0)


def _dot_16bit(x, w_terms):
    hi, lo = _bf16_terms(x, 2)
    return jnp.dot(jnp.concatenate([hi, hi, lo], axis=1), w_terms, preferred_element_type=F32)


PREFIX_TERMS = 3
SPREAD_TERMS = 2


def _prefix_dot(op_tiled, x):
    return jnp.dot(op_tiled, jnp.concatenate(_bf16_terms(x, PREFIX_TERMS), axis=0),
                   preferred_element_type=F32)


def _spread_dot(x, op_tiled):
    return jnp.dot(jnp.concatenate(_bf16_terms(x, SPREAD_TERMS), axis=1), op_tiled,
                   preferred_element_type=F32)


def _sigmoid(x):
    return 1.0 / (1.0 + jnp.exp(-x))


def _silu(x):
    return x * _sigmoid(x)


def _gelu_tanh(x):
    return 0.5 * x * (1.0 + jnp.tanh(math.sqrt(2.0 / math.pi) * (x + 0.044715 * (x * x * x))))


def _softplus(x):
    return jnp.maximum(x, 0.0) + jnp.log(1.0 + jnp.exp(-jnp.abs(x)))


def _pick_tile(total, target):
    best = LANES
    for t in range(LANES, min(total, target) + 1, LANES):
        if total % t == 0:
            best = t
    return best


def _params(n_axes, sem="arbitrary"):
    return pltpu.CompilerParams(dimension_semantics=(sem,) * n_axes, vmem_limit_bytes=VMEM_LIMIT)


def _proj_kernel(x_ref, w_ref, o_ref):
    o_ref[...] = _dot(x_ref[...], w_ref[...])


def _proj_gates_kernel(x_ref, w_ref, wg_ref, o_ref, g_ref):
    g_ref[...] = _dot_16bit(x_ref[...], wg_ref[...])
    o_ref[...] = _dot(x_ref[...], w_ref[...])


PROJ_OUT_BLOCK_BYTES = 8 * 1024 * 1024


def _project(x, w, w_gates=None):
    t, k = x.shape
    n = w.shape[1]
    tm = _pick_tile(t, PROJ_OUT_BLOCK_BYTES // (4 * n))
    x_spec = pl.BlockSpec((tm, k), lambda i: (i, 0))
    w_spec = pl.BlockSpec((k, n), lambda i: (0, 0))
    o_spec = pl.BlockSpec((tm, n), lambda i: (i, 0))
    if w_gates is None:
        return pl.pallas_call(
            _proj_kernel, grid=(t // tm,), in_specs=[x_spec, w_spec], out_specs=o_spec,
            out_shape=jax.ShapeDtypeStruct((t, n), F32), compiler_params=_params(1),
            name="proj")(x, w)
    ng = w_gates.shape[1]
    w_gates = _stack_weight_terms(w_gates)
    return pl.pallas_call(
        _proj_gates_kernel, grid=(t // tm,),
        in_specs=[x_spec, w_spec, pl.BlockSpec(w_gates.shape, lambda i: (0, 0))],
        out_specs=[o_spec, pl.BlockSpec((tm, ng), lambda i: (i, 0))],
        out_shape=[jax.ShapeDtypeStruct((t, n), F32), jax.ShapeDtypeStruct((t, ng), F32)],
        compiler_params=_params(1), name="proj_gates")(x, w, w_gates)


def _chunk_tables(seq_shapes):
    cpos, clast = [], []
    for b, l in seq_shapes:
        n = l // CH + 1
        for _ in range(b):
            cpos += list(range(n))
            clast += [0] * (n - 1) + [1]
    return np.asarray(cpos, np.int32), np.asarray(clast, np.int32)


def _fwd_map(col):
    return lambda g, cp, cl: (g, col)


def _bwd_map(n_chunks, col):
    return lambda g, cp, cl: (n_chunks - 1 - g, col)


def _walk_kernel(bodies, n_inputs, n_scratch, cpos_ref, clast_ref, *refs):
    refs = list(refs)
    ins = [[refs.pop(0) for _ in range(k)] for k in n_inputs]
    outs = [[refs.pop(0) for _ in range(2)] for _ in bodies]

    @pl.when(pl.program_id(0) == 0)
    def _():
        for r in refs:
            r[...] = jnp.zeros_like(r)

    for body, i, o, k in zip(bodies, ins, outs, n_scratch):
        body(cpos_ref, clast_ref, *i, *o, *[refs.pop(0) for _ in range(k)])


def _chunk_walk(parts, cpos, clast, t, name):
    n = t // CH
    blk = (CH, GROUP_W)
    bodies = [p[0] for p in parts]
    grid_spec = pltpu.PrefetchScalarGridSpec(
        num_scalar_prefetch=2, grid=(n,), in_specs=[s for p in parts for s in p[1]],
        out_specs=[pl.BlockSpec(blk, _fwd_map(0)), pl.BlockSpec(blk, _bwd_map(n, 0))] * len(parts),
        scratch_shapes=[s for p in parts for s in p[3]])
    outs = pl.pallas_call(
        functools.partial(_walk_kernel, bodies, [len(p[1]) for p in parts], [len(p[3]) for p in parts]),
        grid_spec=grid_spec, out_shape=[jax.ShapeDtypeStruct((t, GROUP_W), BF16)] * (2 * len(parts)),
        compiler_params=_params(1), name=name)(cpos, clast, *[a for p in parts for a in p[2]])
    return [outs[2 * i:2 * i + 2] for i in range(len(parts))]


def _row_valid(first):
    row = lax.broadcasted_iota(jnp.int32, (CH, 1), 0)
    return row >= jnp.where(first, PAD_ROWS, 0)


def _ret_log_gamma():
    return np.log1p(-np.exp2(-5.0 - np.arange(N_HEADS, dtype=np.float64)))


def _ret_consts():
    lg = _ret_log_gamma()
    pos = np.arange(CH, dtype=np.float64)
    rel = pos[:, None] - pos[None, :]
    dmat = np.zeros((2, N_HEADS, CH, CH), np.float64)
    rows = np.zeros((4, CH, GROUP_W), np.float64)
    for h in range(N_HEADS):
        dmat[0, h] = np.where(rel >= 0, np.exp(np.maximum(rel, 0.0) * lg[h]), 0.0)
        dmat[1, h] = np.where(rel < 0, np.exp(np.maximum(-rel, 0.0) * lg[h]), 0.0)
        sl = slice(h * HEAD_D, (h + 1) * HEAD_D)
        rows[0, :, sl] = np.exp((pos + 1.0) * lg[h])[:, None]
        rows[1, :, sl] = np.exp((CH - 1.0 - pos) * lg[h])[:, None]
        rows[2, :, sl] = np.exp((CH - pos) * lg[h])[:, None]
        rows[3, :, sl] = np.exp(pos * lg[h])[:, None]
    return jnp.asarray(dmat, F32), jnp.asarray(rows, F32)


def _ret_direction(q_ref, k_ref, v_ref, cos_ref, sin_ref, dmat_ref, rin_ref, rout_ref, s_ref, o_ref,
                   reset, first):
    valid = _row_valid(first)
    cos = cos_ref[...]
    sin = sin_ref[...]
    chunk_decay = np.exp(CH * _ret_log_gamma())
    for h in range(N_HEADS):
        sl = slice(h * HEAD_D, (h + 1) * HEAD_D)
        q = q_ref[:, sl]
        k = k_ref[:, sl]
        v = v_ref[:, sl]
        q = q * cos + pltpu.roll(q, HEAD_D // 2, 1) * sin
        k = (k * cos + pltpu.roll(k, HEAD_D // 2, 1) * sin) * (HEAD_D ** -0.5)
        k = jnp.where(valid, k, 0.0)
        scores = _dot_nt(q, k) * dmat_ref[h]
        state = jnp.where(reset, 0.0, s_ref[h])
        o_ref[:, sl] = _dot(jnp.concatenate([scores, q * rin_ref[:, sl]], axis=1),
                            jnp.concatenate([v, state], axis=0)).astype(o_ref.dtype)
        s_ref[h] = float(chunk_decay[h]) * state + _dot_tn(k * rout_ref[:, sl], v)


def _ret_kernel(cpos_ref, clast_ref, qf, kf, vf, cosf, sinf, qb, kb, vb, cosb, sinb, dmat_ref, rows_ref,
                of_ref, ob_ref, sf_ref, sb_ref):
    g = pl.program_id(0)
    gb = pl.num_programs(0) - 1 - g
    _ret_direction(qf, kf, vf, cosf, sinf, dmat_ref.at[0], rows_ref.at[0], rows_ref.at[1], sf_ref, of_ref,
                   cpos_ref[g] == 0, cpos_ref[g] == 0)
    _ret_direction(qb, kb, vb, cosb, sinb, dmat_ref.at[1], rows_ref.at[2], rows_ref.at[3], sb_ref, ob_ref,
                   clast_ref[gb] == 1, cpos_ref[gb] == 0)


def _rope_tables(n_rows):
    inv = ROPE_BASE ** (-jnp.arange(0, HEAD_D, 2, dtype=jnp.float32) / HEAD_D)
    pos = jnp.maximum(jnp.arange(n_rows, dtype=jnp.float32) - PAD_ROWS, 0.0)
    ang = pos[:, None] * inv[None, :]
    cos, sin = jnp.cos(ang), jnp.sin(ang)
    return jnp.concatenate([cos, cos], axis=1), jnp.concatenate([-sin, sin], axis=1)


def _retention(z, cpos, clast, max_rows):
    t = z.shape[0]
    n = t // CH
    cos2, sin2 = _rope_tables(max_rows)
    dmat, rows = _ret_consts()
    blk = (CH, GROUP_W)
    in_specs = []
    for mk, pm in ((_fwd_map, lambda g, cp, cl: (cp[g], 0)),
                   (functools.partial(_bwd_map, n), lambda g, cp, cl: (cp[n - 1 - g], 0))):
        in_specs += [pl.BlockSpec(blk, mk(0)), pl.BlockSpec(blk, mk(1)), pl.BlockSpec(blk, mk(2)),
                     pl.BlockSpec((CH, HEAD_D), pm), pl.BlockSpec((CH, HEAD_D), pm)]
    in_specs += [pl.BlockSpec(dmat.shape, lambda g, cp, cl: (0, 0, 0, 0)),
                 pl.BlockSpec(rows.shape, lambda g, cp, cl: (0, 0, 0))]
    return (_ret_kernel, in_specs, [z, z, z, cos2, sin2, z, z, z, cos2, sin2, dmat, rows],
            [pltpu.VMEM((N_HEADS, HEAD_D, HEAD_D), F32)] * 2)


HALO = SUBLANES


def _shift_rows(x, s, fill, reverse):
    row = lax.broadcasted_iota(jnp.int32, (CH, 1), 0)
    if reverse:
        return jnp.where(row < CH - s, pltpu.roll(x, CH - s, 0), fill)
    return jnp.where(row >= s, pltpu.roll(x, s, 0), fill)


def _lru_direction(x_ref, prev_ref, next_ref, convw_ref, convb_ref, wg_ref, bias_ref, lam_ref, ext_ref,
                   carry_ref, o_ref, reset, first, last, reverse):
    valid = _row_valid(first)
    ext_ref[HALO:HALO + CH, :] = jnp.where(valid, x_ref[...], 0.0)
    ext_ref[0:HALO, :] = jnp.where(first, 0.0, prev_ref[...])
    ext_ref[HALO + CH:, :] = jnp.where(last, 0.0, next_ref[...])
    xc = convb_ref[...] + ext_ref[HALO - 2:HALO - 2 + CH, :] * convw_ref[0:1, :]
    for tap in range(1, 4):
        xc = xc + ext_ref[HALO - 2 + tap:HALO - 2 + tap + CH, :] * convw_ref[tap:tap + 1, :]

    log_sig_lam = -_softplus(-lam_ref[...])
    parts_a, parts_u = [], []
    for grp in range(GROUP_W // LANES):
        sl = slice(grp * LANES, (grp + 1) * LANES)
        xg = xc[:, sl]
        pre = _dot(xg, wg_ref[grp])
        r = _sigmoid(pre[:, :LANES] + bias_ref[0:1, sl])
        i = _sigmoid(pre[:, LANES:] + bias_ref[1:2, sl])
        a = jnp.exp(LRU_C * log_sig_lam[:, sl] * r)
        u = jnp.sqrt(1.0 - a * a) * (i * xg)
        parts_a.append(a)
        parts_u.append(jnp.where(valid, u, 0.0))
    a = jnp.concatenate(parts_a, axis=1)
    u = jnp.concatenate(parts_u, axis=1)

    n_blk = CH // SUBLANES
    a = a.reshape(n_blk, SUBLANES, GROUP_W)
    u = u.reshape(n_blk, SUBLANES, GROUP_W)
    row_in_blk = lax.broadcasted_iota(jnp.int32, (1, SUBLANES, 1), 1)
    s = 1
    while s < SUBLANES:
        keep = (row_in_blk < SUBLANES - s) if reverse else (row_in_blk >= s)
        shift = SUBLANES - s if reverse else s
        u = u + a * jnp.where(keep, pltpu.roll(u, shift, 1), 0.0)
        a = a * jnp.where(keep, pltpu.roll(a, shift, 1), 1.0)
        s *= 2
    edge = 0 if reverse else SUBLANES - 1
    carry = jnp.where(reset, 0.0, carry_ref[...])
    blocks = [None] * n_blk
    for blk in (reversed(range(n_blk)) if reverse else range(n_blk)):
        blocks[blk] = u[blk] + a[blk] * carry
        carry = blocks[blk][edge:edge + 1, :]
    o_ref[...] = jnp.concatenate(blocks, axis=0).astype(o_ref.dtype)
    carry_ref[...] = carry


def _lru_kernel(cpos_ref, clast_ref, xf, pf, nf, xb, pb, nb, convw_ref, convb_ref, wg_ref, bias_ref, lam_ref,
                of_ref, ob_ref, extf_ref, extb_ref, cf_ref, cb_ref):
    g = pl.program_id(0)
    gb = pl.num_programs(0) - 1 - g
    _lru_direction(xf, pf, nf, convw_ref, convb_ref, wg_ref.at[0], bias_ref.at[0], lam_ref.at[0], extf_ref,
                   cf_ref, of_ref, cpos_ref[g] == 0, cpos_ref[g] == 0, clast_ref[g] == 1, False)
    _lru_direction(xb, pb, nb, convw_ref, convb_ref, wg_ref.at[1], bias_ref.at[1], lam_ref.at[1], extb_ref,
                   cb_ref, ob_ref, clast_ref[gb] == 1, cpos_ref[gb] == 0, clast_ref[gb] == 1, True)


def _lru_gate_weights(wa, wx):
    per = LANES // LRU_BW

    def block_diag(w):
        w = w.reshape(2, LRU_BLOCKS // per, per, LRU_BW, LRU_BW)
        eye = jnp.eye(per, dtype=w.dtype)
        return jnp.einsum("dgpij,pq->dgpiqj", w, eye).reshape(2, LRU_BLOCKS // per, LANES, LANES)

    return jnp.concatenate([block_diag(wa), block_diag(wx)], axis=-1).astype(BF16)


def _rglru(z, col, cpos, clast, conv_w, conv_b, wa, ba, wx, bx, lam):
    t = z.shape[0]
    n = t // CH
    per = CH // HALO
    n_halo = t // HALO
    blk = (CH, GROUP_W)
    hblk = (HALO, GROUP_W)
    wg = _lru_gate_weights(wa, wx)
    bias = jnp.stack([ba, bx], axis=1)
    lam = lam.reshape(2, 1, GROUP_W)

    def prev_f(g, cp, cl):
        return (jnp.maximum(g * per - 1, 0), col)

    def next_f(g, cp, cl):
        return (jnp.minimum((g + 1) * per, n_halo - 1), col)

    def prev_b(g, cp, cl):
        return (jnp.maximum((n - 1 - g) * per - 1, 0), col)

    def next_b(g, cp, cl):
        return (jnp.minimum((n - g) * per, n_halo - 1), col)

    full = lambda a: pl.BlockSpec(a.shape, lambda g, cp, cl: (0,) * a.ndim)
    conv_b2 = conv_b.reshape(1, GROUP_W)
    in_specs = [pl.BlockSpec(blk, _fwd_map(col)), pl.BlockSpec(hblk, prev_f), pl.BlockSpec(hblk, next_f),
                pl.BlockSpec(blk, _bwd_map(n, col)), pl.BlockSpec(hblk, prev_b), pl.BlockSpec(hblk, next_b),
                full(conv_w), full(conv_b2), full(wg), full(bias), full(lam)]
    return (_lru_kernel, in_specs, [z, z, z, z, z, z, conv_w, conv_b2, wg, bias, lam],
            [pltpu.VMEM((CH + 2 * HALO, GROUP_W), F32)] * 2 + [pltpu.VMEM((1, GROUP_W), F32)] * 2)


def _tri_consts():
    pos = np.arange(CH)
    lower = (pos[:, None] >= pos[None, :]).astype(np.float32)
    ops = np.stack([lower, lower.T])
    return jnp.asarray(np.tile(ops, (1, 1, PREFIX_TERMS)), BF16)


def _hg_masks():
    pos = np.arange(CH)
    same_blk = pos[:, None] // SUB == pos[None, :] // SUB
    same_grp = pos[:, None] // GRP == pos[None, :] // GRP
    return jnp.asarray(np.stack([same_blk, same_grp]).astype(np.float32))


def _hg_direction(q_ref, f_ref, v_ref, lb, cum_ref, mask_ref, st_ref, o_ref, reset, first, reverse):
    valid = _row_valid(first)
    sub_i =lax.broadcasted_iota(jnp.int32, (SUB, HEAD_D), 0)
    lane_j = lax.broadcasted_iota(jnp.int32, (SUB, HEAD_D), 1) & (SUB - 1)
    causal = (sub_i <= lane_j) if reverse else (sub_i >= lane_j)
    pick = [causal & (lane_j == j) for j in range(SUB)]
    cum_op = cum_ref[...]
    same_blk = mask_ref[0]
    same_grp = mask_ref[1]
    zero = jnp.zeros((SUB, HEAD_D), F32)
    n_blk = CH // SUB
    per = GRP // SUB
    n_grp = CH // GRP

    def edge(unit, idx):
        return unit * idx if reverse else unit * (idx + 1) - 1

    def split_product(b, qs, ks, bs, pieces):
        q_slabs, k_slabs = [], []
        for ref, q_active, k_active in pieces:
            qp, kp = [], []
            for i in range(n_blk):
                r = ref(i)
                qp.append(qs[i] * jnp.exp2(bs[i] - b[r:r + 1, :]) if q_active(i) else zero)
                kp.append(ks[i] * jnp.exp2(b[r:r + 1, :] - bs[i]) if k_active(i) else zero)
            q_slabs.append(jnp.concatenate(qp, axis=0))
            k_slabs.append(jnp.concatenate(kp, axis=0))
        return _dot_nt(jnp.concatenate(q_slabs, axis=1), jnp.concatenate(k_slabs, axis=1))

    level1 = []
    for c in (range(1, per) if reverse else range(per - 1)):
        level1.append((lambda i, c=c: edge(SUB, (i // per) * per + c),
                       (lambda i, c=c: i % per < c) if reverse else (lambda i, c=c: i % per > c),
                       lambda i, c=c: i % per == c))
    level2 = []
    for gc in (range(1, n_grp) if reverse else range(n_grp - 1)):
        level2.append((lambda i, gc=gc: edge(GRP, gc),
                       (lambda i, gc=gc: i // per < gc) if reverse else (lambda i, gc=gc: i // per > gc),
                       lambda i, gc=gc: i // per == gc))

    for h in range(N_HEADS):
        sl = slice(h * HEAD_D, (h + 1) * HEAD_D)
        lbh = lb[:, sl]
        q = _silu(q_ref[:, sl])
        f = lbh + (1.0 - lbh) * _sigmoid(f_ref[:, sl])
        k = jnp.where(valid, 1.0 - f, 0.0)
        v = v_ref[:, sl]
        b = _prefix_dot(cum_op, jnp.log(f)) * LOG2_E
        qs = [q[i * SUB:(i + 1) * SUB] for i in range(n_blk)]
        ks = [k[i * SUB:(i + 1) * SUB] for i in range(n_blk)]
        bs = [b[i * SUB:(i + 1) * SUB] for i in range(n_blk)]

        tiles = [qs[blk] * jnp.exp2(jnp.minimum(bs[blk] - bs[blk][j:j + 1, :], 0.0))
                 for blk in range(n_blk) for j in range(SUB)]
        pair = _dot_nt(jnp.concatenate(tiles, axis=0), k)
        rows = []
        for blk in range(n_blk):
            base = blk * SUB * SUB
            a_blk = jnp.where(pick[0], pair[base:base + SUB], 0.0)
            for j in range(1, SUB):
                a_blk = a_blk + jnp.where(pick[j], pair[base + j * SUB:base + (j + 1) * SUB], 0.0)
            rows.append(a_blk)
        scores = jnp.concatenate(rows, axis=0) * same_blk
        scores = scores + split_product(b, qs, ks, bs, level1) * same_grp
        scores = scores + split_product(b, qs, ks, bs, level2)

        state = jnp.where(reset, 0.0, st_ref[h])
        o_ref[:, sl] = (_dot(scores, v) + _dot_nt(q * jnp.exp2(b), state)).astype(o_ref.dtype)
        b_tot = b[0:1, :] if reverse else b[CH - 1:CH, :]
        st_ref[h] = state * jnp.exp2(b_tot) + _dot_tn(v, k * jnp.exp2(b_tot - b))


def _hg_lower_bound(lower_ref, layer):
    low = lower_ref[...]
    e = jnp.exp(low - jnp.max(low, axis=0, keepdims=True))
    soft = e / jnp.sum(e, axis=0, keepdims=True)
    lb = jnp.zeros((1, GROUP_W), F32)
    for l in range(1, layer + 1):
        lb = lb + soft[l:l + 1, :]
    return lb


def _hg_kernel(layer, cpos_ref, clast_ref, qf, ff, vf, qb, fb, vb, lower_ref, cum_ref, mask_ref,
               of_ref, ob_ref, sf_ref, sb_ref):
    g = pl.program_id(0)
    gb = pl.num_programs(0) - 1 - g
    lb = _hg_lower_bound(lower_ref, layer)
    _hg_direction(qf, ff, vf, lb, cum_ref.at[0], mask_ref, sf_ref, of_ref,
                  cpos_ref[g] == 0, cpos_ref[g] == 0, False)
    _hg_direction(qb, fb, vb, lb, cum_ref.at[1], mask_ref, sb_ref, ob_ref,
                  clast_ref[gb] == 1, cpos_ref[gb] == 0, True)


def _hgrn2(z, cpos, clast, lower, layer):
    t = z.shape[0]
    n = t // CH
    blk = (CH, GROUP_W)
    cum = _tri_consts()
    masks = _hg_masks()
    full = lambda a: pl.BlockSpec(a.shape, lambda g, cp, cl: (0,) * a.ndim)
    in_specs = [pl.BlockSpec(blk, _fwd_map(0)), pl.BlockSpec(blk, _fwd_map(1)), pl.BlockSpec(blk, _fwd_map(3)),
                pl.BlockSpec(blk, _bwd_map(n, 0)), pl.BlockSpec(blk, _bwd_map(n, 2)),
                pl.BlockSpec(blk, _bwd_map(n, 3)), full(lower), full(cum), full(masks)]
    return (functools.partial(_hg_kernel, layer), in_specs, [z, z, z, z, z, z, lower, cum, masks],
            [pltpu.VMEM((N_HEADS, HEAD_D, HEAD_D), F32)] * 2)


def _log_sigmoid(x):
    return jnp.minimum(x, 0.0) - jnp.log(1.0 + jnp.exp(-jnp.abs(x)))


def _ml_direction(q_ref, k_ref, v_ref, gi_ref, gf_ref, bias_ref, cum_op, spread_ref, s_ref, m_ref, o_ref,
                  reset, first, d, reverse):
    valid = _row_valid(first)
    cum = _prefix_dot(cum_op, _log_sigmoid(gf_ref[...] + bias_ref[1]))
    a = jnp.where(valid, gi_ref[...] + bias_ref[0] - cum, NEG_BIG)
    run = a
    s = 1
    while s < CH:
        run = jnp.maximum(run, _shift_rows(run, s, NEG_BIG, reverse))
        s *= 2
    m_st = jnp.where(reset, NEG_BIG, m_ref[...])
    mx = jnp.maximum(m_st, run)
    edge = 0 if reverse else CH - 1
    mx_last = mx[edge:edge + 1, :]
    m_ref[...] = cum[edge:edge + 1, :] + mx_last
    decay = jnp.exp(m_st - mx_last)

    stacked = jnp.concatenate([mx, m_st - mx, -(cum + mx), a - mx_last], axis=0) * LOG2_E
    spread_b = _spread_dot(stacked, spread_ref[...])
    mx_b, inter_b, floor_b, end_b = (spread_b[i * CH:(i + 1) * CH] for i in range(4))
    a_t = (a * LOG2_E).T

    row = lax.broadcasted_iota(jnp.int32, (CH, CH), 0)
    col = lax.broadcasted_iota(jnp.int32, (CH, CH), 1)
    causal = (col >= row) if reverse else (col <= row)
    ones = jnp.ones((CH, HEAD_D), F32)
    for h in range(N_HEADS):
        sl = slice(h * HEAD_D, (h + 1) * HEAD_D)
        x = d * N_HEADS + h
        q = q_ref[:, sl]
        k = jnp.where(valid, k_ref[:, sl] * (HEAD_D ** -0.5), 0.0)
        v_ext = jnp.concatenate([v_ref[:, sl], ones], axis=1)
        w = jnp.where(causal, jnp.exp2(jnp.minimum(a_t[x:x + 1, :] - mx_b[:, sl], 0.0)), 0.0)
        qk = _dot_nt(q, k) * w
        s_inter = jnp.exp2(jnp.minimum(inter_b[:, sl], 0.0))
        state = jnp.where(reset, 0.0, s_ref[h])
        ext = _dot(jnp.concatenate([qk, q * s_inter], axis=1), jnp.concatenate([v_ext, state], axis=0))
        o_ref[:, sl] = (ext[:, :HEAD_D] / jnp.maximum(jnp.abs(ext[:, HEAD_D:]), jnp.exp2(floor_b[:, sl]))
                        ).astype(o_ref.dtype)
        kw = k * jnp.exp2(jnp.minimum(end_b[:, sl], 0.0))
        s_ref[h] = decay[:, x:x + 1] * state + _dot_tn(kw, v_ext)


def _ml_kernel(cpos_ref, clast_ref, qf, kf, vf, gif, gff, qb, kb, vb, gib, gfb, bias_ref, cum_ref, spread_ref,
               of_ref, ob_ref, sfs, mfs, sbs, mbs):
    g = pl.program_id(0)
    gb = pl.num_programs(0) - 1 - g
    _ml_direction(qf, kf, vf, gif, gff, bias_ref, cum_ref[0], spread_ref.at[0], sfs, mfs, of_ref,
                  cpos_ref[g] == 0, cpos_ref[g] == 0, 0, False)
    _ml_direction(qb, kb, vb, gib, gfb, bias_ref, cum_ref[1], spread_ref.at[1], sbs, mbs, ob_ref,
                  clast_ref[gb] == 1, cpos_ref[gb] == 0, 1, True)


def _ml_spread():
    spread = np.zeros((2, LANES, GROUP_W), np.float32)
    for d in range(2):
        for h in range(N_HEADS):
            spread[d, d * N_HEADS + h, h * HEAD_D:(h + 1) * HEAD_D] = 1.0
    return jnp.asarray(np.tile(spread, (1, SPREAD_TERMS, 1)), BF16)


def _mlstm(z, gates, cpos, clast, ml_bi, ml_bf):
    t = z.shape[0]
    n = t // CH
    blk = (CH, GROUP_W)
    gblk = (CH, LANES)
    cum = _tri_consts()
    spread = _ml_spread()
    pad = jnp.zeros((LANES - 2 * N_HEADS,), F32)
    bias = jnp.stack([jnp.concatenate([ml_bi.reshape(-1), pad]),
                      jnp.concatenate([ml_bf.reshape(-1), pad])]).reshape(2, 1, LANES)
    full = lambda a: pl.BlockSpec(a.shape, lambda g, cp, cl: (0,) * a.ndim)
    in_specs = [pl.BlockSpec(blk, _fwd_map(5)), pl.BlockSpec(blk, _fwd_map(6)), pl.BlockSpec(blk, _fwd_map(7)),
                pl.BlockSpec(gblk, _fwd_map(0)), pl.BlockSpec(gblk, _fwd_map(1)),
                pl.BlockSpec(blk, _bwd_map(n, 5)), pl.BlockSpec(blk, _bwd_map(n, 6)),
                pl.BlockSpec(blk, _bwd_map(n, 7)),
                pl.BlockSpec(gblk, _bwd_map(n, 0)), pl.BlockSpec(gblk, _bwd_map(n, 1)),
                full(bias), full(cum), full(spread)]
    state = [pltpu.VMEM((N_HEADS, HEAD_D, 2 * HEAD_D), F32), pltpu.VMEM((1, LANES), F32)]
    return (_ml_kernel, in_specs, [z, z, z, gates, gates, z, z, z, gates, gates, bias, cum, spread], state * 2)


def _head_norm(x, center):
    outs = []
    for h in range(N_HEADS):
        xh = x[:, h * HEAD_D:(h + 1) * HEAD_D]
        if center:
            xh = xh - jnp.mean(xh, axis=1, keepdims=True)
        outs.append(xh * lax.rsqrt(jnp.mean(xh * xh, axis=1, keepdims=True) + EPS))
    return jnp.concatenate(outs, axis=1)


def _layer_norm(x, g, b):
    xc = x - jnp.mean(x, axis=1, keepdims=True)
    return xc * lax.rsqrt(jnp.mean(xc * xc, axis=1, keepdims=True) + EPS) * g + b


def _mix_out_kernel(alpha, even, af, ab, ag, bf, bb, bg, h_ref, w_ref, lng_ref, lnb_ref, o_ref):
    a = af[...].astype(F32) + ab[...].astype(F32)
    b = bf[...].astype(F32) + bb[...].astype(F32)
    if even:
        a = _head_norm(a, False) * _silu(ag[...])
        b = b * _gelu_tanh(bg[...])
    else:
        a = _head_norm(a, False) * _silu(ag[...])
        b = _head_norm(b, True) * _sigmoid(bg[...])
    mix = _dot(a, w_ref[0:GROUP_W, :]) + _dot(b, w_ref[GROUP_W:, :])
    o_ref[...] = _layer_norm(alpha * h_ref[...] + mix, lng_ref[...], lnb_ref[...])


def _mix_out(alpha, even, af, ab, bf, bb, z, col_a, col_b, h, w_out, ln_g, ln_b):
    t = h.shape[0]
    tm = _pick_tile(t, 512)
    half = lambda c: pl.BlockSpec((tm, GROUP_W), lambda i: (i, c))
    rows = pl.BlockSpec((tm, D_MODEL), lambda i: (i, 0))
    vec = pl.BlockSpec((1, D_MODEL), lambda i: (0, 0))
    return pl.pallas_call(
        functools.partial(_mix_out_kernel, alpha, even), grid=(t // tm,),
        in_specs=[half(0), half(0), half(col_a), half(0), half(0), half(col_b), rows,
                  pl.BlockSpec((D_MODEL, D_MODEL), lambda i: (0, 0)), vec, vec],
        out_specs=rows, out_shape=jax.ShapeDtypeStruct((t, D_MODEL), F32),
        compiler_params=_params(1), name="mix_out")(
            af, ab, z, bf, bb, z, h, w_out, ln_g.reshape(1, -1), ln_b.reshape(1, -1))


def _ffn_kernel(alpha, h_ref, wg_ref, wu_ref, wo_ref, lng_ref, lnb_ref, o_ref, acc_ref):
    f = pl.program_id(1)

    @pl.when(f == 0)
    def _():
        acc_ref[...] = jnp.zeros_like(acc_ref)

    x = h_ref[...].astype(BF16)
    act = _silu(_dot(x, wg_ref[...])) * _dot(x, wu_ref[...])
    acc_ref[...] += _dot(act, wo_ref[...])

    @pl.when(f == pl.num_programs(1) - 1)
    def _():
        o_ref[...] = _layer_norm(alpha * h_ref[...] + acc_ref[...], lng_ref[...], lnb_ref[...])


def _ffn(alpha, h, wi, wo, ln_g, ln_b):
    t = h.shape[0]
    d_ff = wo.shape[0]
    tm = _pick_tile(t, 1024)
    tf = _pick_tile(d_ff, 1536)
    nf = d_ff // tf
    rows = pl.BlockSpec((tm, D_MODEL), lambda i, f: (i, 0))
    vec = pl.BlockSpec((1, D_MODEL), lambda i, f: (0, 0))
    return pl.pallas_call(
        functools.partial(_ffn_kernel, alpha), grid=(t // tm, nf),
        in_specs=[rows, pl.BlockSpec((D_MODEL, tf), lambda i, f: (0, f)),
                  pl.BlockSpec((D_MODEL, tf), lambda i, f: (0, f + nf)),
                  pl.BlockSpec((tf, D_MODEL), lambda i, f: (f, 0)), vec, vec],
        out_specs=rows, out_shape=jax.ShapeDtypeStruct((t, D_MODEL), F32),
        scratch_shapes=[pltpu.VMEM((tm, D_MODEL), F32)],
        compiler_params=_params(2), name="ffn")(h, wi, wi, wo, ln_g.reshape(1, -1), ln_b.reshape(1, -1))


EXPERT_TILE = 1024
DMA_UNROLL = 8
DMA_PRIORITIES = 2


def _route_kernel(pad_starts, h_ref, router_ref, tri_ref, meta_ref, gate_ref, cnt_ref, carry_ref):
    @pl.when(pl.program_id(0) == 0)
    def _():
        carry_ref[...] = jnp.zeros_like(carry_ref)

    tm = h_ref.shape[0]
    row = pl.program_id(0) * tm + lax.broadcasted_iota(jnp.int32, (tm, 1), 0)
    is_pad = row < 0
    for s in pad_starts:
        is_pad = is_pad | ((row >= s) & (row < s + PAD_ROWS))
    is_token = jnp.logical_not(is_pad)

    logits = _dot_16bit(h_ref[...], router_ref[...])
    lane = lax.broadcasted_iota(jnp.int32, logits.shape, 1)
    logits = jnp.where(lane < N_EXP, logits, -jnp.inf)
    top1 = jnp.max(logits, axis=1, keepdims=True)
    idx1 = jnp.min(jnp.where(logits == top1, lane, LANES), axis=1, keepdims=True)
    rest = jnp.where(lane == idx1, -jnp.inf, logits)
    top2 = jnp.max(rest, axis=1, keepdims=True)
    idx2 = jnp.min(jnp.where(rest == top2, lane, LANES), axis=1, keepdims=True)
    g2 = jnp.exp(top2 - top1)
    denom = 1.0 + g2
    hit1 = lane == idx1
    hit2 = lane == idx2
    both = jnp.where(is_token, jnp.where(hit1, 1.0, 0.0) + jnp.where(hit2, 1.0, 0.0), 0.0)
    prefix = _dot(tri_ref[...], both) + carry_ref[...]
    rank1 = jnp.sum(jnp.where(hit1, prefix, 0.0), axis=1, keepdims=True).astype(jnp.int32)
    rank2 = jnp.sum(jnp.where(hit2, prefix, 0.0), axis=1, keepdims=True).astype(jnp.int32)
    carry_ref[...] += jnp.sum(both, axis=0, keepdims=True)
    cnt_ref[...] = carry_ref[...]
    meta_ref[...] = jnp.where(lane == 0, idx1, jnp.where(lane == 1, idx2, jnp.where(
        lane == 2, rank1, jnp.where(lane == 3, rank2, jnp.where(is_token & (lane == 4), 1, 0)))))
    gate_ref[...] = jnp.where(lane == 0, 1.0 / denom, jnp.where(lane == 1, g2 / denom, 0.0))


def _route(h, router, pad_starts):
    t = h.shape[0]
    tm = _pick_tile(t, 512)
    router_p = _stack_weight_terms(jnp.pad(router, ((0, 0), (0, LANES - router.shape[1]))))
    pos = np.arange(tm)
    tri = jnp.asarray(pos[:, None] > pos[None, :], BF16)
    rows = lambda w: pl.BlockSpec((tm, w), lambda i: (i, 0))
    const = lambda a: pl.BlockSpec(a.shape, lambda i: (0, 0))
    return pl.pallas_call(
        functools.partial(_route_kernel, pad_starts), grid=(t // tm,),
        in_specs=[rows(D_MODEL), const(router_p), const(tri)],
        out_specs=[rows(LANES), rows(LANES), pl.BlockSpec((1, LANES), lambda i: (0, 0))],
        out_shape=[jax.ShapeDtypeStruct((t, LANES), jnp.int32), jax.ShapeDtypeStruct((t, LANES), F32),
                   jax.ShapeDtypeStruct((1, LANES), F32)],
        scratch_shapes=[pltpu.VMEM((1, LANES), F32)],
        compiler_params=_params(1), name="route")(h, router_p, tri)


def _dispatch_plan(meta, counts, t):
    cnt = counts[0, :N_EXP].astype(jnp.int32)
    padded = ((cnt + EXPERT_TILE - 1) // EXPERT_TILE) * EXPERT_TILE
    ends = jnp.cumsum(padded)
    off = ends - padded

    def base(e):
        return sum(jnp.where(e == i, off[i], 0) for i in range(N_EXP))

    is_token = meta[:, 4] > 0
    n_tiles = -(-2 * t // EXPERT_TILE) + N_EXP
    spare = n_tiles * EXPERT_TILE + 2 * (jnp.cumsum(jnp.logical_not(is_token).astype(jnp.int32)) - 1)
    pos1 = (base(meta[:, 0]) + meta[:, 2]).astype(jnp.int32)
    pos2 = (base(meta[:, 1]) + meta[:, 3]).astype(jnp.int32)
    scatter = (jnp.where(is_token, pos1, spare), jnp.where(is_token, pos2, spare + 1))
    gather = (jnp.where(is_token, pos1, 0), jnp.where(is_token, pos2, 0))
    starts = jnp.arange(n_tiles, dtype=jnp.int32) * EXPERT_TILE
    tile_expert = jnp.minimum(jnp.sum(starts[:, None] >= ends[None, :], axis=1), N_EXP - 1).astype(jnp.int32)
    n_active = (ends[-1] // EXPERT_TILE).astype(jnp.int32).reshape(1)
    return ends.astype(jnp.int32), scatter, gather, tile_expert, n_active, n_tiles


assert D_MODEL == SUBLANES * LANES


def _store_token_tiles(ref, x):
    n = x.shape[0]
    for s in range(SUBLANES):
        ref[pl.ds(s, n, stride=SUBLANES), :] = x[:, s * LANES:(s + 1) * LANES]


def _load_token_tiles(ref, n):
    return jnp.concatenate([ref[pl.ds(s, n, stride=SUBLANES), :] for s in range(SUBLANES)], axis=1)


def _tile_rows(i):
    return pl.ds(pl.multiple_of(i * SUBLANES, SUBLANES), SUBLANES)


def _dispatch_kernel(n_tiles, ends_ref, pos1_ref, pos2_ref, h_ref, xs_ref, tok_ref, zero_ref, sem, zero_sem):
    n = h_ref.shape[0]

    @pl.when(pl.program_id(0) == 0)
    def _():
        zero_ref[...] = jnp.zeros_like(zero_ref)

        def clear_last_tile(e):
            first_row = pl.multiple_of((ends_ref[e] - EXPERT_TILE) * SUBLANES, SUBLANES)
            return pltpu.make_async_copy(zero_ref, xs_ref.at[pl.ds(first_row, EXPERT_TILE * SUBLANES)], zero_sem)

        def has_tiles(e):
            return ends_ref[e] > (ends_ref[e - 1] if e else 0)

        for e in range(N_EXP):
            pl.when(has_tiles(e))(lambda e=e: clear_last_tile(e).start())
        for e in range(N_EXP):
            pl.when(has_tiles(e))(lambda e=e: clear_last_tile(e).wait())

        def clear_tile(i):
            first_row = pl.multiple_of(i * (EXPERT_TILE * SUBLANES), SUBLANES)
            return pltpu.make_async_copy(zero_ref, xs_ref.at[pl.ds(first_row, EXPERT_TILE * SUBLANES)], zero_sem)

        used = ends_ref[N_EXP - 1] // EXPERT_TILE
        lax.fori_loop(used, n_tiles, lambda i, c: (clear_tile(i).start(), c)[1], 0)
        lax.fori_loop(used, n_tiles, lambda i, c: (clear_tile(i).wait(), c)[1], 0)

    _store_token_tiles(tok_ref, h_ref[...])

    def tile_copy(r, p):
        return pltpu.make_async_copy(tok_ref.at[_tile_rows(r)], xs_ref.at[_tile_rows(p)], sem)

    def start(pair, c):
        for prio in range(DMA_PRIORITIES):
            r = pair * DMA_PRIORITIES + prio
            tile_copy(r, pos1_ref[0, 0, r]).start(priority=prio)
            tile_copy(r, pos2_ref[0, 0, r]).start(priority=prio)
        return c

    lax.fori_loop(0, n // DMA_PRIORITIES, start, 0, unroll=DMA_UNROLL // DMA_PRIORITIES)
    all_rows = pltpu.make_async_copy(tok_ref, xs_ref.at[pl.ds(0, n * SUBLANES)], sem)
    all_rows.wait()
    all_rows.wait()


def _dispatch(h, ends, pos1, pos2, n_tiles, n_spare):
    t = h.shape[0]
    n_rows = n_tiles * EXPERT_TILE + n_spare
    tm = _pick_tile(t, 512)
    idx = lambda: pl.BlockSpec((1, 1, tm), lambda i, ends: (i, 0, 0), memory_space=pltpu.SMEM)
    grid_spec = pltpu.PrefetchScalarGridSpec(
        num_scalar_prefetch=1, grid=(t // tm,),
        in_specs=[idx(), idx(), pl.BlockSpec((tm, D_MODEL), lambda i, ends: (i, 0))],
        out_specs=pl.BlockSpec(memory_space=pl.ANY),
        scratch_shapes=[pltpu.VMEM((tm * SUBLANES, LANES), F32),
                        pltpu.VMEM((EXPERT_TILE * SUBLANES, LANES), F32),
                        pltpu.SemaphoreType.DMA(()), pltpu.SemaphoreType.DMA(())])
    return pl.pallas_call(
        functools.partial(_dispatch_kernel, n_tiles), grid_spec=grid_spec,
        out_shape=jax.ShapeDtypeStruct((n_rows * SUBLANES, LANES), F32),
        compiler_params=_params(1), name="dispatch")(
            ends, pos1.reshape(t // tm, 1, tm), pos2.reshape(t // tm, 1, tm), h)


def _experts_kernel(te_ref, na_ref, x_ref, wg_ref, wu_ref, wo_ref, y_ref, xb_ref, acc_ref):
    i = pl.program_id(0)
    f = pl.program_id(1)
    last = f == pl.num_programs(1) - 1
    active = i < na_ref[0]

    @pl.when(active)
    def _():
        @pl.when(f == 0)
        def _():
            acc_ref[...] = jnp.zeros_like(acc_ref)
            xb_ref[...] = _load_token_tiles(x_ref, EXPERT_TILE).astype(BF16)

        x = xb_ref[...]
        act = _silu(_dot(x, wg_ref[0])) * _dot(x, wu_ref[0])
        acc_ref[...] += _dot(act, wo_ref[0])

        @pl.when(last)
        def _():
            _store_token_tiles(y_ref, acc_ref[...])

    @pl.when(jnp.logical_not(active) & last)
    def _():
        y_ref[...] = jnp.zeros_like(y_ref)


def _experts(xs, tile_expert, n_active, wi, wo, n_tiles):
    n_rows = n_tiles * EXPERT_TILE
    e_ff = wo.shape[1]
    tf = _pick_tile(e_ff, 512)
    nf = e_ff // tf
    rows = pl.BlockSpec((EXPERT_TILE * SUBLANES, LANES), lambda i, f, te, na: (i, 0))
    rows_in = pl.BlockSpec((EXPERT_TILE * SUBLANES, LANES),
                           lambda i, f, te, na: (jnp.minimum(i, jnp.maximum(na[0] - 1, 0)), 0))
    grid_spec = pltpu.PrefetchScalarGridSpec(
        num_scalar_prefetch=2, grid=(n_rows // EXPERT_TILE, nf),
        in_specs=[rows_in, pl.BlockSpec((1, D_MODEL, tf), lambda i, f, te, na: (te[i], 0, f)),
                  pl.BlockSpec((1, D_MODEL, tf), lambda i, f, te, na: (te[i], 0, f + nf)),
                  pl.BlockSpec((1, tf, D_MODEL), lambda i, f, te, na: (te[i], f, 0))],
        out_specs=rows,
        scratch_shapes=[pltpu.VMEM((EXPERT_TILE, D_MODEL), BF16), pltpu.VMEM((EXPERT_TILE, D_MODEL), F32)])
    return pl.pallas_call(
        _experts_kernel, grid_spec=grid_spec, out_shape=jax.ShapeDtypeStruct((n_rows * SUBLANES, LANES), F32),
        compiler_params=_params(2), name="experts")(tile_expert, n_active, xs, wi, wi, wo)


def _combine_rows(alpha, pos1_ref, pos2_ref, h_ref, gate_ref, ys_ref, lng_ref, lnb_ref, a_ref, b_ref, sems):
    n = h_ref.shape[0]

    def tile_copy(p, buf, r, which):
        return pltpu.make_async_copy(ys_ref.at[_tile_rows(p)], buf.at[_tile_rows(r)], sems.at[which])

    def start(pair, c):
        for prio in range(DMA_PRIORITIES):
            r = pair * DMA_PRIORITIES + prio
            tile_copy(pos1_ref[0, 0, r], a_ref, r, 0).start(priority=prio)
            tile_copy(pos2_ref[0, 0, r], b_ref, r, 1).start(priority=prio)
        return c

    lax.fori_loop(0, n // DMA_PRIORITIES, start, 0, unroll=DMA_UNROLL // DMA_PRIORITIES)
    for which, buf in enumerate((a_ref, b_ref)):
        pltpu.make_async_copy(ys_ref.at[pl.ds(0, n * SUBLANES)], buf, sems.at[which]).wait()
    gate = gate_ref[...]
    y = gate[:, 0:1] * _load_token_tiles(a_ref, n) + gate[:, 1:2] * _load_token_tiles(b_ref, n)
    return _layer_norm(alpha * h_ref[...] + y, lng_ref[...], lnb_ref[...])


def _combine_kernel(alpha, *refs):
    refs = list(refs)
    o_ref = refs.pop(7)
    o_ref[...] = _combine_rows(alpha, *refs)


def _combine_scratch(tm):
    return [pltpu.VMEM((tm * SUBLANES, LANES), F32), pltpu.VMEM((tm * SUBLANES, LANES), F32),
            pltpu.SemaphoreType.DMA((2,))]


def _combine(alpha, h, gates, ys, pos1, pos2, ln_g, ln_b):
    t = h.shape[0]
    tm = _pick_tile(t, 512)
    idx = lambda: pl.BlockSpec((1, 1, tm), lambda i: (i, 0, 0), memory_space=pltpu.SMEM)
    rows = pl.BlockSpec((tm, D_MODEL), lambda i: (i, 0))
    vec = pl.BlockSpec((1, D_MODEL), lambda i: (0, 0))
    return pl.pallas_call(
        functools.partial(_combine_kernel, alpha), grid=(t // tm,),
        in_specs=[idx(), idx(), rows, pl.BlockSpec((tm, LANES), lambda i: (i, 0)),
                  pl.BlockSpec(memory_space=pl.ANY), vec, vec],
        out_specs=rows, out_shape=jax.ShapeDtypeStruct((t, D_MODEL), F32),
        scratch_shapes=_combine_scratch(tm),
        compiler_params=_params(1), name="combine")(
            pos1.reshape(t // tm, 1, tm), pos2.reshape(t // tm, 1, tm), h, gates, ys,
            ln_g.reshape(1, -1), ln_b.reshape(1, -1))


def _output_tables(seq_shapes):
    kind, live = [], []
    window = [[], []]
    recent = [(0, 0), (0, 0)]
    for grp, (b, l) in enumerate(seq_shapes):
        for i in range(b):
            for c in range(l // CH + 1):
                kind.append(grp)
                live.append(int(c > 0))
                recent[grp] = (i, max(c - 1, 0))
                for g2 in range(2):
                    window[g2].append(recent[g2])
    tables = [kind, live] + [[w[j] for w in window[g2]] for g2 in range(2) for j in range(2)]
    return [jnp.asarray(np.asarray(x, np.int32)) for x in tables]


def _combine_out_kernel(alpha, kind_ref, live_ref, ab_ref, ac_ref, bb_ref, bc_ref, *refs):
    refs = list(refs)
    outs = [refs.pop(7), refs.pop(7)]
    g = pl.program_id(0)

    @pl.when(live_ref[g] == 1)
    def _():
        res = _combine_rows(alpha, *refs)
        for grp in range(2):
            @pl.when(kind_ref[g] == grp)
            def _(grp=grp):
                outs[grp][0] = res


def _combine_out(alpha, h, gates, ys, pos1, pos2, ln_g, ln_b, seq_shapes):
    t = h.shape[0]
    n = t // CH
    tables = _output_tables(seq_shapes)
    idx = lambda: pl.BlockSpec((1, 1, CH), lambda g, *tb: (g, 0, 0), memory_space=pltpu.SMEM)
    rows = lambda w: pl.BlockSpec((CH, w), lambda g, *tb: (g, 0))
    vec = pl.BlockSpec((1, D_MODEL), lambda g, *tb: (0, 0))
    out = lambda grp: pl.BlockSpec((1, CH, D_MODEL),
                                   lambda g, *tb: (tb[2 + 2 * grp][g], tb[3 + 2 * grp][g], 0))
    grid_spec = pltpu.PrefetchScalarGridSpec(
        num_scalar_prefetch=len(tables), grid=(n,),
        in_specs=[idx(), idx(), rows(D_MODEL), rows(LANES), pl.BlockSpec(memory_space=pl.ANY), vec, vec],
        out_specs=[out(0), out(1)], scratch_shapes=_combine_scratch(CH))
    return pl.pallas_call(
        functools.partial(_combine_out_kernel, alpha), grid_spec=grid_spec,
        out_shape=[jax.ShapeDtypeStruct((b, l, D_MODEL), F32) for b, l in seq_shapes],
        compiler_params=_params(1), name="combine_out")(
            *tables, pos1.reshape(n, 1, CH), pos2.reshape(n, 1, CH), h, gates, ys,
            ln_g.reshape(1, -1), ln_b.reshape(1, -1))


def _moe(alpha, h, router, wi, wo, ln_g, ln_b, pad_starts, out_shapes=None):
    t = h.shape[0]
    meta, gates, counts = _route(h, router, pad_starts)
    ends, scatter, gather, tile_expert, n_active, n_tiles = _dispatch_plan(meta, counts, t)
    n_spare = 2 * len(pad_starts) * PAD_ROWS
    xs = _dispatch(h, ends, scatter[0], scatter[1], n_tiles, n_spare)
    ys = _experts(xs, tile_expert, n_active, wi, wo, n_tiles)
    if out_shapes is None:
        return _combine(alpha, h, gates, ys, gather[0], gather[1], ln_g, ln_b)
    return _combine_out(alpha, h, gates, ys, gather[0], gather[1], ln_g, ln_b, out_shapes)


def kernel(x_prompt, x_sample, meta, e_w_in, e_conv_w, e_conv_b, e_lru_wa, e_lru_ba, e_lru_wx, e_lru_bx,
           e_lru_lambda, e_w_out, e_ffn_wi, e_ffn_wo, o_w_in, o_hg_lower, o_ml_bi, o_ml_bf, o_w_out, o_router,
           o_exp_wi, o_exp_wo, ln_g, ln_b):
    groups = (x_prompt, x_sample)
    depth = ln_g.shape[0]
    alpha = (2.0 * depth) ** 0.25
    seq_shapes = [(x.shape[0], x.shape[1]) for x in groups]
    for _, l in seq_shapes:
        assert l % CH == 0
    cpos_np, clast_np = _chunk_tables(seq_shapes)
    cpos, clast = jnp.asarray(cpos_np), jnp.asarray(clast_np)
    max_rows = max(l for _, l in seq_shapes) + CH
    pad_starts = tuple(int(i) * CH for i in np.flatnonzero(cpos_np == 0))

    head = jnp.concatenate([jnp.zeros((PAD_ROWS, D_MODEL), F32), meta.astype(F32)], axis=0)
    parts = []
    for x in groups:
        full = jnp.concatenate([jnp.broadcast_to(head[None], (x.shape[0], CH, D_MODEL)), x], axis=1)
        parts.append(full.reshape(-1, D_MODEL))
    h = jnp.concatenate(parts, axis=0)
    t = h.shape[0]

    n_odd_cols = 9 * GROUP_W
    for layer in range(depth):
        p = layer // 2
        if layer % 2 == 0:
            z = _project(h, e_w_in[p].astype(BF16))
            (ret_f, ret_b), (lru_f, lru_b) = _chunk_walk(
                [_retention(z, cpos, clast, max_rows),
                 _rglru(z, 4, cpos, clast, e_conv_w[p], e_conv_b[p], e_lru_wa[p], e_lru_ba[p],
                        e_lru_wx[p], e_lru_bx[p], e_lru_lambda[p])], cpos, clast, t, "even_mixers")
            h = _mix_out(alpha, True, ret_f, ret_b, lru_f, lru_b, z, 3, 5, h, e_w_out[p].astype(BF16),
                         ln_g[layer, 0], ln_b[layer, 0])
            h = _ffn(alpha, h, e_ffn_wi[p].astype(BF16), e_ffn_wo[p].astype(BF16),
                     ln_g[layer, 1], ln_b[layer, 1])
        else:
            w_in = o_w_in[p]
            n_gate = 2 * N_HEADS
            gate_pad = ((0, 0), (0, LANES - n_gate))
            w_gates = jnp.concatenate([jnp.pad(w_in[:, n_odd_cols:n_odd_cols + n_gate], gate_pad),
                                       jnp.pad(w_in[:, n_odd_cols + n_gate:], gate_pad)], axis=1)
            z, gates = _project(h, w_in[:, :n_odd_cols].astype(BF16), w_gates)
            (hg_f, hg_b), (ml_f, ml_b) = _chunk_walk(
                [_hgrn2(z, cpos, clast, o_hg_lower, layer),
                 _mlstm(z, gates, cpos, clast, o_ml_bi[p], o_ml_bf[p])], cpos, clast, t, "odd_mixers")
            h = _mix_out(alpha, False, hg_f, hg_b, ml_f, ml_b, z, 4, 8, h, o_w_out[p].astype(BF16),
                         ln_g[layer, 0], ln_b[layer, 0])
            h = _moe(alpha, h, o_router[p], o_exp_wi[p].astype(BF16), o_exp_wo[p].astype(BF16),
                     ln_g[layer, 1], ln_b[layer, 1], pad_starts,
                     seq_shapes if layer == depth - 1 else None)
            if layer == depth - 1:
                return tuple(h)

    outs = []
    row = 0
    for b, l in seq_shapes:
        n = b * (l + CH)
        outs.append(h[row:row + n].reshape(b, l + CH, D_MODEL)[:, CH:])
        row += n
    return tuple(outs)
```

```python
import functools
import math

import numpy as np
import jax
import jax.numpy as jnp
from jax import lax
from jax.experimental import pallas as pl
from jax.experimental.pallas import tpu as pltpu

F32 = jnp.float32
BF16 = jnp.bfloat16

D_MODEL = 1024
GROUP_W = D_MODEL // 2
N_HEADS = 4
HEAD_D = GROUP_W // N_HEADS
N_META = 16
ROPE_BASE = 10000.0
LRU_BLOCKS = 8
LRU_BW = GROUP_W // LRU_BLOCKS
LRU_C = 8.0
N_EXP = 8
EPS = 1e-5

LANES = 128
SUBLANES = 8
CH = 128
PAD_ROWS = CH - N_META
SUB = SUBLANES
GRP = 4 * SUB
NEG_BIG = -1e30
LOG2_E = math.log2(math.e)
VMEM_LIMIT = 56 * 1024 * 1024


def _dot(a, b):
    return jnp.dot(a.astype(BF16), b.astype(BF16), preferred_element_type=F32)


def _dot_nt(a, b):
    return lax.dot_general(a.astype(BF16), b.astype(BF16), (((1,), (1,)), ((), ())),
                           preferred_element_type=F32)


def _dot_tn(a, b):
    return _dot(a.T, b)


def _bf16_terms(x, n_terms):
    terms = []
    for _ in range(n_terms):
        t = x.astype(BF16)
        terms.append(t)
        x = x - t.astype(F32)
    return terms


def _stack_weight_terms(w):
    hi, lo = _bf16_terms(w, 2)
    return jnp.concatenate([hi, lo, hi], axis=0)


def _dot_16bit(x, w_terms):
    hi, lo = _bf16_terms(x, 2)
    return jnp.dot(jnp.concatenate([hi, hi, lo], axis=1), w_terms, preferred_element_type=F32)


PREFIX_TERMS = 3
SPREAD_TERMS = 2


def _prefix_dot(op_tiled, x):
    return jnp.dot(op_tiled, jnp.concatenate(_bf16_terms(x, PREFIX_TERMS), axis=0),
                   preferred_element_type=F32)


def _spread_dot(x, op_tiled):
    return jnp.dot(jnp.concatenate(_bf16_terms(x, SPREAD_TERMS), axis=1), op_tiled,
                   preferred_element_type=F32)


def _sigmoid(x):
    return 1.0 / (1.0 + jnp.exp(-x))


def _silu(x):
    return x * _sigmoid(x)


def _gelu_tanh(x):
    return 0.5 * x * (1.0 + jnp.tanh(math.sqrt(2.0 / math.pi) * (x + 0.044715 * (x * x * x))))


def _softplus(x):
    return jnp.maximum(x, 0.0) + jnp.log(1.0 + jnp.exp(-jnp.abs(x)))


def _pick_tile(total, target):
    best = LANES
    for t in range(LANES, min(total, target) + 1, LANES):
        if total % t == 0:
            best = t
    return best


def _params(n_axes, sem="arbitrary"):
    return pltpu.CompilerParams(dimension_semantics=(sem,) * n_axes, vmem_limit_bytes=VMEM_LIMIT)


def _proj_kernel(x_ref, w_ref, o_ref):
    o_ref[...] = _dot(x_ref[...], w_ref[...])


def _proj_gates_kernel(x_ref, w_ref, wg_ref, o_ref, g_ref):
    g_ref[...] = _dot_16bit(x_ref[...], wg_ref[...])
    o_ref[...] = _dot(x_ref[...], w_ref[...])


PROJ_OUT_BLOCK_BYTES = 8 * 1024 * 1024


def _project(x, w, w_gates=None):
    t, k = x.shape
    n = w.shape[1]
    tm = _pick_tile(t, PROJ_OUT_BLOCK_BYTES // (4 * n))
    x_spec = pl.BlockSpec((tm, k), lambda i: (i, 0))
    w_spec = pl.BlockSpec((k, n), lambda i: (0, 0))
    o_spec = pl.BlockSpec((tm, n), lambda i: (i, 0))
    if w_gates is None:
        return pl.pallas_call(
            _proj_kernel, grid=(t // tm,), in_specs=[x_spec, w_spec], out_specs=o_spec,
            out_shape=jax.ShapeDtypeStruct((t, n), F32), compiler_params=_params(1),
            name="proj")(x, w)
    ng = w_gates.shape[1]
    w_gates = _stack_weight_terms(w_gates)
    return pl.pallas_call(
        _proj_gates_kernel, grid=(t // tm,),
        in_specs=[x_spec, w_spec, pl.BlockSpec(w_gates.shape, lambda i: (0, 0))],
        out_specs=[o_spec, pl.BlockSpec((tm, ng), lambda i: (i, 0))],
        out_shape=[jax.ShapeDtypeStruct((t, n), F32), jax.ShapeDtypeStruct((t, ng), F32)],
        compiler_params=_params(1), name="proj_gates")(x, w, w_gates)


def _chunk_tables(seq_shapes):
    cpos, clast = [], []
    for b, l in seq_shapes:
        n = l // CH + 1
        for _ in range(b):
            cpos += list(range(n))
            clast += [0] * (n - 1) + [1]
    return np.asarray(cpos, np.int32), np.asarray(clast, np.int32)


def _fwd_map(col):
    return lambda g, cp, cl: (g, col)


def _bwd_map(n_chunks, col):
    return lambda g, cp, cl: (n_chunks - 1 - g, col)


def _walk_kernel(bodies, n_inputs, n_scratch, cpos_ref, clast_ref, *refs):
    refs = list(refs)
    ins = [[refs.pop(0) for _ in range(k)] for k in n_inputs]
    outs = [[refs.pop(0) for _ in range(2)] for _ in bodies]

    @pl.when(pl.program_id(0) == 0)
    def _():
        for r in refs:
            r[...] = jnp.zeros_like(r)

    for body, i, o, k in zip(bodies, ins, outs, n_scratch):
        body(cpos_ref, clast_ref, *i, *o, *[refs.pop(0) for _ in range(k)])


def _chunk_walk(parts, cpos, clast, t, name):
    n = t // CH
    blk = (CH, GROUP_W)
    bodies = [p[0] for p in parts]
    grid_spec = pltpu.PrefetchScalarGridSpec(
        num_scalar_prefetch=2, grid=(n,), in_specs=[s for p in parts for s in p[1]],
        out_specs=[pl.BlockSpec(blk, _fwd_map(0)), pl.BlockSpec(blk, _bwd_map(n, 0))] * len(parts),
        scratch_shapes=[s for p in parts for s in p[3]])
    outs = pl.pallas_call(
        functools.partial(_walk_kernel, bodies, [len(p[1]) for p in parts], [len(p[3]) for p in parts]),
        grid_spec=grid_spec, out_shape=[jax.ShapeDtypeStruct((t, GROUP_W), BF16)] * (2 * len(parts)),
        compiler_params=_params(1), name=name)(cpos, clast, *[a for p in parts for a in p[2]])
    return [outs[2 * i:2 * i + 2] for i in range(len(parts))]


def _row_valid(first):
    row = lax.broadcasted_iota(jnp.int32, (CH, 1), 0)
    return row >= jnp.where(first, PAD_ROWS, 0)


def _ret_log_gamma():
    return np.log1p(-np.exp2(-5.0 - np.arange(N_HEADS, dtype=np.float64)))


def _ret_consts():
    lg = _ret_log_gamma()
    pos = np.arange(CH, dtype=np.float64)
    rel = pos[:, None] - pos[None, :]
    dmat = np.zeros((2, N_HEADS, CH, CH), np.float64)
    rows = np.zeros((4, CH, GROUP_W), np.float64)
    for h in range(N_HEADS):
        dmat[0, h] = np.where(rel >= 0, np.exp(np.maximum(rel, 0.0) * lg[h]), 0.0)
        dmat[1, h] = np.where(rel < 0, np.exp(np.maximum(-rel, 0.0) * lg[h]), 0.0)
        sl = slice(h * HEAD_D, (h + 1) * HEAD_D)
        rows[0, :, sl] = np.exp((pos + 1.0) * lg[h])[:, None]
        rows[1, :, sl] = np.exp((CH - 1.0 - pos) * lg[h])[:, None]
        rows[2, :, sl] = np.exp((CH - pos) * lg[h])[:, None]
        rows[3, :, sl] = np.exp(pos * lg[h])[:, None]
    return jnp.asarray(dmat, F32), jnp.asarray(rows, F32)


def _ret_direction(q_ref, k_ref, v_ref, cos_ref, sin_ref, dmat_ref, rin_ref, rout_ref, s_ref, o_ref,
                   reset, first):
    valid = _row_valid(first)
    cos = cos_ref[...]
    sin = sin_ref[...]
    chunk_decay = np.exp(CH * _ret_log_gamma())
    for h in range(N_HEADS):
        sl = slice(h * HEAD_D, (h + 1) * HEAD_D)
        q = q_ref[:, sl]
        k = k_ref[:, sl]
        v = v_ref[:, sl]
        q = q * cos + pltpu.roll(q, HEAD_D // 2, 1) * sin
        k = (k * cos + pltpu.roll(k, HEAD_D // 2, 1) * sin) * (HEAD_D ** -0.5)
        k = jnp.where(valid, k, 0.0)
        scores = _dot_nt(q, k) * dmat_ref[h]
        state = jnp.where(reset, 0.0, s_ref[h])
        o_ref[:, sl] = _dot(jnp.concatenate([scores, q * rin_ref[:, sl]], axis=1),
                            jnp.concatenate([v, state], axis=0)).astype(o_ref.dtype)
        s_ref[h] = float(chunk_decay[h]) * state + _dot_tn(k * rout_ref[:, sl], v)


def _ret_kernel(cpos_ref, clast_ref, qf, kf, vf, cosf, sinf, qb, kb, vb, cosb, sinb, dmat_ref, rows_ref,
                of_ref, ob_ref, sf_ref, sb_ref):
    g = pl.program_id(0)
    gb = pl.num_programs(0) - 1 - g
    _ret_direction(qf, kf, vf, cosf, sinf, dmat_ref.at[0], rows_ref.at[0], rows_ref.at[1], sf_ref, of_ref,
                   cpos_ref[g] == 0, cpos_ref[g] == 0)
    _ret_direction(qb, kb, vb, cosb, sinb, dmat_ref.at[1], rows_ref.at[2], rows_ref.at[3], sb_ref, ob_ref,
                   clast_ref[gb] == 1, cpos_ref[gb] == 0)


def _rope_tables(n_rows):
    inv = ROPE_BASE ** (-jnp.arange(0, HEAD_D, 2, dtype=jnp.float32) / HEAD_D)
    pos = jnp.maximum(jnp.arange(n_rows, dtype=jnp.float32) - PAD_ROWS, 0.0)
    ang = pos[:, None] * inv[None, :]
    cos, sin = jnp.cos(ang), jnp.sin(ang)
    return jnp.concatenate([cos, cos], axis=1), jnp.concatenate([-sin, sin], axis=1)


def _retention(z, cpos, clast, max_rows):
    t = z.shape[0]
    n = t // CH
    cos2, sin2 = _rope_tables(max_rows)
    dmat, rows = _ret_consts()
    blk = (CH, GROUP_W)
    in_specs = []
    for mk, pm in ((_fwd_map, lambda g, cp, cl: (cp[g], 0)),
                   (functools.partial(_bwd_map, n), lambda g, cp, cl: (cp[n - 1 - g], 0))):
        in_specs += [pl.BlockSpec(blk, mk(0)), pl.BlockSpec(blk, mk(1)), pl.BlockSpec(blk, mk(2)),
                     pl.BlockSpec((CH, HEAD_D), pm), pl.BlockSpec((CH, HEAD_D), pm)]
    in_specs += [pl.BlockSpec(dmat.shape, lambda g, cp, cl: (0, 0, 0, 0)),
                 pl.BlockSpec(rows.shape, lambda g, cp, cl: (0, 0, 0))]
    return (_ret_kernel, in_specs, [z, z, z, cos2, sin2, z, z, z, cos2, sin2, dmat, rows],
            [pltpu.VMEM((N_HEADS, HEAD_D, HEAD_D), F32)] * 2)


HALO = SUBLANES


def _shift_rows(x, s, fill, reverse):
    row = lax.broadcasted_iota(jnp.int32, (CH, 1), 0)
    if reverse:
        return jnp.where(row < CH - s, pltpu.roll(x, CH - s, 0), fill)
    return jnp.where(row >= s, pltpu.roll(x, s, 0), fill)


def _lru_direction(x_ref, prev_ref, next_ref, convw_ref, convb_ref, wg_ref, bias_ref, lam_ref, ext_ref,
                   carry_ref, o_ref, reset, first, last, reverse):
    valid = _row_valid(first)
    ext_ref[HALO:HALO + CH, :] = jnp.where(valid, x_ref[...], 0.0)
    ext_ref[0:HALO, :] = jnp.where(first, 0.0, prev_ref[...])
    ext_ref[HALO + CH:, :] = jnp.where(last, 0.0, next_ref[...])
    xc = convb_ref[...] + ext_ref[HALO - 2:HALO - 2 + CH, :] * convw_ref[0:1, :]
    for tap in range(1, 4):
        xc = xc + ext_ref[HALO - 2 + tap:HALO - 2 + tap + CH, :] * convw_ref[tap:tap + 1, :]

    log_sig_lam = -_softplus(-lam_ref[...])
    parts_a, parts_u = [], []
    for grp in range(GROUP_W // LANES):
        sl = slice(grp * LANES, (grp + 1) * LANES)
        xg = xc[:, sl]
        pre = _dot(xg, wg_ref[grp])
        r = _sigmoid(pre[:, :LANES] + bias_ref[0:1, sl])
        i = _sigmoid(pre[:, LANES:] + bias_ref[1:2, sl])
        a = jnp.exp(LRU_C * log_sig_lam[:, sl] * r)
        u = jnp.sqrt(1.0 - a * a) * (i * xg)
        parts_a.append(a)
        parts_u.append(jnp.where(valid, u, 0.0))
    a = jnp.concatenate(parts_a, axis=1)
    u = jnp.concatenate(parts_u, axis=1)

    n_blk = CH // SUBLANES
    a = a.reshape(n_blk, SUBLANES, GROUP_W)
    u = u.reshape(n_blk, SUBLANES, GROUP_W)
    row_in_blk = lax.broadcasted_iota(jnp.int32, (1, SUBLANES, 1), 1)
    s = 1
    while s < SUBLANES:
        keep = (row_in_blk < SUBLANES - s) if reverse else (row_in_blk >= s)
        shift = SUBLANES - s if reverse else s
        u = u + a * jnp.where(keep, pltpu.roll(u, shift, 1), 0.0)
        a = a * jnp.where(keep, pltpu.roll(a, shift, 1), 1.0)
        s *= 2
    edge = 0 if reverse else SUBLANES - 1
    carry = jnp.where(reset, 0.0, carry_ref[...])
    blocks = [None] * n_blk
    for blk in (reversed(range(n_blk)) if reverse else range(n_blk)):
        blocks[blk] = u[blk] + a[blk] * carry
        carry = blocks[blk][edge:edge + 1, :]
    o_ref[...] = jnp.concatenate(blocks, axis=0).astype(o_ref.dtype)
    carry_ref[...] = carry


def _lru_kernel(cpos_ref, clast_ref, xf, pf, nf, xb, pb, nb, convw_ref, convb_ref, wg_ref, bias_ref, lam_ref,
                of_ref, ob_ref, extf_ref, extb_ref, cf_ref, cb_ref):
    g = pl.program_id(0)
    gb = pl.num_programs(0) - 1 - g
    _lru_direction(xf, pf, nf, convw_ref, convb_ref, wg_ref.at[0], bias_ref.at[0], lam_ref.at[0], extf_ref,
                   cf_ref, of_ref, cpos_ref[g] == 0, cpos_ref[g] == 0, clast_ref[g] == 1, False)
    _lru_direction(xb, pb, nb, convw_ref, convb_ref, wg_ref.at[1], bias_ref.at[1], lam_ref.at[1], extb_ref,
                   cb_ref, ob_ref, clast_ref[gb] == 1, cpos_ref[gb] == 0, clast_ref[gb] == 1, True)


def _lru_gate_weights(wa, wx):
    per = LANES // LRU_BW

    def block_diag(w):
        w = w.reshape(2, LRU_BLOCKS // per, per, LRU_BW, LRU_BW)
        eye = jnp.eye(per, dtype=w.dtype)
        return jnp.einsum("dgpij,pq->dgpiqj", w, eye).reshape(2, LRU_BLOCKS // per, LANES, LANES)

    return jnp.concatenate([block_diag(wa), block_diag(wx)], axis=-1).astype(BF16)


def _rglru(z, col, cpos, clast, conv_w, conv_b, wa, ba, wx, bx, lam):
    t = z.shape[0]
    n = t // CH
    per = CH // HALO
    n_halo = t // HALO
    blk = (CH, GROUP_W)
    hblk = (HALO, GROUP_W)
    wg = _lru_gate_weights(wa, wx)
    bias = jnp.stack([ba, bx], axis=1)
    lam = lam.reshape(2, 1, GROUP_W)

    def prev_f(g, cp, cl):
        return (jnp.maximum(g * per - 1, 0), col)

    def next_f(g, cp, cl):
        return (jnp.minimum((g + 1) * per, n_halo - 1), col)

    def prev_b(g, cp, cl):
        return (jnp.maximum((n - 1 - g) * per - 1, 0), col)

    def next_b(g, cp, cl):
        return (jnp.minimum((n - g) * per, n_halo - 1), col)

    full = lambda a: pl.BlockSpec(a.shape, lambda g, cp, cl: (0,) * a.ndim)
    conv_b2 = conv_b.reshape(1, GROUP_W)
    in_specs = [pl.BlockSpec(blk, _fwd_map(col)), pl.BlockSpec(hblk, prev_f), pl.BlockSpec(hblk, next_f),
                pl.BlockSpec(blk, _bwd_map(n, col)), pl.BlockSpec(hblk, prev_b), pl.BlockSpec(hblk, next_b),
                full(conv_w), full(conv_b2), full(wg), full(bias), full(lam)]
    return (_lru_kernel, in_specs, [z, z, z, z, z, z, conv_w, conv_b2, wg, bias, lam],
            [pltpu.VMEM((CH + 2 * HALO, GROUP_W), F32)] * 2 + [pltpu.VMEM((1, GROUP_W), F32)] * 2)


def _tri_consts():
    pos = np.arange(CH)
    lower = (pos[:, None] >= pos[None, :]).astype(np.float32)
    ops = np.stack([lower, lower.T])
    return jnp.asarray(np.tile(ops, (1, 1, PREFIX_TERMS)), BF16)


def _hg_masks():
    pos = np.arange(CH)
    same_blk = pos[:, None] // SUB == pos[None, :] // SUB
    same_grp = pos[:, None] // GRP == pos[None, :] // GRP
    return jnp.asarray(np.stack([same_blk, same_grp]).astype(np.float32))


def _hg_direction(q_ref, f_ref, v_ref, lb, cum_ref, mask_ref, st_ref, o_ref, reset, first, reverse):
    valid = _row_valid(first)
    sub_i =lax.broadcasted_iota(jnp.int32, (SUB, HEAD_D), 0)
    lane_j = lax.broadcasted_iota(jnp.int32, (SUB, HEAD_D), 1) & (SUB - 1)
    causal = (sub_i <= lane_j) if reverse else (sub_i >= lane_j)
    pick = [causal & (lane_j == j) for j in range(SUB)]
    cum_op = cum_ref[...]
    same_blk = mask_ref[0]
    same_grp = mask_ref[1]
    zero = jnp.zeros((SUB, HEAD_D), F32)
    n_blk = CH // SUB
    per = GRP // SUB
    n_grp = CH // GRP

    def edge(unit, idx):
        return unit * idx if reverse else unit * (idx + 1) - 1

    def split_product(b, qs, ks, bs, pieces):
        q_slabs, k_slabs = [], []
        for ref, q_active, k_active in pieces:
            qp, kp = [], []
            for i in range(n_blk):
                r = ref(i)
                qp.append(qs[i] * jnp.exp2(bs[i] - b[r:r + 1, :]) if q_active(i) else zero)
                kp.append(ks[i] * jnp.exp2(b[r:r + 1, :] - bs[i]) if k_active(i) else zero)
            q_slabs.append(jnp.concatenate(qp, axis=0))
            k_slabs.append(jnp.concatenate(kp, axis=0))
        return _dot_nt(jnp.concatenate(q_slabs, axis=1), jnp.concatenate(k_slabs, axis=1))

    level1 = []
    for c in (range(1, per) if reverse else range(per - 1)):
        level1.append((lambda i, c=c: edge(SUB, (i // per) * per + c),
                       (lambda i, c=c: i % per < c) if reverse else (lambda i, c=c: i % per > c),
                       lambda i, c=c: i % per == c))
    level2 = []
    for gc in (range(1, n_grp) if reverse else range(n_grp - 1)):
        level2.append((lambda i, gc=gc: edge(GRP, gc),
                       (lambda i, gc=gc: i // per < gc) if reverse else (lambda i, gc=gc: i // per > gc),
                       lambda i, gc=gc: i // per == gc))

    for h in range(N_HEADS):
        sl = slice(h * HEAD_D, (h + 1) * HEAD_D)
        lbh = lb[:, sl]
        q = _silu(q_ref[:, sl])
        f = lbh + (1.0 - lbh) * _sigmoid(f_ref[:, sl])
        k = jnp.where(valid, 1.0 - f, 0.0)
        v = v_ref[:, sl]
        b = _prefix_dot(cum_op, jnp.log(f)) * LOG2_E
        qs = [q[i * SUB:(i + 1) * SUB] for i in range(n_blk)]
        ks = [k[i * SUB:(i + 1) * SUB] for i in range(n_blk)]
        bs = [b[i * SUB:(i + 1) * SUB] for i in range(n_blk)]

        tiles = [qs[blk] * jnp.exp2(jnp.minimum(bs[blk] - bs[blk][j:j + 1, :], 0.0))
                 for blk in range(n_blk) for j in range(SUB)]
        pair = _dot_nt(jnp.concatenate(tiles, axis=0), k)
        rows = []
        for blk in range(n_blk):
            base = blk * SUB * SUB
            a_blk = jnp.where(pick[0], pair[base:base + SUB], 0.0)
            for j in range(1, SUB):
                a_blk = a_blk + jnp.where(pick[j], pair[base + j * SUB:base + (j + 1) * SUB], 0.0)
            rows.append(a_blk)
        scores = jnp.concatenate(rows, axis=0) * same_blk
        scores = scores + split_product(b, qs, ks, bs, level1) * same_grp
        scores = scores + split_product(b, qs, ks, bs, level2)

        state = jnp.where(reset, 0.0, st_ref[h])
        o_ref[:, sl] = (_dot(scores, v) + _dot_nt(q * jnp.exp2(b), state)).astype(o_ref.dtype)
        b_tot = b[0:1, :] if reverse else b[CH - 1:CH, :]
        st_ref[h] = state * jnp.exp2(b_tot) + _dot_tn(v, k * jnp.exp2(b_tot - b))


def _hg_lower_bound(lower_ref, layer):
    low = lower_ref[...]
    e = jnp.exp(low - jnp.max(low, axis=0, keepdims=True))
    soft = e / jnp.sum(e, axis=0, keepdims=True)
    lb = jnp.zeros((1, GROUP_W), F32)
    for l in range(1, layer + 1):
        lb = lb + soft[l:l + 1, :]
    return lb


def _hg_kernel(layer, cpos_ref, clast_ref, qf, ff, vf, qb, fb, vb, lower_ref, cum_ref, mask_ref,
               of_ref, ob_ref, sf_ref, sb_ref):
    g = pl.program_id(0)
    gb = pl.num_programs(0) - 1 - g
    lb = _hg_lower_bound(lower_ref, layer)
    _hg_direction(qf, ff, vf, lb, cum_ref.at[0], mask_ref, sf_ref, of_ref,
                  cpos_ref[g] == 0, cpos_ref[g] == 0, False)
    _hg_direction(qb, fb, vb, lb, cum_ref.at[1], mask_ref, sb_ref, ob_ref,
                  clast_ref[gb] == 1, cpos_ref[gb] == 0, True)


def _hgrn2(z, cpos, clast, lower, layer):
    t = z.shape[0]
    n = t // CH
    blk = (CH, GROUP_W)
    cum = _tri_consts()
    masks = _hg_masks()
    full = lambda a: pl.BlockSpec(a.shape, lambda g, cp, cl: (0,) * a.ndim)
    in_specs = [pl.BlockSpec(blk, _fwd_map(0)), pl.BlockSpec(blk, _fwd_map(1)), pl.BlockSpec(blk, _fwd_map(3)),
                pl.BlockSpec(blk, _bwd_map(n, 0)), pl.BlockSpec(blk, _bwd_map(n, 2)),
                pl.BlockSpec(blk, _bwd_map(n, 3)), full(lower), full(cum), full(masks)]
    return (functools.partial(_hg_kernel, layer), in_specs, [z, z, z, z, z, z, lower, cum, masks],
            [pltpu.VMEM((N_HEADS, HEAD_D, HEAD_D), F32)] * 2)


def _log_sigmoid(x):
    return jnp.minimum(x, 0.0) - jnp.log(1.0 + jnp.exp(-jnp.abs(x)))


def _ml_direction(q_ref, k_ref, v_ref, gi_ref, gf_ref, bias_ref, cum_op, spread_ref, s_ref, m_ref, o_ref,
                  reset, first, d, reverse):
    valid = _row_valid(first)
    cum = _prefix_dot(cum_op, _log_sigmoid(gf_ref[...] + bias_ref[1]))
    a = jnp.where(valid, gi_ref[...] + bias_ref[0] - cum, NEG_BIG)
    run = a
    s = 1
    while s < CH:
        run = jnp.maximum(run, _shift_rows(run, s, NEG_BIG, reverse))
        s *= 2
    m_st = jnp.where(reset, NEG_BIG, m_ref[...])
    mx = jnp.maximum(m_st, run)
    edge = 0 if reverse else CH - 1
    mx_last = mx[edge:edge + 1, :]
    m_ref[...] = cum[edge:edge + 1, :] + mx_last
    decay = jnp.exp(m_st - mx_last)

    stacked = jnp.concatenate([mx, m_st - mx, -(cum + mx), a - mx_last], axis=0) * LOG2_E
    spread_b = _spread_dot(stacked, spread_ref[...])
    mx_b, inter_b, floor_b, end_b = (spread_b[i * CH:(i + 1) * CH] for i in range(4))
    a_t = (a * LOG2_E).T

    row = lax.broadcasted_iota(jnp.int32, (CH, CH), 0)
    col = lax.broadcasted_iota(jnp.int32, (CH, CH), 1)
    causal = (col >= row) if reverse else (col <= row)
    ones = jnp.ones((CH, HEAD_D), F32)
    for h in range(N_HEADS):
        sl = slice(h * HEAD_D, (h + 1) * HEAD_D)
        x = d * N_HEADS + h
        q = q_ref[:, sl]
        k = jnp.where(valid, k_ref[:, sl] * (HEAD_D ** -0.5), 0.0)
        v_ext = jnp.concatenate([v_ref[:, sl], ones], axis=1)
        w = jnp.where(causal, jnp.exp2(jnp.minimum(a_t[x:x + 1, :] - mx_b[:, sl], 0.0)), 0.0)
        qk = _dot_nt(q, k) * w
        s_inter = jnp.exp2(jnp.minimum(inter_b[:, sl], 0.0))
        state = jnp.where(reset, 0.0, s_ref[h])
        ext = _dot(jnp.concatenate([qk, q * s_inter], axis=1), jnp.concatenate([v_ext, state], axis=0))
        o_ref[:, sl] = (ext[:, :HEAD_D] / jnp.maximum(jnp.abs(ext[:, HEAD_D:]), jnp.exp2(floor_b[:, sl]))
                        ).astype(o_ref.dtype)
        kw = k * jnp.exp2(jnp.minimum(end_b[:, sl], 0.0))
        s_ref[h] = decay[:, x:x + 1] * state + _dot_tn(kw, v_ext)


def _ml_kernel(cpos_ref, clast_ref, qf, kf, vf, gif, gff, qb, kb, vb, gib, gfb, bias_ref, cum_ref, spread_ref,
               of_ref, ob_ref, sfs, mfs, sbs, mbs):
    g = pl.program_id(0)
    gb = pl.num_programs(0) - 1 - g
    _ml_direction(qf, kf, vf, gif, gff, bias_ref, cum_ref[0], spread_ref.at[0], sfs, mfs, of_ref,
                  cpos_ref[g] == 0, cpos_ref[g] == 0, 0, False)
    _ml_direction(qb, kb, vb, gib, gfb, bias_ref, cum_ref[1], spread_ref.at[1], sbs, mbs, ob_ref,
                  clast_ref[gb] == 1, cpos_ref[gb] == 0, 1, True)


def _ml_spread():
    spread = np.zeros((2, LANES, GROUP_W), np.float32)
    for d in range(2):
        for h in range(N_HEADS):
            spread[d, d * N_HEADS + h, h * HEAD_D:(h + 1) * HEAD_D] = 1.0
    return jnp.asarray(np.tile(spread, (1, SPREAD_TERMS, 1)), BF16)


def _mlstm(z, gates, cpos, clast, ml_bi, ml_bf):
    t = z.shape[0]
    n = t // CH
    blk = (CH, GROUP_W)
    gblk = (CH, LANES)
    cum = _tri_consts()
    spread = _ml_spread()
    pad = jnp.zeros((LANES - 2 * N_HEADS,), F32)
    bias = jnp.stack([jnp.concatenate([ml_bi.reshape(-1), pad]),
                      jnp.concatenate([ml_bf.reshape(-1), pad])]).reshape(2, 1, LANES)
    full = lambda a: pl.BlockSpec(a.shape, lambda g, cp, cl: (0,) * a.ndim)
    in_specs = [pl.BlockSpec(blk, _fwd_map(5)), pl.BlockSpec(blk, _fwd_map(6)), pl.BlockSpec(blk, _fwd_map(7)),
                pl.BlockSpec(gblk, _fwd_map(0)), pl.BlockSpec(gblk, _fwd_map(1)),
                pl.BlockSpec(blk, _bwd_map(n, 5)), pl.BlockSpec(blk, _bwd_map(n, 6)),
                pl.BlockSpec(blk, _bwd_map(n, 7)),
                pl.BlockSpec(gblk, _bwd_map(n, 0)), pl.BlockSpec(gblk, _bwd_map(n, 1)),
                full(bias), full(cum), full(spread)]
    state = [pltpu.VMEM((N_HEADS, HEAD_D, 2 * HEAD_D), F32), pltpu.VMEM((1, LANES), F32)]
    return (_ml_kernel, in_specs, [z, z, z, gates, gates, z, z, z, gates, gates, bias, cum, spread], state * 2)


def _head_norm(x, center):
    outs = []
    for h in range(N_HEADS):
        xh = x[:, h * HEAD_D:(h + 1) * HEAD_D]
        if center:
            xh = xh - jnp.mean(xh, axis=1, keepdims=True)
        outs.append(xh * lax.rsqrt(jnp.mean(xh * xh, axis=1, keepdims=True) + EPS))
    return jnp.concatenate(outs, axis=1)


def _layer_norm(x, g, b):
    xc = x - jnp.mean(x, axis=1, keepdims=True)
    return xc * lax.rsqrt(jnp.mean(xc * xc, axis=1, keepdims=True) + EPS) * g + b


def _mix_out_kernel(alpha, even, af, ab, ag, bf, bb, bg, h_ref, w_ref, lng_ref, lnb_ref, o_ref):
    a = af[...].astype(F32) + ab[...].astype(F32)
    b = bf[...].astype(F32) + bb[...].astype(F32)
    if even:
        a = _head_norm(a, False) * _silu(ag[...])
        b = b * _gelu_tanh(bg[...])
    else:
        a = _head_norm(a, False) * _silu(ag[...])
        b = _head_norm(b, True) * _sigmoid(bg[...])
    mix = _dot(a, w_ref[0:GROUP_W, :]) + _dot(b, w_ref[GROUP_W:, :])
    o_ref[...] = _layer_norm(alpha * h_ref[...] + mix, lng_ref[...], lnb_ref[...])


def _mix_out(alpha, even, af, ab, bf, bb, z, col_a, col_b, h, w_out, ln_g, ln_b):
    t = h.shape[0]
    tm = _pick_tile(t, 512)
    half = lambda c: pl.BlockSpec((tm, GROUP_W), lambda i: (i, c))
    rows = pl.BlockSpec((tm, D_MODEL), lambda i: (i, 0))
    vec = pl.BlockSpec((1, D_MODEL), lambda i: (0, 0))
    return pl.pallas_call(
        functools.partial(_mix_out_kernel, alpha, even), grid=(t // tm,),
        in_specs=[half(0), half(0), half(col_a), half(0), half(0), half(col_b), rows,
                  pl.BlockSpec((D_MODEL, D_MODEL), lambda i: (0, 0)), vec, vec],
        out_specs=rows, out_shape=jax.ShapeDtypeStruct((t, D_MODEL), F32),
        compiler_params=_params(1), name="mix_out")(
            af, ab, z, bf, bb, z, h, w_out, ln_g.reshape(1, -1), ln_b.reshape(1, -1))


def _ffn_kernel(alpha, h_ref, wg_ref, wu_ref, wo_ref, lng_ref, lnb_ref, o_ref, acc_ref):
    f = pl.program_id(1)

    @pl.when(f == 0)
    def _():
        acc_ref[...] = jnp.zeros_like(acc_ref)

    x = h_ref[...].astype(BF16)
    act = _silu(_dot(x, wg_ref[...])) * _dot(x, wu_ref[...])
    acc_ref[...] += _dot(act, wo_ref[...])

    @pl.when(f == pl.num_programs(1) - 1)
    def _():
        o_ref[...] = _layer_norm(alpha * h_ref[...] + acc_ref[...], lng_ref[...], lnb_ref[...])


def _ffn(alpha, h, wi, wo, ln_g, ln_b):
    t = h.shape[0]
    d_ff = wo.shape[0]
    tm = _pick_tile(t, 1024)
    tf = _pick_tile(d_ff, 1536)
    nf = d_ff // tf
    rows = pl.BlockSpec((tm, D_MODEL), lambda i, f: (i, 0))
    vec = pl.BlockSpec((1, D_MODEL), lambda i, f: (0, 0))
    return pl.pallas_call(
        functools.partial(_ffn_kernel, alpha), grid=(t // tm, nf),
        in_specs=[rows, pl.BlockSpec((D_MODEL, tf), lambda i, f: (0, f)),
                  pl.BlockSpec((D_MODEL, tf), lambda i, f: (0, f + nf)),
                  pl.BlockSpec((tf, D_MODEL), lambda i, f: (f, 0)), vec, vec],
        out_specs=rows, out_shape=jax.ShapeDtypeStruct((t, D_MODEL), F32),
        scratch_shapes=[pltpu.VMEM((tm, D_MODEL), F32)],
        compiler_params=_params(2), name="ffn")(h, wi, wi, wo, ln_g.reshape(1, -1), ln_b.reshape(1, -1))


EXPERT_TILE = 1024
DMA_UNROLL = 8
DMA_PRIORITIES = 2


def _route_kernel(pad_starts, h_ref, router_ref, tri_ref, meta_ref, gate_ref, cnt_ref, carry_ref):
    @pl.when(pl.program_id(0) == 0)
    def _():
        carry_ref[...] = jnp.zeros_like(carry_ref)

    tm = h_ref.shape[0]
    row = pl.program_id(0) * tm + lax.broadcasted_iota(jnp.int32, (tm, 1), 0)
    is_pad = row < 0
    for s in pad_starts:
        is_pad = is_pad | ((row >= s) & (row < s + PAD_ROWS))
    is_token = jnp.logical_not(is_pad)

    logits = _dot_16bit(h_ref[...], router_ref[...])
    lane = lax.broadcasted_iota(jnp.int32, logits.shape, 1)
    logits = jnp.where(lane < N_EXP, logits, -jnp.inf)
    top1 = jnp.max(logits, axis=1, keepdims=True)
    idx1 = jnp.min(jnp.where(logits == top1, lane, LANES), axis=1, keepdims=True)
    rest = jnp.where(lane == idx1, -jnp.inf, logits)
    top2 = jnp.max(rest, axis=1, keepdims=True)
    idx2 = jnp.min(jnp.where(rest == top2, lane, LANES), axis=1, keepdims=True)
    g2 = jnp.exp(top2 - top1)
    denom = 1.0 + g2
    hit1 = lane == idx1
    hit2 = lane == idx2
    both = jnp.where(is_token, jnp.where(hit1, 1.0, 0.0) + jnp.where(hit2, 1.0, 0.0), 0.0)
    prefix = _dot(tri_ref[...], both) + carry_ref[...]
    rank1 = jnp.sum(jnp.where(hit1, prefix, 0.0), axis=1, keepdims=True).astype(jnp.int32)
    rank2 = jnp.sum(jnp.where(hit2, prefix, 0.0), axis=1, keepdims=True).astype(jnp.int32)
    carry_ref[...] += jnp.sum(both, axis=0, keepdims=True)
    cnt_ref[...] = carry_ref[...]
    meta_ref[...] = jnp.where(lane == 0, idx1, jnp.where(lane == 1, idx2, jnp.where(
        lane == 2, rank1, jnp.where(lane == 3, rank2, jnp.where(is_token & (lane == 4), 1, 0)))))
    gate_ref[...] = jnp.where(lane == 0, 1.0 / denom, jnp.where(lane == 1, g2 / denom, 0.0))


def _route(h, router, pad_starts):
    t = h.shape[0]
    tm = _pick_tile(t, 512)
    router_p = _stack_weight_terms(jnp.pad(router, ((0, 0), (0, LANES - router.shape[1]))))
    pos = np.arange(tm)
    tri = jnp.asarray(pos[:, None] > pos[None, :], BF16)
    rows = lambda w: pl.BlockSpec((tm, w), lambda i: (i, 0))
    const = lambda a: pl.BlockSpec(a.shape, lambda i: (0, 0))
    return pl.pallas_call(
        functools.partial(_route_kernel, pad_starts), grid=(t // tm,),
        in_specs=[rows(D_MODEL), const(router_p), const(tri)],
        out_specs=[rows(LANES), rows(LANES), pl.BlockSpec((1, LANES), lambda i: (0, 0))],
        out_shape=[jax.ShapeDtypeStruct((t, LANES), jnp.int32), jax.ShapeDtypeStruct((t, LANES), F32),
                   jax.ShapeDtypeStruct((1, LANES), F32)],
        scratch_shapes=[pltpu.VMEM((1, LANES), F32)],
        compiler_params=_params(1), name="route")(h, router_p, tri)


def _dispatch_plan(meta, counts, t):
    cnt = counts[0, :N_EXP].astype(jnp.int32)
    padded = ((cnt + EXPERT_TILE - 1) // EXPERT_TILE) * EXPERT_TILE
    ends = jnp.cumsum(padded)
    off = ends - padded

    def base(e):
        return sum(jnp.where(e == i, off[i], 0) for i in range(N_EXP))

    is_token = meta[:, 4] > 0
    n_tiles = -(-2 * t // EXPERT_TILE) + N_EXP
    spare = n_tiles * EXPERT_TILE + 2 * (jnp.cumsum(jnp.logical_not(is_token).astype(jnp.int32)) - 1)
    pos1 = (base(meta[:, 0]) + meta[:, 2]).astype(jnp.int32)
    pos2 = (base(meta[:, 1]) + meta[:, 3]).astype(jnp.int32)
    scatter = (jnp.where(is_token, pos1, spare), jnp.where(is_token, pos2, spare + 1))
    gather = (jnp.where(is_token, pos1, 0), jnp.where(is_token, pos2, 0))
    starts = jnp.arange(n_tiles, dtype=jnp.int32) * EXPERT_TILE
    tile_expert = jnp.minimum(jnp.sum(starts[:, None] >= ends[None, :], axis=1), N_EXP - 1).astype(jnp.int32)
    n_active = (ends[-1] // EXPERT_TILE).astype(jnp.int32).reshape(1)
    return ends.astype(jnp.int32), scatter, gather, tile_expert, n_active, n_tiles


assert D_MODEL == SUBLANES * LANES


def _store_token_tiles(ref, x):
    n = x.shape[0]
    for s in range(SUBLANES):
        ref[pl.ds(s, n, stride=SUBLANES), :] = x[:, s * LANES:(s + 1) * LANES]


def _load_token_tiles(ref, n):
    return jnp.concatenate([ref[pl.ds(s, n, stride=SUBLANES), :] for s in range(SUBLANES)], axis=1)


def _tile_rows(i):
    return pl.ds(pl.multiple_of(i * SUBLANES, SUBLANES), SUBLANES)


def _dispatch_kernel(n_tiles, ends_ref, pos1_ref, pos2_ref, h_ref, xs_ref, tok_ref, zero_ref, sem, zero_sem):
    n = h_ref.shape[0]

    @pl.when(pl.program_id(0) == 0)
    def _():
        zero_ref[...] = jnp.zeros_like(zero_ref)

        def clear_last_tile(e):
            first_row = pl.multiple_of((ends_ref[e] - EXPERT_TILE) * SUBLANES, SUBLANES)
            return pltpu.make_async_copy(zero_ref, xs_ref.at[pl.ds(first_row, EXPERT_TILE * SUBLANES)], zero_sem)

        def has_tiles(e):
            return ends_ref[e] > (ends_ref[e - 1] if e else 0)

        for e in range(N_EXP):
            pl.when(has_tiles(e))(lambda e=e: clear_last_tile(e).start())
        for e in range(N_EXP):
            pl.when(has_tiles(e))(lambda e=e: clear_last_tile(e).wait())

        def clear_tile(i):
            first_row = pl.multiple_of(i * (EXPERT_TILE * SUBLANES), SUBLANES)
            return pltpu.make_async_copy(zero_ref, xs_ref.at[pl.ds(first_row, EXPERT_TILE * SUBLANES)], zero_sem)

        used = ends_ref[N_EXP - 1] // EXPERT_TILE
        lax.fori_loop(used, n_tiles, lambda i, c: (clear_tile(i).start(), c)[1], 0)
        lax.fori_loop(used, n_tiles, lambda i, c: (clear_tile(i).wait(), c)[1], 0)

    g = pl.program_id(0)
    slot = g % 2
    _store_token_tiles(tok_ref.at[slot], h_ref[...])

    def tile_copy(r, p):
        return pltpu.make_async_copy(tok_ref.at[slot].at[_tile_rows(r)], xs_ref.at[_tile_rows(p)], sem.at[slot])

    def start(pair, c):
        for prio in range(DMA_PRIORITIES):
            r = pair * DMA_PRIORITIES + prio
            tile_copy(r, pos1_ref[0, 0, r]).start(priority=prio)
            tile_copy(r, pos2_ref[0, 0, r]).start(priority=prio)
        return c

    lax.fori_loop(0, n // DMA_PRIORITIES, start, 0, unroll=DMA_UNROLL // DMA_PRIORITIES)

    def wait_slot(s):
        all_rows = pltpu.make_async_copy(tok_ref.at[s], xs_ref.at[pl.ds(0, n * SUBLANES)], sem.at[s])
        all_rows.wait()
        all_rows.wait()

    pl.when(g > 0)(lambda: wait_slot(1 - slot))
    pl.when(g == pl.num_programs(0) - 1)(lambda: wait_slot(slot))


def _dispatch(h, ends, pos1, pos2, n_tiles, n_spare):
    t = h.shape[0]
    n_rows = n_tiles * EXPERT_TILE + n_spare
    tm = _pick_tile(t, 512)
    idx = lambda: pl.BlockSpec((1, 1, tm), lambda i, ends: (i, 0, 0), memory_space=pltpu.SMEM)
    grid_spec = pltpu.PrefetchScalarGridSpec(
        num_scalar_prefetch=1, grid=(t // tm,),
        in_specs=[idx(), idx(), pl.BlockSpec((tm, D_MODEL), lambda i, ends: (i, 0))],
        out_specs=pl.BlockSpec(memory_space=pl.ANY),
        scratch_shapes=[pltpu.VMEM((2, tm * SUBLANES, LANES), F32),
                        pltpu.VMEM((EXPERT_TILE * SUBLANES, LANES), F32),
                        pltpu.SemaphoreType.DMA((2,)), pltpu.SemaphoreType.DMA(())])
    return pl.pallas_call(
        functools.partial(_dispatch_kernel, n_tiles), grid_spec=grid_spec,
        out_shape=jax.ShapeDtypeStruct((n_rows * SUBLANES, LANES), F32),
        compiler_params=_params(1), name="dispatch")(
            ends, pos1.reshape(t // tm, 1, tm), pos2.reshape(t // tm, 1, tm), h)


def _experts_kernel(te_ref, na_ref, x_ref, wg_ref, wu_ref, wo_ref, y_ref, xb_ref, acc_ref):
    i = pl.program_id(0)
    f = pl.program_id(1)
    last = f == pl.num_programs(1) - 1
    active = i < na_ref[0]

    @pl.when(active)
    def _():
        @pl.when(f == 0)
        def _():
            acc_ref[...] = jnp.zeros_like(acc_ref)
            xb_ref[...] = _load_token_tiles(x_ref, EXPERT_TILE).astype(BF16)

        x = xb_ref[...]
        act = _silu(_dot(x, wg_ref[0])) * _dot(x, wu_ref[0])
        acc_ref[...] += _dot(act, wo_ref[0])

        @pl.when(last)
        def _():
            _store_token_tiles(y_ref, acc_ref[...])

    @pl.when(jnp.logical_not(active) & last)
    def _():
        y_ref[...] = jnp.zeros_like(y_ref)


def _experts(xs, tile_expert, n_active, wi, wo, n_tiles):
    n_rows = n_tiles * EXPERT_TILE
    e_ff = wo.shape[1]
    tf = _pick_tile(e_ff, 512)
    nf = e_ff // tf
    rows = pl.BlockSpec((EXPERT_TILE * SUBLANES, LANES), lambda i, f, te, na: (i, 0))
    rows_in = pl.BlockSpec((EXPERT_TILE * SUBLANES, LANES),
                           lambda i, f, te, na: (jnp.minimum(i, jnp.maximum(na[0] - 1, 0)), 0))
    grid_spec = pltpu.PrefetchScalarGridSpec(
        num_scalar_prefetch=2, grid=(n_rows // EXPERT_TILE, nf),
        in_specs=[rows_in, pl.BlockSpec((1, D_MODEL, tf), lambda i, f, te, na: (te[i], 0, f)),
                  pl.BlockSpec((1, D_MODEL, tf), lambda i, f, te, na: (te[i], 0, f + nf)),
                  pl.BlockSpec((1, tf, D_MODEL), lambda i, f, te, na: (te[i], f, 0))],
        out_specs=rows,
        scratch_shapes=[pltpu.VMEM((EXPERT_TILE, D_MODEL), BF16), pltpu.VMEM((EXPERT_TILE, D_MODEL), F32)])
    return pl.pallas_call(
        _experts_kernel, grid_spec=grid_spec, out_shape=jax.ShapeDtypeStruct((n_rows * SUBLANES, LANES), F32),
        compiler_params=_params(2), name="experts")(tile_expert, n_active, xs, wi, wi, wo)


def _gather_start(n, pos1_ref, pos2_ref, ys_ref, a_ref, b_ref, sems):
    def tile_copy(p, buf, r, which):
        return pltpu.make_async_copy(ys_ref.at[_tile_rows(p)], buf.at[_tile_rows(r)], sems.at[which])

    def start(pair, c):
        for prio in range(DMA_PRIORITIES):
            r = pair * DMA_PRIORITIES + prio
            tile_copy(pos1_ref[0, 0, r], a_ref, r, 0).start(priority=prio)
            tile_copy(pos2_ref[0, 0, r], b_ref, r, 1).start(priority=prio)
        return c

    lax.fori_loop(0, n // DMA_PRIORITIES, start, 0, unroll=DMA_UNROLL // DMA_PRIORITIES)


def _gather_finish(alpha, h_ref, gate_ref, ys_ref, lng_ref, lnb_ref, a_ref, b_ref, sems):
    n = h_ref.shape[0]
    for which, buf in enumerate((a_ref, b_ref)):
        pltpu.make_async_copy(ys_ref.at[pl.ds(0, n * SUBLANES)], buf, sems.at[which]).wait()
    gate = gate_ref[...]
    y = gate[:, 0:1] * _load_token_tiles(a_ref, n) + gate[:, 1:2] * _load_token_tiles(b_ref, n)
    return _layer_norm(alpha * h_ref[...] + y, lng_ref[...], lnb_ref[...])


def _combine_kernel(alpha, pos1_ref, pos2_ref, h_ref, gate_ref, ys_ref, lng_ref, lnb_ref, o_ref,
                    a_ref, b_ref, sems):
    _gather_start(h_ref.shape[0], pos1_ref, pos2_ref, ys_ref, a_ref, b_ref, sems)
    o_ref[...] = _gather_finish(alpha, h_ref, gate_ref, ys_ref, lng_ref, lnb_ref, a_ref, b_ref, sems)


def _combine_scratch(tm):
    return [pltpu.VMEM((tm * SUBLANES, LANES), F32), pltpu.VMEM((tm * SUBLANES, LANES), F32),
            pltpu.SemaphoreType.DMA((2,))]


def _combine(alpha, h, gates, ys, pos1, pos2, ln_g, ln_b):
    t = h.shape[0]
    tm = _pick_tile(t, 512)
    idx = lambda: pl.BlockSpec((1, 1, tm), lambda i: (i, 0, 0), memory_space=pltpu.SMEM)
    rows = pl.BlockSpec((tm, D_MODEL), lambda i: (i, 0))
    vec = pl.BlockSpec((1, D_MODEL), lambda i: (0, 0))
    return pl.pallas_call(
        functools.partial(_combine_kernel, alpha), grid=(t // tm,),
        in_specs=[idx(), idx(), rows, pl.BlockSpec((tm, LANES), lambda i: (i, 0)),
                  pl.BlockSpec(memory_space=pl.ANY), vec, vec],
        out_specs=rows, out_shape=jax.ShapeDtypeStruct((t, D_MODEL), F32),
        scratch_shapes=_combine_scratch(tm),
        compiler_params=_params(1), name="combine")(
            pos1.reshape(t // tm, 1, tm), pos2.reshape(t // tm, 1, tm), h, gates, ys,
            ln_g.reshape(1, -1), ln_b.reshape(1, -1))


def _output_tables(seq_shapes):
    kind, live = [], []
    window = [[], []]
    recent = [(0, 0), (0, 0)]
    for grp, (b, l) in enumerate(seq_shapes):
        for i in range(b):
            for c in range(l // CH + 1):
                kind.append(grp)
                live.append(int(c > 0))
                recent[grp] = (i, max(c - 1, 0))
                for g2 in range(2):
                    window[g2].append(recent[g2])
    tables = [kind, live] + [[w[j] for w in window[g2]] for g2 in range(2) for j in range(2)]
    return [jnp.asarray(np.asarray(x, np.int32)) for x in tables]


def _combine_out_kernel(alpha, kind_ref, live_ref, ab_ref, ac_ref, bb_ref, bc_ref,
                        pos1_ref, pos2_ref, nxt1_ref, nxt2_ref, h_ref, gate_ref, ys_ref, lng_ref, lnb_ref,
                        outa_ref, outb_ref, a_ref, b_ref, sems):
    g = pl.program_id(0)
    last = pl.num_programs(0) - 1
    slot = g % 2
    nxt = jnp.minimum(g + 1, last)

    @pl.when((g == 0) & (live_ref[0] == 1))
    def _():
        _gather_start(CH, pos1_ref, pos2_ref, ys_ref, a_ref.at[0], b_ref.at[0], sems.at[0])

    @pl.when((g < last) & (live_ref[nxt] == 1))
    def _():
        _gather_start(CH, nxt1_ref, nxt2_ref, ys_ref, a_ref.at[1 - slot], b_ref.at[1 - slot], sems.at[1 - slot])

    @pl.when(live_ref[g] == 1)
    def _():
        res = _gather_finish(alpha, h_ref, gate_ref, ys_ref, lng_ref, lnb_ref,
                             a_ref.at[slot], b_ref.at[slot], sems.at[slot])
        for grp, out_ref in enumerate((outa_ref, outb_ref)):
            @pl.when(kind_ref[g] == grp)
            def _(out_ref=out_ref):
                out_ref[0] = res


def _combine_out(alpha, h, gates, ys, pos1, pos2, ln_g, ln_b, seq_shapes):
    t = h.shape[0]
    n = t // CH
    tables = _output_tables(seq_shapes)
    idx = lambda: pl.BlockSpec((1, 1, CH), lambda g, *tb: (g, 0, 0), memory_space=pltpu.SMEM)
    idx_next = lambda: pl.BlockSpec((1, 1, CH), lambda g, *tb: (jnp.minimum(g + 1, n - 1), 0, 0),
                                    memory_space=pltpu.SMEM)
    rows = lambda w: pl.BlockSpec((CH, w), lambda g, *tb: (g, 0))
    vec = pl.BlockSpec((1, D_MODEL), lambda g, *tb: (0, 0))
    out = lambda grp: pl.BlockSpec((1, CH, D_MODEL),
                                   lambda g, *tb: (tb[2 + 2 * grp][g], tb[3 + 2 * grp][g], 0))
    fetched = pltpu.VMEM((2, CH * SUBLANES, LANES), F32)
    grid_spec = pltpu.PrefetchScalarGridSpec(
        num_scalar_prefetch=len(tables), grid=(n,),
        in_specs=[idx(), idx(), idx_next(), idx_next(), rows(D_MODEL), rows(LANES),
                  pl.BlockSpec(memory_space=pl.ANY), vec, vec],
        out_specs=[out(0), out(1)], scratch_shapes=[fetched, fetched, pltpu.SemaphoreType.DMA((2, 2))])
    pos1, pos2 = pos1.reshape(n, 1, CH), pos2.reshape(n, 1, CH)
    return pl.pallas_call(
        functools.partial(_combine_out_kernel, alpha), grid_spec=grid_spec,
        out_shape=[jax.ShapeDtypeStruct((b, l, D_MODEL), F32) for b, l in seq_shapes],
        compiler_params=_params(1), name="combine_out")(
            *tables, pos1, pos2, pos1, pos2, h, gates, ys, ln_g.reshape(1, -1), ln_b.reshape(1, -1))


def _moe(alpha, h, router, wi, wo, ln_g, ln_b, pad_starts, out_shapes=None):
    t = h.shape[0]
    meta, gates, counts = _route(h, router, pad_starts)
    ends, scatter, gather, tile_expert, n_active, n_tiles = _dispatch_plan(meta, counts, t)
    n_spare = 2 * len(pad_starts) * PAD_ROWS
    xs = _dispatch(h, ends, scatter[0], scatter[1], n_tiles, n_spare)
    ys = _experts(xs, tile_expert, n_active, wi, wo, n_tiles)
    if out_shapes is None:
        return _combine(alpha, h, gates, ys, gather[0], gather[1], ln_g, ln_b)
    return _combine_out(alpha, h, gates, ys, gather[0], gather[1], ln_g, ln_b, out_shapes)


def kernel(x_prompt, x_sample, meta, e_w_in, e_conv_w, e_conv_b, e_lru_wa, e_lru_ba, e_lru_wx, e_lru_bx,
           e_lru_lambda, e_w_out, e_ffn_wi, e_ffn_wo, o_w_in, o_hg_lower, o_ml_bi, o_ml_bf, o_w_out, o_router,
           o_exp_wi, o_exp_wo, ln_g, ln_b):
    groups = (x_prompt, x_sample)
    depth = ln_g.shape[0]
    alpha = (2.0 * depth) ** 0.25
    seq_shapes = [(x.shape[0], x.shape[1]) for x in groups]
    for _, l in seq_shapes:
        assert l % CH == 0
    cpos_np, clast_np = _chunk_tables(seq_shapes)
    cpos, clast = jnp.asarray(cpos_np), jnp.asarray(clast_np)
    max_rows = max(l for _, l in seq_shapes) + CH
    pad_starts = tuple(int(i) * CH for i in np.flatnonzero(cpos_np == 0))

    head = jnp.concatenate([jnp.zeros((PAD_ROWS, D_MODEL), F32), meta.astype(F32)], axis=0)
    parts = []
    for x in groups:
        full = jnp.concatenate([jnp.broadcast_to(head[None], (x.shape[0], CH, D_MODEL)), x], axis=1)
        parts.append(full.reshape(-1, D_MODEL))
    h = jnp.concatenate(parts, axis=0)
    t = h.shape[0]

    n_odd_cols = 9 * GROUP_W
    for layer in range(depth):
        p = layer // 2
        if layer % 2 == 0:
            z = _project(h, e_w_in[p].astype(BF16))
            (ret_f, ret_b), (lru_f, lru_b) = _chunk_walk(
                [_retention(z, cpos, clast, max_rows),
                 _rglru(z, 4, cpos, clast, e_conv_w[p], e_conv_b[p], e_lru_wa[p], e_lru_ba[p],
                        e_lru_wx[p], e_lru_bx[p], e_lru_lambda[p])], cpos, clast, t, "even_mixers")
            h = _mix_out(alpha, True, ret_f, ret_b, lru_f, lru_b, z, 3, 5, h, e_w_out[p].astype(BF16),
                         ln_g[layer, 0], ln_b[layer, 0])
            h = _ffn(alpha, h, e_ffn_wi[p].astype(BF16), e_ffn_wo[p].astype(BF16),
                     ln_g[layer, 1], ln_b[layer, 1])
        else:
            w_in = o_w_in[p]
            n_gate = 2 * N_HEADS
            gate_pad = ((0, 0), (0, LANES - n_gate))
            w_gates = jnp.concatenate([jnp.pad(w_in[:, n_odd_cols:n_odd_cols + n_gate], gate_pad),
                                       jnp.pad(w_in[:, n_odd_cols + n_gate:], gate_pad)], axis=1)
            z, gates = _project(h, w_in[:, :n_odd_cols].astype(BF16), w_gates)
            (hg_f, hg_b), (ml_f, ml_b) = _chunk_walk(
                [_hgrn2(z, cpos, clast, o_hg_lower, layer),
                 _mlstm(z, gates, cpos, clast, o_ml_bi[p], o_ml_bf[p])], cpos, clast, t, "odd_mixers")
            h = _mix_out(alpha, False, hg_f, hg_b, ml_f, ml_b, z, 4, 8, h, o_w_out[p].astype(BF16),
                         ln_g[layer, 0], ln_b[layer, 0])
            h = _moe(alpha, h, o_router[p], o_exp_wi[p].astype(BF16), o_exp_wo[p].astype(BF16),
                     ln_g[layer, 1], ln_b[layer, 1], pad_starts,
                     seq_shapes if layer == depth - 1 else None)
            if layer == depth - 1:
                return tuple(h)

    outs = []
    row = 0
    for b, l in seq_shapes:
        n = b * (l + CH)
        outs.append(h[row:row + n].reshape(b, l + CH, D_MODEL)[:, CH:])
        row += n
    return tuple(outs)
```

```python
import functools
import math

import numpy as np
import jax
import jax.numpy as jnp
from jax import lax
from jax.experimental import pallas as pl
from jax.experimental.pallas import tpu as pltpu

F32 = jnp.float32
BF16 = jnp.bfloat16

D_MODEL = 1024
GROUP_W = D_MODEL // 2
N_HEADS = 4
HEAD_D = GROUP_W // N_HEADS
N_META = 16
ROPE_BASE = 10000.0
LRU_BLOCKS = 8
LRU_BW = GROUP_W // LRU_BLOCKS
LRU_C = 8.0
N_EXP = 8
EPS = 1e-5

LANES = 128
SUBLANES = 8
CH = 128
PAD_ROWS = CH - N_META
SUB = SUBLANES
GRP = 4 * SUB
NEG_BIG = -1e30
LOG2_E = math.log2(math.e)
VMEM_LIMIT = 56 * 1024 * 1024


def _dot(a, b):
    return jnp.dot(a.astype(BF16), b.astype(BF16), preferred_element_type=F32)


def _dot_nt(a, b):
    return lax.dot_general(a.astype(BF16), b.astype(BF16), (((1,), (1,)), ((), ())),
                           preferred_element_type=F32)


def _dot_tn(a, b):
    return _dot(a.T, b)


def _bf16_terms(x, n_terms):
    terms = []
    for _ in range(n_terms):
        t = x.astype(BF16)
        terms.append(t)
        x = x - t.astype(F32)
    return terms


def _stack_weight_terms(w):
    hi, lo = _bf16_terms(w, 2)
    return jnp.concatenate([hi, lo, hi], axis=0)


def _dot_16bit(x, w_terms):
    hi, lo = _bf16_terms(x, 2)
    return jnp.dot(jnp.concatenate([hi, hi, lo], axis=1), w_terms, preferred_element_type=F32)


PREFIX_TERMS = 3
SPREAD_TERMS = 2


def _prefix_dot(op_tiled, x):
    return jnp.dot(op_tiled, jnp.concatenate(_bf16_terms(x, PREFIX_TERMS), axis=0),
                   preferred_element_type=F32)


def _spread_dot(x, op_tiled):
    return jnp.dot(jnp.concatenate(_bf16_terms(x, SPREAD_TERMS), axis=1), op_tiled,
                   preferred_element_type=F32)


def _sigmoid(x):
    return 1.0 / (1.0 + jnp.exp(-x))


def _silu(x):
    return x * _sigmoid(x)


def _gelu_tanh(x):
    return 0.5 * x * (1.0 + jnp.tanh(math.sqrt(2.0 / math.pi) * (x + 0.044715 * (x * x * x))))


def _softplus(x):
    return jnp.maximum(x, 0.0) + jnp.log(1.0 + jnp.exp(-jnp.abs(x)))


def _pick_tile(total, target):
    best = LANES
    for t in range(LANES, min(total, target) + 1, LANES):
        if total % t == 0:
            best = t
    return best


def _params(n_axes, sem="arbitrary"):
    return pltpu.CompilerParams(dimension_semantics=(sem,) * n_axes, vmem_limit_bytes=VMEM_LIMIT)


def _proj_kernel(x_ref, w_ref, o_ref):
    o_ref[...] = _dot(x_ref[...], w_ref[...])


def _proj_gates_kernel(x_ref, w_ref, wg_ref, o_ref, g_ref):
    g_ref[...] = _dot_16bit(x_ref[...], wg_ref[...])
    o_ref[...] = _dot(x_ref[...], w_ref[...])


PROJ_OUT_BLOCK_BYTES = 8 * 1024 * 1024


def _project(x, w, w_gates=None):
    t, k = x.shape
    n = w.shape[1]
    tm = _pick_tile(t, PROJ_OUT_BLOCK_BYTES // (4 * n))
    x_spec = pl.BlockSpec((tm, k), lambda i: (i, 0))
    w_spec = pl.BlockSpec((k, n), lambda i: (0, 0))
    o_spec = pl.BlockSpec((tm, n), lambda i: (i, 0))
    if w_gates is None:
        return pl.pallas_call(
            _proj_kernel, grid=(t // tm,), in_specs=[x_spec, w_spec], out_specs=o_spec,
            out_shape=jax.ShapeDtypeStruct((t, n), F32), compiler_params=_params(1),
            name="proj")(x, w)
    ng = w_gates.shape[1]
    w_gates = _stack_weight_terms(w_gates)
    return pl.pallas_call(
        _proj_gates_kernel, grid=(t // tm,),
        in_specs=[x_spec, w_spec, pl.BlockSpec(w_gates.shape, lambda i: (0, 0))],
        out_specs=[o_spec, pl.BlockSpec((tm, ng), lambda i: (i, 0))],
        out_shape=[jax.ShapeDtypeStruct((t, n), F32), jax.ShapeDtypeStruct((t, ng), F32)],
        compiler_params=_params(1), name="proj_gates")(x, w, w_gates)


def _chunk_tables(seq_shapes):
    cpos, clast = [], []
    for b, l in seq_shapes:
        n = l // CH + 1
        for _ in range(b):
            cpos += list(range(n))
            clast += [0] * (n - 1) + [1]
    return np.asarray(cpos, np.int32), np.asarray(clast, np.int32)


def _fwd_map(col):
    return lambda g, cp, cl: (g, col)


def _bwd_map(n_chunks, col):
    return lambda g, cp, cl: (n_chunks - 1 - g, col)


def _walk_kernel(bodies, n_inputs, n_scratch, cpos_ref, clast_ref, *refs):
    refs = list(refs)
    ins = [[refs.pop(0) for _ in range(k)] for k in n_inputs]
    outs = [[refs.pop(0) for _ in range(2)] for _ in bodies]

    @pl.when(pl.program_id(0) == 0)
    def _():
        for r in refs:
            r[...] = jnp.zeros_like(r)

    for body, i, o, k in zip(bodies, ins, outs, n_scratch):
        body(cpos_ref, clast_ref, *i, *o, *[refs.pop(0) for _ in range(k)])


def _chunk_walk(parts, cpos, clast, t, name):
    n = t // CH
    blk = (CH, GROUP_W)
    bodies = [p[0] for p in parts]
    grid_spec = pltpu.PrefetchScalarGridSpec(
        num_scalar_prefetch=2, grid=(n,), in_specs=[s for p in parts for s in p[1]],
        out_specs=[pl.BlockSpec(blk, _fwd_map(0)), pl.BlockSpec(blk, _bwd_map(n, 0))] * len(parts),
        scratch_shapes=[s for p in parts for s in p[3]])
    outs = pl.pallas_call(
        functools.partial(_walk_kernel, bodies, [len(p[1]) for p in parts], [len(p[3]) for p in parts]),
        grid_spec=grid_spec, out_shape=[jax.ShapeDtypeStruct((t, GROUP_W), BF16)] * (2 * len(parts)),
        compiler_params=_params(1), name=name)(cpos, clast, *[a for p in parts for a in p[2]])
    return [outs[2 * i:2 * i + 2] for i in range(len(parts))]


def _row_valid(first):
    row = lax.broadcasted_iota(jnp.int32, (CH, 1), 0)
    return row >= jnp.where(first, PAD_ROWS, 0)


def _ret_log_gamma():
    return np.log1p(-np.exp2(-5.0 - np.arange(N_HEADS, dtype=np.float64)))


def _ret_consts():
    lg = _ret_log_gamma()
    pos = np.arange(CH, dtype=np.float64)
    rel = pos[:, None] - pos[None, :]
    dmat = np.zeros((2, N_HEADS, CH, CH), np.float64)
    rows = np.zeros((4, CH, GROUP_W), np.float64)
    for h in range(N_HEADS):
        dmat[0, h] = np.where(rel >= 0, np.exp(np.maximum(rel, 0.0) * lg[h]), 0.0)
        dmat[1, h] = np.where(rel < 0, np.exp(np.maximum(-rel, 0.0) * lg[h]), 0.0)
        sl = slice(h * HEAD_D, (h + 1) * HEAD_D)
        rows[0, :, sl] = np.exp((pos + 1.0) * lg[h])[:, None]
        rows[1, :, sl] = np.exp((CH - 1.0 - pos) * lg[h])[:, None]
        rows[2, :, sl] = np.exp((CH - pos) * lg[h])[:, None]
        rows[3, :, sl] = np.exp(pos * lg[h])[:, None]
    return jnp.asarray(dmat, F32), jnp.asarray(rows, F32)


def _ret_direction(q_ref, k_ref, v_ref, cos_ref, sin_ref, dmat_ref, rin_ref, rout_ref, s_ref, o_ref,
                   reset, first):
    valid = _row_valid(first)
    cos = cos_ref[...]
    sin = sin_ref[...]
    chunk_decay = np.exp(CH * _ret_log_gamma())
    for h in range(N_HEADS):
        sl = slice(h * HEAD_D, (h + 1) * HEAD_D)
        q = q_ref[:, sl]
        k = k_ref[:, sl]
        v = v_ref[:, sl]
        q = q * cos + pltpu.roll(q, HEAD_D // 2, 1) * sin
        k = (k * cos + pltpu.roll(k, HEAD_D // 2, 1) * sin) * (HEAD_D ** -0.5)
        k = jnp.where(valid, k, 0.0)
        scores = _dot_nt(q, k) * dmat_ref[h]
        state = jnp.where(reset, 0.0, s_ref[h])
        o_ref[:, sl] = _dot(jnp.concatenate([scores, q * rin_ref[:, sl]], axis=1),
                            jnp.concatenate([v, state], axis=0)).astype(o_ref.dtype)
        s_ref[h] = float(chunk_decay[h]) * state + _dot_tn(k * rout_ref[:, sl], v)


def _ret_kernel(cpos_ref, clast_ref, qf, kf, vf, cosf, sinf, qb, kb, vb, cosb, sinb, dmat_ref, rows_ref,
                of_ref, ob_ref, sf_ref, sb_ref):
    g = pl.program_id(0)
    gb = pl.num_programs(0) - 1 - g
    _ret_direction(qf, kf, vf, cosf, sinf, dmat_ref.at[0], rows_ref.at[0], rows_ref.at[1], sf_ref, of_ref,
                   cpos_ref[g] == 0, cpos_ref[g] == 0)
    _ret_direction(qb, kb, vb, cosb, sinb, dmat_ref.at[1], rows_ref.at[2], rows_ref.at[3], sb_ref, ob_ref,
                   clast_ref[gb] == 1, cpos_ref[gb] == 0)


def _rope_tables(n_rows):
    inv = ROPE_BASE ** (-jnp.arange(0, HEAD_D, 2, dtype=jnp.float32) / HEAD_D)
    pos = jnp.maximum(jnp.arange(n_rows, dtype=jnp.float32) - PAD_ROWS, 0.0)
    ang = pos[:, None] * inv[None, :]
    cos, sin = jnp.cos(ang), jnp.sin(ang)
    return jnp.concatenate([cos, cos], axis=1), jnp.concatenate([-sin, sin], axis=1)


def _retention(z, cpos, clast, max_rows):
    t = z.shape[0]
    n = t // CH
    cos2, sin2 = _rope_tables(max_rows)
    dmat, rows = _ret_consts()
    blk = (CH, GROUP_W)
    in_specs = []
    for mk, pm in ((_fwd_map, lambda g, cp, cl: (cp[g], 0)),
                   (functools.partial(_bwd_map, n), lambda g, cp, cl: (cp[n - 1 - g], 0))):
        in_specs += [pl.BlockSpec(blk, mk(0)), pl.BlockSpec(blk, mk(1)), pl.BlockSpec(blk, mk(2)),
                     pl.BlockSpec((CH, HEAD_D), pm), pl.BlockSpec((CH, HEAD_D), pm)]
    in_specs += [pl.BlockSpec(dmat.shape, lambda g, cp, cl: (0, 0, 0, 0)),
                 pl.BlockSpec(rows.shape, lambda g, cp, cl: (0, 0, 0))]
    return (_ret_kernel, in_specs, [z, z, z, cos2, sin2, z, z, z, cos2, sin2, dmat, rows],
            [pltpu.VMEM((N_HEADS, HEAD_D, HEAD_D), F32)] * 2)


HALO = SUBLANES


def _shift_rows(x, s, fill, reverse):
    row = lax.broadcasted_iota(jnp.int32, (CH, 1), 0)
    if reverse:
        return jnp.where(row < CH - s, pltpu.roll(x, CH - s, 0), fill)
    return jnp.where(row >= s, pltpu.roll(x, s, 0), fill)


def _lru_direction(x_ref, prev_ref, next_ref, convw_ref, convb_ref, wg_ref, bias_ref, lam_ref, ext_ref,
                   carry_ref, o_ref, reset, first, last, reverse):
    valid = _row_valid(first)
    ext_ref[HALO:HALO + CH, :] = jnp.where(valid, x_ref[...], 0.0)
    ext_ref[0:HALO, :] = jnp.where(first, 0.0, prev_ref[...])
    ext_ref[HALO + CH:, :] = jnp.where(last, 0.0, next_ref[...])
    xc = convb_ref[...] + ext_ref[HALO - 2:HALO - 2 + CH, :] * convw_ref[0:1, :]
    for tap in range(1, 4):
        xc = xc + ext_ref[HALO - 2 + tap:HALO - 2 + tap + CH, :] * convw_ref[tap:tap + 1, :]

    log_sig_lam = -_softplus(-lam_ref[...])
    parts_a, parts_u = [], []
    for grp in range(GROUP_W // LANES):
        sl = slice(grp * LANES, (grp + 1) * LANES)
        xg = xc[:, sl]
        pre = _dot(xg, wg_ref[grp])
        r = _sigmoid(pre[:, :LANES] + bias_ref[0:1, sl])
        i = _sigmoid(pre[:, LANES:] + bias_ref[1:2, sl])
        a = jnp.exp(LRU_C * log_sig_lam[:, sl] * r)
        u = jnp.sqrt(1.0 - a * a) * (i * xg)
        parts_a.append(a)
        parts_u.append(jnp.where(valid, u, 0.0))
    a = jnp.concatenate(parts_a, axis=1)
    u = jnp.concatenate(parts_u, axis=1)

    n_blk = CH // SUBLANES
    a = a.reshape(n_blk, SUBLANES, GROUP_W)
    u = u.reshape(n_blk, SUBLANES, GROUP_W)
    row_in_blk = lax.broadcasted_iota(jnp.int32, (1, SUBLANES, 1), 1)
    s = 1
    while s < SUBLANES:
        keep = (row_in_blk < SUBLANES - s) if reverse else (row_in_blk >= s)
        shift = SUBLANES - s if reverse else s
        u = u + a * jnp.where(keep, pltpu.roll(u, shift, 1), 0.0)
        a = a * jnp.where(keep, pltpu.roll(a, shift, 1), 1.0)
        s *= 2
    edge = 0 if reverse else SUBLANES - 1
    carry = jnp.where(reset, 0.0, carry_ref[...])
    blocks = [None] * n_blk
    for blk in (reversed(range(n_blk)) if reverse else range(n_blk)):
        blocks[blk] = u[blk] + a[blk] * carry
        carry = blocks[blk][edge:edge + 1, :]
    o_ref[...] = jnp.concatenate(blocks, axis=0).astype(o_ref.dtype)
    carry_ref[...] = carry


def _lru_kernel(cpos_ref, clast_ref, xf, pf, nf, xb, pb, nb, convw_ref, convb_ref, wg_ref, bias_ref, lam_ref,
                of_ref, ob_ref, extf_ref, extb_ref, cf_ref, cb_ref):
    g = pl.program_id(0)
    gb = pl.num_programs(0) - 1 - g
    _lru_direction(xf, pf, nf, convw_ref, convb_ref, wg_ref.at[0], bias_ref.at[0], lam_ref.at[0], extf_ref,
                   cf_ref, of_ref, cpos_ref[g] == 0, cpos_ref[g] == 0, clast_ref[g] == 1, False)
    _lru_direction(xb, pb, nb, convw_ref, convb_ref, wg_ref.at[1], bias_ref.at[1], lam_ref.at[1], extb_ref,
                   cb_ref, ob_ref, clast_ref[gb] == 1, cpos_ref[gb] == 0, clast_ref[gb] == 1, True)


def _lru_gate_weights(wa, wx):
    per = LANES // LRU_BW

    def block_diag(w):
        w = w.reshape(2, LRU_BLOCKS // per, per, LRU_BW, LRU_BW)
        eye = jnp.eye(per, dtype=w.dtype)
        return jnp.einsum("dgpij,pq->dgpiqj", w, eye).reshape(2, LRU_BLOCKS // per, LANES, LANES)

    return jnp.concatenate([block_diag(wa), block_diag(wx)], axis=-1).astype(BF16)


def _rglru(z, col, cpos, clast, conv_w, conv_b, wa, ba, wx, bx, lam):
    t = z.shape[0]
    n = t // CH
    per = CH // HALO
    n_halo = t // HALO
    blk = (CH, GROUP_W)
    hblk = (HALO, GROUP_W)
    wg = _lru_gate_weights(wa, wx)
    bias = jnp.stack([ba, bx], axis=1)
    lam = lam.reshape(2, 1, GROUP_W)

    def prev_f(g, cp, cl):
        return (jnp.maximum(g * per - 1, 0), col)

    def next_f(g, cp, cl):
        return (jnp.minimum((g + 1) * per, n_halo - 1), col)

    def prev_b(g, cp, cl):
        return (jnp.maximum((n - 1 - g) * per - 1, 0), col)

    def next_b(g, cp, cl):
        return (jnp.minimum((n - g) * per, n_halo - 1), col)

    full = lambda a: pl.BlockSpec(a.shape, lambda g, cp, cl: (0,) * a.ndim)
    conv_b2 = conv_b.reshape(1, GROUP_W)
    in_specs = [pl.BlockSpec(blk, _fwd_map(col)), pl.BlockSpec(hblk, prev_f), pl.BlockSpec(hblk, next_f),
                pl.BlockSpec(blk, _bwd_map(n, col)), pl.BlockSpec(hblk, prev_b), pl.BlockSpec(hblk, next_b),
                full(conv_w), full(conv_b2), full(wg), full(bias), full(lam)]
    return (_lru_kernel, in_specs, [z, z, z, z, z, z, conv_w, conv_b2, wg, bias, lam],
            [pltpu.VMEM((CH + 2 * HALO, GROUP_W), F32)] * 2 + [pltpu.VMEM((1, GROUP_W), F32)] * 2)


def _tri_consts():
    pos = np.arange(CH)
    lower = (pos[:, None] >= pos[None, :]).astype(np.float32)
    ops = np.stack([lower, lower.T])
    return jnp.asarray(np.tile(ops, (1, 1, PREFIX_TERMS)), BF16)


def _hg_masks():
    pos = np.arange(CH)
    same_blk = pos[:, None] // SUB == pos[None, :] // SUB
    same_grp = pos[:, None] // GRP == pos[None, :] // GRP
    return jnp.asarray(np.stack([same_blk, same_grp]).astype(np.float32))


def _hg_direction(q_ref, f_ref, v_ref, lb, cum_ref, mask_ref, st_ref, o_ref, reset, first, reverse):
    valid = _row_valid(first)
    sub_i =lax.broadcasted_iota(jnp.int32, (SUB, HEAD_D), 0)
    lane_j = lax.broadcasted_iota(jnp.int32, (SUB, HEAD_D), 1) & (SUB - 1)
    causal = (sub_i <= lane_j) if reverse else (sub_i >= lane_j)
    pick = [causal & (lane_j == j) for j in range(SUB)]
    cum_op = cum_ref[...]
    same_blk = mask_ref[0]
    same_grp = mask_ref[1]
    zero = jnp.zeros((SUB, HEAD_D), F32)
    n_blk = CH // SUB
    per = GRP // SUB
    n_grp = CH // GRP

    def edge(unit, idx):
        return unit * idx if reverse else unit * (idx + 1) - 1

    def split_product(b, qs, ks, bs, pieces):
        q_slabs, k_slabs = [], []
        for ref, q_active, k_active in pieces:
            qp, kp = [], []
            for i in range(n_blk):
                r = ref(i)
                qp.append(qs[i] * jnp.exp2(bs[i] - b[r:r + 1, :]) if q_active(i) else zero)
                kp.append(ks[i] * jnp.exp2(b[r:r + 1, :] - bs[i]) if k_active(i) else zero)
            q_slabs.append(jnp.concatenate(qp, axis=0))
            k_slabs.append(jnp.concatenate(kp, axis=0))
        return _dot_nt(jnp.concatenate(q_slabs, axis=1), jnp.concatenate(k_slabs, axis=1))

    level1 = []
    for c in (range(1, per) if reverse else range(per - 1)):
        level1.append((lambda i, c=c: edge(SUB, (i // per) * per + c),
                       (lambda i, c=c: i % per < c) if reverse else (lambda i, c=c: i % per > c),
                       lambda i, c=c: i % per == c))
    level2 = []
    for gc in (range(1, n_grp) if reverse else range(n_grp - 1)):
        level2.append((lambda i, gc=gc: edge(GRP, gc),
                       (lambda i, gc=gc: i // per < gc) if reverse else (lambda i, gc=gc: i // per > gc),
                       lambda i, gc=gc: i // per == gc))

    for h in range(N_HEADS):
        sl = slice(h * HEAD_D, (h + 1) * HEAD_D)
        lbh = lb[:, sl]
        q = _silu(q_ref[:, sl])
        f = lbh + (1.0 - lbh) * _sigmoid(f_ref[:, sl])
        k = jnp.where(valid, 1.0 - f, 0.0)
        v = v_ref[:, sl]
        b = _prefix_dot(cum_op, jnp.log(f)) * LOG2_E
        qs = [q[i * SUB:(i + 1) * SUB] for i in range(n_blk)]
        ks = [k[i * SUB:(i + 1) * SUB] for i in range(n_blk)]
        bs = [b[i * SUB:(i + 1) * SUB] for i in range(n_blk)]

        tiles = [qs[blk] * jnp.exp2(jnp.minimum(bs[blk] - bs[blk][j:j + 1, :], 0.0))
                 for blk in range(n_blk) for j in range(SUB)]
        pair = _dot_nt(jnp.concatenate(tiles, axis=0), k)
        rows = []
        for blk in range(n_blk):
            base = blk * SUB * SUB
            a_blk = jnp.where(pick[0], pair[base:base + SUB], 0.0)
            for j in range(1, SUB):
                a_blk = a_blk + jnp.where(pick[j], pair[base + j * SUB:base + (j + 1) * SUB], 0.0)
            rows.append(a_blk)
        scores = jnp.concatenate(rows, axis=0) * same_blk
        scores = scores + split_product(b, qs, ks, bs, level1) * same_grp
        scores = scores + split_product(b, qs, ks, bs, level2)

        state = jnp.where(reset, 0.0, st_ref[h])
        o_ref[:, sl] = (_dot(scores, v) + _dot_nt(q * jnp.exp2(b), state)).astype(o_ref.dtype)
        b_tot = b[0:1, :] if reverse else b[CH - 1:CH, :]
        st_ref[h] = state * jnp.exp2(b_tot) + _dot_tn(v, k * jnp.exp2(b_tot - b))


def _hg_lower_bound(lower_ref, layer):
    low = lower_ref[...]
    e = jnp.exp(low - jnp.max(low, axis=0, keepdims=True))
    soft = e / jnp.sum(e, axis=0, keepdims=True)
    lb = jnp.zeros((1, GROUP_W), F32)
    for l in range(1, layer + 1):
        lb = lb + soft[l:l + 1, :]
    return lb


def _hg_kernel(layer, cpos_ref, clast_ref, qf, ff, vf, qb, fb, vb, lower_ref, cum_ref, mask_ref,
               of_ref, ob_ref, sf_ref, sb_ref):
    g = pl.program_id(0)
    gb = pl.num_programs(0) - 1 - g
    lb = _hg_lower_bound(lower_ref, layer)
    _hg_direction(qf, ff, vf, lb, cum_ref.at[0], mask_ref, sf_ref, of_ref,
                  cpos_ref[g] == 0, cpos_ref[g] == 0, False)
    _hg_direction(qb, fb, vb, lb, cum_ref.at[1], mask_ref, sb_ref, ob_ref,
                  clast_ref[gb] == 1, cpos_ref[gb] == 0, True)


def _hgrn2(z, cpos, clast, lower, layer):
    t = z.shape[0]
    n = t // CH
    blk = (CH, GROUP_W)
    cum = _tri_consts()
    masks = _hg_masks()
    full = lambda a: pl.BlockSpec(a.shape, lambda g, cp, cl: (0,) * a.ndim)
    in_specs = [pl.BlockSpec(blk, _fwd_map(0)), pl.BlockSpec(blk, _fwd_map(1)), pl.BlockSpec(blk, _fwd_map(3)),
                pl.BlockSpec(blk, _bwd_map(n, 0)), pl.BlockSpec(blk, _bwd_map(n, 2)),
                pl.BlockSpec(blk, _bwd_map(n, 3)), full(lower), full(cum), full(masks)]
    return (functools.partial(_hg_kernel, layer), in_specs, [z, z, z, z, z, z, lower, cum, masks],
            [pltpu.VMEM((N_HEADS, HEAD_D, HEAD_D), F32)] * 2)


def _log_sigmoid(x):
    return jnp.minimum(x, 0.0) - jnp.log(1.0 + jnp.exp(-jnp.abs(x)))


def _ml_direction(q_ref, k_ref, v_ref, gi_ref, gf_ref, bias_ref, cum_op, spread_ref, s_ref, m_ref, o_ref,
                  reset, first, d, reverse):
    valid = _row_valid(first)
    cum = _prefix_dot(cum_op, _log_sigmoid(gf_ref[...] + bias_ref[1]))
    a = jnp.where(valid, gi_ref[...] + bias_ref[0] - cum, NEG_BIG)
    run = a
    s = 1
    while s < CH:
        run = jnp.maximum(run, _shift_rows(run, s, NEG_BIG, reverse))
        s *= 2
    m_st = jnp.where(reset, NEG_BIG, m_ref[...])
    mx = jnp.maximum(m_st, run)
    edge = 0 if reverse else CH - 1
    mx_last = mx[edge:edge + 1, :]
    m_ref[...] = cum[edge:edge + 1, :] + mx_last
    decay = jnp.exp(m_st - mx_last)

    stacked = jnp.concatenate([mx, m_st - mx, -(cum + mx), a - mx_last], axis=0) * LOG2_E
    spread_b = _spread_dot(stacked, spread_ref[...])
    mx_b, inter_b, floor_b, end_b = (spread_b[i * CH:(i + 1) * CH] for i in range(4))
    a_t = (a * LOG2_E).T

    row = lax.broadcasted_iota(jnp.int32, (CH, CH), 0)
    col = lax.broadcasted_iota(jnp.int32, (CH, CH), 1)
    causal = (col >= row) if reverse else (col <= row)
    ones = jnp.ones((CH, HEAD_D), F32)
    for h in range(N_HEADS):
        sl = slice(h * HEAD_D, (h + 1) * HEAD_D)
        x = d * N_HEADS + h
        q = q_ref[:, sl]
        k = jnp.where(valid, k_ref[:, sl] * (HEAD_D ** -0.5), 0.0)
        v_ext = jnp.concatenate([v_ref[:, sl], ones], axis=1)
        w = jnp.where(causal, jnp.exp2(jnp.minimum(a_t[x:x + 1, :] - mx_b[:, sl], 0.0)), 0.0)
        qk = _dot_nt(q, k) * w
        s_inter = jnp.exp2(jnp.minimum(inter_b[:, sl], 0.0))
        state = jnp.where(reset, 0.0, s_ref[h])
        ext = _dot(jnp.concatenate([qk, q * s_inter], axis=1), jnp.concatenate([v_ext, state], axis=0))
        o_ref[:, sl] = (ext[:, :HEAD_D] / jnp.maximum(jnp.abs(ext[:, HEAD_D:]), jnp.exp2(floor_b[:, sl]))
                        ).astype(o_ref.dtype)
        kw = k * jnp.exp2(jnp.minimum(end_b[:, sl], 0.0))
        s_ref[h] = decay[:, x:x + 1] * state + _dot_tn(kw, v_ext)


def _ml_kernel(cpos_ref, clast_ref, qf, kf, vf, gif, gff, qb, kb, vb, gib, gfb, bias_ref, cum_ref, spread_ref,
               of_ref, ob_ref, sfs, mfs, sbs, mbs):
    g = pl.program_id(0)
    gb = pl.num_programs(0) - 1 - g
    _ml_direction(qf, kf, vf, gif, gff, bias_ref, cum_ref[0], spread_ref.at[0], sfs, mfs, of_ref,
                  cpos_ref[g] == 0, cpos_ref[g] == 0, 0, False)
    _ml_direction(qb, kb, vb, gib, gfb, bias_ref, cum_ref[1], spread_ref.at[1], sbs, mbs, ob_ref,
                  clast_ref[gb] == 1, cpos_ref[gb] == 0, 1, True)


def _ml_spread():
    spread = np.zeros((2, LANES, GROUP_W), np.float32)
    for d in range(2):
        for h in range(N_HEADS):
            spread[d, d * N_HEADS + h, h * HEAD_D:(h + 1) * HEAD_D] = 1.0
    return jnp.asarray(np.tile(spread, (1, SPREAD_TERMS, 1)), BF16)


def _mlstm(z, gates, cpos, clast, ml_bi, ml_bf):
    t = z.shape[0]
    n = t // CH
    blk = (CH, GROUP_W)
    gblk = (CH, LANES)
    cum = _tri_consts()
    spread = _ml_spread()
    pad = jnp.zeros((LANES - 2 * N_HEADS,), F32)
    bias = jnp.stack([jnp.concatenate([ml_bi.reshape(-1), pad]),
                      jnp.concatenate([ml_bf.reshape(-1), pad])]).reshape(2, 1, LANES)
    full = lambda a: pl.BlockSpec(a.shape, lambda g, cp, cl: (0,) * a.ndim)
    in_specs = [pl.BlockSpec(blk, _fwd_map(5)), pl.BlockSpec(blk, _fwd_map(6)), pl.BlockSpec(blk, _fwd_map(7)),
                pl.BlockSpec(gblk, _fwd_map(0)), pl.BlockSpec(gblk, _fwd_map(1)),
                pl.BlockSpec(blk, _bwd_map(n, 5)), pl.BlockSpec(blk, _bwd_map(n, 6)),
                pl.BlockSpec(blk, _bwd_map(n, 7)),
                pl.BlockSpec(gblk, _bwd_map(n, 0)), pl.BlockSpec(gblk, _bwd_map(n, 1)),
                full(bias), full(cum), full(spread)]
    state = [pltpu.VMEM((N_HEADS, HEAD_D, 2 * HEAD_D), F32), pltpu.VMEM((1, LANES), F32)]
    return (_ml_kernel, in_specs, [z, z, z, gates, gates, z, z, z, gates, gates, bias, cum, spread], state * 2)


def _head_norm(x, center):
    outs = []
    for h in range(N_HEADS):
        xh = x[:, h * HEAD_D:(h + 1) * HEAD_D]
        if center:
            xh = xh - jnp.mean(xh, axis=1, keepdims=True)
        outs.append(xh * lax.rsqrt(jnp.mean(xh * xh, axis=1, keepdims=True) + EPS))
    return jnp.concatenate(outs, axis=1)


def _layer_norm(x, g, b):
    xc = x - jnp.mean(x, axis=1, keepdims=True)
    return xc * lax.rsqrt(jnp.mean(xc * xc, axis=1, keepdims=True) + EPS) * g + b


def _mix_rows(alpha, even, af, ab, ag, bf, bb, bg, h_ref, w_ref, lng_ref, lnb_ref):
    a = af[...].astype(F32) + ab[...].astype(F32)
    b = bf[...].astype(F32) + bb[...].astype(F32)
    if even:
        a = _head_norm(a, False) * _silu(ag[...])
        b = b * _gelu_tanh(bg[...])
    else:
        a = _head_norm(a, False) * _silu(ag[...])
        b = _head_norm(b, True) * _sigmoid(bg[...])
    mix = _dot(a, w_ref[0:GROUP_W, :]) + _dot(b, w_ref[GROUP_W:, :])
    return _layer_norm(alpha * h_ref[...] + mix, lng_ref[...], lnb_ref[...])


N_MIX_OPERANDS = 10


def _mix_specs(tm, col_a, col_b, index):
    half = lambda c: pl.BlockSpec((tm, GROUP_W), lambda *g: (index(*g), c))
    rows = pl.BlockSpec((tm, D_MODEL), lambda *g: (index(*g), 0))
    vec = pl.BlockSpec((1, D_MODEL), lambda *g: (0, 0))
    return [half(0), half(0), half(col_a), half(0), half(0), half(col_b), rows,
            pl.BlockSpec((D_MODEL, D_MODEL), lambda *g: (0, 0)), vec, vec]


def _mix_ffn_kernel(alpha, *refs):
    mix_refs = refs[:N_MIX_OPERANDS]
    wg_ref, wu_ref, wo_ref, lng_ref, lnb_ref, o_ref, h1_ref, acc_ref = refs[N_MIX_OPERANDS:]
    f = pl.program_id(1)

    @pl.when(f == 0)
    def _():
        h1_ref[...] = _mix_rows(alpha, True, *mix_refs)
        acc_ref[...] = jnp.zeros_like(acc_ref)

    x = h1_ref[...].astype(BF16)
    act = _silu(_dot(x, wg_ref[...])) * _dot(x, wu_ref[...])
    acc_ref[...] += _dot(act, wo_ref[...])

    @pl.when(f == pl.num_programs(1) - 1)
    def _():
        o_ref[...] = _layer_norm(alpha * h1_ref[...] + acc_ref[...], lng_ref[...], lnb_ref[...])


def _mix_ffn(alpha, af, ab, bf, bb, z, col_a, col_b, h, w_out, ln1_g, ln1_b, wi, wo, ln2_g, ln2_b):
    t = h.shape[0]
    d_ff = wo.shape[0]
    tm = _pick_tile(t, 640)
    tf = _pick_tile(d_ff, 1536)
    nf = d_ff // tf
    rows = pl.BlockSpec((tm, D_MODEL), lambda i, f: (i, 0))
    vec = pl.BlockSpec((1, D_MODEL), lambda i, f: (0, 0))
    return pl.pallas_call(
        functools.partial(_mix_ffn_kernel, alpha), grid=(t // tm, nf),
        in_specs=_mix_specs(tm, col_a, col_b, lambda i, f: i) + [
            pl.BlockSpec((D_MODEL, tf), lambda i, f: (0, f)),
            pl.BlockSpec((D_MODEL, tf), lambda i, f: (0, f + nf)),
            pl.BlockSpec((tf, D_MODEL), lambda i, f: (f, 0)), vec, vec],
        out_specs=rows, out_shape=jax.ShapeDtypeStruct((t, D_MODEL), F32),
        scratch_shapes=[pltpu.VMEM((tm, D_MODEL), F32), pltpu.VMEM((tm, D_MODEL), F32)],
        compiler_params=_params(2), name="mix_ffn")(
            af, ab, z, bf, bb, z, h, w_out, ln1_g.reshape(1, -1), ln1_b.reshape(1, -1),
            wi, wi, wo, ln2_g.reshape(1, -1), ln2_b.reshape(1, -1))


EXPERT_TILE = 1024
DMA_UNROLL = 8
DMA_PRIORITIES = 2


def _mix_route_kernel(alpha, pad_starts, *refs):
    mix_refs = refs[:N_MIX_OPERANDS]
    router_ref, tri_ref, o_ref, meta_ref, gate_ref, cnt_ref, carry_ref = refs[N_MIX_OPERANDS:]

    @pl.when(pl.program_id(0) == 0)
    def _():
        carry_ref[...] = jnp.zeros_like(carry_ref)

    h1 = _mix_rows(alpha, False, *mix_refs)
    o_ref[...] = h1
    tm = h1.shape[0]
    row = pl.program_id(0) * tm + lax.broadcasted_iota(jnp.int32, (tm, 1), 0)
    is_pad = row < 0
    for s in pad_starts:
        is_pad = is_pad | ((row >= s) & (row < s + PAD_ROWS))
    is_token = jnp.logical_not(is_pad)

    logits = _dot_16bit(h1, router_ref[...])
    lane = lax.broadcasted_iota(jnp.int32, logits.shape, 1)
    logits = jnp.where(lane < N_EXP, logits, -jnp.inf)
    top1 = jnp.max(logits, axis=1, keepdims=True)
    idx1 = jnp.min(jnp.where(logits == top1, lane, LANES), axis=1, keepdims=True)
    rest = jnp.where(lane == idx1, -jnp.inf, logits)
    top2 = jnp.max(rest, axis=1, keepdims=True)
    idx2 = jnp.min(jnp.where(rest == top2, lane, LANES), axis=1, keepdims=True)
    g2 = jnp.exp(top2 - top1)
    denom = 1.0 + g2
    hit1 = lane == idx1
    hit2 = lane == idx2
    both = jnp.where(is_token, jnp.where(hit1, 1.0, 0.0) + jnp.where(hit2, 1.0, 0.0), 0.0)
    prefix = _dot(tri_ref[...], both) + carry_ref[...]
    rank1 = jnp.sum(jnp.where(hit1, prefix, 0.0), axis=1, keepdims=True).astype(jnp.int32)
    rank2 = jnp.sum(jnp.where(hit2, prefix, 0.0), axis=1, keepdims=True).astype(jnp.int32)
    carry_ref[...] += jnp.sum(both, axis=0, keepdims=True)
    cnt_ref[...] = carry_ref[...]
    meta_ref[...] = jnp.where(lane == 0, idx1, jnp.where(lane == 1, idx2, jnp.where(
        lane == 2, rank1, jnp.where(lane == 3, rank2, jnp.where(is_token & (lane == 4), 1, 0)))))
    gate_ref[...] = jnp.where(lane == 0, 1.0 / denom, jnp.where(lane == 1, g2 / denom, 0.0))


def _mix_route(alpha, af, ab, bf, bb, z, col_a, col_b, h, w_out, ln_g, ln_b, router, pad_starts):
    t = h.shape[0]
    tm = _pick_tile(t, 512)
    router_p = _stack_weight_terms(jnp.pad(router, ((0, 0), (0, LANES - router.shape[1]))))
    pos = np.arange(tm)
    tri = jnp.asarray(pos[:, None] > pos[None, :], BF16)
    rows = lambda w: pl.BlockSpec((tm, w), lambda i: (i, 0))
    const = lambda a: pl.BlockSpec(a.shape, lambda i: (0, 0))
    return pl.pallas_call(
        functools.partial(_mix_route_kernel, alpha, pad_starts), grid=(t // tm,),
        in_specs=_mix_specs(tm, col_a, col_b, lambda i: i) + [const(router_p), const(tri)],
        out_specs=[rows(D_MODEL), rows(LANES), rows(LANES), pl.BlockSpec((1, LANES), lambda i: (0, 0))],
        out_shape=[jax.ShapeDtypeStruct((t, D_MODEL), F32), jax.ShapeDtypeStruct((t, LANES), jnp.int32),
                   jax.ShapeDtypeStruct((t, LANES), F32), jax.ShapeDtypeStruct((1, LANES), F32)],
        scratch_shapes=[pltpu.VMEM((1, LANES), F32)],
        compiler_params=_params(1), name="mix_route")(
            af, ab, z, bf, bb, z, h, w_out, ln_g.reshape(1, -1), ln_b.reshape(1, -1), router_p, tri)


def _dispatch_plan(meta, counts, t):
    cnt = counts[0, :N_EXP].astype(jnp.int32)
    padded = ((cnt + EXPERT_TILE - 1) // EXPERT_TILE) * EXPERT_TILE
    ends = jnp.cumsum(padded)
    off = ends - padded

    def base(e):
        return sum(jnp.where(e == i, off[i], 0) for i in range(N_EXP))

    is_token = meta[:, 4] > 0
    n_tiles = -(-2 * t // EXPERT_TILE) + N_EXP
    spare = n_tiles * EXPERT_TILE + 2 * (jnp.cumsum(jnp.logical_not(is_token).astype(jnp.int32)) - 1)
    pos1 = (base(meta[:, 0]) + meta[:, 2]).astype(jnp.int32)
    pos2 = (base(meta[:, 1]) + meta[:, 3]).astype(jnp.int32)
    scatter = (jnp.where(is_token, pos1, spare), jnp.where(is_token, pos2, spare + 1))
    gather = (jnp.where(is_token, pos1, 0), jnp.where(is_token, pos2, 0))
    starts = jnp.arange(n_tiles, dtype=jnp.int32) * EXPERT_TILE
    tile_expert = jnp.minimum(jnp.sum(starts[:, None] >= ends[None, :], axis=1), N_EXP - 1).astype(jnp.int32)
    n_active = (ends[-1] // EXPERT_TILE).astype(jnp.int32).reshape(1)
    return ends.astype(jnp.int32), scatter, gather, tile_expert, n_active, n_tiles


assert D_MODEL == SUBLANES * LANES


def _store_token_tiles(ref, x):
    n = x.shape[0]
    for s in range(SUBLANES):
        ref[pl.ds(s, n, stride=SUBLANES), :] = x[:, s * LANES:(s + 1) * LANES]


def _load_token_tiles(ref, n):
    return jnp.concatenate([ref[pl.ds(s, n, stride=SUBLANES), :] for s in range(SUBLANES)], axis=1)


def _tile_rows(i):
    return pl.ds(pl.multiple_of(i * SUBLANES, SUBLANES), SUBLANES)


def _dispatch_kernel(n_tiles, ends_ref, pos1_ref, pos2_ref, h_ref, xs_ref, tok_ref, zero_ref, sem, zero_sem):
    n = h_ref.shape[0]

    @pl.when(pl.program_id(0) == 0)
    def _():
        zero_ref[...] = jnp.zeros_like(zero_ref)

        def clear_last_tile(e):
            first_row = pl.multiple_of((ends_ref[e] - EXPERT_TILE) * SUBLANES, SUBLANES)
            return pltpu.make_async_copy(zero_ref, xs_ref.at[pl.ds(first_row, EXPERT_TILE * SUBLANES)], zero_sem)

        def has_tiles(e):
            return ends_ref[e] > (ends_ref[e - 1] if e else 0)

        for e in range(N_EXP):
            pl.when(has_tiles(e))(lambda e=e: clear_last_tile(e).start())
        for e in range(N_EXP):
            pl.when(has_tiles(e))(lambda e=e: clear_last_tile(e).wait())

        def clear_tile(i):
            first_row = pl.multiple_of(i * (EXPERT_TILE * SUBLANES), SUBLANES)
            return pltpu.make_async_copy(zero_ref, xs_ref.at[pl.ds(first_row, EXPERT_TILE * SUBLANES)], zero_sem)

        used = ends_ref[N_EXP - 1] // EXPERT_TILE
        lax.fori_loop(used, n_tiles, lambda i, c: (clear_tile(i).start(), c)[1], 0)
        lax.fori_loop(used, n_tiles, lambda i, c: (clear_tile(i).wait(), c)[1], 0)

    g = pl.program_id(0)
    slot = g % 2
    _store_token_tiles(tok_ref.at[slot], h_ref[...])

    def tile_copy(r, p):
        return pltpu.make_async_copy(tok_ref.at[slot].at[_tile_rows(r)], xs_ref.at[_tile_rows(p)], sem.at[slot])

    def start(pair, c):
        for prio in range(DMA_PRIORITIES):
            r = pair * DMA_PRIORITIES + prio
            tile_copy(r, pos1_ref[0, 0, r]).start(priority=prio)
            tile_copy(r, pos2_ref[0, 0, r]).start(priority=prio)
        return c

    lax.fori_loop(0, n // DMA_PRIORITIES, start, 0, unroll=DMA_UNROLL // DMA_PRIORITIES)

    def wait_slot(s):
        all_rows = pltpu.make_async_copy(tok_ref.at[s], xs_ref.at[pl.ds(0, n * SUBLANES)], sem.at[s])
        all_rows.wait()
        all_rows.wait()

    pl.when(g > 0)(lambda: wait_slot(1 - slot))
    pl.when(g == pl.num_programs(0) - 1)(lambda: wait_slot(slot))


def _dispatch(h, ends, pos1, pos2, n_tiles, n_spare):
    t = h.shape[0]
    n_rows = n_tiles * EXPERT_TILE + n_spare
    tm = _pick_tile(t, 512)
    idx = lambda: pl.BlockSpec((1, 1, tm), lambda i, ends: (i, 0, 0), memory_space=pltpu.SMEM)
    grid_spec = pltpu.PrefetchScalarGridSpec(
        num_scalar_prefetch=1, grid=(t // tm,),
        in_specs=[idx(), idx(), pl.BlockSpec((tm, D_MODEL), lambda i, ends: (i, 0))],
        out_specs=pl.BlockSpec(memory_space=pl.ANY),
        scratch_shapes=[pltpu.VMEM((2, tm * SUBLANES, LANES), F32),
                        pltpu.VMEM((EXPERT_TILE * SUBLANES, LANES), F32),
                        pltpu.SemaphoreType.DMA((2,)), pltpu.SemaphoreType.DMA(())])
    return pl.pallas_call(
        functools.partial(_dispatch_kernel, n_tiles), grid_spec=grid_spec,
        out_shape=jax.ShapeDtypeStruct((n_rows * SUBLANES, LANES), F32),
        compiler_params=_params(1), name="dispatch")(
            ends, pos1.reshape(t // tm, 1, tm), pos2.reshape(t // tm, 1, tm), h)


def _experts_kernel(te_ref, na_ref, x_ref, wg_ref, wu_ref, wo_ref, y_ref, xb_ref, acc_ref):
    i = pl.program_id(0)
    f = pl.program_id(1)
    last = f == pl.num_programs(1) - 1
    active = i < na_ref[0]

    @pl.when(active)
    def _():
        @pl.when(f == 0)
        def _():
            acc_ref[...] = jnp.zeros_like(acc_ref)
            xb_ref[...] = _load_token_tiles(x_ref, EXPERT_TILE).astype(BF16)

        x = xb_ref[...]
        act = _silu(_dot(x, wg_ref[0])) * _dot(x, wu_ref[0])
        acc_ref[...] += _dot(act, wo_ref[0])

        @pl.when(last)
        def _():
            _store_token_tiles(y_ref, acc_ref[...])

    @pl.when(jnp.logical_not(active) & last)
    def _():
        y_ref[...] = jnp.zeros_like(y_ref)


def _experts(xs, tile_expert, n_active, wi, wo, n_tiles):
    n_rows = n_tiles * EXPERT_TILE
    e_ff = wo.shape[1]
    tf = _pick_tile(e_ff, 512)
    nf = e_ff // tf
    rows = pl.BlockSpec((EXPERT_TILE * SUBLANES, LANES), lambda i, f, te, na: (i, 0))
    rows_in = pl.BlockSpec((EXPERT_TILE * SUBLANES, LANES),
                           lambda i, f, te, na: (jnp.minimum(i, jnp.maximum(na[0] - 1, 0)), 0))
    grid_spec = pltpu.PrefetchScalarGridSpec(
        num_scalar_prefetch=2, grid=(n_rows // EXPERT_TILE, nf),
        in_specs=[rows_in, pl.BlockSpec((1, D_MODEL, tf), lambda i, f, te, na: (te[i], 0, f)),
                  pl.BlockSpec((1, D_MODEL, tf), lambda i, f, te, na: (te[i], 0, f + nf)),
                  pl.BlockSpec((1, tf, D_MODEL), lambda i, f, te, na: (te[i], f, 0))],
        out_specs=rows,
        scratch_shapes=[pltpu.VMEM((EXPERT_TILE, D_MODEL), BF16), pltpu.VMEM((EXPERT_TILE, D_MODEL), F32)])
    return pl.pallas_call(
        _experts_kernel, grid_spec=grid_spec, out_shape=jax.ShapeDtypeStruct((n_rows * SUBLANES, LANES), F32),
        compiler_params=_params(2), name="experts")(tile_expert, n_active, xs, wi, wi, wo)


def _gather_start(n, pos1_ref, pos2_ref, ys_ref, a_ref, b_ref, sems):
    def tile_copy(p, buf, r, which):
        return pltpu.make_async_copy(ys_ref.at[_tile_rows(p)], buf.at[_tile_rows(r)], sems.at[which])

    def start(pair, c):
        for prio in range(DMA_PRIORITIES):
            r = pair * DMA_PRIORITIES + prio
            tile_copy(pos1_ref[0, 0, r], a_ref, r, 0).start(priority=prio)
            tile_copy(pos2_ref[0, 0, r], b_ref, r, 1).start(priority=prio)
        return c

    lax.fori_loop(0, n // DMA_PRIORITIES, start, 0, unroll=DMA_UNROLL // DMA_PRIORITIES)


def _gather_finish(alpha, h_ref, gate_ref, ys_ref, lng_ref, lnb_ref, a_ref, b_ref, sems):
    n = h_ref.shape[0]
    for which, buf in enumerate((a_ref, b_ref)):
        pltpu.make_async_copy(ys_ref.at[pl.ds(0, n * SUBLANES)], buf, sems.at[which]).wait()
    gate = gate_ref[...]
    y = gate[:, 0:1] * _load_token_tiles(a_ref, n) + gate[:, 1:2] * _load_token_tiles(b_ref, n)
    return _layer_norm(alpha * h_ref[...] + y, lng_ref[...], lnb_ref[...])


def _combine_kernel(alpha, pos1_ref, pos2_ref, h_ref, gate_ref, ys_ref, lng_ref, lnb_ref, o_ref,
                    a_ref, b_ref, sems):
    _gather_start(h_ref.shape[0], pos1_ref, pos2_ref, ys_ref, a_ref, b_ref, sems)
    o_ref[...] = _gather_finish(alpha, h_ref, gate_ref, ys_ref, lng_ref, lnb_ref, a_ref, b_ref, sems)


def _combine_scratch(tm):
    return [pltpu.VMEM((tm * SUBLANES, LANES), F32), pltpu.VMEM((tm * SUBLANES, LANES), F32),
            pltpu.SemaphoreType.DMA((2,))]


def _combine(alpha, h, gates, ys, pos1, pos2, ln_g, ln_b):
    t = h.shape[0]
    tm = _pick_tile(t, 512)
    idx = lambda: pl.BlockSpec((1, 1, tm), lambda i: (i, 0, 0), memory_space=pltpu.SMEM)
    rows = pl.BlockSpec((tm, D_MODEL), lambda i: (i, 0))
    vec = pl.BlockSpec((1, D_MODEL), lambda i: (0, 0))
    return pl.pallas_call(
        functools.partial(_combine_kernel, alpha), grid=(t // tm,),
        in_specs=[idx(), idx(), rows, pl.BlockSpec((tm, LANES), lambda i: (i, 0)),
                  pl.BlockSpec(memory_space=pl.ANY), vec, vec],
        out_specs=rows, out_shape=jax.ShapeDtypeStruct((t, D_MODEL), F32),
        scratch_shapes=_combine_scratch(tm),
        compiler_params=_params(1), name="combine")(
            pos1.reshape(t // tm, 1, tm), pos2.reshape(t // tm, 1, tm), h, gates, ys,
            ln_g.reshape(1, -1), ln_b.reshape(1, -1))


def _output_tables(seq_shapes):
    kind, live = [], []
    window = [[], []]
    recent = [(0, 0), (0, 0)]
    for grp, (b, l) in enumerate(seq_shapes):
        for i in range(b):
            for c in range(l // CH + 1):
                kind.append(grp)
                live.append(int(c > 0))
                recent[grp] = (i, max(c - 1, 0))
                for g2 in range(2):
                    window[g2].append(recent[g2])
    tables = [kind, live] + [[w[j] for w in window[g2]] for g2 in range(2) for j in range(2)]
    return [jnp.asarray(np.asarray(x, np.int32)) for x in tables]


def _combine_out_kernel(alpha, kind_ref, live_ref, ab_ref, ac_ref, bb_ref, bc_ref,
                        pos1_ref, pos2_ref, nxt1_ref, nxt2_ref, h_ref, gate_ref, ys_ref, lng_ref, lnb_ref,
                        outa_ref, outb_ref, a_ref, b_ref, sems):
    g = pl.program_id(0)
    last = pl.num_programs(0) - 1
    slot = g % 2
    nxt = jnp.minimum(g + 1, last)

    @pl.when((g == 0) & (live_ref[0] == 1))
    def _():
        _gather_start(CH, pos1_ref, pos2_ref, ys_ref, a_ref.at[0], b_ref.at[0], sems.at[0])

    @pl.when((g < last) & (live_ref[nxt] == 1))
    def _():
        _gather_start(CH, nxt1_ref, nxt2_ref, ys_ref, a_ref.at[1 - slot], b_ref.at[1 - slot], sems.at[1 - slot])

    @pl.when(live_ref[g] == 1)
    def _():
        res = _gather_finish(alpha, h_ref, gate_ref, ys_ref, lng_ref, lnb_ref,
                             a_ref.at[slot], b_ref.at[slot], sems.at[slot])
        for grp, out_ref in enumerate((outa_ref, outb_ref)):
            @pl.when(kind_ref[g] == grp)
            def _(out_ref=out_ref):
                out_ref[0] = res


def _combine_out(alpha, h, gates, ys, pos1, pos2, ln_g, ln_b, seq_shapes):
    t = h.shape[0]
    n = t // CH
    tables = _output_tables(seq_shapes)
    idx = lambda: pl.BlockSpec((1, 1, CH), lambda g, *tb: (g, 0, 0), memory_space=pltpu.SMEM)
    idx_next = lambda: pl.BlockSpec((1, 1, CH), lambda g, *tb: (jnp.minimum(g + 1, n - 1), 0, 0),
                                    memory_space=pltpu.SMEM)
    rows = lambda w: pl.BlockSpec((CH, w), lambda g, *tb: (g, 0))
    vec = pl.BlockSpec((1, D_MODEL), lambda g, *tb: (0, 0))
    out = lambda grp: pl.BlockSpec((1, CH, D_MODEL),
                                   lambda g, *tb: (tb[2 + 2 * grp][g], tb[3 + 2 * grp][g], 0))
    fetched = pltpu.VMEM((2, CH * SUBLANES, LANES), F32)
    grid_spec = pltpu.PrefetchScalarGridSpec(
        num_scalar_prefetch=len(tables), grid=(n,),
        in_specs=[idx(), idx(), idx_next(), idx_next(), rows(D_MODEL), rows(LANES),
                  pl.BlockSpec(memory_space=pl.ANY), vec, vec],
        out_specs=[out(0), out(1)], scratch_shapes=[fetched, fetched, pltpu.SemaphoreType.DMA((2, 2))])
    pos1, pos2 = pos1.reshape(n, 1, CH), pos2.reshape(n, 1, CH)
    return pl.pallas_call(
        functools.partial(_combine_out_kernel, alpha), grid_spec=grid_spec,
        out_shape=[jax.ShapeDtypeStruct((b, l, D_MODEL), F32) for b, l in seq_shapes],
        compiler_params=_params(1), name="combine_out")(
            *tables, pos1, pos2, pos1, pos2, h, gates, ys, ln_g.reshape(1, -1), ln_b.reshape(1, -1))


def _moe(alpha, h, meta, gates, counts, wi, wo, ln_g, ln_b, n_pad_rows, out_shapes=None):
    t = h.shape[0]
    ends, scatter, gather, tile_expert, n_active, n_tiles = _dispatch_plan(meta, counts, t)
    n_spare = 2 * n_pad_rows
    xs = _dispatch(h, ends, scatter[0], scatter[1], n_tiles, n_spare)
    ys = _experts(xs, tile_expert, n_active, wi, wo, n_tiles)
    if out_shapes is None:
        return _combine(alpha, h, gates, ys, gather[0], gather[1], ln_g, ln_b)
    return _combine_out(alpha, h, gates, ys, gather[0], gather[1], ln_g, ln_b, out_shapes)


def kernel(x_prompt, x_sample, meta, e_w_in, e_conv_w, e_conv_b, e_lru_wa, e_lru_ba, e_lru_wx, e_lru_bx,
           e_lru_lambda, e_w_out, e_ffn_wi, e_ffn_wo, o_w_in, o_hg_lower, o_ml_bi, o_ml_bf, o_w_out, o_router,
           o_exp_wi, o_exp_wo, ln_g, ln_b):
    groups = (x_prompt, x_sample)
    depth = ln_g.shape[0]
    alpha = (2.0 * depth) ** 0.25
    seq_shapes = [(x.shape[0], x.shape[1]) for x in groups]
    for _, l in seq_shapes:
        assert l % CH == 0
    cpos_np, clast_np = _chunk_tables(seq_shapes)
    cpos, clast = jnp.asarray(cpos_np), jnp.asarray(clast_np)
    max_rows = max(l for _, l in seq_shapes) + CH
    pad_starts = tuple(int(i) * CH for i in np.flatnonzero(cpos_np == 0))

    head = jnp.concatenate([jnp.zeros((PAD_ROWS, D_MODEL), F32), meta.astype(F32)], axis=0)
    parts = []
    for x in groups:
        full = jnp.concatenate([jnp.broadcast_to(head[None], (x.shape[0], CH, D_MODEL)), x], axis=1)
        parts.append(full.reshape(-1, D_MODEL))
    h = jnp.concatenate(parts, axis=0)
    t = h.shape[0]

    n_odd_cols = 9 * GROUP_W
    for layer in range(depth):
        p = layer // 2
        if layer % 2 == 0:
            z = _project(h, e_w_in[p].astype(BF16))
            (ret_f, ret_b), (lru_f, lru_b) = _chunk_walk(
                [_retention(z, cpos, clast, max_rows),
                 _rglru(z, 4, cpos, clast, e_conv_w[p], e_conv_b[p], e_lru_wa[p], e_lru_ba[p],
                        e_lru_wx[p], e_lru_bx[p], e_lru_lambda[p])], cpos, clast, t, "even_mixers")
            h = _mix_ffn(alpha, ret_f, ret_b, lru_f, lru_b, z, 3, 5, h, e_w_out[p].astype(BF16),
                         ln_g[layer, 0], ln_b[layer, 0], e_ffn_wi[p].astype(BF16), e_ffn_wo[p].astype(BF16),
                         ln_g[layer, 1], ln_b[layer, 1])
        else:
            w_in = o_w_in[p]
            n_gate = 2 * N_HEADS
            gate_pad = ((0, 0), (0, LANES - n_gate))
            w_gates = jnp.concatenate([jnp.pad(w_in[:, n_odd_cols:n_odd_cols + n_gate], gate_pad),
                                       jnp.pad(w_in[:, n_odd_cols + n_gate:], gate_pad)], axis=1)
            z, gates = _project(h, w_in[:, :n_odd_cols].astype(BF16), w_gates)
            (hg_f, hg_b), (ml_f, ml_b) = _chunk_walk(
                [_hgrn2(z, cpos, clast, o_hg_lower, layer),
                 _mlstm(z, gates, cpos, clast, o_ml_bi[p], o_ml_bf[p])], cpos, clast, t, "odd_mixers")
            h, meta_r, gates_r, counts = _mix_route(
                alpha, hg_f, hg_b, ml_f, ml_b, z, 4, 8, h, o_w_out[p].astype(BF16),
                ln_g[layer, 0], ln_b[layer, 0], o_router[p], pad_starts)
            h = _moe(alpha, h, meta_r, gates_r, counts, o_exp_wi[p].astype(BF16), o_exp_wo[p].astype(BF16),
                     ln_g[layer, 1], ln_b[layer, 1], len(pad_starts) * PAD_ROWS,
                     seq_shapes if layer == depth - 1 else None)
            if layer == depth - 1:
                return tuple(h)

    outs = []
    row = 0
    for b, l in seq_shapes:
        n = b * (l + CH)
        outs.append(h[row:row + n].reshape(b, l + CH, D_MODEL)[:, CH:])
        row += n
    return tuple(outs)
```

```python
import functools
import math

import numpy as np
import jax
import jax.numpy as jnp
from jax import lax
from jax.experimental import pallas as pl
from jax.experimental.pallas import tpu as pltpu

F32 = jnp.float32
BF16 = jnp.bfloat16

D_MODEL = 1024
GROUP_W = D_MODEL // 2
N_HEADS = 4
HEAD_D = GROUP_W // N_HEADS
N_META = 16
ROPE_BASE = 10000.0
LRU_BLOCKS = 8
LRU_BW = GROUP_W // LRU_BLOCKS
LRU_C = 8.0
N_EXP = 8
EPS = 1e-5

LANES = 128
SUBLANES = 8
CH = 128
PAD_ROWS = CH - N_META
SUB = SUBLANES
GRP = 4 * SUB
NEG_BIG = -1e30
LOG2_E = math.log2(math.e)
VMEM_LIMIT = 56 * 1024 * 1024


def _dot(a, b):
    return jnp.dot(a.astype(BF16), b.astype(BF16), preferred_element_type=F32)


def _dot_nt(a, b):
    return lax.dot_general(a.astype(BF16), b.astype(BF16), (((1,), (1,)), ((), ())),
                           preferred_element_type=F32)


def _dot_tn(a, b):
    return _dot(a.T, b)


def _bf16_terms(x, n_terms):
    terms = []
    for _ in range(n_terms):
        t = x.astype(BF16)
        terms.append(t)
        x = x - t.astype(F32)
    return terms


def _stack_weight_terms(w):
    hi, lo = _bf16_terms(w, 2)
    return jnp.concatenate([hi, lo, hi], axis=0)


def _dot_16bit(x, w_terms):
    hi, lo = _bf16_terms(x, 2)
    return jnp.dot(jnp.concatenate([hi, hi, lo], axis=1), w_terms, preferred_element_type=F32)


PREFIX_TERMS = 3
SPREAD_TERMS = 2


def _prefix_dot(op_tiled, x):
    return jnp.dot(op_tiled, jnp.concatenate(_bf16_terms(x, PREFIX_TERMS), axis=0),
                   preferred_element_type=F32)


def _spread_dot(x, op_tiled):
    return jnp.dot(jnp.concatenate(_bf16_terms(x, SPREAD_TERMS), axis=1), op_tiled,
                   preferred_element_type=F32)


def _sigmoid(x):
    return 1.0 / (1.0 + jnp.exp(-x))


def _silu(x):
    return x * _sigmoid(x)


def _gelu_tanh(x):
    return 0.5 * x * (1.0 + jnp.tanh(math.sqrt(2.0 / math.pi) * (x + 0.044715 * (x * x * x))))


def _softplus(x):
    return jnp.maximum(x, 0.0) + jnp.log(1.0 + jnp.exp(-jnp.abs(x)))


def _pick_tile(total, target):
    best = LANES
    for t in range(LANES, min(total, target) + 1, LANES):
        if total % t == 0:
            best = t
    return best


def _params(n_axes, sem="arbitrary"):
    return pltpu.CompilerParams(dimension_semantics=(sem,) * n_axes, vmem_limit_bytes=VMEM_LIMIT)


def _proj_kernel(x_ref, w_ref, o_ref):
    o_ref[...] = _dot(x_ref[...], w_ref[...])


def _proj_gates_kernel(x_ref, w_ref, wg_ref, o_ref, g_ref):
    g_ref[...] = _dot_16bit(x_ref[...], wg_ref[...])
    o_ref[...] = _dot(x_ref[...], w_ref[...])


PROJ_OUT_BLOCK_BYTES = 8 * 1024 * 1024


def _project(x, w, w_gates=None):
    t, k = x.shape
    n = w.shape[1]
    tm = _pick_tile(t, PROJ_OUT_BLOCK_BYTES // (4 * n))
    x_spec = pl.BlockSpec((tm, k), lambda i: (i, 0))
    w_spec = pl.BlockSpec((k, n), lambda i: (0, 0))
    o_spec = pl.BlockSpec((tm, n), lambda i: (i, 0))
    if w_gates is None:
        return pl.pallas_call(
            _proj_kernel, grid=(t // tm,), in_specs=[x_spec, w_spec], out_specs=o_spec,
            out_shape=jax.ShapeDtypeStruct((t, n), F32), compiler_params=_params(1),
            name="proj")(x, w)
    ng = w_gates.shape[1]
    w_gates = _stack_weight_terms(w_gates)
    return pl.pallas_call(
        _proj_gates_kernel, grid=(t // tm,),
        in_specs=[x_spec, w_spec, pl.BlockSpec(w_gates.shape, lambda i: (0, 0))],
        out_specs=[o_spec, pl.BlockSpec((tm, ng), lambda i: (i, 0))],
        out_shape=[jax.ShapeDtypeStruct((t, n), F32), jax.ShapeDtypeStruct((t, ng), F32)],
        compiler_params=_params(1), name="proj_gates")(x, w, w_gates)


def _chunk_tables(seq_shapes):
    cpos, clast = [], []
    for b, l in seq_shapes:
        n = l // CH + 1
        for _ in range(b):
            cpos += list(range(n))
            clast += [0] * (n - 1) + [1]
    return np.asarray(cpos, np.int32), np.asarray(clast, np.int32)


def _fwd_map(col):
    return lambda g, cp, cl: (g, col)


def _bwd_map(n_chunks, col):
    return lambda g, cp, cl: (n_chunks - 1 - g, col)


def _walk_kernel(bodies, n_inputs, n_scratch, cpos_ref, clast_ref, *refs):
    refs = list(refs)
    ins = [[refs.pop(0) for _ in range(k)] for k in n_inputs]
    outs = [[refs.pop(0) for _ in range(2)] for _ in bodies]

    @pl.when(pl.program_id(0) == 0)
    def _():
        for r in refs:
            r[...] = jnp.zeros_like(r)

    for body, i, o, k in zip(bodies, ins, outs, n_scratch):
        body(cpos_ref, clast_ref, *i, *o, *[refs.pop(0) for _ in range(k)])


def _chunk_walk(parts, cpos, clast, t, name):
    n = t // CH
    blk = (CH, GROUP_W)
    bodies = [p[0] for p in parts]
    grid_spec = pltpu.PrefetchScalarGridSpec(
        num_scalar_prefetch=2, grid=(n,), in_specs=[s for p in parts for s in p[1]],
        out_specs=[pl.BlockSpec(blk, _fwd_map(0)), pl.BlockSpec(blk, _bwd_map(n, 0))] * len(parts),
        scratch_shapes=[s for p in parts for s in p[3]])
    outs = pl.pallas_call(
        functools.partial(_walk_kernel, bodies, [len(p[1]) for p in parts], [len(p[3]) for p in parts]),
        grid_spec=grid_spec, out_shape=[jax.ShapeDtypeStruct((t, GROUP_W), BF16)] * (2 * len(parts)),
        compiler_params=_params(1), name=name)(cpos, clast, *[a for p in parts for a in p[2]])
    return [outs[2 * i:2 * i + 2] for i in range(len(parts))]


def _row_valid(first):
    row = lax.broadcasted_iota(jnp.int32, (CH, 1), 0)
    return row >= jnp.where(first, PAD_ROWS, 0)


def _ret_log_gamma():
    return np.log1p(-np.exp2(-5.0 - np.arange(N_HEADS, dtype=np.float64)))


def _ret_consts():
    lg = _ret_log_gamma()
    pos = np.arange(CH, dtype=np.float64)
    rel = pos[:, None] - pos[None, :]
    dmat = np.zeros((2, N_HEADS, CH, CH), np.float64)
    rows = np.zeros((4, CH, GROUP_W), np.float64)
    for h in range(N_HEADS):
        dmat[0, h] = np.where(rel >= 0, np.exp(np.maximum(rel, 0.0) * lg[h]), 0.0)
        dmat[1, h] = np.where(rel < 0, np.exp(np.maximum(-rel, 0.0) * lg[h]), 0.0)
        sl = slice(h * HEAD_D, (h + 1) * HEAD_D)
        rows[0, :, sl] = np.exp((pos + 1.0) * lg[h])[:, None]
        rows[1, :, sl] = np.exp((CH - 1.0 - pos) * lg[h])[:, None]
        rows[2, :, sl] = np.exp((CH - pos) * lg[h])[:, None]
        rows[3, :, sl] = np.exp(pos * lg[h])[:, None]
    return jnp.asarray(dmat, F32), jnp.asarray(rows, F32)


def _ret_direction(q_ref, k_ref, v_ref, cos_ref, sin_ref, dmat_ref, rin_ref, rout_ref, s_ref, o_ref,
                   reset, first):
    valid = _row_valid(first)
    cos = cos_ref[...]
    sin = sin_ref[...]
    chunk_decay = np.exp(CH * _ret_log_gamma())
    for h in range(N_HEADS):
        sl = slice(h * HEAD_D, (h + 1) * HEAD_D)
        q = q_ref[:, sl]
        k = k_ref[:, sl]
        v = v_ref[:, sl]
        q = q * cos + pltpu.roll(q, HEAD_D // 2, 1) * sin
        k = (k * cos + pltpu.roll(k, HEAD_D // 2, 1) * sin) * (HEAD_D ** -0.5)
        k = jnp.where(valid, k, 0.0)
        scores = _dot_nt(q, k) * dmat_ref[h]
        state = jnp.where(reset, 0.0, s_ref[h])
        o_ref[:, sl] = _dot(jnp.concatenate([scores, q * rin_ref[:, sl]], axis=1),
                            jnp.concatenate([v, state], axis=0)).astype(o_ref.dtype)
        s_ref[h] = float(chunk_decay[h]) * state + _dot_tn(k * rout_ref[:, sl], v)


def _ret_kernel(cpos_ref, clast_ref, qf, kf, vf, cosf, sinf, qb, kb, vb, cosb, sinb, dmat_ref, rows_ref,
                of_ref, ob_ref, sf_ref, sb_ref):
    g = pl.program_id(0)
    gb = pl.num_programs(0) - 1 - g
    _ret_direction(qf, kf, vf, cosf, sinf, dmat_ref.at[0], rows_ref.at[0], rows_ref.at[1], sf_ref, of_ref,
                   cpos_ref[g] == 0, cpos_ref[g] == 0)
    _ret_direction(qb, kb, vb, cosb, sinb, dmat_ref.at[1], rows_ref.at[2], rows_ref.at[3], sb_ref, ob_ref,
                   clast_ref[gb] == 1, cpos_ref[gb] == 0)


def _rope_tables(n_rows):
    inv = ROPE_BASE ** (-jnp.arange(0, HEAD_D, 2, dtype=jnp.float32) / HEAD_D)
    pos = jnp.maximum(jnp.arange(n_rows, dtype=jnp.float32) - PAD_ROWS, 0.0)
    ang = pos[:, None] * inv[None, :]
    cos, sin = jnp.cos(ang), jnp.sin(ang)
    return jnp.concatenate([cos, cos], axis=1), jnp.concatenate([-sin, sin], axis=1)


def _retention(z, cpos, clast, max_rows):
    t = z.shape[0]
    n = t // CH
    cos2, sin2 = _rope_tables(max_rows)
    dmat, rows = _ret_consts()
    blk = (CH, GROUP_W)
    in_specs = []
    for mk, pm in ((_fwd_map, lambda g, cp, cl: (cp[g], 0)),
                   (functools.partial(_bwd_map, n), lambda g, cp, cl: (cp[n - 1 - g], 0))):
        in_specs += [pl.BlockSpec(blk, mk(0)), pl.BlockSpec(blk, mk(1)), pl.BlockSpec(blk, mk(2)),
                     pl.BlockSpec((CH, HEAD_D), pm), pl.BlockSpec((CH, HEAD_D), pm)]
    in_specs += [pl.BlockSpec(dmat.shape, lambda g, cp, cl: (0, 0, 0, 0)),
                 pl.BlockSpec(rows.shape, lambda g, cp, cl: (0, 0, 0))]
    return (_ret_kernel, in_specs, [z, z, z, cos2, sin2, z, z, z, cos2, sin2, dmat, rows],
            [pltpu.VMEM((N_HEADS, HEAD_D, HEAD_D), F32)] * 2)


HALO = SUBLANES


def _shift_rows(x, s, fill, reverse):
    row = lax.broadcasted_iota(jnp.int32, (CH, 1), 0)
    if reverse:
        return jnp.where(row < CH - s, pltpu.roll(x, CH - s, 0), fill)
    return jnp.where(row >= s, pltpu.roll(x, s, 0), fill)


def _lru_direction(x_ref, prev_ref, next_ref, convw_ref, convb_ref, wg_ref, bias_ref, lam_ref, ext_ref,
                   carry_ref, o_ref, reset, first, last, reverse):
    valid = _row_valid(first)
    ext_ref[HALO:HALO + CH, :] = jnp.where(valid, x_ref[...], 0.0)
    ext_ref[0:HALO, :] = jnp.where(first, 0.0, prev_ref[...])
    ext_ref[HALO + CH:, :] = jnp.where(last, 0.0, next_ref[...])
    xc = convb_ref[...] + ext_ref[HALO - 2:HALO - 2 + CH, :] * convw_ref[0:1, :]
    for tap in range(1, 4):
        xc = xc + ext_ref[HALO - 2 + tap:HALO - 2 + tap + CH, :] * convw_ref[tap:tap + 1, :]

    log_sig_lam = -_softplus(-lam_ref[...])
    parts_a, parts_u = [], []
    for grp in range(GROUP_W // LANES):
        sl = slice(grp * LANES, (grp + 1) * LANES)
        xg = xc[:, sl]
        pre = _dot(xg, wg_ref[grp])
        r = _sigmoid(pre[:, :LANES] + bias_ref[0:1, sl])
        i = _sigmoid(pre[:, LANES:] + bias_ref[1:2, sl])
        a = jnp.exp(LRU_C * log_sig_lam[:, sl] * r)
        u = jnp.sqrt(1.0 - a * a) * (i * xg)
        parts_a.append(a)
        parts_u.append(jnp.where(valid, u, 0.0))
    a = jnp.concatenate(parts_a, axis=1)
    u = jnp.concatenate(parts_u, axis=1)

    n_blk = CH // SUBLANES
    a = a.reshape(n_blk, SUBLANES, GROUP_W)
    u = u.reshape(n_blk, SUBLANES, GROUP_W)
    row_in_blk = lax.broadcasted_iota(jnp.int32, (1, SUBLANES, 1), 1)
    s = 1
    while s < SUBLANES:
        keep = (row_in_blk < SUBLANES - s) if reverse else (row_in_blk >= s)
        shift = SUBLANES - s if reverse else s
        u = u + a * jnp.where(keep, pltpu.roll(u, shift, 1), 0.0)
        a = a * jnp.where(keep, pltpu.roll(a, shift, 1), 1.0)
        s *= 2
    edge = 0 if reverse else SUBLANES - 1
    carry = jnp.where(reset, 0.0, carry_ref[...])
    blocks = [None] * n_blk
    for blk in (reversed(range(n_blk)) if reverse else range(n_blk)):
        blocks[blk] = u[blk] + a[blk] * carry
        carry = blocks[blk][edge:edge + 1, :]
    o_ref[...] = jnp.concatenate(blocks, axis=0).astype(o_ref.dtype)
    carry_ref[...] = carry


def _lru_kernel(cpos_ref, clast_ref, xf, pf, nf, xb, pb, nb, convw_ref, convb_ref, wg_ref, bias_ref, lam_ref,
                of_ref, ob_ref, extf_ref, extb_ref, cf_ref, cb_ref):
    g = pl.program_id(0)
    gb = pl.num_programs(0) - 1 - g
    _lru_direction(xf, pf, nf, convw_ref, convb_ref, wg_ref.at[0], bias_ref.at[0], lam_ref.at[0], extf_ref,
                   cf_ref, of_ref, cpos_ref[g] == 0, cpos_ref[g] == 0, clast_ref[g] == 1, False)
    _lru_direction(xb, pb, nb, convw_ref, convb_ref, wg_ref.at[1], bias_ref.at[1], lam_ref.at[1], extb_ref,
                   cb_ref, ob_ref, clast_ref[gb] == 1, cpos_ref[gb] == 0, clast_ref[gb] == 1, True)


def _lru_gate_weights(wa, wx):
    per = LANES // LRU_BW

    def block_diag(w):
        w = w.reshape(2, LRU_BLOCKS // per, per, LRU_BW, LRU_BW)
        eye = jnp.eye(per, dtype=w.dtype)
        return jnp.einsum("dgpij,pq->dgpiqj", w, eye).reshape(2, LRU_BLOCKS // per, LANES, LANES)

    return jnp.concatenate([block_diag(wa), block_diag(wx)], axis=-1).astype(BF16)


def _rglru(z, col, cpos, clast, conv_w, conv_b, wa, ba, wx, bx, lam):
    t = z.shape[0]
    n = t // CH
    per = CH // HALO
    n_halo = t // HALO
    blk = (CH, GROUP_W)
    hblk = (HALO, GROUP_W)
    wg = _lru_gate_weights(wa, wx)
    bias = jnp.stack([ba, bx], axis=1)
    lam = lam.reshape(2, 1, GROUP_W)

    def prev_f(g, cp, cl):
        return (jnp.maximum(g * per - 1, 0), col)

    def next_f(g, cp, cl):
        return (jnp.minimum((g + 1) * per, n_halo - 1), col)

    def prev_b(g, cp, cl):
        return (jnp.maximum((n - 1 - g) * per - 1, 0), col)

    def next_b(g, cp, cl):
        return (jnp.minimum((n - g) * per, n_halo - 1), col)

    full = lambda a: pl.BlockSpec(a.shape, lambda g, cp, cl: (0,) * a.ndim)
    conv_b2 = conv_b.reshape(1, GROUP_W)
    in_specs = [pl.BlockSpec(blk, _fwd_map(col)), pl.BlockSpec(hblk, prev_f), pl.BlockSpec(hblk, next_f),
                pl.BlockSpec(blk, _bwd_map(n, col)), pl.BlockSpec(hblk, prev_b), pl.BlockSpec(hblk, next_b),
                full(conv_w), full(conv_b2), full(wg), full(bias), full(lam)]
    return (_lru_kernel, in_specs, [z, z, z, z, z, z, conv_w, conv_b2, wg, bias, lam],
            [pltpu.VMEM((CH + 2 * HALO, GROUP_W), F32)] * 2 + [pltpu.VMEM((1, GROUP_W), F32)] * 2)


def _tri_consts():
    pos = np.arange(CH)
    lower = (pos[:, None] >= pos[None, :]).astype(np.float32)
    ops = np.stack([lower, lower.T])
    return jnp.asarray(np.tile(ops, (1, 1, PREFIX_TERMS)), BF16)


def _hg_masks():
    pos = np.arange(CH)
    same_blk = pos[:, None] // SUB == pos[None, :] // SUB
    same_grp = pos[:, None] // GRP == pos[None, :] // GRP
    return jnp.asarray(np.stack([same_blk, same_grp]).astype(np.float32))


def _hg_direction(q_ref, f_ref, v_ref, lb, cum_ref, mask_ref, st_ref, o_ref, reset, first, reverse):
    valid = _row_valid(first)
    sub_i =lax.broadcasted_iota(jnp.int32, (SUB, HEAD_D), 0)
    lane_j = lax.broadcasted_iota(jnp.int32, (SUB, HEAD_D), 1) & (SUB - 1)
    causal = (sub_i <= lane_j) if reverse else (sub_i >= lane_j)
    pick = [causal & (lane_j == j) for j in range(SUB)]
    cum_op = cum_ref[...]
    same_blk = mask_ref[0]
    same_grp = mask_ref[1]
    zero = jnp.zeros((SUB, HEAD_D), F32)
    n_blk = CH // SUB
    per = GRP // SUB
    n_grp = CH // GRP

    def edge(unit, idx):
        return unit * idx if reverse else unit * (idx + 1) - 1

    def split_product(b, qs, ks, bs, pieces):
        q_slabs, k_slabs = [], []
        for ref, q_active, k_active in pieces:
            qp, kp = [], []
            for i in range(n_blk):
                r = ref(i)
                qp.append(qs[i] * jnp.exp2(bs[i] - b[r:r + 1, :]) if q_active(i) else zero)
                kp.append(ks[i] * jnp.exp2(b[r:r + 1, :] - bs[i]) if k_active(i) else zero)
            q_slabs.append(jnp.concatenate(qp, axis=0))
            k_slabs.append(jnp.concatenate(kp, axis=0))
        return _dot_nt(jnp.concatenate(q_slabs, axis=1), jnp.concatenate(k_slabs, axis=1))

    level1 = []
    for c in (range(1, per) if reverse else range(per - 1)):
        level1.append((lambda i, c=c: edge(SUB, (i // per) * per + c),
                       (lambda i, c=c: i % per < c) if reverse else (lambda i, c=c: i % per > c),
                       lambda i, c=c: i % per == c))
    level2 = []
    for gc in (range(1, n_grp) if reverse else range(n_grp - 1)):
        level2.append((lambda i, gc=gc: edge(GRP, gc),
                       (lambda i, gc=gc: i // per < gc) if reverse else (lambda i, gc=gc: i // per > gc),
                       lambda i, gc=gc: i // per == gc))

    for h in range(N_HEADS):
        sl = slice(h * HEAD_D, (h + 1) * HEAD_D)
        lbh = lb[:, sl]
        q = _silu(q_ref[:, sl])
        f = lbh + (1.0 - lbh) * _sigmoid(f_ref[:, sl])
        k = jnp.where(valid, 1.0 - f, 0.0)
        v = v_ref[:, sl]
        b = _prefix_dot(cum_op, jnp.log(f)) * LOG2_E
        qs = [q[i * SUB:(i + 1) * SUB] for i in range(n_blk)]
        ks = [k[i * SUB:(i + 1) * SUB] for i in range(n_blk)]
        bs = [b[i * SUB:(i + 1) * SUB] for i in range(n_blk)]

        tiles = [qs[blk] * jnp.exp2(jnp.minimum(bs[blk] - bs[blk][j:j + 1, :], 0.0))
                 for blk in range(n_blk) for j in range(SUB)]
        pair = _dot_nt(jnp.concatenate(tiles, axis=0), k)
        rows = []
        for blk in range(n_blk):
            base = blk * SUB * SUB
            a_blk = jnp.where(pick[0], pair[base:base + SUB], 0.0)
            for j in range(1, SUB):
                a_blk = a_blk + jnp.where(pick[j], pair[base + j * SUB:base + (j + 1) * SUB], 0.0)
            rows.append(a_blk)
        scores = jnp.concatenate(rows, axis=0) * same_blk
        scores = scores + split_product(b, qs, ks, bs, level1) * same_grp
        scores = scores + split_product(b, qs, ks, bs, level2)

        state = jnp.where(reset, 0.0, st_ref[h])
        o_ref[:, sl] = (_dot(scores, v) + _dot_nt(q * jnp.exp2(b), state)).astype(o_ref.dtype)
        b_tot = b[0:1, :] if reverse else b[CH - 1:CH, :]
        st_ref[h] = state * jnp.exp2(b_tot) + _dot_tn(v, k * jnp.exp2(b_tot - b))


def _hg_lower_bound(lower_ref, layer):
    low = lower_ref[...]
    e = jnp.exp(low - jnp.max(low, axis=0, keepdims=True))
    soft = e / jnp.sum(e, axis=0, keepdims=True)
    lb = jnp.zeros((1, GROUP_W), F32)
    for l in range(1, layer + 1):
        lb = lb + soft[l:l + 1, :]
    return lb


def _hg_kernel(layer, cpos_ref, clast_ref, qf, ff, vf, qb, fb, vb, lower_ref, cum_ref, mask_ref,
               of_ref, ob_ref, sf_ref, sb_ref):
    g = pl.program_id(0)
    gb = pl.num_programs(0) - 1 - g
    lb = _hg_lower_bound(lower_ref, layer)
    _hg_direction(qf, ff, vf, lb, cum_ref.at[0], mask_ref, sf_ref, of_ref,
                  cpos_ref[g] == 0, cpos_ref[g] == 0, False)
    _hg_direction(qb, fb, vb, lb, cum_ref.at[1], mask_ref, sb_ref, ob_ref,
                  clast_ref[gb] == 1, cpos_ref[gb] == 0, True)


def _hgrn2(z, cpos, clast, lower, layer):
    t = z.shape[0]
    n = t // CH
    blk = (CH, GROUP_W)
    cum = _tri_consts()
    masks = _hg_masks()
    full = lambda a: pl.BlockSpec(a.shape, lambda g, cp, cl: (0,) * a.ndim)
    in_specs = [pl.BlockSpec(blk, _fwd_map(0)), pl.BlockSpec(blk, _fwd_map(1)), pl.BlockSpec(blk, _fwd_map(3)),
                pl.BlockSpec(blk, _bwd_map(n, 0)), pl.BlockSpec(blk, _bwd_map(n, 2)),
                pl.BlockSpec(blk, _bwd_map(n, 3)), full(lower), full(cum), full(masks)]
    return (functools.partial(_hg_kernel, layer), in_specs, [z, z, z, z, z, z, lower, cum, masks],
            [pltpu.VMEM((N_HEADS, HEAD_D, HEAD_D), F32)] * 2)


def _log_sigmoid(x):
    return jnp.minimum(x, 0.0) - jnp.log(1.0 + jnp.exp(-jnp.abs(x)))


def _ml_direction(q_ref, k_ref, v_ref, gi_ref, gf_ref, bias_ref, cum_op, spread_ref, s_ref, m_ref, o_ref,
                  reset, first, d, reverse):
    valid = _row_valid(first)
    cum = _prefix_dot(cum_op, _log_sigmoid(gf_ref[...] + bias_ref[1]))
    a = jnp.where(valid, gi_ref[...] + bias_ref[0] - cum, NEG_BIG)
    run = a
    s = 1
    while s < CH:
        run = jnp.maximum(run, _shift_rows(run, s, NEG_BIG, reverse))
        s *= 2
    m_st = jnp.where(reset, NEG_BIG, m_ref[...])
    mx = jnp.maximum(m_st, run)
    edge = 0 if reverse else CH - 1
    mx_last = mx[edge:edge + 1, :]
    m_ref[...] = cum[edge:edge + 1, :] + mx_last
    decay = jnp.exp(m_st - mx_last)

    stacked = jnp.concatenate([mx, m_st - mx, -(cum + mx), a - mx_last], axis=0) * LOG2_E
    spread_b = _spread_dot(stacked, spread_ref[...])
    mx_b, inter_b, floor_b, end_b = (spread_b[i * CH:(i + 1) * CH] for i in range(4))
    a_t = (a * LOG2_E).T

    row = lax.broadcasted_iota(jnp.int32, (CH, CH), 0)
    col = lax.broadcasted_iota(jnp.int32, (CH, CH), 1)
    causal = (col >= row) if reverse else (col <= row)
    ones = jnp.ones((CH, HEAD_D), F32)
    for h in range(N_HEADS):
        sl = slice(h * HEAD_D, (h + 1) * HEAD_D)
        x = d * N_HEADS + h
        q = q_ref[:, sl]
        k = jnp.where(valid, k_ref[:, sl] * (HEAD_D ** -0.5), 0.0)
        v_ext = jnp.concatenate([v_ref[:, sl], ones], axis=1)
        w = jnp.where(causal, jnp.exp2(jnp.minimum(a_t[x:x + 1, :] - mx_b[:, sl], 0.0)), 0.0)
        qk = _dot_nt(q, k) * w
        s_inter = jnp.exp2(jnp.minimum(inter_b[:, sl], 0.0))
        state = jnp.where(reset, 0.0, s_ref[h])
        ext = _dot(jnp.concatenate([qk, q * s_inter], axis=1), jnp.concatenate([v_ext, state], axis=0))
        o_ref[:, sl] = (ext[:, :HEAD_D] / jnp.maximum(jnp.abs(ext[:, HEAD_D:]), jnp.exp2(floor_b[:, sl]))
                        ).astype(o_ref.dtype)
        kw = k * jnp.exp2(jnp.minimum(end_b[:, sl], 0.0))
        s_ref[h] = decay[:, x:x + 1] * state + _dot_tn(kw, v_ext)


def _ml_kernel(cpos_ref, clast_ref, qf, kf, vf, gif, gff, qb, kb, vb, gib, gfb, bias_ref, cum_ref, spread_ref,
               of_ref, ob_ref, sfs, mfs, sbs, mbs):
    g = pl.program_id(0)
    gb = pl.num_programs(0) - 1 - g
    _ml_direction(qf, kf, vf, gif, gff, bias_ref, cum_ref[0], spread_ref.at[0], sfs, mfs, of_ref,
                  cpos_ref[g] == 0, cpos_ref[g] == 0, 0, False)
    _ml_direction(qb, kb, vb, gib, gfb, bias_ref, cum_ref[1], spread_ref.at[1], sbs, mbs, ob_ref,
                  clast_ref[gb] == 1, cpos_ref[gb] == 0, 1, True)


def _ml_spread():
    spread = np.zeros((2, LANES, GROUP_W), np.float32)
    for d in range(2):
        for h in range(N_HEADS):
            spread[d, d * N_HEADS + h, h * HEAD_D:(h + 1) * HEAD_D] = 1.0
    return jnp.asarray(np.tile(spread, (1, SPREAD_TERMS, 1)), BF16)


def _mlstm(z, gates, cpos, clast, ml_bi, ml_bf):
    t = z.shape[0]
    n = t // CH
    blk = (CH, GROUP_W)
    gblk = (CH, LANES)
    cum = _tri_consts()
    spread = _ml_spread()
    pad = jnp.zeros((LANES - 2 * N_HEADS,), F32)
    bias = jnp.stack([jnp.concatenate([ml_bi.reshape(-1), pad]),
                      jnp.concatenate([ml_bf.reshape(-1), pad])]).reshape(2, 1, LANES)
    full = lambda a: pl.BlockSpec(a.shape, lambda g, cp, cl: (0,) * a.ndim)
    in_specs = [pl.BlockSpec(blk, _fwd_map(5)), pl.BlockSpec(blk, _fwd_map(6)), pl.BlockSpec(blk, _fwd_map(7)),
                pl.BlockSpec(gblk, _fwd_map(0)), pl.BlockSpec(gblk, _fwd_map(1)),
                pl.BlockSpec(blk, _bwd_map(n, 5)), pl.BlockSpec(blk, _bwd_map(n, 6)),
                pl.BlockSpec(blk, _bwd_map(n, 7)),
                pl.BlockSpec(gblk, _bwd_map(n, 0)), pl.BlockSpec(gblk, _bwd_map(n, 1)),
                full(bias), full(cum), full(spread)]
    state = [pltpu.VMEM((N_HEADS, HEAD_D, 2 * HEAD_D), F32), pltpu.VMEM((1, LANES), F32)]
    return (_ml_kernel, in_specs, [z, z, z, gates, gates, z, z, z, gates, gates, bias, cum, spread], state * 2)


def _head_norm(x, center):
    outs = []
    for h in range(N_HEADS):
        xh = x[:, h * HEAD_D:(h + 1) * HEAD_D]
        if center:
            xh = xh - jnp.mean(xh, axis=1, keepdims=True)
        outs.append(xh * lax.rsqrt(jnp.mean(xh * xh, axis=1, keepdims=True) + EPS))
    return jnp.concatenate(outs, axis=1)


def _layer_norm(x, g, b):
    xc = x - jnp.mean(x, axis=1, keepdims=True)
    return xc * lax.rsqrt(jnp.mean(xc * xc, axis=1, keepdims=True) + EPS) * g + b


def _mix_rows(alpha, even, af, ab, ag, bf, bb, bg, h_ref, w_ref, lng_ref, lnb_ref):
    a = af[...].astype(F32) + ab[...].astype(F32)
    b = bf[...].astype(F32) + bb[...].astype(F32)
    if even:
        a = _head_norm(a, False) * _silu(ag[...])
        b = b * _gelu_tanh(bg[...])
    else:
        a = _head_norm(a, False) * _silu(ag[...])
        b = _head_norm(b, True) * _sigmoid(bg[...])
    mix = _dot(a, w_ref[0:GROUP_W, :]) + _dot(b, w_ref[GROUP_W:, :])
    return _layer_norm(alpha * h_ref[...] + mix, lng_ref[...], lnb_ref[...])


N_MIX_OPERANDS = 10


def _mix_specs(tm, col_a, col_b, index):
    half = lambda c: pl.BlockSpec((tm, GROUP_W), lambda *g: (index(*g), c))
    rows = pl.BlockSpec((tm, D_MODEL), lambda *g: (index(*g), 0))
    vec = pl.BlockSpec((1, D_MODEL), lambda *g: (0, 0))
    return [half(0), half(0), half(col_a), half(0), half(0), half(col_b), rows,
            pl.BlockSpec((D_MODEL, D_MODEL), lambda *g: (0, 0)), vec, vec]


def _mix_ffn_kernel(alpha, *refs):
    mix_refs = refs[:N_MIX_OPERANDS]
    wg_ref, wu_ref, wo_ref, lng_ref, lnb_ref, o_ref, h1_ref, acc_ref = refs[N_MIX_OPERANDS:]
    f = pl.program_id(1)

    @pl.when(f == 0)
    def _():
        h1_ref[...] = _mix_rows(alpha, True, *mix_refs)
        acc_ref[...] = jnp.zeros_like(acc_ref)

    x = h1_ref[...].astype(BF16)
    act = _silu(_dot(x, wg_ref[...])) * _dot(x, wu_ref[...])
    acc_ref[...] += _dot(act, wo_ref[...])

    @pl.when(f == pl.num_programs(1) - 1)
    def _():
        o_ref[...] = _layer_norm(alpha * h1_ref[...] + acc_ref[...], lng_ref[...], lnb_ref[...])


def _mix_ffn(alpha, af, ab, bf, bb, z, col_a, col_b, h, w_out, ln1_g, ln1_b, wi, wo, ln2_g, ln2_b):
    t = h.shape[0]
    d_ff = wo.shape[0]
    tm = _pick_tile(t, 640)
    tf = _pick_tile(d_ff, 1536)
    nf = d_ff // tf
    rows = pl.BlockSpec((tm, D_MODEL), lambda i, f: (i, 0))
    vec = pl.BlockSpec((1, D_MODEL), lambda i, f: (0, 0))
    return pl.pallas_call(
        functools.partial(_mix_ffn_kernel, alpha), grid=(t // tm, nf),
        in_specs=_mix_specs(tm, col_a, col_b, lambda i, f: i) + [
            pl.BlockSpec((D_MODEL, tf), lambda i, f: (0, f)),
            pl.BlockSpec((D_MODEL, tf), lambda i, f: (0, f + nf)),
            pl.BlockSpec((tf, D_MODEL), lambda i, f: (f, 0)), vec, vec],
        out_specs=rows, out_shape=jax.ShapeDtypeStruct((t, D_MODEL), F32),
        scratch_shapes=[pltpu.VMEM((tm, D_MODEL), F32), pltpu.VMEM((tm, D_MODEL), F32)],
        compiler_params=_params(2), name="mix_ffn")(
            af, ab, z, bf, bb, z, h, w_out, ln1_g.reshape(1, -1), ln1_b.reshape(1, -1),
            wi, wi, wo, ln2_g.reshape(1, -1), ln2_b.reshape(1, -1))


EXPERT_TILE = 1024
DMA_UNROLL = 8
DMA_PRIORITIES = 2


def _mix_route_kernel(alpha, pad_starts, *refs):
    mix_refs = refs[:N_MIX_OPERANDS]
    router_ref, tri_ref, o_ref, meta_ref, gate_ref, cnt_ref, carry_ref = refs[N_MIX_OPERANDS:]

    @pl.when(pl.program_id(0) == 0)
    def _():
        carry_ref[...] = jnp.zeros_like(carry_ref)

    h1 = _mix_rows(alpha, False, *mix_refs)
    o_ref[...] = h1
    tm = h1.shape[0]
    row = pl.program_id(0) * tm + lax.broadcasted_iota(jnp.int32, (tm, 1), 0)
    is_pad = row < 0
    for s in pad_starts:
        is_pad = is_pad | ((row >= s) & (row < s + PAD_ROWS))
    is_token = jnp.logical_not(is_pad)

    logits = _dot_16bit(h1, router_ref[...])
    lane = lax.broadcasted_iota(jnp.int32, logits.shape, 1)
    logits = jnp.where(lane < N_EXP, logits, -jnp.inf)
    top1 = jnp.max(logits, axis=1, keepdims=True)
    idx1 = jnp.min(jnp.where(logits == top1, lane, LANES), axis=1, keepdims=True)
    rest = jnp.where(lane == idx1, -jnp.inf, logits)
    top2 = jnp.max(rest, axis=1, keepdims=True)
    idx2 = jnp.min(jnp.where(rest == top2, lane, LANES), axis=1, keepdims=True)
    g2 = jnp.exp(top2 - top1)
    denom = 1.0 + g2
    hit1 = lane == idx1
    hit2 = lane == idx2
    both = jnp.where(is_token, jnp.where(hit1, 1.0, 0.0) + jnp.where(hit2, 1.0, 0.0), 0.0)
    prefix = _dot(tri_ref[...], both) + carry_ref[...]
    rank1 = jnp.sum(jnp.where(hit1, prefix, 0.0), axis=1, keepdims=True).astype(jnp.int32)
    rank2 = jnp.sum(jnp.where(hit2, prefix, 0.0), axis=1, keepdims=True).astype(jnp.int32)
    carry_ref[...] += jnp.sum(both, axis=0, keepdims=True)
    cnt_ref[...] = carry_ref[...]
    meta_ref[...] = jnp.where(lane == 0, idx1, jnp.where(lane == 1, idx2, jnp.where(
        lane == 2, rank1, jnp.where(lane == 3, rank2, jnp.where(is_token & (lane == 4), 1, 0)))))
    gate_ref[...] = jnp.where(lane == 0, 1.0 / denom, jnp.where(lane == 1, g2 / denom, 0.0))


def _mix_route(alpha, af, ab, bf, bb, z, col_a, col_b, h, w_out, ln_g, ln_b, router, pad_starts):
    t = h.shape[0]
    tm = _pick_tile(t, 512)
    router_p = _stack_weight_terms(jnp.pad(router, ((0, 0), (0, LANES - router.shape[1]))))
    pos = np.arange(tm)
    tri = jnp.asarray(pos[:, None] > pos[None, :], BF16)
    rows = lambda w: pl.BlockSpec((tm, w), lambda i: (i, 0))
    const = lambda a: pl.BlockSpec(a.shape, lambda i: (0, 0))
    return pl.pallas_call(
        functools.partial(_mix_route_kernel, alpha, pad_starts), grid=(t // tm,),
        in_specs=_mix_specs(tm, col_a, col_b, lambda i: i) + [const(router_p), const(tri)],
        out_specs=[rows(D_MODEL), rows(LANES), rows(LANES), pl.BlockSpec((1, LANES), lambda i: (0, 0))],
        out_shape=[jax.ShapeDtypeStruct((t, D_MODEL), F32), jax.ShapeDtypeStruct((t, LANES), jnp.int32),
                   jax.ShapeDtypeStruct((t, LANES), F32), jax.ShapeDtypeStruct((1, LANES), F32)],
        scratch_shapes=[pltpu.VMEM((1, LANES), F32)],
        compiler_params=_params(1), name="mix_route")(
            af, ab, z, bf, bb, z, h, w_out, ln_g.reshape(1, -1), ln_b.reshape(1, -1), router_p, tri)


def _dispatch_plan(meta, counts, t):
    cnt = counts[0, :N_EXP].astype(jnp.int32)
    padded = ((cnt + EXPERT_TILE - 1) // EXPERT_TILE) * EXPERT_TILE
    ends = jnp.cumsum(padded)
    off = ends - padded

    def base(e):
        return sum(jnp.where(e == i, off[i], 0) for i in range(N_EXP))

    is_token = meta[:, 4] > 0
    n_tiles = -(-2 * t // EXPERT_TILE) + N_EXP
    spare = n_tiles * EXPERT_TILE + 2 * (jnp.cumsum(jnp.logical_not(is_token).astype(jnp.int32)) - 1)
    pos1 = (base(meta[:, 0]) + meta[:, 2]).astype(jnp.int32)
    pos2 = (base(meta[:, 1]) + meta[:, 3]).astype(jnp.int32)
    scatter = (jnp.where(is_token, pos1, spare), jnp.where(is_token, pos2, spare + 1))
    gather = (jnp.where(is_token, pos1, 0), jnp.where(is_token, pos2, 0))
    starts = jnp.arange(n_tiles, dtype=jnp.int32) * EXPERT_TILE
    tile_expert = jnp.minimum(jnp.sum(starts[:, None] >= ends[None, :], axis=1), N_EXP - 1).astype(jnp.int32)
    n_active = (ends[-1] // EXPERT_TILE).astype(jnp.int32).reshape(1)
    return ends.astype(jnp.int32), scatter, gather, tile_expert, n_active, n_tiles


assert D_MODEL == SUBLANES * LANES


def _store_token_tiles(ref, x):
    n = x.shape[0]
    for s in range(SUBLANES):
        ref[pl.ds(s, n, stride=SUBLANES), :] = x[:, s * LANES:(s + 1) * LANES]


def _load_token_tiles(ref, n):
    return jnp.concatenate([ref[pl.ds(s, n, stride=SUBLANES), :] for s in range(SUBLANES)], axis=1)


def _tile_rows(i):
    if isinstance(i, int):
        return pl.ds(i * SUBLANES, SUBLANES)
    return pl.ds(pl.multiple_of(i * SUBLANES, SUBLANES), SUBLANES)


def _dispatch_kernel(n_tiles, ends_ref, pos1_ref, pos2_ref, h_ref, xs_ref, tok_ref, zero_ref, sem, zero_sem):
    n = h_ref.shape[0]

    @pl.when(pl.program_id(0) == 0)
    def _():
        zero_ref[...] = jnp.zeros_like(zero_ref)

        def clear_last_tile(e):
            first_row = pl.multiple_of((ends_ref[e] - EXPERT_TILE) * SUBLANES, SUBLANES)
            return pltpu.make_async_copy(zero_ref, xs_ref.at[pl.ds(first_row, EXPERT_TILE * SUBLANES)], zero_sem)

        def has_tiles(e):
            return ends_ref[e] > (ends_ref[e - 1] if e else 0)

        for e in range(N_EXP):
            pl.when(has_tiles(e))(lambda e=e: clear_last_tile(e).start())
        for e in range(N_EXP):
            pl.when(has_tiles(e))(lambda e=e: clear_last_tile(e).wait())

        def clear_tile(i):
            first_row = pl.multiple_of(i * (EXPERT_TILE * SUBLANES), SUBLANES)
            return pltpu.make_async_copy(zero_ref, xs_ref.at[pl.ds(first_row, EXPERT_TILE * SUBLANES)], zero_sem)

        used = ends_ref[N_EXP - 1] // EXPERT_TILE
        lax.fori_loop(used, n_tiles, lambda i, c: (clear_tile(i).start(), c)[1], 0)
        lax.fori_loop(used, n_tiles, lambda i, c: (clear_tile(i).wait(), c)[1], 0)

    g = pl.program_id(0)
    slot = g % 2
    _store_token_tiles(tok_ref.at[slot], h_ref[...])

    def tile_copy(r, p):
        return pltpu.make_async_copy(tok_ref.at[slot].at[_tile_rows(r)], xs_ref.at[_tile_rows(p)], sem.at[slot])

    def start(pair, c):
        for prio in range(DMA_PRIORITIES):
            r = pair * DMA_PRIORITIES + prio
            tile_copy(r, pos1_ref[0, 0, r]).start(priority=prio)
            tile_copy(r, pos2_ref[0, 0, r]).start(priority=prio)
        return c

    lax.fori_loop(0, n // DMA_PRIORITIES, start, 0, unroll=DMA_UNROLL // DMA_PRIORITIES)

    def wait_slot(s):
        all_rows = pltpu.make_async_copy(tok_ref.at[s], xs_ref.at[pl.ds(0, n * SUBLANES)], sem.at[s])
        all_rows.wait()
        all_rows.wait()

    pl.when(g > 0)(lambda: wait_slot(1 - slot))
    pl.when(g == pl.num_programs(0) - 1)(lambda: wait_slot(slot))


def _dispatch(h, ends, pos1, pos2, n_tiles, n_spare):
    t = h.shape[0]
    n_rows = n_tiles * EXPERT_TILE + n_spare
    tm = _pick_tile(t, 512)
    idx = lambda: pl.BlockSpec((1, 1, tm), lambda i, ends: (i, 0, 0), memory_space=pltpu.SMEM)
    grid_spec = pltpu.PrefetchScalarGridSpec(
        num_scalar_prefetch=1, grid=(t // tm,),
        in_specs=[idx(), idx(), pl.BlockSpec((tm, D_MODEL), lambda i, ends: (i, 0))],
        out_specs=pl.BlockSpec(memory_space=pl.ANY),
        scratch_shapes=[pltpu.VMEM((2, tm * SUBLANES, LANES), F32),
                        pltpu.VMEM((EXPERT_TILE * SUBLANES, LANES), F32),
                        pltpu.SemaphoreType.DMA((2,)), pltpu.SemaphoreType.DMA(())])
    return pl.pallas_call(
        functools.partial(_dispatch_kernel, n_tiles), grid_spec=grid_spec,
        out_shape=jax.ShapeDtypeStruct((n_rows * SUBLANES, LANES), F32),
        compiler_params=_params(1), name="dispatch")(
            ends, pos1.reshape(t // tm, 1, tm), pos2.reshape(t // tm, 1, tm), h)


def _experts_kernel(te_ref, na_ref, x_ref, wg_ref, wu_ref, wo_ref, y_ref, xb_ref, acc_ref):
    i = pl.program_id(0)
    f = pl.program_id(1)
    last = f == pl.num_programs(1) - 1
    active = i < na_ref[0]

    @pl.when(active)
    def _():
        @pl.when(f == 0)
        def _():
            acc_ref[...] = jnp.zeros_like(acc_ref)
            xb_ref[...] = _load_token_tiles(x_ref, EXPERT_TILE).astype(BF16)

        x = xb_ref[...]
        act = _silu(_dot(x, wg_ref[0])) * _dot(x, wu_ref[0])
        acc_ref[...] += _dot(act, wo_ref[0])

        @pl.when(last)
        def _():
            _store_token_tiles(y_ref, acc_ref[...])

    @pl.when(jnp.logical_not(active) & last)
    def _():
        y_ref[...] = jnp.zeros_like(y_ref)


def _experts(xs, tile_expert, n_active, wi, wo, n_tiles):
    n_rows = n_tiles * EXPERT_TILE
    e_ff = wo.shape[1]
    tf = _pick_tile(e_ff, 512)
    nf = e_ff // tf
    rows = pl.BlockSpec((EXPERT_TILE * SUBLANES, LANES), lambda i, f, te, na: (i, 0))
    rows_in = pl.BlockSpec((EXPERT_TILE * SUBLANES, LANES),
                           lambda i, f, te, na: (jnp.minimum(i, jnp.maximum(na[0] - 1, 0)), 0))
    grid_spec = pltpu.PrefetchScalarGridSpec(
        num_scalar_prefetch=2, grid=(n_rows // EXPERT_TILE, nf),
        in_specs=[rows_in, pl.BlockSpec((1, D_MODEL, tf), lambda i, f, te, na: (te[i], 0, f)),
                  pl.BlockSpec((1, D_MODEL, tf), lambda i, f, te, na: (te[i], 0, f + nf)),
                  pl.BlockSpec((1, tf, D_MODEL), lambda i, f, te, na: (te[i], f, 0))],
        out_specs=rows,
        scratch_shapes=[pltpu.VMEM((EXPERT_TILE, D_MODEL), BF16), pltpu.VMEM((EXPERT_TILE, D_MODEL), F32)])
    return pl.pallas_call(
        _experts_kernel, grid_spec=grid_spec, out_shape=jax.ShapeDtypeStruct((n_rows * SUBLANES, LANES), F32),
        compiler_params=_params(2), name="experts")(tile_expert, n_active, xs, wi, wi, wo)


def _gather_start(n, pos1_ref, pos2_ref, ys_ref, a_ref, b_ref, sems, inline=False):
    def tile_copy(p, buf, r, which):
        return pltpu.make_async_copy(ys_ref.at[_tile_rows(p)], buf.at[_tile_rows(r)], sems.at[which])

    def start(pair, c):
        for prio in range(DMA_PRIORITIES):
            r = pair * DMA_PRIORITIES + prio
            tile_copy(pos1_ref[0, 0, r], a_ref, r, 0).start(priority=prio)
            tile_copy(pos2_ref[0, 0, r], b_ref, r, 1).start(priority=prio)
        return c

    if inline:
        for pair in range(n // DMA_PRIORITIES):
            start(pair, 0)
    else:
        lax.fori_loop(0, n // DMA_PRIORITIES, start, 0, unroll=DMA_UNROLL // DMA_PRIORITIES)


def _gather_wait(n, ys_ref, a_ref, b_ref, sems):
    for which, buf in enumerate((a_ref, b_ref)):
        pltpu.make_async_copy(ys_ref.at[pl.ds(0, n * SUBLANES)], buf, sems.at[which]).wait()


def _gather_finish(alpha, h_ref, gate_ref, ys_ref, lng_ref, lnb_ref, a_ref, b_ref, sems):
    n = h_ref.shape[0]
    _gather_wait(n, ys_ref, a_ref, b_ref, sems)
    gate = gate_ref[...]
    y = gate[:, 0:1] * _load_token_tiles(a_ref, n) + gate[:, 1:2] * _load_token_tiles(b_ref, n)
    return _layer_norm(alpha * h_ref[...] + y, lng_ref[...], lnb_ref[...])


def _combine_kernel(alpha, pos1_ref, pos2_ref, h_ref, gate_ref, ys_ref, lng_ref, lnb_ref, o_ref,
                    a_ref, b_ref, sems):
    _gather_start(h_ref.shape[0], pos1_ref, pos2_ref, ys_ref, a_ref, b_ref, sems)
    o_ref[...] = _gather_finish(alpha, h_ref, gate_ref, ys_ref, lng_ref, lnb_ref, a_ref, b_ref, sems)


def _combine_scratch(tm):
    return [pltpu.VMEM((tm * SUBLANES, LANES), F32), pltpu.VMEM((tm * SUBLANES, LANES), F32),
            pltpu.SemaphoreType.DMA((2,))]


def _combine(alpha, h, gates, ys, pos1, pos2, ln_g, ln_b):
    t = h.shape[0]
    tm = _pick_tile(t, 512)
    idx = lambda: pl.BlockSpec((1, 1, tm), lambda i: (i, 0, 0), memory_space=pltpu.SMEM)
    rows = pl.BlockSpec((tm, D_MODEL), lambda i: (i, 0))
    vec = pl.BlockSpec((1, D_MODEL), lambda i: (0, 0))
    return pl.pallas_call(
        functools.partial(_combine_kernel, alpha), grid=(t // tm,),
        in_specs=[idx(), idx(), rows, pl.BlockSpec((tm, LANES), lambda i: (i, 0)),
                  pl.BlockSpec(memory_space=pl.ANY), vec, vec],
        out_specs=rows, out_shape=jax.ShapeDtypeStruct((t, D_MODEL), F32),
        scratch_shapes=_combine_scratch(tm),
        compiler_params=_params(1), name="combine")(
            pos1.reshape(t // tm, 1, tm), pos2.reshape(t // tm, 1, tm), h, gates, ys,
            ln_g.reshape(1, -1), ln_b.reshape(1, -1))


def _output_tables(seq_shapes):
    kind, live = [], []
    window = [[], []]
    recent = [(0, 0), (0, 0)]
    for grp, (b, l) in enumerate(seq_shapes):
        for i in range(b):
            for c in range(l // CH + 1):
                kind.append(grp)
                live.append(int(c > 0))
                recent[grp] = (i, max(c - 1, 0))
                for g2 in range(2):
                    window[g2].append(recent[g2])
    tables = [kind, live] + [[w[j] for w in window[g2]] for g2 in range(2) for j in range(2)]
    return [jnp.asarray(np.asarray(x, np.int32)) for x in tables]


def _combine_out_kernel(alpha, kind_ref, live_ref, ab_ref, ac_ref, bb_ref, bc_ref,
                        pos1_ref, pos2_ref, nxt1_ref, nxt2_ref, h_ref, gate_ref, ys_ref, lng_ref, lnb_ref,
                        outa_ref, outb_ref, a_ref, b_ref, sems):
    g = pl.program_id(0)
    last = pl.num_programs(0) - 1
    slot = g % 2
    other = (a_ref.at[1 - slot], b_ref.at[1 - slot], sems.at[1 - slot])

    @pl.when(g == 0)
    def _():
        _gather_start(CH, pos1_ref, pos2_ref, ys_ref, a_ref.at[0], b_ref.at[0], sems.at[0])

    res = _gather_finish(alpha, h_ref, gate_ref, ys_ref, lng_ref, lnb_ref,
                         a_ref.at[slot], b_ref.at[slot], sems.at[slot])
    _gather_start(CH, nxt1_ref, nxt2_ref, ys_ref, *other, inline=True)
    for grp, out_ref in enumerate((outa_ref, outb_ref)):
        @pl.when((kind_ref[g] == grp) & (live_ref[g] == 1))
        def _(out_ref=out_ref):
            out_ref[0] = res

    @pl.when(g == last)
    def _():
        _gather_wait(CH, ys_ref, *other)


def _combine_out(alpha, h, gates, ys, pos1, pos2, ln_g, ln_b, seq_shapes):
    t = h.shape[0]
    n = t // CH
    tables = _output_tables(seq_shapes)
    idx = lambda: pl.BlockSpec((1, 1, CH), lambda g, *tb: (g, 0, 0), memory_space=pltpu.SMEM)
    idx_next = lambda: pl.BlockSpec((1, 1, CH), lambda g, *tb: (jnp.minimum(g + 1, n - 1), 0, 0),
                                    memory_space=pltpu.SMEM)
    rows = lambda w: pl.BlockSpec((CH, w), lambda g, *tb: (g, 0))
    vec = pl.BlockSpec((1, D_MODEL), lambda g, *tb: (0, 0))
    out = lambda grp: pl.BlockSpec((1, CH, D_MODEL),
                                   lambda g, *tb: (tb[2 + 2 * grp][g], tb[3 + 2 * grp][g], 0))
    fetched = pltpu.VMEM((2, CH * SUBLANES, LANES), F32)
    grid_spec = pltpu.PrefetchScalarGridSpec(
        num_scalar_prefetch=len(tables), grid=(n,),
        in_specs=[idx(), idx(), idx_next(), idx_next(), rows(D_MODEL), rows(LANES),
                  pl.BlockSpec(memory_space=pl.ANY), vec, vec],
        out_specs=[out(0), out(1)], scratch_shapes=[fetched, fetched, pltpu.SemaphoreType.DMA((2, 2))])
    pos1, pos2 = pos1.reshape(n, 1, CH), pos2.reshape(n, 1, CH)
    return pl.pallas_call(
        functools.partial(_combine_out_kernel, alpha), grid_spec=grid_spec,
        out_shape=[jax.ShapeDtypeStruct((b, l, D_MODEL), F32) for b, l in seq_shapes],
        compiler_params=_params(1), name="combine_out")(
            *tables, pos1, pos2, pos1, pos2, h, gates, ys, ln_g.reshape(1, -1), ln_b.reshape(1, -1))


def _moe(alpha, h, meta, gates, counts, wi, wo, ln_g, ln_b, n_pad_rows, out_shapes=None):
    t = h.shape[0]
    ends, scatter, gather, tile_expert, n_active, n_tiles = _dispatch_plan(meta, counts, t)
    n_spare = 2 * n_pad_rows
    xs = _dispatch(h, ends, scatter[0], scatter[1], n_tiles, n_spare)
    ys = _experts(xs, tile_expert, n_active, wi, wo, n_tiles)
    if out_shapes is None:
        return _combine(alpha, h, gates, ys, gather[0], gather[1], ln_g, ln_b)
    return _combine_out(alpha, h, gates, ys, gather[0], gather[1], ln_g, ln_b, out_shapes)


def kernel(x_prompt, x_sample, meta, e_w_in, e_conv_w, e_conv_b, e_lru_wa, e_lru_ba, e_lru_wx, e_lru_bx,
           e_lru_lambda, e_w_out, e_ffn_wi, e_ffn_wo, o_w_in, o_hg_lower, o_ml_bi, o_ml_bf, o_w_out, o_router,
           o_exp_wi, o_exp_wo, ln_g, ln_b):
    groups = (x_prompt, x_sample)
    depth = ln_g.shape[0]
    alpha = (2.0 * depth) ** 0.25
    seq_shapes = [(x.shape[0], x.shape[1]) for x in groups]
    for _, l in seq_shapes:
        assert l % CH == 0
    cpos_np, clast_np = _chunk_tables(seq_shapes)
    cpos, clast = jnp.asarray(cpos_np), jnp.asarray(clast_np)
    max_rows = max(l for _, l in seq_shapes) + CH
    pad_starts = tuple(int(i) * CH for i in np.flatnonzero(cpos_np == 0))

    head = jnp.concatenate([jnp.zeros((PAD_ROWS, D_MODEL), F32), meta.astype(F32)], axis=0)
    parts = []
    for x in groups:
        full = jnp.concatenate([jnp.broadcast_to(head[None], (x.shape[0], CH, D_MODEL)), x], axis=1)
        parts.append(full.reshape(-1, D_MODEL))
    h = jnp.concatenate(parts, axis=0)
    t = h.shape[0]

    n_odd_cols = 9 * GROUP_W
    for layer in range(depth):
        p = layer // 2
        if layer % 2 == 0:
            z = _project(h, e_w_in[p].astype(BF16))
            (ret_f, ret_b), (lru_f, lru_b) = _chunk_walk(
                [_retention(z, cpos, clast, max_rows),
                 _rglru(z, 4, cpos, clast, e_conv_w[p], e_conv_b[p], e_lru_wa[p], e_lru_ba[p],
                        e_lru_wx[p], e_lru_bx[p], e_lru_lambda[p])], cpos, clast, t, "even_mixers")
            h = _mix_ffn(alpha, ret_f, ret_b, lru_f, lru_b, z, 3, 5, h, e_w_out[p].astype(BF16),
                         ln_g[layer, 0], ln_b[layer, 0], e_ffn_wi[p].astype(BF16), e_ffn_wo[p].astype(BF16),
                         ln_g[layer, 1], ln_b[layer, 1])
        else:
            w_in = o_w_in[p]
            n_gate = 2 * N_HEADS
            gate_pad = ((0, 0), (0, LANES - n_gate))
            w_gates = jnp.concatenate([jnp.pad(w_in[:, n_odd_cols:n_odd_cols + n_gate], gate_pad),
                                       jnp.pad(w_in[:, n_odd_cols + n_gate:], gate_pad)], axis=1)
            z, gates = _project(h, w_in[:, :n_odd_cols].astype(BF16), w_gates)
            (hg_f, hg_b), (ml_f, ml_b) = _chunk_walk(
                [_hgrn2(z, cpos, clast, o_hg_lower, layer),
                 _mlstm(z, gates, cpos, clast, o_ml_bi[p], o_ml_bf[p])], cpos, clast, t, "odd_mixers")
            h, meta_r, gates_r, counts = _mix_route(
                alpha, hg_f, hg_b, ml_f, ml_b, z, 4, 8, h, o_w_out[p].astype(BF16),
                ln_g[layer, 0], ln_b[layer, 0], o_router[p], pad_starts)
            h = _moe(alpha, h, meta_r, gates_r, counts, o_exp_wi[p].astype(BF16), o_exp_wo[p].astype(BF16),
                     ln_g[layer, 1], ln_b[layer, 1], len(pad_starts) * PAD_ROWS,
                     seq_shapes if layer == depth - 1 else None)
            if layer == depth - 1:
                return tuple(h)

    outs = []
    row = 0
    for b, l in seq_shapes:
        n = b * (l + CH)
        outs.append(h[row:row + n].reshape(b, l + CH, D_MODEL)[:, CH:])
        row += n
    return tuple(outs)
```

```python
import functools
import math

import numpy as np
import jax
import jax.numpy as jnp
from jax import lax
from jax.experimental import pallas as pl
from jax.experimental.pallas import tpu as pltpu

F32 = jnp.float32
BF16 = jnp.bfloat16

D_MODEL = 1024
GROUP_W = D_MODEL // 2
N_HEADS = 4
HEAD_D = GROUP_W // N_HEADS
N_META = 16
ROPE_BASE = 10000.0
LRU_BLOCKS = 8
LRU_BW = GROUP_W // LRU_BLOCKS
LRU_C = 8.0
N_EXP = 8
EPS = 1e-5

LANES = 128
SUBLANES = 8
CH = 128
PAD_ROWS = CH - N_META
SUB = SUBLANES
GRP = 4 * SUB
ODD_CHUNKS_PER_STEP = 2
NEG_BIG = -1e30
LOG2_E = math.log2(math.e)
VMEM_LIMIT = 56 * 1024 * 1024


def _dot(a, b):
    return jnp.dot(a.astype(BF16), b.astype(BF16), preferred_element_type=F32)


def _dot_nt(a, b):
    return lax.dot_general(a.astype(BF16), b.astype(BF16), (((1,), (1,)), ((), ())),
                           preferred_element_type=F32)


def _dot_tn(a, b):
    return _dot(a.T, b)


def _bf16_terms(x, n_terms):
    terms = []
    for _ in range(n_terms):
        t = x.astype(BF16)
        terms.append(t)
        x = x - t.astype(F32)
    return terms


def _stack_weight_terms(w):
    hi, lo = _bf16_terms(w, 2)
    return jnp.concatenate([hi, lo, hi], axis=0)


def _dot_16bit(x, w_terms):
    hi, lo = _bf16_terms(x, 2)
    return jnp.dot(jnp.concatenate([hi, hi, lo], axis=1), w_terms, preferred_element_type=F32)


PREFIX_TERMS = 3
SPREAD_TERMS = 2


def _prefix_dot(op_tiled, x):
    return jnp.dot(op_tiled, jnp.concatenate(_bf16_terms(x, PREFIX_TERMS), axis=0),
                   preferred_element_type=F32)


def _spread_dot(x, op_tiled):
    return jnp.dot(jnp.concatenate(_bf16_terms(x, SPREAD_TERMS), axis=1), op_tiled,
                   preferred_element_type=F32)


def _sigmoid(x):
    return 1.0 / (1.0 + jnp.exp(-x))


def _silu(x):
    return x * _sigmoid(x)


def _gelu_tanh(x):
    return 0.5 * x * (1.0 + jnp.tanh(math.sqrt(2.0 / math.pi) * (x + 0.044715 * (x * x * x))))


def _softplus(x):
    return jnp.maximum(x, 0.0) + jnp.log(1.0 + jnp.exp(-jnp.abs(x)))


def _pick_tile(total, target):
    best = LANES
    for t in range(LANES, min(total, target) + 1, LANES):
        if total % t == 0:
            best = t
    return best


def _params(n_axes, sem="arbitrary"):
    return pltpu.CompilerParams(dimension_semantics=(sem,) * n_axes, vmem_limit_bytes=VMEM_LIMIT)


def _proj_kernel(x_ref, w_ref, o_ref):
    o_ref[...] = _dot(x_ref[...], w_ref[...])


def _proj_gates_kernel(x_ref, w_ref, wg_ref, o_ref, g_ref):
    g_ref[...] = _dot_16bit(x_ref[...], wg_ref[...])
    o_ref[...] = _dot(x_ref[...], w_ref[...])


PROJ_OUT_BLOCK_BYTES = 8 * 1024 * 1024


def _project(x, w, w_gates=None):
    t, k = x.shape
    n = w.shape[1]
    tm = _pick_tile(t, PROJ_OUT_BLOCK_BYTES // (4 * n))
    x_spec = pl.BlockSpec((tm, k), lambda i: (i, 0))
    w_spec = pl.BlockSpec((k, n), lambda i: (0, 0))
    o_spec = pl.BlockSpec((tm, n), lambda i: (i, 0))
    if w_gates is None:
        return pl.pallas_call(
            _proj_kernel, grid=(t // tm,), in_specs=[x_spec, w_spec], out_specs=o_spec,
            out_shape=jax.ShapeDtypeStruct((t, n), F32), compiler_params=_params(1),
            name="proj")(x, w)
    ng = w_gates.shape[1]
    w_gates = _stack_weight_terms(w_gates)
    return pl.pallas_call(
        _proj_gates_kernel, grid=(t // tm,),
        in_specs=[x_spec, w_spec, pl.BlockSpec(w_gates.shape, lambda i: (0, 0))],
        out_specs=[o_spec, pl.BlockSpec((tm, ng), lambda i: (i, 0))],
        out_shape=[jax.ShapeDtypeStruct((t, n), F32), jax.ShapeDtypeStruct((t, ng), F32)],
        compiler_params=_params(1), name="proj_gates")(x, w, w_gates)


def _chunk_tables(seq_shapes):
    cpos, clast = [], []
    for b, l in seq_shapes:
        n = l // CH + 1
        for _ in range(b):
            cpos += list(range(n))
            clast += [0] * (n - 1) + [1]
    return np.asarray(cpos, np.int32), np.asarray(clast, np.int32)


def _fwd_map(col):
    return lambda g, cp, cl: (g, col)


def _bwd_map(n_chunks, col):
    return lambda g, cp, cl: (n_chunks - 1 - g, col)


def _walk_kernel(bodies, tags, n_scratch, cps, cpos_ref, clast_ref, *refs):
    refs = list(refs)
    ins = [[refs.pop(0) for _ in t] for t in tags]
    outs = [[refs.pop(0) for _ in range(2)] for _ in bodies]
    scratch = [[refs.pop(0) for _ in range(k)] for k in n_scratch]

    @pl.when(pl.program_id(0) == 0)
    def _():
        for part in scratch:
            for r in part:
                r[...] = jnp.zeros_like(r)

    n_chunks = pl.num_programs(0) * cps
    for j in range(cps):
        g = pl.program_id(0) * cps + j
        gb = n_chunks - 1 - g
        rows = {"f": pl.ds(j * CH, CH), "b": pl.ds((cps - 1 - j) * CH, CH)}

        def view(ref, tag):
            return ref if tag == "c" or cps == 1 else ref.at[rows[tag]]

        for body, i, tg, o, s in zip(bodies, ins, tags, outs, scratch):
            body(cpos_ref, clast_ref, g, gb, *[view(r, t) for r, t in zip(i, tg)],
                 view(o[0], "f"), view(o[1], "b"), *s)


def _chunk_walk(parts, cpos, clast, t, name, cps=1):
    n = t // (CH * cps)
    blk = (CH * cps, GROUP_W)
    bodies = [p[0] for p in parts]
    grid_spec = pltpu.PrefetchScalarGridSpec(
        num_scalar_prefetch=2, grid=(n,), in_specs=[s for p in parts for s in p[1]],
        out_specs=[pl.BlockSpec(blk, _fwd_map(0)), pl.BlockSpec(blk, _bwd_map(n, 0))] * len(parts),
        scratch_shapes=[s for p in parts for s in p[3]])
    outs = pl.pallas_call(
        functools.partial(_walk_kernel, bodies, [p[4] for p in parts], [len(p[3]) for p in parts], cps),
        grid_spec=grid_spec, out_shape=[jax.ShapeDtypeStruct((t, GROUP_W), BF16)] * (2 * len(parts)),
        compiler_params=_params(1), name=name)(cpos, clast, *[a for p in parts for a in p[2]])
    return [outs[2 * i:2 * i + 2] for i in range(len(parts))]


def _row_valid(first):
    row = lax.broadcasted_iota(jnp.int32, (CH, 1), 0)
    return row >= jnp.where(first, PAD_ROWS, 0)


def _ret_log_gamma():
    return np.log1p(-np.exp2(-5.0 - np.arange(N_HEADS, dtype=np.float64)))


def _ret_consts():
    lg = _ret_log_gamma()
    pos = np.arange(CH, dtype=np.float64)
    rel = pos[:, None] - pos[None, :]
    dmat = np.zeros((2, N_HEADS, CH, CH), np.float64)
    rows = np.zeros((4, CH, GROUP_W), np.float64)
    for h in range(N_HEADS):
        dmat[0, h] = np.where(rel >= 0, np.exp(np.maximum(rel, 0.0) * lg[h]), 0.0)
        dmat[1, h] = np.where(rel < 0, np.exp(np.maximum(-rel, 0.0) * lg[h]), 0.0)
        sl = slice(h * HEAD_D, (h + 1) * HEAD_D)
        rows[0, :, sl] = np.exp((pos + 1.0) * lg[h])[:, None]
        rows[1, :, sl] = np.exp((CH - 1.0 - pos) * lg[h])[:, None]
        rows[2, :, sl] = np.exp((CH - pos) * lg[h])[:, None]
        rows[3, :, sl] = np.exp(pos * lg[h])[:, None]
    return jnp.asarray(dmat, F32), jnp.asarray(rows, F32)


def _ret_direction(q_ref, k_ref, v_ref, cos_ref, sin_ref, dmat_ref, rin_ref, rout_ref, s_ref, o_ref,
                   reset, first):
    valid = _row_valid(first)
    cos = cos_ref[...]
    sin = sin_ref[...]
    chunk_decay = np.exp(CH * _ret_log_gamma())
    for h in range(N_HEADS):
        sl = slice(h * HEAD_D, (h + 1) * HEAD_D)
        q = q_ref[:, sl]
        k = k_ref[:, sl]
        v = v_ref[:, sl]
        q = q * cos + pltpu.roll(q, HEAD_D // 2, 1) * sin
        k = (k * cos + pltpu.roll(k, HEAD_D // 2, 1) * sin) * (HEAD_D ** -0.5)
        k = jnp.where(valid, k, 0.0)
        scores = _dot_nt(q, k) * dmat_ref[h]
        state = jnp.where(reset, 0.0, s_ref[h])
        o_ref[:, sl] = _dot(jnp.concatenate([scores, q * rin_ref[:, sl]], axis=1),
                            jnp.concatenate([v, state], axis=0)).astype(o_ref.dtype)
        s_ref[h] = float(chunk_decay[h]) * state + _dot_tn(k * rout_ref[:, sl], v)


def _ret_kernel(cpos_ref, clast_ref, g, gb, qf, kf, vf, cosf, sinf, qb, kb, vb, cosb, sinb, dmat_ref, rows_ref,
                of_ref, ob_ref, sf_ref, sb_ref):
    _ret_direction(qf, kf, vf, cosf, sinf, dmat_ref.at[0], rows_ref.at[0], rows_ref.at[1], sf_ref, of_ref,
                   cpos_ref[g] == 0, cpos_ref[g] == 0)
    _ret_direction(qb, kb, vb, cosb, sinb, dmat_ref.at[1], rows_ref.at[2], rows_ref.at[3], sb_ref, ob_ref,
                   clast_ref[gb] == 1, cpos_ref[gb] == 0)


def _rope_tables(n_rows):
    inv = ROPE_BASE ** (-jnp.arange(0, HEAD_D, 2, dtype=jnp.float32) / HEAD_D)
    pos = jnp.maximum(jnp.arange(n_rows, dtype=jnp.float32) - PAD_ROWS, 0.0)
    ang = pos[:, None] * inv[None, :]
    cos, sin = jnp.cos(ang), jnp.sin(ang)
    return jnp.concatenate([cos, cos], axis=1), jnp.concatenate([-sin, sin], axis=1)


def _retention(z, cpos, clast, max_rows):
    t = z.shape[0]
    n = t // CH
    cos2, sin2 = _rope_tables(max_rows)
    dmat, rows = _ret_consts()
    blk = (CH, GROUP_W)
    in_specs = []
    for mk, pm in ((_fwd_map, lambda g, cp, cl: (cp[g], 0)),
                   (functools.partial(_bwd_map, n), lambda g, cp, cl: (cp[n - 1 - g], 0))):
        in_specs += [pl.BlockSpec(blk, mk(0)), pl.BlockSpec(blk, mk(1)), pl.BlockSpec(blk, mk(2)),
                     pl.BlockSpec((CH, HEAD_D), pm), pl.BlockSpec((CH, HEAD_D), pm)]
    in_specs += [pl.BlockSpec(dmat.shape, lambda g, cp, cl: (0, 0, 0, 0)),
                 pl.BlockSpec(rows.shape, lambda g, cp, cl: (0, 0, 0))]
    return (_ret_kernel, in_specs, [z, z, z, cos2, sin2, z, z, z, cos2, sin2, dmat, rows],
            [pltpu.VMEM((N_HEADS, HEAD_D, HEAD_D), F32)] * 2, "fffffbbbbbcc")


HALO = SUBLANES


def _shift_rows(x, s, fill, reverse):
    row = lax.broadcasted_iota(jnp.int32, (CH, 1), 0)
    if reverse:
        return jnp.where(row < CH - s, pltpu.roll(x, CH - s, 0), fill)
    return jnp.where(row >= s, pltpu.roll(x, s, 0), fill)


def _lru_direction(x_ref, prev_ref, next_ref, convw_ref, convb_ref, wg_ref, bias_ref, lam_ref, ext_ref,
                   carry_ref, o_ref, reset, first, last, reverse):
    valid = _row_valid(first)
    ext_ref[HALO:HALO + CH, :] = jnp.where(valid, x_ref[...], 0.0)
    ext_ref[0:HALO, :] = jnp.where(first, 0.0, prev_ref[...])
    ext_ref[HALO + CH:, :] = jnp.where(last, 0.0, next_ref[...])
    xc = convb_ref[...] + ext_ref[HALO - 2:HALO - 2 + CH, :] * convw_ref[0:1, :]
    for tap in range(1, 4):
        xc = xc + ext_ref[HALO - 2 + tap:HALO - 2 + tap + CH, :] * convw_ref[tap:tap + 1, :]

    log_sig_lam = -_softplus(-lam_ref[...])
    parts_a, parts_u = [], []
    for grp in range(GROUP_W // LANES):
        sl = slice(grp * LANES, (grp + 1) * LANES)
        xg = xc[:, sl]
        pre = _dot(xg, wg_ref[grp])
        r = _sigmoid(pre[:, :LANES] + bias_ref[0:1, sl])
        i = _sigmoid(pre[:, LANES:] + bias_ref[1:2, sl])
        a = jnp.exp(LRU_C * log_sig_lam[:, sl] * r)
        u = jnp.sqrt(1.0 - a * a) * (i * xg)
        parts_a.append(a)
        parts_u.append(jnp.where(valid, u, 0.0))
    a = jnp.concatenate(parts_a, axis=1)
    u = jnp.concatenate(parts_u, axis=1)

    n_blk = CH // SUBLANES
    a = a.reshape(n_blk, SUBLANES, GROUP_W)
    u = u.reshape(n_blk, SUBLANES, GROUP_W)
    row_in_blk = lax.broadcasted_iota(jnp.int32, (1, SUBLANES, 1), 1)
    s = 1
    while s < SUBLANES:
        keep = (row_in_blk < SUBLANES - s) if reverse else (row_in_blk >= s)
        shift = SUBLANES - s if reverse else s
        u = u + a * jnp.where(keep, pltpu.roll(u, shift, 1), 0.0)
        a = a * jnp.where(keep, pltpu.roll(a, shift, 1), 1.0)
        s *= 2
    edge = 0 if reverse else SUBLANES - 1
    carry = jnp.where(reset, 0.0, carry_ref[...])
    blocks = [None] * n_blk
    for blk in (reversed(range(n_blk)) if reverse else range(n_blk)):
        blocks[blk] = u[blk] + a[blk] * carry
        carry = blocks[blk][edge:edge + 1, :]
    o_ref[...] = jnp.concatenate(blocks, axis=0).astype(o_ref.dtype)
    carry_ref[...] = carry


def _lru_kernel(cpos_ref, clast_ref, g, gb, xf, pf, nf, xb, pb, nb, convw_ref, convb_ref, wg_ref, bias_ref,
                lam_ref, of_ref, ob_ref, extf_ref, extb_ref, cf_ref, cb_ref):
    _lru_direction(xf, pf, nf, convw_ref, convb_ref, wg_ref.at[0], bias_ref.at[0], lam_ref.at[0], extf_ref,
                   cf_ref, of_ref, cpos_ref[g] == 0, cpos_ref[g] == 0, clast_ref[g] == 1, False)
    _lru_direction(xb, pb, nb, convw_ref, convb_ref, wg_ref.at[1], bias_ref.at[1], lam_ref.at[1], extb_ref,
                   cb_ref, ob_ref, clast_ref[gb] == 1, cpos_ref[gb] == 0, clast_ref[gb] == 1, True)


def _lru_gate_weights(wa, wx):
    per = LANES // LRU_BW

    def block_diag(w):
        w = w.reshape(2, LRU_BLOCKS // per, per, LRU_BW, LRU_BW)
        eye = jnp.eye(per, dtype=w.dtype)
        return jnp.einsum("dgpij,pq->dgpiqj", w, eye).reshape(2, LRU_BLOCKS // per, LANES, LANES)

    return jnp.concatenate([block_diag(wa), block_diag(wx)], axis=-1).astype(BF16)


def _rglru(z, col, cpos, clast, conv_w, conv_b, wa, ba, wx, bx, lam):
    t = z.shape[0]
    n = t // CH
    per = CH // HALO
    n_halo = t // HALO
    blk = (CH, GROUP_W)
    hblk = (HALO, GROUP_W)
    wg = _lru_gate_weights(wa, wx)
    bias = jnp.stack([ba, bx], axis=1)
    lam = lam.reshape(2, 1, GROUP_W)

    def prev_f(g, cp, cl):
        return (jnp.maximum(g * per - 1, 0), col)

    def next_f(g, cp, cl):
        return (jnp.minimum((g + 1) * per, n_halo - 1), col)

    def prev_b(g, cp, cl):
        return (jnp.maximum((n - 1 - g) * per - 1, 0), col)

    def next_b(g, cp, cl):
        return (jnp.minimum((n - g) * per, n_halo - 1), col)

    full = lambda a: pl.BlockSpec(a.shape, lambda g, cp, cl: (0,) * a.ndim)
    conv_b2 = conv_b.reshape(1, GROUP_W)
    in_specs = [pl.BlockSpec(blk, _fwd_map(col)), pl.BlockSpec(hblk, prev_f), pl.BlockSpec(hblk, next_f),
                pl.BlockSpec(blk, _bwd_map(n, col)), pl.BlockSpec(hblk, prev_b), pl.BlockSpec(hblk, next_b),
                full(conv_w), full(conv_b2), full(wg), full(bias), full(lam)]
    return (_lru_kernel, in_specs, [z, z, z, z, z, z, conv_w, conv_b2, wg, bias, lam],
            [pltpu.VMEM((CH + 2 * HALO, GROUP_W), F32)] * 2 + [pltpu.VMEM((1, GROUP_W), F32)] * 2,
            "fffbbbccccc")


def _tri_consts():
    pos = np.arange(CH)
    lower = (pos[:, None] >= pos[None, :]).astype(np.float32)
    ops = np.stack([lower, lower.T])
    return jnp.asarray(np.tile(ops, (1, 1, PREFIX_TERMS)), BF16)


def _hg_masks():
    pos = np.arange(CH)
    same_blk = pos[:, None] // SUB == pos[None, :] // SUB
    same_grp = pos[:, None] // GRP == pos[None, :] // GRP
    return jnp.asarray(np.stack([same_blk, same_grp]).astype(np.float32))


def _hg_direction(q_ref, f_ref, v_ref, lb, cum_ref, mask_ref, st_ref, o_ref, reset, first, reverse):
    valid = _row_valid(first)
    sub_i =lax.broadcasted_iota(jnp.int32, (SUB, HEAD_D), 0)
    lane_j = lax.broadcasted_iota(jnp.int32, (SUB, HEAD_D), 1) & (SUB - 1)
    causal = (sub_i <= lane_j) if reverse else (sub_i >= lane_j)
    pick = [causal & (lane_j == j) for j in range(SUB)]
    cum_op = cum_ref[...]
    same_blk = mask_ref[0]
    same_grp = mask_ref[1]
    zero = jnp.zeros((SUB, HEAD_D), F32)
    n_blk = CH // SUB
    per = GRP // SUB
    n_grp = CH // GRP

    def edge(unit, idx):
        return unit * idx if reverse else unit * (idx + 1) - 1

    def split_product(b, qs, ks, bs, pieces):
        q_slabs, k_slabs = [], []
        for ref, q_active, k_active in pieces:
            qp, kp = [], []
            for i in range(n_blk):
                r = ref(i)
                qp.append(qs[i] * jnp.exp2(bs[i] - b[r:r + 1, :]) if q_active(i) else zero)
                kp.append(ks[i] * jnp.exp2(b[r:r + 1, :] - bs[i]) if k_active(i) else zero)
            q_slabs.append(jnp.concatenate(qp, axis=0))
            k_slabs.append(jnp.concatenate(kp, axis=0))
        return _dot_nt(jnp.concatenate(q_slabs, axis=1), jnp.concatenate(k_slabs, axis=1))

    level1 = []
    for c in (range(1, per) if reverse else range(per - 1)):
        level1.append((lambda i, c=c: edge(SUB, (i // per) * per + c),
                       (lambda i, c=c: i % per < c) if reverse else (lambda i, c=c: i % per > c),
                       lambda i, c=c: i % per == c))
    level2 = []
    for gc in (range(1, n_grp) if reverse else range(n_grp - 1)):
        level2.append((lambda i, gc=gc: edge(GRP, gc),
                       (lambda i, gc=gc: i // per < gc) if reverse else (lambda i, gc=gc: i // per > gc),
                       lambda i, gc=gc: i // per == gc))

    for h in range(N_HEADS):
        sl = slice(h * HEAD_D, (h + 1) * HEAD_D)
        lbh = lb[:, sl]
        q = _silu(q_ref[:, sl])
        f = lbh + (1.0 - lbh) * _sigmoid(f_ref[:, sl])
        k = jnp.where(valid, 1.0 - f, 0.0)
        v = v_ref[:, sl]
        b = _prefix_dot(cum_op, jnp.log(f)) * LOG2_E
        qs = [q[i * SUB:(i + 1) * SUB] for i in range(n_blk)]
        ks = [k[i * SUB:(i + 1) * SUB] for i in range(n_blk)]
        bs = [b[i * SUB:(i + 1) * SUB] for i in range(n_blk)]

        tiles = [qs[blk] * jnp.exp2(jnp.minimum(bs[blk] - bs[blk][j:j + 1, :], 0.0))
                 for blk in range(n_blk) for j in range(SUB)]
        pair = _dot_nt(jnp.concatenate(tiles, axis=0), k)
        rows = []
        for blk in range(n_blk):
            base = blk * SUB * SUB
            a_blk = jnp.where(pick[0], pair[base:base + SUB], 0.0)
            for j in range(1, SUB):
                a_blk = a_blk + jnp.where(pick[j], pair[base + j * SUB:base + (j + 1) * SUB], 0.0)
            rows.append(a_blk)
        scores = jnp.concatenate(rows, axis=0) * same_blk
        scores = scores + split_product(b, qs, ks, bs, level1) * same_grp
        scores = scores + split_product(b, qs, ks, bs, level2)

        state = jnp.where(reset, 0.0, st_ref[h])
        o_ref[:, sl] = (_dot(scores, v) + _dot_nt(q * jnp.exp2(b), state)).astype(o_ref.dtype)
        b_tot = b[0:1, :] if reverse else b[CH - 1:CH, :]
        st_ref[h] = state * jnp.exp2(b_tot) + _dot_tn(v, k * jnp.exp2(b_tot - b))


def _hg_lower_bound(lower_ref, layer):
    low = lower_ref[...]
    e = jnp.exp(low - jnp.max(low, axis=0, keepdims=True))
    soft = e / jnp.sum(e, axis=0, keepdims=True)
    lb = jnp.zeros((1, GROUP_W), F32)
    for l in range(1, layer + 1):
        lb = lb + soft[l:l + 1, :]
    return lb


def _hg_kernel(layer, cpos_ref, clast_ref, g, gb, qf, ff, vf, qb, fb, vb, lower_ref, cum_ref, mask_ref,
               of_ref, ob_ref, sf_ref, sb_ref):
    lb =_hg_lower_bound(lower_ref, layer)
    _hg_direction(qf, ff, vf, lb, cum_ref.at[0], mask_ref, sf_ref, of_ref,
                  cpos_ref[g] == 0, cpos_ref[g] == 0, False)
    _hg_direction(qb, fb, vb, lb, cum_ref.at[1], mask_ref, sb_ref, ob_ref,
                  clast_ref[gb] == 1, cpos_ref[gb] == 0, True)


def _hgrn2(z, lower, layer, cps):
    t = z.shape[0]
    n = t // (CH * cps)
    blk = (CH * cps, GROUP_W)
    cum = _tri_consts()
    masks = _hg_masks()
    full = lambda a: pl.BlockSpec(a.shape, lambda g, cp, cl: (0,) * a.ndim)
    in_specs = [pl.BlockSpec(blk, _fwd_map(0)), pl.BlockSpec(blk, _fwd_map(1)), pl.BlockSpec(blk, _fwd_map(3)),
                pl.BlockSpec(blk, _bwd_map(n, 0)), pl.BlockSpec(blk, _bwd_map(n, 2)),
                pl.BlockSpec(blk, _bwd_map(n, 3)), full(lower), full(cum), full(masks)]
    return (functools.partial(_hg_kernel, layer), in_specs, [z, z, z, z, z, z, lower, cum, masks],
            [pltpu.VMEM((N_HEADS, HEAD_D, HEAD_D), F32)] * 2, "fffbbbccc")


def _log_sigmoid(x):
    return jnp.minimum(x, 0.0) - jnp.log(1.0 + jnp.exp(-jnp.abs(x)))


def _ml_direction(q_ref, k_ref, v_ref, gi_ref, gf_ref, bias_ref, cum_op, spread_ref, s_ref, m_ref, o_ref,
                  reset, first, d, reverse):
    valid = _row_valid(first)
    cum = _prefix_dot(cum_op, _log_sigmoid(gf_ref[...] + bias_ref[1]))
    a = jnp.where(valid, gi_ref[...] + bias_ref[0] - cum, NEG_BIG)
    run = a
    s = 1
    while s < CH:
        run = jnp.maximum(run, _shift_rows(run, s, NEG_BIG, reverse))
        s *= 2
    m_st = jnp.where(reset, NEG_BIG, m_ref[...])
    mx = jnp.maximum(m_st, run)
    edge = 0 if reverse else CH - 1
    mx_last = mx[edge:edge + 1, :]
    m_ref[...] = cum[edge:edge + 1, :] + mx_last
    decay = jnp.exp(m_st - mx_last)

    stacked = jnp.concatenate([mx, m_st - mx, -(cum + mx), a - mx_last], axis=0) * LOG2_E
    spread_b = _spread_dot(stacked, spread_ref[...])
    mx_b, inter_b, floor_b, end_b = (spread_b[i * CH:(i + 1) * CH] for i in range(4))
    a_t = (a * LOG2_E).T

    row = lax.broadcasted_iota(jnp.int32, (CH, CH), 0)
    col = lax.broadcasted_iota(jnp.int32, (CH, CH), 1)
    causal = (col >= row) if reverse else (col <= row)
    ones = jnp.ones((CH, HEAD_D), F32)
    for h in range(N_HEADS):
        sl = slice(h * HEAD_D, (h + 1) * HEAD_D)
        x = d * N_HEADS + h
        q = q_ref[:, sl]
        k = jnp.where(valid, k_ref[:, sl] * (HEAD_D ** -0.5), 0.0)
        v_ext = jnp.concatenate([v_ref[:, sl], ones], axis=1)
        w = jnp.where(causal, jnp.exp2(jnp.minimum(a_t[x:x + 1, :] - mx_b[:, sl], 0.0)), 0.0)
        qk = _dot_nt(q, k) * w
        s_inter = jnp.exp2(jnp.minimum(inter_b[:, sl], 0.0))
        state = jnp.where(reset, 0.0, s_ref[h])
        ext = _dot(jnp.concatenate([qk, q * s_inter], axis=1), jnp.concatenate([v_ext, state], axis=0))
        o_ref[:, sl] = (ext[:, :HEAD_D] / jnp.maximum(jnp.abs(ext[:, HEAD_D:]), jnp.exp2(floor_b[:, sl]))
                        ).astype(o_ref.dtype)
        kw = k * jnp.exp2(jnp.minimum(end_b[:, sl], 0.0))
        s_ref[h] = decay[:, x:x + 1] * state + _dot_tn(kw, v_ext)


def _ml_kernel(cpos_ref, clast_ref, g, gb, qf, kf, vf, gif, gff, qb, kb, vb, gib, gfb, bias_ref, cum_ref,
               spread_ref, of_ref, ob_ref, sfs, mfs, sbs, mbs):
    _ml_direction(qf, kf, vf, gif, gff, bias_ref, cum_ref[0], spread_ref.at[0], sfs, mfs, of_ref,
                  cpos_ref[g] == 0, cpos_ref[g] == 0, 0, False)
    _ml_direction(qb, kb, vb, gib, gfb, bias_ref, cum_ref[1], spread_ref.at[1], sbs, mbs, ob_ref,
                  clast_ref[gb] == 1, cpos_ref[gb] == 0, 1, True)


def _ml_spread():
    spread = np.zeros((2, LANES, GROUP_W), np.float32)
    for d in range(2):
        for h in range(N_HEADS):
            spread[d, d * N_HEADS + h, h * HEAD_D:(h + 1) * HEAD_D] = 1.0
    return jnp.asarray(np.tile(spread, (1, SPREAD_TERMS, 1)), BF16)


def _mlstm(z, gates, ml_bi, ml_bf, cps):
    t = z.shape[0]
    n = t // (CH * cps)
    blk = (CH * cps, GROUP_W)
    gblk = (CH * cps, LANES)
    cum = _tri_consts()
    spread = _ml_spread()
    pad = jnp.zeros((LANES - 2 * N_HEADS,), F32)
    bias = jnp.stack([jnp.concatenate([ml_bi.reshape(-1), pad]),
                      jnp.concatenate([ml_bf.reshape(-1), pad])]).reshape(2, 1, LANES)
    full = lambda a: pl.BlockSpec(a.shape, lambda g, cp, cl: (0,) * a.ndim)
    in_specs = [pl.BlockSpec(blk, _fwd_map(5)), pl.BlockSpec(blk, _fwd_map(6)), pl.BlockSpec(blk, _fwd_map(7)),
                pl.BlockSpec(gblk, _fwd_map(0)), pl.BlockSpec(gblk, _fwd_map(1)),
                pl.BlockSpec(blk, _bwd_map(n, 5)), pl.BlockSpec(blk, _bwd_map(n, 6)),
                pl.BlockSpec(blk, _bwd_map(n, 7)),
                pl.BlockSpec(gblk, _bwd_map(n, 0)), pl.BlockSpec(gblk, _bwd_map(n, 1)),
                full(bias), full(cum), full(spread)]
    state = [pltpu.VMEM((N_HEADS, HEAD_D, 2 * HEAD_D), F32), pltpu.VMEM((1, LANES), F32)]
    return (_ml_kernel, in_specs, [z, z, z, gates, gates, z, z, z, gates, gates, bias, cum, spread], state * 2,
            "fffffbbbbbccc")


def _head_norm(x, center):
    outs = []
    for h in range(N_HEADS):
        xh = x[:, h * HEAD_D:(h + 1) * HEAD_D]
        if center:
            xh = xh - jnp.mean(xh, axis=1, keepdims=True)
        outs.append(xh * lax.rsqrt(jnp.mean(xh * xh, axis=1, keepdims=True) + EPS))
    return jnp.concatenate(outs, axis=1)


def _layer_norm(x, g, b):
    xc = x - jnp.mean(x, axis=1, keepdims=True)
    return xc * lax.rsqrt(jnp.mean(xc * xc, axis=1, keepdims=True) + EPS) * g + b


def _mix_rows(alpha, even, af, ab, ag, bf, bb, bg, h_ref, w_ref, lng_ref, lnb_ref):
    a = af[...].astype(F32) + ab[...].astype(F32)
    b = bf[...].astype(F32) + bb[...].astype(F32)
    if even:
        a = _head_norm(a, False) * _silu(ag[...])
        b = b * _gelu_tanh(bg[...])
    else:
        a = _head_norm(a, False) * _silu(ag[...])
        b = _head_norm(b, True) * _sigmoid(bg[...])
    mix = _dot(a, w_ref[0:GROUP_W, :]) + _dot(b, w_ref[GROUP_W:, :])
    return _layer_norm(alpha * h_ref[...] + mix, lng_ref[...], lnb_ref[...])


N_MIX_OPERANDS = 10


def _mix_specs(tm, col_a, col_b, index):
    half = lambda c: pl.BlockSpec((tm, GROUP_W), lambda *g: (index(*g), c))
    rows = pl.BlockSpec((tm, D_MODEL), lambda *g: (index(*g), 0))
    vec = pl.BlockSpec((1, D_MODEL), lambda *g: (0, 0))
    return [half(0), half(0), half(col_a), half(0), half(0), half(col_b), rows,
            pl.BlockSpec((D_MODEL, D_MODEL), lambda *g: (0, 0)), vec, vec]


def _mix_ffn_kernel(alpha, *refs):
    mix_refs = refs[:N_MIX_OPERANDS]
    wg_ref, wu_ref, wo_ref, lng_ref, lnb_ref, o_ref, h1_ref, acc_ref = refs[N_MIX_OPERANDS:]
    f = pl.program_id(1)

    @pl.when(f == 0)
    def _():
        h1_ref[...] = _mix_rows(alpha, True, *mix_refs)
        acc_ref[...] = jnp.zeros_like(acc_ref)

    x = h1_ref[...].astype(BF16)
    act = _silu(_dot(x, wg_ref[...])) * _dot(x, wu_ref[...])
    acc_ref[...] += _dot(act, wo_ref[...])

    @pl.when(f == pl.num_programs(1) - 1)
    def _():
        o_ref[...] = _layer_norm(alpha * h1_ref[...] + acc_ref[...], lng_ref[...], lnb_ref[...])


def _mix_ffn(alpha, af, ab, bf, bb, z, col_a, col_b, h, w_out, ln1_g, ln1_b, wi, wo, ln2_g, ln2_b):
    t = h.shape[0]
    d_ff = wo.shape[0]
    tm = _pick_tile(t, 640)
    tf = _pick_tile(d_ff, 1536)
    nf = d_ff // tf
    rows = pl.BlockSpec((tm, D_MODEL), lambda i, f: (i, 0))
    vec = pl.BlockSpec((1, D_MODEL), lambda i, f: (0, 0))
    return pl.pallas_call(
        functools.partial(_mix_ffn_kernel, alpha), grid=(t // tm, nf),
        in_specs=_mix_specs(tm, col_a, col_b, lambda i, f: i) + [
            pl.BlockSpec((D_MODEL, tf), lambda i, f: (0, f)),
            pl.BlockSpec((D_MODEL, tf), lambda i, f: (0, f + nf)),
            pl.BlockSpec((tf, D_MODEL), lambda i, f: (f, 0)), vec, vec],
        out_specs=rows, out_shape=jax.ShapeDtypeStruct((t, D_MODEL), F32),
        scratch_shapes=[pltpu.VMEM((tm, D_MODEL), F32), pltpu.VMEM((tm, D_MODEL), F32)],
        compiler_params=_params(2), name="mix_ffn")(
            af, ab, z, bf, bb, z, h, w_out, ln1_g.reshape(1, -1), ln1_b.reshape(1, -1),
            wi, wi, wo, ln2_g.reshape(1, -1), ln2_b.reshape(1, -1))


EXPERT_TILE = 1024
DMA_UNROLL = 8
DMA_PRIORITIES = 2


def _mix_route_kernel(alpha, pad_starts, *refs):
    mix_refs = refs[:N_MIX_OPERANDS]
    router_ref, tri_ref, o_ref, meta_ref, gate_ref, cnt_ref, carry_ref = refs[N_MIX_OPERANDS:]

    @pl.when(pl.program_id(0) == 0)
    def _():
        carry_ref[...] = jnp.zeros_like(carry_ref)

    h1 = _mix_rows(alpha, False, *mix_refs)
    o_ref[...] = h1
    tm = h1.shape[0]
    row = pl.program_id(0) * tm + lax.broadcasted_iota(jnp.int32, (tm, 1), 0)
    is_pad = row < 0
    for s in pad_starts:
        is_pad = is_pad | ((row >= s) & (row < s + PAD_ROWS))
    is_token = jnp.logical_not(is_pad)

    logits = _dot_16bit(h1, router_ref[...])
    lane = lax.broadcasted_iota(jnp.int32, logits.shape, 1)
    logits = jnp.where(lane < N_EXP, logits, -jnp.inf)
    top1 = jnp.max(logits, axis=1, keepdims=True)
    idx1 = jnp.min(jnp.where(logits == top1, lane, LANES), axis=1, keepdims=True)
    rest = jnp.where(lane == idx1, -jnp.inf, logits)
    top2 = jnp.max(rest, axis=1, keepdims=True)
    idx2 = jnp.min(jnp.where(rest == top2, lane, LANES), axis=1, keepdims=True)
    g2 = jnp.exp(top2 - top1)
    denom = 1.0 + g2
    hit1 = lane == idx1
    hit2 = lane == idx2
    both = jnp.where(is_token, jnp.where(hit1, 1.0, 0.0) + jnp.where(hit2, 1.0, 0.0), 0.0)
    prefix = _dot(tri_ref[...], both) + carry_ref[...]
    rank1 = jnp.sum(jnp.where(hit1, prefix, 0.0), axis=1, keepdims=True).astype(jnp.int32)
    rank2 = jnp.sum(jnp.where(hit2, prefix, 0.0), axis=1, keepdims=True).astype(jnp.int32)
    carry_ref[...] += jnp.sum(both, axis=0, keepdims=True)
    cnt_ref[...] = carry_ref[...]
    meta_ref[...] = jnp.where(lane == 0, idx1, jnp.where(lane == 1, idx2, jnp.where(
        lane == 2, rank1, jnp.where(lane == 3, rank2, jnp.where(is_token & (lane == 4), 1, 0)))))
    gate_ref[...] = jnp.where(lane == 0, 1.0 / denom, jnp.where(lane == 1, g2 / denom, 0.0))


def _mix_route(alpha, af, ab, bf, bb, z, col_a, col_b, h, w_out, ln_g, ln_b, router, pad_starts):
    t = h.shape[0]
    tm = _pick_tile(t, 512)
    router_p = _stack_weight_terms(jnp.pad(router, ((0, 0), (0, LANES - router.shape[1]))))
    pos = np.arange(tm)
    tri = jnp.asarray(pos[:, None] > pos[None, :], BF16)
    rows = lambda w: pl.BlockSpec((tm, w), lambda i: (i, 0))
    const = lambda a: pl.BlockSpec(a.shape, lambda i: (0, 0))
    return pl.pallas_call(
        functools.partial(_mix_route_kernel, alpha, pad_starts), grid=(t // tm,),
        in_specs=_mix_specs(tm, col_a, col_b, lambda i: i) + [const(router_p), const(tri)],
        out_specs=[rows(D_MODEL), rows(LANES), rows(LANES), pl.BlockSpec((1, LANES), lambda i: (0, 0))],
        out_shape=[jax.ShapeDtypeStruct((t, D_MODEL), F32), jax.ShapeDtypeStruct((t, LANES), jnp.int32),
                   jax.ShapeDtypeStruct((t, LANES), F32), jax.ShapeDtypeStruct((1, LANES), F32)],
        scratch_shapes=[pltpu.VMEM((1, LANES), F32)],
        compiler_params=_params(1), name="mix_route")(
            af, ab, z, bf, bb, z, h, w_out, ln_g.reshape(1, -1), ln_b.reshape(1, -1), router_p, tri)


def _dispatch_plan(meta, counts, t):
    cnt = counts[0, :N_EXP].astype(jnp.int32)
    padded = ((cnt + EXPERT_TILE - 1) // EXPERT_TILE) * EXPERT_TILE
    ends = jnp.cumsum(padded)
    off = ends - padded

    def base(e):
        return sum(jnp.where(e == i, off[i], 0) for i in range(N_EXP))

    is_token = meta[:, 4] > 0
    n_tiles = -(-2 * t // EXPERT_TILE) + N_EXP
    spare = n_tiles * EXPERT_TILE + 2 * (jnp.cumsum(jnp.logical_not(is_token).astype(jnp.int32)) - 1)
    pos1 = (base(meta[:, 0]) + meta[:, 2]).astype(jnp.int32)
    pos2 = (base(meta[:, 1]) + meta[:, 3]).astype(jnp.int32)
    scatter = (jnp.where(is_token, pos1, spare), jnp.where(is_token, pos2, spare + 1))
    gather = (jnp.where(is_token, pos1, 0), jnp.where(is_token, pos2, 0))
    starts = jnp.arange(n_tiles, dtype=jnp.int32) * EXPERT_TILE
    tile_expert = jnp.minimum(jnp.sum(starts[:, None] >= ends[None, :], axis=1), N_EXP - 1).astype(jnp.int32)
    n_active = (ends[-1] // EXPERT_TILE).astype(jnp.int32).reshape(1)
    return ends.astype(jnp.int32), scatter, gather, tile_expert, n_active, n_tiles


assert D_MODEL == SUBLANES * LANES


def _store_token_tiles(ref, x):
    n = x.shape[0]
    for s in range(SUBLANES):
        ref[pl.ds(s, n, stride=SUBLANES), :] = x[:, s * LANES:(s + 1) * LANES]


def _load_token_tiles(ref, n):
    return jnp.concatenate([ref[pl.ds(s, n, stride=SUBLANES), :] for s in range(SUBLANES)], axis=1)


def _tile_rows(i):
    return pl.ds(pl.multiple_of(i * SUBLANES, SUBLANES), SUBLANES)


def _dispatch_kernel(n_tiles, ends_ref, pos1_ref, pos2_ref, h_ref, xs_ref, tok_ref, zero_ref, sem, zero_sem):
    n = h_ref.shape[0]

    @pl.when(pl.program_id(0) == 0)
    def _():
        zero_ref[...] = jnp.zeros_like(zero_ref)

        def clear_last_tile(e):
            first_row = pl.multiple_of((ends_ref[e] - EXPERT_TILE) * SUBLANES, SUBLANES)
            return pltpu.make_async_copy(zero_ref, xs_ref.at[pl.ds(first_row, EXPERT_TILE * SUBLANES)], zero_sem)

        def has_tiles(e):
            return ends_ref[e] > (ends_ref[e - 1] if e else 0)

        for e in range(N_EXP):
            pl.when(has_tiles(e))(lambda e=e: clear_last_tile(e).start())
        for e in range(N_EXP):
            pl.when(has_tiles(e))(lambda e=e: clear_last_tile(e).wait())

        def clear_tile(i):
            first_row = pl.multiple_of(i * (EXPERT_TILE * SUBLANES), SUBLANES)
            return pltpu.make_async_copy(zero_ref, xs_ref.at[pl.ds(first_row, EXPERT_TILE * SUBLANES)], zero_sem)

        used = ends_ref[N_EXP - 1] // EXPERT_TILE
        lax.fori_loop(used, n_tiles, lambda i, c: (clear_tile(i).start(), c)[1], 0)
        lax.fori_loop(used, n_tiles, lambda i, c: (clear_tile(i).wait(), c)[1], 0)

    g = pl.program_id(0)
    slot = g % 2
    _store_token_tiles(tok_ref.at[slot], h_ref[...])

    def tile_copy(r, p):
        return pltpu.make_async_copy(tok_ref.at[slot].at[_tile_rows(r)], xs_ref.at[_tile_rows(p)], sem.at[slot])

    def start(pair, c):
        for prio in range(DMA_PRIORITIES):
            r = pair * DMA_PRIORITIES + prio
            tile_copy(r, pos1_ref[0, 0, r]).start(priority=prio)
            tile_copy(r, pos2_ref[0, 0, r]).start(priority=prio)
        return c

    lax.fori_loop(0, n // DMA_PRIORITIES, start, 0, unroll=DMA_UNROLL // DMA_PRIORITIES)

    def wait_slot(s):
        all_rows = pltpu.make_async_copy(tok_ref.at[s], xs_ref.at[pl.ds(0, n * SUBLANES)], sem.at[s])
        all_rows.wait()
        all_rows.wait()

    pl.when(g > 0)(lambda: wait_slot(1 - slot))
    pl.when(g == pl.num_programs(0) - 1)(lambda: wait_slot(slot))


def _dispatch(h, ends, pos1, pos2, n_tiles, n_spare):
    t = h.shape[0]
    n_rows = n_tiles * EXPERT_TILE + n_spare
    tm = _pick_tile(t, 512)
    idx = lambda: pl.BlockSpec((1, 1, tm), lambda i, ends: (i, 0, 0), memory_space=pltpu.SMEM)
    grid_spec = pltpu.PrefetchScalarGridSpec(
        num_scalar_prefetch=1, grid=(t // tm,),
        in_specs=[idx(), idx(), pl.BlockSpec((tm, D_MODEL), lambda i, ends: (i, 0))],
        out_specs=pl.BlockSpec(memory_space=pl.ANY),
        scratch_shapes=[pltpu.VMEM((2, tm * SUBLANES, LANES), F32),
                        pltpu.VMEM((EXPERT_TILE * SUBLANES, LANES), F32),
                        pltpu.SemaphoreType.DMA((2,)), pltpu.SemaphoreType.DMA(())])
    return pl.pallas_call(
        functools.partial(_dispatch_kernel, n_tiles), grid_spec=grid_spec,
        out_shape=jax.ShapeDtypeStruct((n_rows * SUBLANES, LANES), F32),
        compiler_params=_params(1), name="dispatch")(
            ends, pos1.reshape(t // tm, 1, tm), pos2.reshape(t // tm, 1, tm), h)


def _experts_kernel(te_ref, na_ref, x_ref, wg_ref, wu_ref, wo_ref, y_ref, xb_ref, acc_ref):
    i = pl.program_id(0)
    f = pl.program_id(1)
    last = f == pl.num_programs(1) - 1
    active = i < na_ref[0]

    @pl.when(active)
    def _():
        @pl.when(f == 0)
        def _():
            acc_ref[...] = jnp.zeros_like(acc_ref)
            xb_ref[...] = _load_token_tiles(x_ref, EXPERT_TILE).astype(BF16)

        x = xb_ref[...]
        act = _silu(_dot(x, wg_ref[0])) * _dot(x, wu_ref[0])
        acc_ref[...] += _dot(act, wo_ref[0])

        @pl.when(last)
        def _():
            _store_token_tiles(y_ref, acc_ref[...])

    @pl.when(jnp.logical_not(active) & last)
    def _():
        y_ref[...] = jnp.zeros_like(y_ref)


def _experts(xs, tile_expert, n_active, wi, wo, n_tiles):
    n_rows = n_tiles * EXPERT_TILE
    e_ff = wo.shape[1]
    tf = _pick_tile(e_ff, 512)
    nf = e_ff // tf
    rows = pl.BlockSpec((EXPERT_TILE * SUBLANES, LANES), lambda i, f, te, na: (i, 0))
    rows_in = pl.BlockSpec((EXPERT_TILE * SUBLANES, LANES),
                           lambda i, f, te, na: (jnp.minimum(i, jnp.maximum(na[0] - 1, 0)), 0))
    grid_spec = pltpu.PrefetchScalarGridSpec(
        num_scalar_prefetch=2, grid=(n_rows // EXPERT_TILE, nf),
        in_specs=[rows_in, pl.BlockSpec((1, D_MODEL, tf), lambda i, f, te, na: (te[i], 0, f)),
                  pl.BlockSpec((1, D_MODEL, tf), lambda i, f, te, na: (te[i], 0, f + nf)),
                  pl.BlockSpec((1, tf, D_MODEL), lambda i, f, te, na: (te[i], f, 0))],
        out_specs=rows,
        scratch_shapes=[pltpu.VMEM((EXPERT_TILE, D_MODEL), BF16), pltpu.VMEM((EXPERT_TILE, D_MODEL), F32)])
    return pl.pallas_call(
        _experts_kernel, grid_spec=grid_spec, out_shape=jax.ShapeDtypeStruct((n_rows * SUBLANES, LANES), F32),
        compiler_params=_params(2), name="experts")(tile_expert, n_active, xs, wi, wi, wo)


def _gather_start(n, pos1_ref, pos2_ref, ys_ref, a_ref, b_ref, sems):
    def tile_copy(p, buf, r, which):
        return pltpu.make_async_copy(ys_ref.at[_tile_rows(p)], buf.at[_tile_rows(r)], sems.at[which])

    def start(pair, c):
        for prio in range(DMA_PRIORITIES):
            r = pair * DMA_PRIORITIES + prio
            tile_copy(pos1_ref[0, 0, r], a_ref, r, 0).start(priority=prio)
            tile_copy(pos2_ref[0, 0, r], b_ref, r, 1).start(priority=prio)
        return c

    lax.fori_loop(0, n // DMA_PRIORITIES, start, 0, unroll=DMA_UNROLL // DMA_PRIORITIES)


def _gather_finish(alpha, h_ref, gate_ref, ys_ref, lng_ref, lnb_ref, a_ref, b_ref, sems):
    n = h_ref.shape[0]
    for which, buf in enumerate((a_ref, b_ref)):
        pltpu.make_async_copy(ys_ref.at[pl.ds(0, n * SUBLANES)], buf, sems.at[which]).wait()
    gate = gate_ref[...]
    y = gate[:, 0:1] * _load_token_tiles(a_ref, n) + gate[:, 1:2] * _load_token_tiles(b_ref, n)
    return _layer_norm(alpha * h_ref[...] + y, lng_ref[...], lnb_ref[...])


def _combine_kernel(alpha, pos1_ref, pos2_ref, h_ref, gate_ref, ys_ref, lng_ref, lnb_ref, o_ref,
                    a_ref, b_ref, sems):
    _gather_start(h_ref.shape[0], pos1_ref, pos2_ref, ys_ref, a_ref, b_ref, sems)
    o_ref[...] = _gather_finish(alpha, h_ref, gate_ref, ys_ref, lng_ref, lnb_ref, a_ref, b_ref, sems)


def _combine_scratch(tm):
    return [pltpu.VMEM((tm * SUBLANES, LANES), F32), pltpu.VMEM((tm * SUBLANES, LANES), F32),
            pltpu.SemaphoreType.DMA((2,))]


def _combine(alpha, h, gates, ys, pos1, pos2, ln_g, ln_b):
    t = h.shape[0]
    tm = _pick_tile(t, 512)
    idx = lambda: pl.BlockSpec((1, 1, tm), lambda i: (i, 0, 0), memory_space=pltpu.SMEM)
    rows = pl.BlockSpec((tm, D_MODEL), lambda i: (i, 0))
    vec = pl.BlockSpec((1, D_MODEL), lambda i: (0, 0))
    return pl.pallas_call(
        functools.partial(_combine_kernel, alpha), grid=(t // tm,),
        in_specs=[idx(), idx(), rows, pl.BlockSpec((tm, LANES), lambda i: (i, 0)),
                  pl.BlockSpec(memory_space=pl.ANY), vec, vec],
        out_specs=rows, out_shape=jax.ShapeDtypeStruct((t, D_MODEL), F32),
        scratch_shapes=_combine_scratch(tm),
        compiler_params=_params(1), name="combine")(
            pos1.reshape(t // tm, 1, tm), pos2.reshape(t // tm, 1, tm), h, gates, ys,
            ln_g.reshape(1, -1), ln_b.reshape(1, -1))


def _output_tables(seq_shapes):
    kind, live = [], []
    window = [[], []]
    recent = [(0, 0), (0, 0)]
    for grp, (b, l) in enumerate(seq_shapes):
        for i in range(b):
            for c in range(l // CH + 1):
                kind.append(grp)
                live.append(int(c > 0))
                recent[grp] = (i, max(c - 1, 0))
                for g2 in range(2):
                    window[g2].append(recent[g2])
    tables = [kind, live] + [[w[j] for w in window[g2]] for g2 in range(2) for j in range(2)]
    return [jnp.asarray(np.asarray(x, np.int32)) for x in tables]


def _combine_out_kernel(alpha, kind_ref, live_ref, ab_ref, ac_ref, bb_ref, bc_ref,
                        pos1_ref, pos2_ref, nxt1_ref, nxt2_ref, h_ref, gate_ref, ys_ref, lng_ref, lnb_ref,
                        outa_ref, outb_ref, a_ref, b_ref, sems):
    g = pl.program_id(0)
    last = pl.num_programs(0) - 1
    slot = g % 2
    nxt = jnp.minimum(g + 1, last)

    @pl.when((g == 0) & (live_ref[0] == 1))
    def _():
        _gather_start(CH, pos1_ref, pos2_ref, ys_ref, a_ref.at[0], b_ref.at[0], sems.at[0])

    @pl.when((g < last) & (live_ref[nxt] == 1))
    def _():
        _gather_start(CH, nxt1_ref, nxt2_ref, ys_ref, a_ref.at[1 - slot], b_ref.at[1 - slot], sems.at[1 - slot])

    @pl.when(live_ref[g] == 1)
    def _():
        res = _gather_finish(alpha, h_ref, gate_ref, ys_ref, lng_ref, lnb_ref,
                             a_ref.at[slot], b_ref.at[slot], sems.at[slot])
        for grp, out_ref in enumerate((outa_ref, outb_ref)):
            @pl.when(kind_ref[g] == grp)
            def _(out_ref=out_ref):
                out_ref[0] = res


def _combine_out(alpha, h, gates, ys, pos1, pos2, ln_g, ln_b, seq_shapes):
    t = h.shape[0]
    n = t // CH
    tables = _output_tables(seq_shapes)
    idx = lambda: pl.BlockSpec((1, 1, CH), lambda g, *tb: (g, 0, 0), memory_space=pltpu.SMEM)
    idx_next = lambda: pl.BlockSpec((1, 1, CH), lambda g, *tb: (jnp.minimum(g + 1, n - 1), 0, 0),
                                    memory_space=pltpu.SMEM)
    rows = lambda w: pl.BlockSpec((CH, w), lambda g, *tb: (g, 0))
    vec = pl.BlockSpec((1, D_MODEL), lambda g, *tb: (0, 0))
    out = lambda grp: pl.BlockSpec((1, CH, D_MODEL),
                                   lambda g, *tb: (tb[2 + 2 * grp][g], tb[3 + 2 * grp][g], 0))
    fetched = pltpu.VMEM((2, CH * SUBLANES, LANES), F32)
    grid_spec = pltpu.PrefetchScalarGridSpec(
        num_scalar_prefetch=len(tables), grid=(n,),
        in_specs=[idx(), idx(), idx_next(), idx_next(), rows(D_MODEL), rows(LANES),
                  pl.BlockSpec(memory_space=pl.ANY), vec, vec],
        out_specs=[out(0), out(1)], scratch_shapes=[fetched, fetched, pltpu.SemaphoreType.DMA((2, 2))])
    pos1, pos2 = pos1.reshape(n, 1, CH), pos2.reshape(n, 1, CH)
    return pl.pallas_call(
        functools.partial(_combine_out_kernel, alpha), grid_spec=grid_spec,
        out_shape=[jax.ShapeDtypeStruct((b, l, D_MODEL), F32) for b, l in seq_shapes],
        compiler_params=_params(1), name="combine_out")(
            *tables, pos1, pos2, pos1, pos2, h, gates, ys, ln_g.reshape(1, -1), ln_b.reshape(1, -1))


def _moe(alpha, h, meta, gates, counts, wi, wo, ln_g, ln_b, n_pad_rows, out_shapes=None):
    t = h.shape[0]
    ends, scatter, gather, tile_expert, n_active, n_tiles = _dispatch_plan(meta, counts, t)
    n_spare = 2 * n_pad_rows
    xs = _dispatch(h, ends, scatter[0], scatter[1], n_tiles, n_spare)
    ys = _experts(xs, tile_expert, n_active, wi, wo, n_tiles)
    if out_shapes is None:
        return _combine(alpha, h, gates, ys, gather[0], gather[1], ln_g, ln_b)
    return _combine_out(alpha, h, gates, ys, gather[0], gather[1], ln_g, ln_b, out_shapes)


def kernel(x_prompt, x_sample, meta, e_w_in, e_conv_w, e_conv_b, e_lru_wa, e_lru_ba, e_lru_wx, e_lru_bx,
           e_lru_lambda, e_w_out, e_ffn_wi, e_ffn_wo, o_w_in, o_hg_lower, o_ml_bi, o_ml_bf, o_w_out, o_router,
           o_exp_wi, o_exp_wo, ln_g, ln_b):
    groups = (x_prompt, x_sample)
    depth = ln_g.shape[0]
    alpha = (2.0 * depth) ** 0.25
    seq_shapes = [(x.shape[0], x.shape[1]) for x in groups]
    for _, l in seq_shapes:
        assert l % CH == 0
    cpos_np, clast_np = _chunk_tables(seq_shapes)
    cpos, clast = jnp.asarray(cpos_np), jnp.asarray(clast_np)
    max_rows = max(l for _, l in seq_shapes) + CH
    pad_starts = tuple(int(i) * CH for i in np.flatnonzero(cpos_np == 0))

    head = jnp.concatenate([jnp.zeros((PAD_ROWS, D_MODEL), F32), meta.astype(F32)], axis=0)
    parts = []
    for x in groups:
        full = jnp.concatenate([jnp.broadcast_to(head[None], (x.shape[0], CH, D_MODEL)), x], axis=1)
        parts.append(full.reshape(-1, D_MODEL))
    h = jnp.concatenate(parts, axis=0)
    t = h.shape[0]

    n_odd_cols = 9 * GROUP_W
    for layer in range(depth):
        p = layer // 2
        if layer % 2 == 0:
            z = _project(h, e_w_in[p].astype(BF16))
            (ret_f, ret_b), (lru_f, lru_b) = _chunk_walk(
                [_retention(z, cpos, clast, max_rows),
                 _rglru(z, 4, cpos, clast, e_conv_w[p], e_conv_b[p], e_lru_wa[p], e_lru_ba[p],
                        e_lru_wx[p], e_lru_bx[p], e_lru_lambda[p])], cpos, clast, t, "even_mixers")
            h = _mix_ffn(alpha, ret_f, ret_b, lru_f, lru_b, z, 3, 5, h, e_w_out[p].astype(BF16),
                         ln_g[layer, 0], ln_b[layer, 0], e_ffn_wi[p].astype(BF16), e_ffn_wo[p].astype(BF16),
                         ln_g[layer, 1], ln_b[layer, 1])
        else:
            w_in = o_w_in[p]
            n_gate = 2 * N_HEADS
            gate_pad = ((0, 0), (0, LANES - n_gate))
            w_gates = jnp.concatenate([jnp.pad(w_in[:, n_odd_cols:n_odd_cols + n_gate], gate_pad),
                                       jnp.pad(w_in[:, n_odd_cols + n_gate:], gate_pad)], axis=1)
            z, gates = _project(h, w_in[:, :n_odd_cols].astype(BF16), w_gates)
            cps = ODD_CHUNKS_PER_STEP if (t // CH) % ODD_CHUNKS_PER_STEP == 0 else 1
            (hg_f, hg_b), (ml_f, ml_b) = _chunk_walk(
                [_hgrn2(z, o_hg_lower, layer, cps), _mlstm(z, gates, o_ml_bi[p], o_ml_bf[p], cps)],
                cpos, clast, t, "odd_mixers", cps)
            h, meta_r, gates_r, counts = _mix_route(
                alpha, hg_f, hg_b, ml_f, ml_b, z, 4, 8, h, o_w_out[p].astype(BF16),
                ln_g[layer, 0], ln_b[layer, 0], o_router[p], pad_starts)
            h = _moe(alpha, h, meta_r, gates_r, counts, o_exp_wi[p].astype(BF16), o_exp_wo[p].astype(BF16),
                     ln_g[layer, 1], ln_b[layer, 1], len(pad_starts) * PAD_ROWS,
                     seq_shapes if layer == depth - 1 else None)
            if layer == depth - 1:
                return tuple(h)

    outs = []
    row = 0
    for b, l in seq_shapes:
        n = b * (l + CH)
        outs.append(h[row:row + n].reshape(b, l + CH, D_MODEL)[:, CH:])
        row += n
    return tuple(outs)
```

```python
import functools
import math

import numpy as np
import jax
import jax.numpy as jnp
from jax import lax
from jax.experimental import pallas as pl
from jax.experimental.pallas import tpu as pltpu

F32 = jnp.float32
BF16 = jnp.bfloat16

D_MODEL = 1024
GROUP_W = D_MODEL // 2
N_HEADS = 4
HEAD_D = GROUP_W // N_HEADS
N_META = 16
ROPE_BASE = 10000.0
LRU_BLOCKS = 8
LRU_BW = GROUP_W // LRU_BLOCKS
LRU_C = 8.0
N_EXP = 8
EPS = 1e-5

LANES = 128
SUBLANES = 8
CH = 128
PAD_ROWS = CH - N_META
SUB = SUBLANES
GRP = 4 * SUB
ODD_CHUNKS_PER_STEP = 3
NEG_BIG = -1e30
LOG2_E = math.log2(math.e)
VMEM_LIMIT = 56 * 1024 * 1024


def _dot(a, b):
    return jnp.dot(a.astype(BF16), b.astype(BF16), preferred_element_type=F32)


def _dot_nt(a, b):
    return lax.dot_general(a.astype(BF16), b.astype(BF16), (((1,), (1,)), ((), ())),
                           preferred_element_type=F32)


def _dot_tn(a, b):
    return _dot(a.T, b)


def _bf16_terms(x, n_terms):
    terms = []
    for _ in range(n_terms):
        t = x.astype(BF16)
        terms.append(t)
        x = x - t.astype(F32)
    return terms


def _stack_weight_terms(w):
    hi, lo = _bf16_terms(w, 2)
    return jnp.concatenate([hi, lo, hi], axis=0)


def _dot_16bit(x, w_terms):
    hi, lo = _bf16_terms(x, 2)
    return jnp.dot(jnp.concatenate([hi, hi, lo], axis=1), w_terms, preferred_element_type=F32)


PREFIX_TERMS = 3
SPREAD_TERMS = 2


def _prefix_dot(op_tiled, x):
    return jnp.dot(op_tiled, jnp.concatenate(_bf16_terms(x, PREFIX_TERMS), axis=0),
                   preferred_element_type=F32)


def _spread_dot(x, op_tiled):
    return jnp.dot(jnp.concatenate(_bf16_terms(x, SPREAD_TERMS), axis=1), op_tiled,
                   preferred_element_type=F32)


def _sigmoid(x):
    return 1.0 / (1.0 + jnp.exp(-x))


def _silu(x):
    return x * _sigmoid(x)


def _gelu_tanh(x):
    return 0.5 * x * (1.0 + jnp.tanh(math.sqrt(2.0 / math.pi) * (x + 0.044715 * (x * x * x))))


def _softplus(x):
    return jnp.maximum(x, 0.0) + jnp.log(1.0 + jnp.exp(-jnp.abs(x)))


def _pick_tile(total, target):
    best = LANES
    for t in range(LANES, min(total, target) + 1, LANES):
        if total % t == 0:
            best = t
    return best


def _params(n_axes, sem="arbitrary"):
    return pltpu.CompilerParams(dimension_semantics=(sem,) * n_axes, vmem_limit_bytes=VMEM_LIMIT)


def _proj_kernel(x_ref, w_ref, o_ref):
    o_ref[...] = _dot(x_ref[...], w_ref[...])


def _proj_gates_kernel(x_ref, w_ref, wg_ref, o_ref, g_ref):
    g_ref[...] = _dot_16bit(x_ref[...], wg_ref[...])
    o_ref[...] = _dot(x_ref[...], w_ref[...])


PROJ_OUT_BLOCK_BYTES = 8 * 1024 * 1024


def _project(x, w, w_gates=None):
    t, k = x.shape
    n = w.shape[1]
    tm = _pick_tile(t, PROJ_OUT_BLOCK_BYTES // (4 * n))
    x_spec = pl.BlockSpec((tm, k), lambda i: (i, 0))
    w_spec = pl.BlockSpec((k, n), lambda i: (0, 0))
    o_spec = pl.BlockSpec((tm, n), lambda i: (i, 0))
    if w_gates is None:
        return pl.pallas_call(
            _proj_kernel, grid=(t // tm,), in_specs=[x_spec, w_spec], out_specs=o_spec,
            out_shape=jax.ShapeDtypeStruct((t, n), F32), compiler_params=_params(1),
            name="proj")(x, w)
    ng = w_gates.shape[1]
    w_gates = _stack_weight_terms(w_gates)
    return pl.pallas_call(
        _proj_gates_kernel, grid=(t // tm,),
        in_specs=[x_spec, w_spec, pl.BlockSpec(w_gates.shape, lambda i: (0, 0))],
        out_specs=[o_spec, pl.BlockSpec((tm, ng), lambda i: (i, 0))],
        out_shape=[jax.ShapeDtypeStruct((t, n), F32), jax.ShapeDtypeStruct((t, ng), F32)],
        compiler_params=_params(1), name="proj_gates")(x, w, w_gates)


def _chunk_tables(seq_shapes):
    cpos, clast = [], []
    for b, l in seq_shapes:
        n = l // CH + 1
        for _ in range(b):
            cpos += list(range(n))
            clast += [0] * (n - 1) + [1]
    return np.asarray(cpos, np.int32), np.asarray(clast, np.int32)


def _fwd_map(col):
    return lambda g, cp, cl: (g, col)


def _bwd_map(n_chunks, col):
    return lambda g, cp, cl: (n_chunks - 1 - g, col)


def _walk_kernel(bodies, tags, n_scratch, cps, cpos_ref, clast_ref, *refs):
    refs = list(refs)
    ins = [[refs.pop(0) for _ in t] for t in tags]
    outs = [[refs.pop(0) for _ in range(2)] for _ in bodies]
    scratch = [[refs.pop(0) for _ in range(k)] for k in n_scratch]

    @pl.when(pl.program_id(0) == 0)
    def _():
        for part in scratch:
            for r in part:
                r[...] = jnp.zeros_like(r)

    n_chunks = pl.num_programs(0) * cps
    for j in range(cps):
        g = pl.program_id(0) * cps + j
        gb = n_chunks - 1 - g
        rows = {"f": pl.ds(j * CH, CH), "b": pl.ds((cps - 1 - j) * CH, CH)}

        def view(ref, tag):
            return ref if tag == "c" or cps == 1 else ref.at[rows[tag]]

        for body, i, tg, o, s in zip(bodies, ins, tags, outs, scratch):
            body(cpos_ref, clast_ref, g, gb, *[view(r, t) for r, t in zip(i, tg)],
                 view(o[0], "f"), view(o[1], "b"), *s)


def _chunk_walk(parts, cpos, clast, t, name, cps=1):
    n = t // (CH * cps)
    blk = (CH * cps, GROUP_W)
    bodies = [p[0] for p in parts]
    grid_spec = pltpu.PrefetchScalarGridSpec(
        num_scalar_prefetch=2, grid=(n,), in_specs=[s for p in parts for s in p[1]],
        out_specs=[pl.BlockSpec(blk, _fwd_map(0)), pl.BlockSpec(blk, _bwd_map(n, 0))] * len(parts),
        scratch_shapes=[s for p in parts for s in p[3]])
    outs = pl.pallas_call(
        functools.partial(_walk_kernel, bodies, [p[4] for p in parts], [len(p[3]) for p in parts], cps),
        grid_spec=grid_spec, out_shape=[jax.ShapeDtypeStruct((t, GROUP_W), BF16)] * (2 * len(parts)),
        compiler_params=_params(1), name=name)(cpos, clast, *[a for p in parts for a in p[2]])
    return [outs[2 * i:2 * i + 2] for i in range(len(parts))]


def _row_valid(first):
    row = lax.broadcasted_iota(jnp.int32, (CH, 1), 0)
    return row >= jnp.where(first, PAD_ROWS, 0)


def _ret_log_gamma():
    return np.log1p(-np.exp2(-5.0 - np.arange(N_HEADS, dtype=np.float64)))


def _ret_consts():
    lg = _ret_log_gamma()
    pos = np.arange(CH, dtype=np.float64)
    rel = pos[:, None] - pos[None, :]
    dmat = np.zeros((2, N_HEADS, CH, CH), np.float64)
    rows = np.zeros((4, CH, GROUP_W), np.float64)
    for h in range(N_HEADS):
        dmat[0, h] = np.where(rel >= 0, np.exp(np.maximum(rel, 0.0) * lg[h]), 0.0)
        dmat[1, h] = np.where(rel < 0, np.exp(np.maximum(-rel, 0.0) * lg[h]), 0.0)
        sl = slice(h * HEAD_D, (h + 1) * HEAD_D)
        rows[0, :, sl] = np.exp((pos + 1.0) * lg[h])[:, None]
        rows[1, :, sl] = np.exp((CH - 1.0 - pos) * lg[h])[:, None]
        rows[2, :, sl] = np.exp((CH - pos) * lg[h])[:, None]
        rows[3, :, sl] = np.exp(pos * lg[h])[:, None]
    return jnp.asarray(dmat, F32), jnp.asarray(rows, F32)


def _ret_direction(q_ref, k_ref, v_ref, cos_ref, sin_ref, dmat_ref, rin_ref, rout_ref, s_ref, o_ref,
                   reset, first):
    valid = _row_valid(first)
    cos = cos_ref[...]
    sin = sin_ref[...]
    chunk_decay = np.exp(CH * _ret_log_gamma())
    for h in range(N_HEADS):
        sl = slice(h * HEAD_D, (h + 1) * HEAD_D)
        q = q_ref[:, sl]
        k = k_ref[:, sl]
        v = v_ref[:, sl]
        q = q * cos + pltpu.roll(q, HEAD_D // 2, 1) * sin
        k = (k * cos + pltpu.roll(k, HEAD_D // 2, 1) * sin) * (HEAD_D ** -0.5)
        k = jnp.where(valid, k, 0.0)
        scores = _dot_nt(q, k) * dmat_ref[h]
        state = jnp.where(reset, 0.0, s_ref[h])
        o_ref[:, sl] = _dot(jnp.concatenate([scores, q * rin_ref[:, sl]], axis=1),
                            jnp.concatenate([v, state], axis=0)).astype(o_ref.dtype)
        s_ref[h] = float(chunk_decay[h]) * state + _dot_tn(k * rout_ref[:, sl], v)


def _ret_kernel(cpos_ref, clast_ref, g, gb, qf, kf, vf, cosf, sinf, qb, kb, vb, cosb, sinb, dmat_ref, rows_ref,
                of_ref, ob_ref, sf_ref, sb_ref):
    _ret_direction(qf, kf, vf, cosf, sinf, dmat_ref.at[0], rows_ref.at[0], rows_ref.at[1], sf_ref, of_ref,
                   cpos_ref[g] == 0, cpos_ref[g] == 0)
    _ret_direction(qb, kb, vb, cosb, sinb, dmat_ref.at[1], rows_ref.at[2], rows_ref.at[3], sb_ref, ob_ref,
                   clast_ref[gb] == 1, cpos_ref[gb] == 0)


def _rope_tables(n_rows):
    inv = ROPE_BASE ** (-jnp.arange(0, HEAD_D, 2, dtype=jnp.float32) / HEAD_D)
    pos = jnp.maximum(jnp.arange(n_rows, dtype=jnp.float32) - PAD_ROWS, 0.0)
    ang = pos[:, None] * inv[None, :]
    cos, sin = jnp.cos(ang), jnp.sin(ang)
    return jnp.concatenate([cos, cos], axis=1), jnp.concatenate([-sin, sin], axis=1)


def _retention(z, cpos, clast, max_rows):
    t = z.shape[0]
    n = t // CH
    cos2, sin2 = _rope_tables(max_rows)
    dmat, rows = _ret_consts()
    blk = (CH, GROUP_W)
    in_specs = []
    for mk, pm in ((_fwd_map, lambda g, cp, cl: (cp[g], 0)),
                   (functools.partial(_bwd_map, n), lambda g, cp, cl: (cp[n - 1 - g], 0))):
        in_specs += [pl.BlockSpec(blk, mk(0)), pl.BlockSpec(blk, mk(1)), pl.BlockSpec(blk, mk(2)),
                     pl.BlockSpec((CH, HEAD_D), pm), pl.BlockSpec((CH, HEAD_D), pm)]
    in_specs += [pl.BlockSpec(dmat.shape, lambda g, cp, cl: (0, 0, 0, 0)),
                 pl.BlockSpec(rows.shape, lambda g, cp, cl: (0, 0, 0))]
    return (_ret_kernel, in_specs, [z, z, z, cos2, sin2, z, z, z, cos2, sin2, dmat, rows],
            [pltpu.VMEM((N_HEADS, HEAD_D, HEAD_D), F32)] * 2, "fffffbbbbbcc")


HALO = SUBLANES


def _shift_rows(x, s, fill, reverse):
    row = lax.broadcasted_iota(jnp.int32, (CH, 1), 0)
    if reverse:
        return jnp.where(row < CH - s, pltpu.roll(x, CH - s, 0), fill)
    return jnp.where(row >= s, pltpu.roll(x, s, 0), fill)


def _lru_direction(x_ref, prev_ref, next_ref, convw_ref, convb_ref, wg_ref, bias_ref, lam_ref, ext_ref,
                   carry_ref, o_ref, reset, first, last, reverse):
    valid = _row_valid(first)
    ext_ref[HALO:HALO + CH, :] = jnp.where(valid, x_ref[...], 0.0)
    ext_ref[0:HALO, :] = jnp.where(first, 0.0, prev_ref[...])
    ext_ref[HALO + CH:, :] = jnp.where(last, 0.0, next_ref[...])
    xc = convb_ref[...] + ext_ref[HALO - 2:HALO - 2 + CH, :] * convw_ref[0:1, :]
    for tap in range(1, 4):
        xc = xc + ext_ref[HALO - 2 + tap:HALO - 2 + tap + CH, :] * convw_ref[tap:tap + 1, :]

    log_sig_lam = -_softplus(-lam_ref[...])
    parts_a, parts_u = [], []
    for grp in range(GROUP_W // LANES):
        sl = slice(grp * LANES, (grp + 1) * LANES)
        xg = xc[:, sl]
        pre = _dot(xg, wg_ref[grp])
        r = _sigmoid(pre[:, :LANES] + bias_ref[0:1, sl])
        i = _sigmoid(pre[:, LANES:] + bias_ref[1:2, sl])
        a = jnp.exp(LRU_C * log_sig_lam[:, sl] * r)
        u = jnp.sqrt(1.0 - a * a) * (i * xg)
        parts_a.append(a)
        parts_u.append(jnp.where(valid, u, 0.0))
    a = jnp.concatenate(parts_a, axis=1)
    u = jnp.concatenate(parts_u, axis=1)

    n_blk = CH // SUBLANES
    a = a.reshape(n_blk, SUBLANES, GROUP_W)
    u = u.reshape(n_blk, SUBLANES, GROUP_W)
    row_in_blk = lax.broadcasted_iota(jnp.int32, (1, SUBLANES, 1), 1)
    s = 1
    while s < SUBLANES:
        keep = (row_in_blk < SUBLANES - s) if reverse else (row_in_blk >= s)
        shift = SUBLANES - s if reverse else s
        u = u + a * jnp.where(keep, pltpu.roll(u, shift, 1), 0.0)
        a = a * jnp.where(keep, pltpu.roll(a, shift, 1), 1.0)
        s *= 2
    edge = 0 if reverse else SUBLANES - 1
    carry = jnp.where(reset, 0.0, carry_ref[...])
    blocks = [None] * n_blk
    for blk in (reversed(range(n_blk)) if reverse else range(n_blk)):
        blocks[blk] = u[blk] + a[blk] * carry
        carry = blocks[blk][edge:edge + 1, :]
    o_ref[...] = jnp.concatenate(blocks, axis=0).astype(o_ref.dtype)
    carry_ref[...] = carry


def _lru_kernel(cpos_ref, clast_ref, g, gb, xf, pf, nf, xb, pb, nb, convw_ref, convb_ref, wg_ref, bias_ref,
                lam_ref, of_ref, ob_ref, extf_ref, extb_ref, cf_ref, cb_ref):
    _lru_direction(xf, pf, nf, convw_ref, convb_ref, wg_ref.at[0], bias_ref.at[0], lam_ref.at[0], extf_ref,
                   cf_ref, of_ref, cpos_ref[g] == 0, cpos_ref[g] == 0, clast_ref[g] == 1, False)
    _lru_direction(xb, pb, nb, convw_ref, convb_ref, wg_ref.at[1], bias_ref.at[1], lam_ref.at[1], extb_ref,
                   cb_ref, ob_ref, clast_ref[gb] == 1, cpos_ref[gb] == 0, clast_ref[gb] == 1, True)


def _lru_gate_weights(wa, wx):
    per = LANES // LRU_BW

    def block_diag(w):
        w = w.reshape(2, LRU_BLOCKS // per, per, LRU_BW, LRU_BW)
        eye = jnp.eye(per, dtype=w.dtype)
        return jnp.einsum("dgpij,pq->dgpiqj", w, eye).reshape(2, LRU_BLOCKS // per, LANES, LANES)

    return jnp.concatenate([block_diag(wa), block_diag(wx)], axis=-1).astype(BF16)


def _rglru(z, col, cpos, clast, conv_w, conv_b, wa, ba, wx, bx, lam):
    t = z.shape[0]
    n = t // CH
    per = CH // HALO
    n_halo = t // HALO
    blk = (CH, GROUP_W)
    hblk = (HALO, GROUP_W)
    wg = _lru_gate_weights(wa, wx)
    bias = jnp.stack([ba, bx], axis=1)
    lam = lam.reshape(2, 1, GROUP_W)

    def prev_f(g, cp, cl):
        return (jnp.maximum(g * per - 1, 0), col)

    def next_f(g, cp, cl):
        return (jnp.minimum((g + 1) * per, n_halo - 1), col)

    def prev_b(g, cp, cl):
        return (jnp.maximum((n - 1 - g) * per - 1, 0), col)

    def next_b(g, cp, cl):
        return (jnp.minimum((n - g) * per, n_halo - 1), col)

    full = lambda a: pl.BlockSpec(a.shape, lambda g, cp, cl: (0,) * a.ndim)
    conv_b2 = conv_b.reshape(1, GROUP_W)
    in_specs = [pl.BlockSpec(blk, _fwd_map(col)), pl.BlockSpec(hblk, prev_f), pl.BlockSpec(hblk, next_f),
                pl.BlockSpec(blk, _bwd_map(n, col)), pl.BlockSpec(hblk, prev_b), pl.BlockSpec(hblk, next_b),
                full(conv_w), full(conv_b2), full(wg), full(bias), full(lam)]
    return (_lru_kernel, in_specs, [z, z, z, z, z, z, conv_w, conv_b2, wg, bias, lam],
            [pltpu.VMEM((CH + 2 * HALO, GROUP_W), F32)] * 2 + [pltpu.VMEM((1, GROUP_W), F32)] * 2,
            "fffbbbccccc")


def _tri_consts():
    pos = np.arange(CH)
    lower = (pos[:, None] >= pos[None, :]).astype(np.float32)
    ops = np.stack([lower, lower.T])
    return jnp.asarray(np.tile(ops, (1, 1, PREFIX_TERMS)), BF16)


def _hg_masks():
    pos = np.arange(CH)
    same_blk = pos[:, None] // SUB == pos[None, :] // SUB
    same_grp = pos[:, None] // GRP == pos[None, :] // GRP
    return jnp.asarray(np.stack([same_blk, same_grp]).astype(np.float32))


def _hg_direction(q_ref, f_ref, v_ref, lb, cum_ref, mask_ref, st_ref, o_ref, reset, first, reverse):
    valid = _row_valid(first)
    sub_i =lax.broadcasted_iota(jnp.int32, (SUB, HEAD_D), 0)
    lane_j = lax.broadcasted_iota(jnp.int32, (SUB, HEAD_D), 1) & (SUB - 1)
    causal = (sub_i <= lane_j) if reverse else (sub_i >= lane_j)
    pick = [causal & (lane_j == j) for j in range(SUB)]
    cum_op = cum_ref[...]
    same_blk = mask_ref[0]
    same_grp = mask_ref[1]
    zero = jnp.zeros((SUB, HEAD_D), F32)
    n_blk = CH // SUB
    per = GRP // SUB
    n_grp = CH // GRP

    def edge(unit, idx):
        return unit * idx if reverse else unit * (idx + 1) - 1

    def split_product(b, qs, ks, bs, pieces):
        q_slabs, k_slabs = [], []
        for ref, q_active, k_active in pieces:
            qp, kp = [], []
            for i in range(n_blk):
                r = ref(i)
                qp.append(qs[i] * jnp.exp2(bs[i] - b[r:r + 1, :]) if q_active(i) else zero)
                kp.append(ks[i] * jnp.exp2(b[r:r + 1, :] - bs[i]) if k_active(i) else zero)
            q_slabs.append(jnp.concatenate(qp, axis=0))
            k_slabs.append(jnp.concatenate(kp, axis=0))
        return _dot_nt(jnp.concatenate(q_slabs, axis=1), jnp.concatenate(k_slabs, axis=1))

    level1 = []
    for c in (range(1, per) if reverse else range(per - 1)):
        level1.append((lambda i, c=c: edge(SUB, (i // per) * per + c),
                       (lambda i, c=c: i % per < c) if reverse else (lambda i, c=c: i % per > c),
                       lambda i, c=c: i % per == c))
    level2 = []
    for gc in (range(1, n_grp) if reverse else range(n_grp - 1)):
        level2.append((lambda i, gc=gc: edge(GRP, gc),
                       (lambda i, gc=gc: i // per < gc) if reverse else (lambda i, gc=gc: i // per > gc),
                       lambda i, gc=gc: i // per == gc))

    for h in range(N_HEADS):
        sl = slice(h * HEAD_D, (h + 1) * HEAD_D)
        lbh = lb[:, sl]
        q = _silu(q_ref[:, sl])
        f = lbh + (1.0 - lbh) * _sigmoid(f_ref[:, sl])
        k = jnp.where(valid, 1.0 - f, 0.0)
        v = v_ref[:, sl]
        b = _prefix_dot(cum_op, jnp.log(f)) * LOG2_E
        qs = [q[i * SUB:(i + 1) * SUB] for i in range(n_blk)]
        ks = [k[i * SUB:(i + 1) * SUB] for i in range(n_blk)]
        bs = [b[i * SUB:(i + 1) * SUB] for i in range(n_blk)]

        tiles = [qs[blk] * jnp.exp2(jnp.minimum(bs[blk] - bs[blk][j:j + 1, :], 0.0))
                 for blk in range(n_blk) for j in range(SUB)]
        pair = _dot_nt(jnp.concatenate(tiles, axis=0), k)
        rows = []
        for blk in range(n_blk):
            base = blk * SUB * SUB
            a_blk = jnp.where(pick[0], pair[base:base + SUB], 0.0)
            for j in range(1, SUB):
                a_blk = a_blk + jnp.where(pick[j], pair[base + j * SUB:base + (j + 1) * SUB], 0.0)
            rows.append(a_blk)
        scores = jnp.concatenate(rows, axis=0) * same_blk
        scores = scores + split_product(b, qs, ks, bs, level1) * same_grp
        scores = scores + split_product(b, qs, ks, bs, level2)

        state = jnp.where(reset, 0.0, st_ref[h])
        o_ref[:, sl] = (_dot(scores, v) + _dot_nt(q * jnp.exp2(b), state)).astype(o_ref.dtype)
        b_tot = b[0:1, :] if reverse else b[CH - 1:CH, :]
        st_ref[h] = state * jnp.exp2(b_tot) + _dot_tn(v, k * jnp.exp2(b_tot - b))


def _hg_lower_bound(lower_ref, layer):
    low = lower_ref[...]
    e = jnp.exp(low - jnp.max(low, axis=0, keepdims=True))
    soft = e / jnp.sum(e, axis=0, keepdims=True)
    lb = jnp.zeros((1, GROUP_W), F32)
    for l in range(1, layer + 1):
        lb = lb + soft[l:l + 1, :]
    return lb


def _hg_kernel(layer, cpos_ref, clast_ref, g, gb, qf, ff, vf, qb, fb, vb, lower_ref, cum_ref, mask_ref,
               of_ref, ob_ref, sf_ref, sb_ref):
    lb =_hg_lower_bound(lower_ref, layer)
    _hg_direction(qf, ff, vf, lb, cum_ref.at[0], mask_ref, sf_ref, of_ref,
                  cpos_ref[g] == 0, cpos_ref[g] == 0, False)
    _hg_direction(qb, fb, vb, lb, cum_ref.at[1], mask_ref, sb_ref, ob_ref,
                  clast_ref[gb] == 1, cpos_ref[gb] == 0, True)


def _hgrn2(z, lower, layer, cps):
    t = z.shape[0]
    n = t // (CH * cps)
    blk = (CH * cps, GROUP_W)
    cum = _tri_consts()
    masks = _hg_masks()
    full = lambda a: pl.BlockSpec(a.shape, lambda g, cp, cl: (0,) * a.ndim)
    in_specs = [pl.BlockSpec(blk, _fwd_map(0)), pl.BlockSpec(blk, _fwd_map(1)), pl.BlockSpec(blk, _fwd_map(3)),
                pl.BlockSpec(blk, _bwd_map(n, 0)), pl.BlockSpec(blk, _bwd_map(n, 2)),
                pl.BlockSpec(blk, _bwd_map(n, 3)), full(lower), full(cum), full(masks)]
    return (functools.partial(_hg_kernel, layer), in_specs, [z, z, z, z, z, z, lower, cum, masks],
            [pltpu.VMEM((N_HEADS, HEAD_D, HEAD_D), F32)] * 2, "fffbbbccc")


def _log_sigmoid(x):
    return jnp.minimum(x, 0.0) - jnp.log(1.0 + jnp.exp(-jnp.abs(x)))


def _ml_direction(q_ref, k_ref, v_ref, gi_ref, gf_ref, bias_ref, cum_op, spread_ref, s_ref, m_ref, o_ref,
                  reset, first, d, reverse):
    valid = _row_valid(first)
    cum = _prefix_dot(cum_op, _log_sigmoid(gf_ref[...] + bias_ref[1]))
    a = jnp.where(valid, gi_ref[...] + bias_ref[0] - cum, NEG_BIG)
    run = a
    s = 1
    while s < CH:
        run = jnp.maximum(run, _shift_rows(run, s, NEG_BIG, reverse))
        s *= 2
    m_st = jnp.where(reset, NEG_BIG, m_ref[...])
    mx = jnp.maximum(m_st, run)
    edge = 0 if reverse else CH - 1
    mx_last = mx[edge:edge + 1, :]
    m_ref[...] = cum[edge:edge + 1, :] + mx_last
    decay = jnp.exp(m_st - mx_last)

    stacked = jnp.concatenate([mx, m_st - mx, -(cum + mx), a - mx_last], axis=0) * LOG2_E
    spread_b = _spread_dot(stacked, spread_ref[...])
    mx_b, inter_b, floor_b, end_b = (spread_b[i * CH:(i + 1) * CH] for i in range(4))
    a_t = (a * LOG2_E).T

    row = lax.broadcasted_iota(jnp.int32, (CH, CH), 0)
    col = lax.broadcasted_iota(jnp.int32, (CH, CH), 1)
    causal = (col >= row) if reverse else (col <= row)
    ones = jnp.ones((CH, HEAD_D), F32)
    for h in range(N_HEADS):
        sl = slice(h * HEAD_D, (h + 1) * HEAD_D)
        x = d * N_HEADS + h
        q = q_ref[:, sl]
        k = jnp.where(valid, k_ref[:, sl] * (HEAD_D ** -0.5), 0.0)
        v_ext = jnp.concatenate([v_ref[:, sl], ones], axis=1)
        w = jnp.where(causal, jnp.exp2(jnp.minimum(a_t[x:x + 1, :] - mx_b[:, sl], 0.0)), 0.0)
        qk = _dot_nt(q, k) * w
        s_inter = jnp.exp2(jnp.minimum(inter_b[:, sl], 0.0))
        state = jnp.where(reset, 0.0, s_ref[h])
        ext = _dot(jnp.concatenate([qk, q * s_inter], axis=1), jnp.concatenate([v_ext, state], axis=0))
        o_ref[:, sl] = (ext[:, :HEAD_D] / jnp.maximum(jnp.abs(ext[:, HEAD_D:]), jnp.exp2(floor_b[:, sl]))
                        ).astype(o_ref.dtype)
        kw = k * jnp.exp2(jnp.minimum(end_b[:, sl], 0.0))
        s_ref[h] = decay[:, x:x + 1] * state + _dot_tn(kw, v_ext)


def _ml_kernel(cpos_ref, clast_ref, g, gb, qf, kf, vf, gif, gff, qb, kb, vb, gib, gfb, bias_ref, cum_ref,
               spread_ref, of_ref, ob_ref, sfs, mfs, sbs, mbs):
    _ml_direction(qf, kf, vf, gif, gff, bias_ref, cum_ref[0], spread_ref.at[0], sfs, mfs, of_ref,
                  cpos_ref[g] == 0, cpos_ref[g] == 0, 0, False)
    _ml_direction(qb, kb, vb, gib, gfb, bias_ref, cum_ref[1], spread_ref.at[1], sbs, mbs, ob_ref,
                  clast_ref[gb] == 1, cpos_ref[gb] == 0, 1, True)


def _ml_spread():
    spread = np.zeros((2, LANES, GROUP_W), np.float32)
    for d in range(2):
        for h in range(N_HEADS):
            spread[d, d * N_HEADS + h, h * HEAD_D:(h + 1) * HEAD_D] = 1.0
    return jnp.asarray(np.tile(spread, (1, SPREAD_TERMS, 1)), BF16)


def _mlstm(z, gates, ml_bi, ml_bf, cps):
    t = z.shape[0]
    n = t // (CH * cps)
    blk = (CH * cps, GROUP_W)
    gblk = (CH * cps, LANES)
    cum = _tri_consts()
    spread = _ml_spread()
    pad = jnp.zeros((LANES - 2 * N_HEADS,), F32)
    bias = jnp.stack([jnp.concatenate([ml_bi.reshape(-1), pad]),
                      jnp.concatenate([ml_bf.reshape(-1), pad])]).reshape(2, 1, LANES)
    full = lambda a: pl.BlockSpec(a.shape, lambda g, cp, cl: (0,) * a.ndim)
    in_specs = [pl.BlockSpec(blk, _fwd_map(5)), pl.BlockSpec(blk, _fwd_map(6)), pl.BlockSpec(blk, _fwd_map(7)),
                pl.BlockSpec(gblk, _fwd_map(0)), pl.BlockSpec(gblk, _fwd_map(1)),
                pl.BlockSpec(blk, _bwd_map(n, 5)), pl.BlockSpec(blk, _bwd_map(n, 6)),
                pl.BlockSpec(blk, _bwd_map(n, 7)),
                pl.BlockSpec(gblk, _bwd_map(n, 0)), pl.BlockSpec(gblk, _bwd_map(n, 1)),
                full(bias), full(cum), full(spread)]
    state = [pltpu.VMEM((N_HEADS, HEAD_D, 2 * HEAD_D), F32), pltpu.VMEM((1, LANES), F32)]
    return (_ml_kernel, in_specs, [z, z, z, gates, gates, z, z, z, gates, gates, bias, cum, spread], state * 2,
            "fffffbbbbbccc")


def _head_norm(x, center):
    outs = []
    for h in range(N_HEADS):
        xh = x[:, h * HEAD_D:(h + 1) * HEAD_D]
        if center:
            xh = xh - jnp.mean(xh, axis=1, keepdims=True)
        outs.append(xh * lax.rsqrt(jnp.mean(xh * xh, axis=1, keepdims=True) + EPS))
    return jnp.concatenate(outs, axis=1)


def _layer_norm(x, g, b):
    xc = x - jnp.mean(x, axis=1, keepdims=True)
    return xc * lax.rsqrt(jnp.mean(xc * xc, axis=1, keepdims=True) + EPS) * g + b


def _mix_rows(alpha, even, af, ab, ag, bf, bb, bg, h_ref, w_ref, lng_ref, lnb_ref):
    a = af[...].astype(F32) + ab[...].astype(F32)
    b = bf[...].astype(F32) + bb[...].astype(F32)
    if even:
        a = _head_norm(a, False) * _silu(ag[...])
        b = b * _gelu_tanh(bg[...])
    else:
        a = _head_norm(a, False) * _silu(ag[...])
        b = _head_norm(b, True) * _sigmoid(bg[...])
    mix = _dot(a, w_ref[0:GROUP_W, :]) + _dot(b, w_ref[GROUP_W:, :])
    return _layer_norm(alpha * h_ref[...] + mix, lng_ref[...], lnb_ref[...])


N_MIX_OPERANDS = 10


def _mix_specs(tm, col_a, col_b, index):
    half = lambda c: pl.BlockSpec((tm, GROUP_W), lambda *g: (index(*g), c))
    rows = pl.BlockSpec((tm, D_MODEL), lambda *g: (index(*g), 0))
    vec = pl.BlockSpec((1, D_MODEL), lambda *g: (0, 0))
    return [half(0), half(0), half(col_a), half(0), half(0), half(col_b), rows,
            pl.BlockSpec((D_MODEL, D_MODEL), lambda *g: (0, 0)), vec, vec]


def _mix_ffn_kernel(alpha, *refs):
    mix_refs = refs[:N_MIX_OPERANDS]
    wg_ref, wu_ref, wo_ref, lng_ref, lnb_ref, o_ref, h1_ref, acc_ref = refs[N_MIX_OPERANDS:]
    f = pl.program_id(1)

    @pl.when(f == 0)
    def _():
        h1_ref[...] = _mix_rows(alpha, True, *mix_refs)
        acc_ref[...] = jnp.zeros_like(acc_ref)

    x = h1_ref[...].astype(BF16)
    act = _silu(_dot(x, wg_ref[...])) * _dot(x, wu_ref[...])
    acc_ref[...] += _dot(act, wo_ref[...])

    @pl.when(f == pl.num_programs(1) - 1)
    def _():
        o_ref[...] = _layer_norm(alpha * h1_ref[...] + acc_ref[...], lng_ref[...], lnb_ref[...])


def _mix_ffn(alpha, af, ab, bf, bb, z, col_a, col_b, h, w_out, ln1_g, ln1_b, wi, wo, ln2_g, ln2_b):
    t = h.shape[0]
    d_ff = wo.shape[0]
    tm = _pick_tile(t, 640)
    tf = _pick_tile(d_ff, 1536)
    nf = d_ff // tf
    rows = pl.BlockSpec((tm, D_MODEL), lambda i, f: (i, 0))
    vec = pl.BlockSpec((1, D_MODEL), lambda i, f: (0, 0))
    return pl.pallas_call(
        functools.partial(_mix_ffn_kernel, alpha), grid=(t // tm, nf),
        in_specs=_mix_specs(tm, col_a, col_b, lambda i, f: i) + [
            pl.BlockSpec((D_MODEL, tf), lambda i, f: (0, f)),
            pl.BlockSpec((D_MODEL, tf), lambda i, f: (0, f + nf)),
            pl.BlockSpec((tf, D_MODEL), lambda i, f: (f, 0)), vec, vec],
        out_specs=rows, out_shape=jax.ShapeDtypeStruct((t, D_MODEL), F32),
        scratch_shapes=[pltpu.VMEM((tm, D_MODEL), F32), pltpu.VMEM((tm, D_MODEL), F32)],
        compiler_params=_params(2), name="mix_ffn")(
            af, ab, z, bf, bb, z, h, w_out, ln1_g.reshape(1, -1), ln1_b.reshape(1, -1),
            wi, wi, wo, ln2_g.reshape(1, -1), ln2_b.reshape(1, -1))


EXPERT_TILE = 1024
DMA_UNROLL = 8
DMA_PRIORITIES = 2


def _mix_route_kernel(alpha, pad_starts, *refs):
    mix_refs = refs[:N_MIX_OPERANDS]
    router_ref, tri_ref, o_ref, meta_ref, gate_ref, cnt_ref, carry_ref = refs[N_MIX_OPERANDS:]

    @pl.when(pl.program_id(0) == 0)
    def _():
        carry_ref[...] = jnp.zeros_like(carry_ref)

    h1 = _mix_rows(alpha, False, *mix_refs)
    o_ref[...] = h1
    tm = h1.shape[0]
    row = pl.program_id(0) * tm + lax.broadcasted_iota(jnp.int32, (tm, 1), 0)
    is_pad = row < 0
    for s in pad_starts:
        is_pad = is_pad | ((row >= s) & (row < s + PAD_ROWS))
    is_token = jnp.logical_not(is_pad)

    logits = _dot_16bit(h1, router_ref[...])
    lane = lax.broadcasted_iota(jnp.int32, logits.shape, 1)
    logits = jnp.where(lane < N_EXP, logits, -jnp.inf)
    top1 = jnp.max(logits, axis=1, keepdims=True)
    idx1 = jnp.min(jnp.where(logits == top1, lane, LANES), axis=1, keepdims=True)
    rest = jnp.where(lane == idx1, -jnp.inf, logits)
    top2 = jnp.max(rest, axis=1, keepdims=True)
    idx2 = jnp.min(jnp.where(rest == top2, lane, LANES), axis=1, keepdims=True)
    g2 = jnp.exp(top2 - top1)
    denom = 1.0 + g2
    hit1 = lane == idx1
    hit2 = lane == idx2
    both = jnp.where(is_token, jnp.where(hit1, 1.0, 0.0) + jnp.where(hit2, 1.0, 0.0), 0.0)
    prefix = _dot(tri_ref[...], both) + carry_ref[...]
    rank1 = jnp.sum(jnp.where(hit1, prefix, 0.0), axis=1, keepdims=True).astype(jnp.int32)
    rank2 = jnp.sum(jnp.where(hit2, prefix, 0.0), axis=1, keepdims=True).astype(jnp.int32)
    carry_ref[...] += jnp.sum(both, axis=0, keepdims=True)
    cnt_ref[...] = carry_ref[...]
    meta_ref[...] = jnp.where(lane == 0, idx1, jnp.where(lane == 1, idx2, jnp.where(
        lane == 2, rank1, jnp.where(lane == 3, rank2, jnp.where(is_token & (lane == 4), 1, 0)))))
    gate_ref[...] = jnp.where(lane == 0, 1.0 / denom, jnp.where(lane == 1, g2 / denom, 0.0))


def _mix_route(alpha, af, ab, bf, bb, z, col_a, col_b, h, w_out, ln_g, ln_b, router, pad_starts):
    t = h.shape[0]
    tm = _pick_tile(t, 512)
    router_p = _stack_weight_terms(jnp.pad(router, ((0, 0), (0, LANES - router.shape[1]))))
    pos = np.arange(tm)
    tri = jnp.asarray(pos[:, None] > pos[None, :], BF16)
    rows = lambda w: pl.BlockSpec((tm, w), lambda i: (i, 0))
    const = lambda a: pl.BlockSpec(a.shape, lambda i: (0, 0))
    return pl.pallas_call(
        functools.partial(_mix_route_kernel, alpha, pad_starts), grid=(t // tm,),
        in_specs=_mix_specs(tm, col_a, col_b, lambda i: i) + [const(router_p), const(tri)],
        out_specs=[rows(D_MODEL), rows(LANES), rows(LANES), pl.BlockSpec((1, LANES), lambda i: (0, 0))],
        out_shape=[jax.ShapeDtypeStruct((t, D_MODEL), F32), jax.ShapeDtypeStruct((t, LANES), jnp.int32),
                   jax.ShapeDtypeStruct((t, LANES), F32), jax.ShapeDtypeStruct((1, LANES), F32)],
        scratch_shapes=[pltpu.VMEM((1, LANES), F32)],
        compiler_params=_params(1), name="mix_route")(
            af, ab, z, bf, bb, z, h, w_out, ln_g.reshape(1, -1), ln_b.reshape(1, -1), router_p, tri)


def _dispatch_plan(meta, counts, t):
    cnt = counts[0, :N_EXP].astype(jnp.int32)
    padded = ((cnt + EXPERT_TILE - 1) // EXPERT_TILE) * EXPERT_TILE
    ends = jnp.cumsum(padded)
    off = ends - padded

    def base(e):
        return sum(jnp.where(e == i, off[i], 0) for i in range(N_EXP))

    is_token = meta[:, 4] > 0
    n_tiles = -(-2 * t // EXPERT_TILE) + N_EXP
    spare = n_tiles * EXPERT_TILE + 2 * (jnp.cumsum(jnp.logical_not(is_token).astype(jnp.int32)) - 1)
    pos1 = (base(meta[:, 0]) + meta[:, 2]).astype(jnp.int32)
    pos2 = (base(meta[:, 1]) + meta[:, 3]).astype(jnp.int32)
    scatter = (jnp.where(is_token, pos1, spare), jnp.where(is_token, pos2, spare + 1))
    gather = (jnp.where(is_token, pos1, 0), jnp.where(is_token, pos2, 0))
    starts = jnp.arange(n_tiles, dtype=jnp.int32) * EXPERT_TILE
    tile_expert = jnp.minimum(jnp.sum(starts[:, None] >= ends[None, :], axis=1), N_EXP - 1).astype(jnp.int32)
    n_active = (ends[-1] // EXPERT_TILE).astype(jnp.int32).reshape(1)
    return ends.astype(jnp.int32), scatter, gather, tile_expert, n_active, n_tiles


assert D_MODEL == SUBLANES * LANES


def _store_token_tiles(ref, x):
    n = x.shape[0]
    for s in range(SUBLANES):
        ref[pl.ds(s, n, stride=SUBLANES), :] = x[:, s * LANES:(s + 1) * LANES]


def _load_token_tiles(ref, n):
    return jnp.concatenate([ref[pl.ds(s, n, stride=SUBLANES), :] for s in range(SUBLANES)], axis=1)


def _tile_rows(i):
    return pl.ds(pl.multiple_of(i * SUBLANES, SUBLANES), SUBLANES)


def _dispatch_kernel(n_tiles, ends_ref, pos1_ref, pos2_ref, h_ref, xs_ref, tok_ref, zero_ref, sem, zero_sem):
    n = h_ref.shape[0]

    @pl.when(pl.program_id(0) == 0)
    def _():
        zero_ref[...] = jnp.zeros_like(zero_ref)

        def clear_last_tile(e):
            first_row = pl.multiple_of((ends_ref[e] - EXPERT_TILE) * SUBLANES, SUBLANES)
            return pltpu.make_async_copy(zero_ref, xs_ref.at[pl.ds(first_row, EXPERT_TILE * SUBLANES)], zero_sem)

        def has_tiles(e):
            return ends_ref[e] > (ends_ref[e - 1] if e else 0)

        for e in range(N_EXP):
            pl.when(has_tiles(e))(lambda e=e: clear_last_tile(e).start())
        for e in range(N_EXP):
            pl.when(has_tiles(e))(lambda e=e: clear_last_tile(e).wait())

        def clear_tile(i):
            first_row = pl.multiple_of(i * (EXPERT_TILE * SUBLANES), SUBLANES)
            return pltpu.make_async_copy(zero_ref, xs_ref.at[pl.ds(first_row, EXPERT_TILE * SUBLANES)], zero_sem)

        used = ends_ref[N_EXP - 1] // EXPERT_TILE
        lax.fori_loop(used, n_tiles, lambda i, c: (clear_tile(i).start(), c)[1], 0)
        lax.fori_loop(used, n_tiles, lambda i, c: (clear_tile(i).wait(), c)[1], 0)

    g = pl.program_id(0)
    slot = g % 2
    _store_token_tiles(tok_ref.at[slot], h_ref[...])

    def tile_copy(r, p):
        return pltpu.make_async_copy(tok_ref.at[slot].at[_tile_rows(r)], xs_ref.at[_tile_rows(p)], sem.at[slot])

    def start(pair, c):
        for prio in range(DMA_PRIORITIES):
            r = pair * DMA_PRIORITIES + prio
            tile_copy(r, pos1_ref[0, 0, r]).start(priority=prio)
            tile_copy(r, pos2_ref[0, 0, r]).start(priority=prio)
        return c

    lax.fori_loop(0, n // DMA_PRIORITIES, start, 0, unroll=DMA_UNROLL // DMA_PRIORITIES)

    def wait_slot(s):
        all_rows = pltpu.make_async_copy(tok_ref.at[s], xs_ref.at[pl.ds(0, n * SUBLANES)], sem.at[s])
        all_rows.wait()
        all_rows.wait()

    pl.when(g > 0)(lambda: wait_slot(1 - slot))
    pl.when(g == pl.num_programs(0) - 1)(lambda: wait_slot(slot))


def _dispatch(h, ends, pos1, pos2, n_tiles, n_spare):
    t = h.shape[0]
    n_rows = n_tiles * EXPERT_TILE + n_spare
    tm = _pick_tile(t, 512)
    idx = lambda: pl.BlockSpec((1, 1, tm), lambda i, ends: (i, 0, 0), memory_space=pltpu.SMEM)
    grid_spec = pltpu.PrefetchScalarGridSpec(
        num_scalar_prefetch=1, grid=(t // tm,),
        in_specs=[idx(), idx(), pl.BlockSpec((tm, D_MODEL), lambda i, ends: (i, 0))],
        out_specs=pl.BlockSpec(memory_space=pl.ANY),
        scratch_shapes=[pltpu.VMEM((2, tm * SUBLANES, LANES), F32),
                        pltpu.VMEM((EXPERT_TILE * SUBLANES, LANES), F32),
                        pltpu.SemaphoreType.DMA((2,)), pltpu.SemaphoreType.DMA(())])
    return pl.pallas_call(
        functools.partial(_dispatch_kernel, n_tiles), grid_spec=grid_spec,
        out_shape=jax.ShapeDtypeStruct((n_rows * SUBLANES, LANES), F32),
        compiler_params=_params(1), name="dispatch")(
            ends, pos1.reshape(t // tm, 1, tm), pos2.reshape(t // tm, 1, tm), h)


def _experts_kernel(te_ref, na_ref, x_ref, wg_ref, wu_ref, wo_ref, y_ref, xb_ref, acc_ref):
    i = pl.program_id(0)
    f = pl.program_id(1)
    last = f == pl.num_programs(1) - 1
    active = i < na_ref[0]

    @pl.when(active)
    def _():
        @pl.when(f == 0)
        def _():
            acc_ref[...] = jnp.zeros_like(acc_ref)
            xb_ref[...] = _load_token_tiles(x_ref, EXPERT_TILE).astype(BF16)

        x = xb_ref[...]
        act = _silu(_dot(x, wg_ref[0])) * _dot(x, wu_ref[0])
        acc_ref[...] += _dot(act, wo_ref[0])

        @pl.when(last)
        def _():
            _store_token_tiles(y_ref, acc_ref[...])

    @pl.when(jnp.logical_not(active) & last)
    def _():
        y_ref[...] = jnp.zeros_like(y_ref)


def _experts(xs, tile_expert, n_active, wi, wo, n_tiles):
    n_rows = n_tiles * EXPERT_TILE
    e_ff = wo.shape[1]
    tf = _pick_tile(e_ff, 512)
    nf = e_ff // tf
    rows = pl.BlockSpec((EXPERT_TILE * SUBLANES, LANES), lambda i, f, te, na: (i, 0))
    rows_in = pl.BlockSpec((EXPERT_TILE * SUBLANES, LANES),
                           lambda i, f, te, na: (jnp.minimum(i, jnp.maximum(na[0] - 1, 0)), 0))
    grid_spec = pltpu.PrefetchScalarGridSpec(
        num_scalar_prefetch=2, grid=(n_rows // EXPERT_TILE, nf),
        in_specs=[rows_in, pl.BlockSpec((1, D_MODEL, tf), lambda i, f, te, na: (te[i], 0, f)),
                  pl.BlockSpec((1, D_MODEL, tf), lambda i, f, te, na: (te[i], 0, f + nf)),
                  pl.BlockSpec((1, tf, D_MODEL), lambda i, f, te, na: (te[i], f, 0))],
        out_specs=rows,
        scratch_shapes=[pltpu.VMEM((EXPERT_TILE, D_MODEL), BF16), pltpu.VMEM((EXPERT_TILE, D_MODEL), F32)])
    return pl.pallas_call(
        _experts_kernel, grid_spec=grid_spec, out_shape=jax.ShapeDtypeStruct((n_rows * SUBLANES, LANES), F32),
        compiler_params=_params(2), name="experts")(tile_expert, n_active, xs, wi, wi, wo)


def _gather_start(n, pos1_ref, pos2_ref, ys_ref, a_ref, b_ref, sems):
    def tile_copy(p, buf, r, which):
        return pltpu.make_async_copy(ys_ref.at[_tile_rows(p)], buf.at[_tile_rows(r)], sems.at[which])

    def start(pair, c):
        for prio in range(DMA_PRIORITIES):
            r = pair * DMA_PRIORITIES + prio
            tile_copy(pos1_ref[0, 0, r], a_ref, r, 0).start(priority=prio)
            tile_copy(pos2_ref[0, 0, r], b_ref, r, 1).start(priority=prio)
        return c

    lax.fori_loop(0, n // DMA_PRIORITIES, start, 0, unroll=DMA_UNROLL // DMA_PRIORITIES)


def _gather_finish(alpha, h_ref, gate_ref, ys_ref, lng_ref, lnb_ref, a_ref, b_ref, sems):
    n = h_ref.shape[0]
    for which, buf in enumerate((a_ref, b_ref)):
        pltpu.make_async_copy(ys_ref.at[pl.ds(0, n * SUBLANES)], buf, sems.at[which]).wait()
    gate = gate_ref[...]
    y = gate[:, 0:1] * _load_token_tiles(a_ref, n) + gate[:, 1:2] * _load_token_tiles(b_ref, n)
    return _layer_norm(alpha * h_ref[...] + y, lng_ref[...], lnb_ref[...])


def _combine_kernel(alpha, pos1_ref, pos2_ref, h_ref, gate_ref, ys_ref, lng_ref, lnb_ref, o_ref,
                    a_ref, b_ref, sems):
    _gather_start(h_ref.shape[0], pos1_ref, pos2_ref, ys_ref, a_ref, b_ref, sems)
    o_ref[...] = _gather_finish(alpha, h_ref, gate_ref, ys_ref, lng_ref, lnb_ref, a_ref, b_ref, sems)


def _combine_scratch(tm):
    return [pltpu.VMEM((tm * SUBLANES, LANES), F32), pltpu.VMEM((tm * SUBLANES, LANES), F32),
            pltpu.SemaphoreType.DMA((2,))]


def _combine(alpha, h, gates, ys, pos1, pos2, ln_g, ln_b):
    t = h.shape[0]
    tm = _pick_tile(t, 512)
    idx = lambda: pl.BlockSpec((1, 1, tm), lambda i: (i, 0, 0), memory_space=pltpu.SMEM)
    rows = pl.BlockSpec((tm, D_MODEL), lambda i: (i, 0))
    vec = pl.BlockSpec((1, D_MODEL), lambda i: (0, 0))
    return pl.pallas_call(
        functools.partial(_combine_kernel, alpha), grid=(t // tm,),
        in_specs=[idx(), idx(), rows, pl.BlockSpec((tm, LANES), lambda i: (i, 0)),
                  pl.BlockSpec(memory_space=pl.ANY), vec, vec],
        out_specs=rows, out_shape=jax.ShapeDtypeStruct((t, D_MODEL), F32),
        scratch_shapes=_combine_scratch(tm),
        compiler_params=_params(1), name="combine")(
            pos1.reshape(t // tm, 1, tm), pos2.reshape(t // tm, 1, tm), h, gates, ys,
            ln_g.reshape(1, -1), ln_b.reshape(1, -1))


def _output_tables(seq_shapes):
    kind, live = [], []
    window = [[], []]
    recent = [(0, 0), (0, 0)]
    for grp, (b, l) in enumerate(seq_shapes):
        for i in range(b):
            for c in range(l // CH + 1):
                kind.append(grp)
                live.append(int(c > 0))
                recent[grp] = (i, max(c - 1, 0))
                for g2 in range(2):
                    window[g2].append(recent[g2])
    tables = [kind, live] + [[w[j] for w in window[g2]] for g2 in range(2) for j in range(2)]
    return [jnp.asarray(np.asarray(x, np.int32)) for x in tables]


def _combine_out_kernel(alpha, kind_ref, live_ref, ab_ref, ac_ref, bb_ref, bc_ref,
                        pos1_ref, pos2_ref, nxt1_ref, nxt2_ref, h_ref, gate_ref, ys_ref, lng_ref, lnb_ref,
                        outa_ref, outb_ref, a_ref, b_ref, sems):
    g = pl.program_id(0)
    last = pl.num_programs(0) - 1
    slot = g % 2
    nxt = jnp.minimum(g + 1, last)

    @pl.when((g == 0) & (live_ref[0] == 1))
    def _():
        _gather_start(CH, pos1_ref, pos2_ref, ys_ref, a_ref.at[0], b_ref.at[0], sems.at[0])

    @pl.when((g < last) & (live_ref[nxt] == 1))
    def _():
        _gather_start(CH, nxt1_ref, nxt2_ref, ys_ref, a_ref.at[1 - slot], b_ref.at[1 - slot], sems.at[1 - slot])

    @pl.when(live_ref[g] == 1)
    def _():
        res = _gather_finish(alpha, h_ref, gate_ref, ys_ref, lng_ref, lnb_ref,
                             a_ref.at[slot], b_ref.at[slot], sems.at[slot])
        for grp, out_ref in enumerate((outa_ref, outb_ref)):
            @pl.when(kind_ref[g] == grp)
            def _(out_ref=out_ref):
                out_ref[0] = res


def _combine_out(alpha, h, gates, ys, pos1, pos2, ln_g, ln_b, seq_shapes):
    t = h.shape[0]
    n = t // CH
    tables = _output_tables(seq_shapes)
    idx = lambda: pl.BlockSpec((1, 1, CH), lambda g, *tb: (g, 0, 0), memory_space=pltpu.SMEM)
    idx_next = lambda: pl.BlockSpec((1, 1, CH), lambda g, *tb: (jnp.minimum(g + 1, n - 1), 0, 0),
                                    memory_space=pltpu.SMEM)
    rows = lambda w: pl.BlockSpec((CH, w), lambda g, *tb: (g, 0))
    vec = pl.BlockSpec((1, D_MODEL), lambda g, *tb: (0, 0))
    out = lambda grp: pl.BlockSpec((1, CH, D_MODEL),
                                   lambda g, *tb: (tb[2 + 2 * grp][g], tb[3 + 2 * grp][g], 0))
    fetched = pltpu.VMEM((2, CH * SUBLANES, LANES), F32)
    grid_spec = pltpu.PrefetchScalarGridSpec(
        num_scalar_prefetch=len(tables), grid=(n,),
        in_specs=[idx(), idx(), idx_next(), idx_next(), rows(D_MODEL), rows(LANES),
                  pl.BlockSpec(memory_space=pl.ANY), vec, vec],
        out_specs=[out(0), out(1)], scratch_shapes=[fetched, fetched, pltpu.SemaphoreType.DMA((2, 2))])
    pos1, pos2 = pos1.reshape(n, 1, CH), pos2.reshape(n, 1, CH)
    return pl.pallas_call(
        functools.partial(_combine_out_kernel, alpha), grid_spec=grid_spec,
        out_shape=[jax.ShapeDtypeStruct((b, l, D_MODEL), F32) for b, l in seq_shapes],
        compiler_params=_params(1), name="combine_out")(
            *tables, pos1, pos2, pos1, pos2, h, gates, ys, ln_g.reshape(1, -1), ln_b.reshape(1, -1))


def _moe(alpha, h, meta, gates, counts, wi, wo, ln_g, ln_b, n_pad_rows, out_shapes=None):
    t = h.shape[0]
    ends, scatter, gather, tile_expert, n_active, n_tiles = _dispatch_plan(meta, counts, t)
    n_spare = 2 * n_pad_rows
    xs = _dispatch(h, ends, scatter[0], scatter[1], n_tiles, n_spare)
    ys = _experts(xs, tile_expert, n_active, wi, wo, n_tiles)
    if out_shapes is None:
        return _combine(alpha, h, gates, ys, gather[0], gather[1], ln_g, ln_b)
    return _combine_out(alpha, h, gates, ys, gather[0], gather[1], ln_g, ln_b, out_shapes)


def kernel(x_prompt, x_sample, meta, e_w_in, e_conv_w, e_conv_b, e_lru_wa, e_lru_ba, e_lru_wx, e_lru_bx,
           e_lru_lambda, e_w_out, e_ffn_wi, e_ffn_wo, o_w_in, o_hg_lower, o_ml_bi, o_ml_bf, o_w_out, o_router,
           o_exp_wi, o_exp_wo, ln_g, ln_b):
    groups = (x_prompt, x_sample)
    depth = ln_g.shape[0]
    alpha = (2.0 * depth) ** 0.25
    seq_shapes = [(x.shape[0], x.shape[1]) for x in groups]
    for _, l in seq_shapes:
        assert l % CH == 0
    cpos_np, clast_np = _chunk_tables(seq_shapes)
    cpos, clast = jnp.asarray(cpos_np), jnp.asarray(clast_np)
    max_rows = max(l for _, l in seq_shapes) + CH
    pad_starts = tuple(int(i) * CH for i in np.flatnonzero(cpos_np == 0))

    head = jnp.concatenate([jnp.zeros((PAD_ROWS, D_MODEL), F32), meta.astype(F32)], axis=0)
    parts = []
    for x in groups:
        full = jnp.concatenate([jnp.broadcast_to(head[None], (x.shape[0], CH, D_MODEL)), x], axis=1)
        parts.append(full.reshape(-1, D_MODEL))
    h = jnp.concatenate(parts, axis=0)
    t = h.shape[0]

    n_odd_cols = 9 * GROUP_W
    for layer in range(depth):
        p = layer // 2
        if layer % 2 == 0:
            z = _project(h, e_w_in[p].astype(BF16))
            (ret_f, ret_b), (lru_f, lru_b) = _chunk_walk(
                [_retention(z, cpos, clast, max_rows),
                 _rglru(z, 4, cpos, clast, e_conv_w[p], e_conv_b[p], e_lru_wa[p], e_lru_ba[p],
                        e_lru_wx[p], e_lru_bx[p], e_lru_lambda[p])], cpos, clast, t, "even_mixers")
            h = _mix_ffn(alpha, ret_f, ret_b, lru_f, lru_b, z, 3, 5, h, e_w_out[p].astype(BF16),
                         ln_g[layer, 0], ln_b[layer, 0], e_ffn_wi[p].astype(BF16), e_ffn_wo[p].astype(BF16),
                         ln_g[layer, 1], ln_b[layer, 1])
        else:
            w_in = o_w_in[p]
            n_gate = 2 * N_HEADS
            gate_pad = ((0, 0), (0, LANES - n_gate))
            w_gates = jnp.concatenate([jnp.pad(w_in[:, n_odd_cols:n_odd_cols + n_gate], gate_pad),
                                       jnp.pad(w_in[:, n_odd_cols + n_gate:], gate_pad)], axis=1)
            z, gates = _project(h, w_in[:, :n_odd_cols].astype(BF16), w_gates)
            cps = ODD_CHUNKS_PER_STEP if (t // CH) % ODD_CHUNKS_PER_STEP == 0 else 1
            (hg_f, hg_b), (ml_f, ml_b) = _chunk_walk(
                [_hgrn2(z, o_hg_lower, layer, cps), _mlstm(z, gates, o_ml_bi[p], o_ml_bf[p], cps)],
                cpos, clast, t, "odd_mixers", cps)
            h, meta_r, gates_r, counts = _mix_route(
                alpha, hg_f, hg_b, ml_f, ml_b, z, 4, 8, h, o_w_out[p].astype(BF16),
                ln_g[layer, 0], ln_b[layer, 0], o_router[p], pad_starts)
            h = _moe(alpha, h, meta_r, gates_r, counts, o_exp_wi[p].astype(BF16), o_exp_wo[p].astype(BF16),
                     ln_g[layer, 1], ln_b[layer, 1], len(pad_starts) * PAD_ROWS,
                     seq_shapes if layer == depth - 1 else None)
            if layer == depth - 1:
                return tuple(h)

    outs = []
    row = 0
    for b, l in seq_shapes:
        n = b * (l + CH)
        outs.append(h[row:row + n].reshape(b, l + CH, D_MODEL)[:, CH:])
        row += n
    return tuple(outs)
```

```python
import functools
import math

import numpy as np
import jax
import jax.numpy as jnp
from jax import lax
from jax.experimental import pallas as pl
from jax.experimental.pallas import tpu as pltpu

F32 = jnp.float32
BF16 = jnp.bfloat16

D_MODEL = 1024
GROUP_W = D_MODEL // 2
N_HEADS = 4
HEAD_D = GROUP_W // N_HEADS
N_META = 16
ROPE_BASE = 10000.0
LRU_BLOCKS = 8
LRU_BW = GROUP_W // LRU_BLOCKS
LRU_C = 8.0
N_EXP = 8
EPS = 1e-5

LANES = 128
SUBLANES = 8
CH = 128
PAD_ROWS = CH - N_META
SUB = SUBLANES
GRP = 4 * SUB
ODD_CHUNKS_PER_STEP = 2
NEG_BIG = -1e30
LOG2_E = math.log2(math.e)
VMEM_LIMIT = 56 * 1024 * 1024


def _dot(a, b):
    return jnp.dot(a.astype(BF16), b.astype(BF16), preferred_element_type=F32)


def _dot_nt(a, b):
    return lax.dot_general(a.astype(BF16), b.astype(BF16), (((1,), (1,)), ((), ())),
                           preferred_element_type=F32)


def _dot_tn(a, b):
    return _dot(a.T, b)


def _bf16_terms(x, n_terms):
    terms = []
    for _ in range(n_terms):
        t = x.astype(BF16)
        terms.append(t)
        x = x - t.astype(F32)
    return terms


def _stack_weight_terms(w):
    hi, lo = _bf16_terms(w, 2)
    return jnp.concatenate([hi, lo, hi], axis=0)


def _dot_16bit(x, w_terms):
    hi, lo = _bf16_terms(x, 2)
    return jnp.dot(jnp.concatenate([hi, hi, lo], axis=1), w_terms, preferred_element_type=F32)


PREFIX_TERMS = 3
SPREAD_TERMS = 2


def _prefix_dot(op_tiled, x):
    return jnp.dot(op_tiled, jnp.concatenate(_bf16_terms(x, PREFIX_TERMS), axis=0),
                   preferred_element_type=F32)


def _spread_dot(x, op_tiled):
    return jnp.dot(jnp.concatenate(_bf16_terms(x, SPREAD_TERMS), axis=1), op_tiled,
                   preferred_element_type=F32)


def _sigmoid(x):
    return 1.0 / (1.0 + jnp.exp(-x))


def _silu(x):
    return x * _sigmoid(x)


def _gelu_tanh(x):
    return 0.5 * x * (1.0 + jnp.tanh(math.sqrt(2.0 / math.pi) * (x + 0.044715 * (x * x * x))))


def _softplus(x):
    return jnp.maximum(x, 0.0) + jnp.log(1.0 + jnp.exp(-jnp.abs(x)))


def _pick_tile(total, target):
    best = LANES
    for t in range(LANES, min(total, target) + 1, LANES):
        if total % t == 0:
            best = t
    return best


def _params(n_axes, sem="arbitrary"):
    return pltpu.CompilerParams(dimension_semantics=(sem,) * n_axes, vmem_limit_bytes=VMEM_LIMIT)


def _proj_kernel(x_ref, w_ref, o_ref):
    o_ref[...] = _dot(x_ref[...], w_ref[...])


def _proj_gates_kernel(x_ref, w_ref, wg_ref, o_ref, g_ref):
    g_ref[...] = _dot_16bit(x_ref[...], wg_ref[...])
    o_ref[...] = _dot(x_ref[...], w_ref[...])


PROJ_OUT_BLOCK_BYTES = 8 * 1024 * 1024


def _project(x, w, w_gates=None):
    t, k = x.shape
    n = w.shape[1]
    tm = _pick_tile(t, PROJ_OUT_BLOCK_BYTES // (4 * n))
    x_spec = pl.BlockSpec((tm, k), lambda i: (i, 0))
    w_spec = pl.BlockSpec((k, n), lambda i: (0, 0))
    o_spec = pl.BlockSpec((tm, n), lambda i: (i, 0))
    if w_gates is None:
        return pl.pallas_call(
            _proj_kernel, grid=(t // tm,), in_specs=[x_spec, w_spec], out_specs=o_spec,
            out_shape=jax.ShapeDtypeStruct((t, n), F32), compiler_params=_params(1),
            name="proj")(x, w)
    ng = w_gates.shape[1]
    w_gates = _stack_weight_terms(w_gates)
    return pl.pallas_call(
        _proj_gates_kernel, grid=(t // tm,),
        in_specs=[x_spec, w_spec, pl.BlockSpec(w_gates.shape, lambda i: (0, 0))],
        out_specs=[o_spec, pl.BlockSpec((tm, ng), lambda i: (i, 0))],
        out_shape=[jax.ShapeDtypeStruct((t, n), F32), jax.ShapeDtypeStruct((t, ng), F32)],
        compiler_params=_params(1), name="proj_gates")(x, w, w_gates)


def _chunk_tables(seq_shapes):
    cpos, clast = [], []
    for b, l in seq_shapes:
        n = l // CH + 1
        for _ in range(b):
            cpos += list(range(n))
            clast += [0] * (n - 1) + [1]
    return np.asarray(cpos, np.int32), np.asarray(clast, np.int32)


def _fwd_map(col):
    return lambda g, cp, cl: (g, col)


def _bwd_map(n_chunks, col):
    return lambda g, cp, cl: (n_chunks - 1 - g, col)


def _walk_kernel(bodies, tags, n_scratch, cps, cpos_ref, clast_ref, *refs):
    refs = list(refs)
    ins = [[refs.pop(0) for _ in t] for t in tags]
    outs = [[refs.pop(0) for _ in range(2)] for _ in bodies]
    scratch = [[refs.pop(0) for _ in range(k)] for k in n_scratch]

    @pl.when(pl.program_id(0) == 0)
    def _():
        for part in scratch:
            for r in part:
                r[...] = jnp.zeros_like(r)

    n_chunks = pl.num_programs(0) * cps
    for j in range(cps):
        g = pl.program_id(0) * cps + j
        gb = n_chunks - 1 - g
        rows = {"f": pl.ds(j * CH, CH), "b": pl.ds((cps - 1 - j) * CH, CH)}

        def view(ref, tag):
            return ref if tag == "c" or cps == 1 else ref.at[rows[tag]]

        for body, i, tg, o, s in zip(bodies, ins, tags, outs, scratch):
            body(cpos_ref, clast_ref, g, gb, *[view(r, t) for r, t in zip(i, tg)],
                 view(o[0], "f"), view(o[1], "b"), *s)


def _chunk_walk(parts, cpos, clast, t, name, cps=1):
    n = t // (CH * cps)
    blk = (CH * cps, GROUP_W)
    bodies = [p[0] for p in parts]
    grid_spec = pltpu.PrefetchScalarGridSpec(
        num_scalar_prefetch=2, grid=(n,), in_specs=[s for p in parts for s in p[1]],
        out_specs=[pl.BlockSpec(blk, _fwd_map(0)), pl.BlockSpec(blk, _bwd_map(n, 0))] * len(parts),
        scratch_shapes=[s for p in parts for s in p[3]])
    outs = pl.pallas_call(
        functools.partial(_walk_kernel, bodies, [p[4] for p in parts], [len(p[3]) for p in parts], cps),
        grid_spec=grid_spec, out_shape=[jax.ShapeDtypeStruct((t, GROUP_W), BF16)] * (2 * len(parts)),
        compiler_params=_params(1), name=name)(cpos, clast, *[a for p in parts for a in p[2]])
    return [outs[2 * i:2 * i + 2] for i in range(len(parts))]


def _row_valid(first):
    row = lax.broadcasted_iota(jnp.int32, (CH, 1), 0)
    return row >= jnp.where(first, PAD_ROWS, 0)


def _ret_log_gamma():
    return np.log1p(-np.exp2(-5.0 - np.arange(N_HEADS, dtype=np.float64)))


def _ret_consts():
    lg = _ret_log_gamma()
    pos = np.arange(CH, dtype=np.float64)
    rel = pos[:, None] - pos[None, :]
    dmat = np.zeros((2, N_HEADS, CH, CH), np.float64)
    rows = np.zeros((4, CH, GROUP_W), np.float64)
    for h in range(N_HEADS):
        dmat[0, h] = np.where(rel >= 0, np.exp(np.maximum(rel, 0.0) * lg[h]), 0.0)
        dmat[1, h] = np.where(rel < 0, np.exp(np.maximum(-rel, 0.0) * lg[h]), 0.0)
        sl = slice(h * HEAD_D, (h + 1) * HEAD_D)
        rows[0, :, sl] = np.exp((pos + 1.0) * lg[h])[:, None]
        rows[1, :, sl] = np.exp((CH - 1.0 - pos) * lg[h])[:, None]
        rows[2, :, sl] = np.exp((CH - pos) * lg[h])[:, None]
        rows[3, :, sl] = np.exp(pos * lg[h])[:, None]
    return jnp.asarray(dmat, F32), jnp.asarray(rows, F32)


def _ret_direction(q_ref, k_ref, v_ref, cos_ref, sin_ref, dmat_ref, rin_ref, rout_ref, s_ref, o_ref,
                   reset, first):
    valid = _row_valid(first)
    cos = cos_ref[...]
    sin = sin_ref[...]
    chunk_decay = np.exp(CH * _ret_log_gamma())
    for h in range(N_HEADS):
        sl = slice(h * HEAD_D, (h + 1) * HEAD_D)
        q = q_ref[:, sl]
        k = k_ref[:, sl]
        v = v_ref[:, sl]
        q = q * cos + pltpu.roll(q, HEAD_D // 2, 1) * sin
        k = (k * cos + pltpu.roll(k, HEAD_D // 2, 1) * sin) * (HEAD_D ** -0.5)
        k = jnp.where(valid, k, 0.0)
        scores = _dot_nt(q, k) * dmat_ref[h]
        state = jnp.where(reset, 0.0, s_ref[h])
        o_ref[:, sl] = _dot(jnp.concatenate([scores, q * rin_ref[:, sl]], axis=1),
                            jnp.concatenate([v, state], axis=0)).astype(o_ref.dtype)
        s_ref[h] = float(chunk_decay[h]) * state + _dot_tn(k * rout_ref[:, sl], v)


def _ret_kernel(cpos_ref, clast_ref, g, gb, qf, kf, vf, cosf, sinf, qb, kb, vb, cosb, sinb, dmat_ref, rows_ref,
                of_ref, ob_ref, sf_ref, sb_ref):
    _ret_direction(qf, kf, vf, cosf, sinf, dmat_ref.at[0], rows_ref.at[0], rows_ref.at[1], sf_ref, of_ref,
                   cpos_ref[g] == 0, cpos_ref[g] == 0)
    _ret_direction(qb, kb, vb, cosb, sinb, dmat_ref.at[1], rows_ref.at[2], rows_ref.at[3], sb_ref, ob_ref,
                   clast_ref[gb] == 1, cpos_ref[gb] == 0)


def _rope_tables(n_rows):
    inv = ROPE_BASE ** (-jnp.arange(0, HEAD_D, 2, dtype=jnp.float32) / HEAD_D)
    pos = jnp.maximum(jnp.arange(n_rows, dtype=jnp.float32) - PAD_ROWS, 0.0)
    ang = pos[:, None] * inv[None, :]
    cos, sin = jnp.cos(ang), jnp.sin(ang)
    return jnp.concatenate([cos, cos], axis=1), jnp.concatenate([-sin, sin], axis=1)


def _retention(z, max_rows):
    t = z.shape[0]
    n = t // CH
    cos2, sin2 = _rope_tables(max_rows)
    dmat, rows = _ret_consts()
    blk = (CH, GROUP_W)
    in_specs = []
    for mk, pm in ((_fwd_map, lambda g, cp, cl: (cp[g], 0)),
                   (functools.partial(_bwd_map, n), lambda g, cp, cl: (cp[n - 1 - g], 0))):
        in_specs += [pl.BlockSpec(blk, mk(0)), pl.BlockSpec(blk, mk(1)), pl.BlockSpec(blk, mk(2)),
                     pl.BlockSpec((CH, HEAD_D), pm), pl.BlockSpec((CH, HEAD_D), pm)]
    in_specs += [pl.BlockSpec(dmat.shape, lambda g, cp, cl: (0, 0, 0, 0)),
                 pl.BlockSpec(rows.shape, lambda g, cp, cl: (0, 0, 0))]
    return (_ret_kernel, in_specs, [z, z, z, cos2, sin2, z, z, z, cos2, sin2, dmat, rows],
            [pltpu.VMEM((N_HEADS, HEAD_D, HEAD_D), F32)] * 2, "fffffbbbbbcc")


HALO = SUBLANES


def _shift_rows(x, s, fill, reverse):
    row = lax.broadcasted_iota(jnp.int32, (CH, 1), 0)
    if reverse:
        return jnp.where(row < CH - s, pltpu.roll(x, CH - s, 0), fill)
    return jnp.where(row >= s, pltpu.roll(x, s, 0), fill)


def _lru_direction(x_ref, prev_ref, next_ref, convw_ref, convb_ref, wg_ref, bias_ref, lam_ref, ext_ref,
                   carry_ref, o_ref, reset, first, last, reverse):
    valid = _row_valid(first)
    ext_ref[HALO:HALO + CH, :] = jnp.where(valid, x_ref[...], 0.0)
    ext_ref[0:HALO, :] = jnp.where(first, 0.0, prev_ref[...])
    ext_ref[HALO + CH:, :] = jnp.where(last, 0.0, next_ref[...])
    xc = convb_ref[...] + ext_ref[HALO - 2:HALO - 2 + CH, :] * convw_ref[0:1, :]
    for tap in range(1, 4):
        xc = xc + ext_ref[HALO - 2 + tap:HALO - 2 + tap + CH, :] * convw_ref[tap:tap + 1, :]

    log_sig_lam = -_softplus(-lam_ref[...])
    parts_a, parts_u = [], []
    for grp in range(GROUP_W // LANES):
        sl = slice(grp * LANES, (grp + 1) * LANES)
        xg = xc[:, sl]
        pre = _dot(xg, wg_ref[grp])
        r = _sigmoid(pre[:, :LANES] + bias_ref[0:1, sl])
        i = _sigmoid(pre[:, LANES:] + bias_ref[1:2, sl])
        a = jnp.exp(LRU_C * log_sig_lam[:, sl] * r)
        u = jnp.sqrt(1.0 - a * a) * (i * xg)
        parts_a.append(a)
        parts_u.append(jnp.where(valid, u, 0.0))
    a = jnp.concatenate(parts_a, axis=1)
    u = jnp.concatenate(parts_u, axis=1)

    n_blk = CH // SUBLANES
    a = a.reshape(n_blk, SUBLANES, GROUP_W)
    u = u.reshape(n_blk, SUBLANES, GROUP_W)
    row_in_blk = lax.broadcasted_iota(jnp.int32, (1, SUBLANES, 1), 1)
    s = 1
    while s < SUBLANES:
        keep = (row_in_blk < SUBLANES - s) if reverse else (row_in_blk >= s)
        shift = SUBLANES - s if reverse else s
        u = u + a * jnp.where(keep, pltpu.roll(u, shift, 1), 0.0)
        a = a * jnp.where(keep, pltpu.roll(a, shift, 1), 1.0)
        s *= 2
    edge = 0 if reverse else SUBLANES - 1
    carry = jnp.where(reset, 0.0, carry_ref[...])
    blocks = [None] * n_blk
    for blk in (reversed(range(n_blk)) if reverse else range(n_blk)):
        blocks[blk] = u[blk] + a[blk] * carry
        carry = blocks[blk][edge:edge + 1, :]
    o_ref[...] = jnp.concatenate(blocks, axis=0).astype(o_ref.dtype)
    carry_ref[...] = carry


def _lru_kernel(cpos_ref, clast_ref, g, gb, xf, pf, nf, xb, pb, nb, convw_ref, convb_ref, wg_ref, bias_ref,
                lam_ref, of_ref, ob_ref, extf_ref, extb_ref, cf_ref, cb_ref):
    _lru_direction(xf, pf, nf, convw_ref, convb_ref, wg_ref.at[0], bias_ref.at[0], lam_ref.at[0], extf_ref,
                   cf_ref, of_ref, cpos_ref[g] == 0, cpos_ref[g] == 0, clast_ref[g] == 1, False)
    _lru_direction(xb, pb, nb, convw_ref, convb_ref, wg_ref.at[1], bias_ref.at[1], lam_ref.at[1], extb_ref,
                   cb_ref, ob_ref, clast_ref[gb] == 1, cpos_ref[gb] == 0, clast_ref[gb] == 1, True)


def _lru_gate_weights(wa, wx):
    per = LANES // LRU_BW

    def block_diag(w):
        w = w.reshape(2, LRU_BLOCKS // per, per, LRU_BW, LRU_BW)
        eye = jnp.eye(per, dtype=w.dtype)
        return jnp.einsum("dgpij,pq->dgpiqj", w, eye).reshape(2, LRU_BLOCKS // per, LANES, LANES)

    return jnp.concatenate([block_diag(wa), block_diag(wx)], axis=-1).astype(BF16)


def _rglru(z, col, conv_w, conv_b, wa, ba, wx, bx, lam):
    t = z.shape[0]
    n = t // CH
    per = CH // HALO
    n_halo = t // HALO
    blk = (CH, GROUP_W)
    hblk = (HALO, GROUP_W)
    wg = _lru_gate_weights(wa, wx)
    bias = jnp.stack([ba, bx], axis=1)
    lam = lam.reshape(2, 1, GROUP_W)

    def prev_f(g, cp, cl):
        return (jnp.maximum(g * per - 1, 0), col)

    def next_f(g, cp, cl):
        return (jnp.minimum((g + 1) * per, n_halo - 1), col)

    def prev_b(g, cp, cl):
        return (jnp.maximum((n - 1 - g) * per - 1, 0), col)

    def next_b(g, cp, cl):
        return (jnp.minimum((n - g) * per, n_halo - 1), col)

    full = lambda a: pl.BlockSpec(a.shape, lambda g, cp, cl: (0,) * a.ndim)
    conv_b2 = conv_b.reshape(1, GROUP_W)
    in_specs = [pl.BlockSpec(blk, _fwd_map(col)), pl.BlockSpec(hblk, prev_f), pl.BlockSpec(hblk, next_f),
                pl.BlockSpec(blk, _bwd_map(n, col)), pl.BlockSpec(hblk, prev_b), pl.BlockSpec(hblk, next_b),
                full(conv_w), full(conv_b2), full(wg), full(bias), full(lam)]
    return (_lru_kernel, in_specs, [z, z, z, z, z, z, conv_w, conv_b2, wg, bias, lam],
            [pltpu.VMEM((CH + 2 * HALO, GROUP_W), F32)] * 2 + [pltpu.VMEM((1, GROUP_W), F32)] * 2,
            "fffbbbccccc")


def _tri_consts():
    pos = np.arange(CH)
    lower = (pos[:, None] >= pos[None, :]).astype(np.float32)
    ops = np.stack([lower, lower.T])
    return jnp.asarray(np.tile(ops, (1, 1, PREFIX_TERMS)), BF16)


def _hg_masks():
    pos = np.arange(CH)
    same_blk = pos[:, None] // SUB == pos[None, :] // SUB
    same_grp = pos[:, None] // GRP == pos[None, :] // GRP
    return jnp.asarray(np.stack([same_blk, same_grp]).astype(np.float32))


def _hg_direction(q_ref, f_ref, v_ref, lb, cum_ref, mask_ref, st_ref, o_ref, reset, first, reverse):
    valid = _row_valid(first)
    sub_i =lax.broadcasted_iota(jnp.int32, (SUB, HEAD_D), 0)
    lane_j = lax.broadcasted_iota(jnp.int32, (SUB, HEAD_D), 1) & (SUB - 1)
    causal = (sub_i <= lane_j) if reverse else (sub_i >= lane_j)
    pick = [causal & (lane_j == j) for j in range(SUB)]
    cum_op = cum_ref[...]
    same_blk = mask_ref[0]
    same_grp = mask_ref[1]
    zero = jnp.zeros((SUB, HEAD_D), F32)
    n_blk = CH // SUB
    per = GRP // SUB
    n_grp = CH // GRP

    def edge(unit, idx):
        return unit * idx if reverse else unit * (idx + 1) - 1

    def split_product(b, qs, ks, bs, pieces):
        q_slabs, k_slabs = [], []
        for ref, q_active, k_active in pieces:
            qp, kp = [], []
            for i in range(n_blk):
                r = ref(i)
                qp.append(qs[i] * jnp.exp2(bs[i] - b[r:r + 1, :]) if q_active(i) else zero)
                kp.append(ks[i] * jnp.exp2(b[r:r + 1, :] - bs[i]) if k_active(i) else zero)
            q_slabs.append(jnp.concatenate(qp, axis=0))
            k_slabs.append(jnp.concatenate(kp, axis=0))
        return _dot_nt(jnp.concatenate(q_slabs, axis=1), jnp.concatenate(k_slabs, axis=1))

    level1 = []
    for c in (range(1, per) if reverse else range(per - 1)):
        level1.append((lambda i, c=c: edge(SUB, (i // per) * per + c),
                       (lambda i, c=c: i % per < c) if reverse else (lambda i, c=c: i % per > c),
                       lambda i, c=c: i % per == c))
    level2 = []
    for gc in (range(1, n_grp) if reverse else range(n_grp - 1)):
        level2.append((lambda i, gc=gc: edge(GRP, gc),
                       (lambda i, gc=gc: i // per < gc) if reverse else (lambda i, gc=gc: i // per > gc),
                       lambda i, gc=gc: i // per == gc))

    for h in range(N_HEADS):
        sl = slice(h * HEAD_D, (h + 1) * HEAD_D)
        lbh = lb[:, sl]
        q = _silu(q_ref[:, sl])
        f = lbh + (1.0 - lbh) * _sigmoid(f_ref[:, sl])
        k = jnp.where(valid, 1.0 - f, 0.0)
        v = v_ref[:, sl]
        b = _prefix_dot(cum_op, jnp.log(f)) * LOG2_E
        qs = [q[i * SUB:(i + 1) * SUB] for i in range(n_blk)]
        ks = [k[i * SUB:(i + 1) * SUB] for i in range(n_blk)]
        bs = [b[i * SUB:(i + 1) * SUB] for i in range(n_blk)]

        tiles = [qs[blk] * jnp.exp2(jnp.minimum(bs[blk] - bs[blk][j:j + 1, :], 0.0))
                 for blk in range(n_blk) for j in range(SUB)]
        pair = _dot_nt(jnp.concatenate(tiles, axis=0), k)
        rows = []
        for blk in range(n_blk):
            base = blk * SUB * SUB
            a_blk = jnp.where(pick[0], pair[base:base + SUB], 0.0)
            for j in range(1, SUB):
                a_blk = a_blk + jnp.where(pick[j], pair[base + j * SUB:base + (j + 1) * SUB], 0.0)
            rows.append(a_blk)
        scores = jnp.concatenate(rows, axis=0) * same_blk
        scores = scores + split_product(b, qs, ks, bs, level1) * same_grp
        scores = scores + split_product(b, qs, ks, bs, level2)

        state = jnp.where(reset, 0.0, st_ref[h])
        o_ref[:, sl] = (_dot(scores, v) + _dot_nt(q * jnp.exp2(b), state)).astype(o_ref.dtype)
        b_tot = b[0:1, :] if reverse else b[CH - 1:CH, :]
        st_ref[h] = state * jnp.exp2(b_tot) + _dot_tn(v, k * jnp.exp2(b_tot - b))


def _hg_lower_bound(lower_ref, layer):
    low = lower_ref[...]
    e = jnp.exp(low - jnp.max(low, axis=0, keepdims=True))
    soft = e / jnp.sum(e, axis=0, keepdims=True)
    lb = jnp.zeros((1, GROUP_W), F32)
    for l in range(1, layer + 1):
        lb = lb + soft[l:l + 1, :]
    return lb


def _hg_kernel(layer, cpos_ref, clast_ref, g, gb, qf, ff, vf, qb, fb, vb, lower_ref, cum_ref, mask_ref,
               of_ref, ob_ref, sf_ref, sb_ref):
    lb =_hg_lower_bound(lower_ref, layer)
    _hg_direction(qf, ff, vf, lb, cum_ref.at[0], mask_ref, sf_ref, of_ref,
                  cpos_ref[g] == 0, cpos_ref[g] == 0, False)
    _hg_direction(qb, fb, vb, lb, cum_ref.at[1], mask_ref, sb_ref, ob_ref,
                  clast_ref[gb] == 1, cpos_ref[gb] == 0, True)


def _hgrn2(z, lower, layer, cps):
    t = z.shape[0]
    n = t // (CH * cps)
    blk = (CH * cps, GROUP_W)
    cum = _tri_consts()
    masks = _hg_masks()
    full = lambda a: pl.BlockSpec(a.shape, lambda g, cp, cl: (0,) * a.ndim)
    in_specs = [pl.BlockSpec(blk, _fwd_map(0)), pl.BlockSpec(blk, _fwd_map(1)), pl.BlockSpec(blk, _fwd_map(3)),
                pl.BlockSpec(blk, _bwd_map(n, 0)), pl.BlockSpec(blk, _bwd_map(n, 2)),
                pl.BlockSpec(blk, _bwd_map(n, 3)), full(lower), full(cum), full(masks)]
    return (functools.partial(_hg_kernel, layer), in_specs, [z, z, z, z, z, z, lower, cum, masks],
            [pltpu.VMEM((N_HEADS, HEAD_D, HEAD_D), F32)] * 2, "fffbbbccc")


def _log_sigmoid(x):
    return jnp.minimum(x, 0.0) - jnp.log(1.0 + jnp.exp(-jnp.abs(x)))


def _ml_direction(q_ref, k_ref, v_ref, gi_ref, gf_ref, bias_ref, cum_op, spread_ref, s_ref, m_ref, o_ref,
                  reset, first, d, reverse):
    valid = _row_valid(first)
    cum = _prefix_dot(cum_op, _log_sigmoid(gf_ref[...] + bias_ref[1]))
    a = jnp.where(valid, gi_ref[...] + bias_ref[0] - cum, NEG_BIG)
    run = a
    s = 1
    while s < CH:
        run = jnp.maximum(run, _shift_rows(run, s, NEG_BIG, reverse))
        s *= 2
    m_st = jnp.where(reset, NEG_BIG, m_ref[...])
    mx = jnp.maximum(m_st, run)
    edge = 0 if reverse else CH - 1
    mx_last = mx[edge:edge + 1, :]
    m_ref[...] = cum[edge:edge + 1, :] + mx_last
    decay = jnp.exp(m_st - mx_last)

    stacked = jnp.concatenate([mx, m_st - mx, -(cum + mx), a - mx_last], axis=0) * LOG2_E
    spread_b = _spread_dot(stacked, spread_ref[...])
    mx_b, inter_b, floor_b, end_b = (spread_b[i * CH:(i + 1) * CH] for i in range(4))
    a_t = (a * LOG2_E).T

    row = lax.broadcasted_iota(jnp.int32, (CH, CH), 0)
    col = lax.broadcasted_iota(jnp.int32, (CH, CH), 1)
    causal = (col >= row) if reverse else (col <= row)
    ones = jnp.ones((CH, HEAD_D), F32)
    for h in range(N_HEADS):
        sl = slice(h * HEAD_D, (h + 1) * HEAD_D)
        x = d * N_HEADS + h
        q = q_ref[:, sl]
        k = jnp.where(valid, k_ref[:, sl] * (HEAD_D ** -0.5), 0.0)
        v_ext = jnp.concatenate([v_ref[:, sl], ones], axis=1)
        w = jnp.where(causal, jnp.exp2(jnp.minimum(a_t[x:x + 1, :] - mx_b[:, sl], 0.0)), 0.0)
        qk = _dot_nt(q, k) * w
        s_inter = jnp.exp2(jnp.minimum(inter_b[:, sl], 0.0))
        state = jnp.where(reset, 0.0, s_ref[h])
        ext = _dot(jnp.concatenate([qk, q * s_inter], axis=1), jnp.concatenate([v_ext, state], axis=0))
        o_ref[:, sl] = (ext[:, :HEAD_D] / jnp.maximum(jnp.abs(ext[:, HEAD_D:]), jnp.exp2(floor_b[:, sl]))
                        ).astype(o_ref.dtype)
        kw = k * jnp.exp2(jnp.minimum(end_b[:, sl], 0.0))
        s_ref[h] = decay[:, x:x + 1] * state + _dot_tn(kw, v_ext)


def _ml_kernel(cpos_ref, clast_ref, g, gb, qf, kf, vf, gif, gff, qb, kb, vb, gib, gfb, bias_ref, cum_ref,
               spread_ref, of_ref, ob_ref, sfs, mfs, sbs, mbs):
    _ml_direction(qf, kf, vf, gif, gff, bias_ref, cum_ref[0], spread_ref.at[0], sfs, mfs, of_ref,
                  cpos_ref[g] == 0, cpos_ref[g] == 0, 0, False)
    _ml_direction(qb, kb, vb, gib, gfb, bias_ref, cum_ref[1], spread_ref.at[1], sbs, mbs, ob_ref,
                  clast_ref[gb] == 1, cpos_ref[gb] == 0, 1, True)


def _ml_spread():
    spread = np.zeros((2, LANES, GROUP_W), np.float32)
    for d in range(2):
        for h in range(N_HEADS):
            spread[d, d * N_HEADS + h, h * HEAD_D:(h + 1) * HEAD_D] = 1.0
    return jnp.asarray(np.tile(spread, (1, SPREAD_TERMS, 1)), BF16)


def _mlstm(z, gates, ml_bi, ml_bf, cps):
    t = z.shape[0]
    n = t // (CH * cps)
    blk = (CH * cps, GROUP_W)
    gblk = (CH * cps, LANES)
    cum = _tri_consts()
    spread = _ml_spread()
    pad = jnp.zeros((LANES - 2 * N_HEADS,), F32)
    bias = jnp.stack([jnp.concatenate([ml_bi.reshape(-1), pad]),
                      jnp.concatenate([ml_bf.reshape(-1), pad])]).reshape(2, 1, LANES)
    full = lambda a: pl.BlockSpec(a.shape, lambda g, cp, cl: (0,) * a.ndim)
    in_specs = [pl.BlockSpec(blk, _fwd_map(5)), pl.BlockSpec(blk, _fwd_map(6)), pl.BlockSpec(blk, _fwd_map(7)),
                pl.BlockSpec(gblk, _fwd_map(0)), pl.BlockSpec(gblk, _fwd_map(1)),
                pl.BlockSpec(blk, _bwd_map(n, 5)), pl.BlockSpec(blk, _bwd_map(n, 6)),
                pl.BlockSpec(blk, _bwd_map(n, 7)),
                pl.BlockSpec(gblk, _bwd_map(n, 0)), pl.BlockSpec(gblk, _bwd_map(n, 1)),
                full(bias), full(cum), full(spread)]
    state = [pltpu.VMEM((N_HEADS, HEAD_D, 2 * HEAD_D), F32), pltpu.VMEM((1, LANES), F32)]
    return (_ml_kernel, in_specs, [z, z, z, gates, gates, z, z, z, gates, gates, bias, cum, spread], state * 2,
            "fffffbbbbbccc")


def _head_norm(x, center):
    outs = []
    for h in range(N_HEADS):
        xh = x[:, h * HEAD_D:(h + 1) * HEAD_D]
        if center:
            xh = xh - jnp.mean(xh, axis=1, keepdims=True)
        outs.append(xh * lax.rsqrt(jnp.mean(xh * xh, axis=1, keepdims=True) + EPS))
    return jnp.concatenate(outs, axis=1)


def _layer_norm(x, g, b):
    xc = x - jnp.mean(x, axis=1, keepdims=True)
    return xc * lax.rsqrt(jnp.mean(xc * xc, axis=1, keepdims=True) + EPS) * g + b


def _mix_rows(alpha, even, af, ab, ag, bf, bb, bg, h_ref, w_ref, lng_ref, lnb_ref):
    a = af[...].astype(F32) + ab[...].astype(F32)
    b = bf[...].astype(F32) + bb[...].astype(F32)
    if even:
        a = _head_norm(a, False) * _silu(ag[...])
        b = b * _gelu_tanh(bg[...])
    else:
        a = _head_norm(a, False) * _silu(ag[...])
        b = _head_norm(b, True) * _sigmoid(bg[...])
    mix = _dot(a, w_ref[0:GROUP_W, :]) + _dot(b, w_ref[GROUP_W:, :])
    return _layer_norm(alpha * h_ref[...] + mix, lng_ref[...], lnb_ref[...])


N_MIX_OPERANDS = 10


def _mix_specs(tm, col_a, col_b, index):
    half = lambda c: pl.BlockSpec((tm, GROUP_W), lambda *g: (index(*g), c))
    rows = pl.BlockSpec((tm, D_MODEL), lambda *g: (index(*g), 0))
    vec = pl.BlockSpec((1, D_MODEL), lambda *g: (0, 0))
    return [half(0), half(0), half(col_a), half(0), half(0), half(col_b), rows,
            pl.BlockSpec((D_MODEL, D_MODEL), lambda *g: (0, 0)), vec, vec]


def _mix_ffn_kernel(alpha, *refs):
    mix_refs = refs[:N_MIX_OPERANDS]
    wg_ref, wu_ref, wo_ref, lng_ref, lnb_ref, o_ref, h1_ref, acc_ref = refs[N_MIX_OPERANDS:]
    f = pl.program_id(1)

    @pl.when(f == 0)
    def _():
        h1_ref[...] = _mix_rows(alpha, True, *mix_refs)
        acc_ref[...] = jnp.zeros_like(acc_ref)

    x = h1_ref[...].astype(BF16)
    act = _silu(_dot(x, wg_ref[...])) * _dot(x, wu_ref[...])
    acc_ref[...] += _dot(act, wo_ref[...])

    @pl.when(f == pl.num_programs(1) - 1)
    def _():
        o_ref[...] = _layer_norm(alpha * h1_ref[...] + acc_ref[...], lng_ref[...], lnb_ref[...])


def _mix_ffn(alpha, af, ab, bf, bb, z, col_a, col_b, h, w_out, ln1_g, ln1_b, wi, wo, ln2_g, ln2_b):
    t = h.shape[0]
    d_ff = wo.shape[0]
    tm = _pick_tile(t, 640)
    tf = _pick_tile(d_ff, 1536)
    nf = d_ff // tf
    rows = pl.BlockSpec((tm, D_MODEL), lambda i, f: (i, 0))
    vec = pl.BlockSpec((1, D_MODEL), lambda i, f: (0, 0))
    return pl.pallas_call(
        functools.partial(_mix_ffn_kernel, alpha), grid=(t // tm, nf),
        in_specs=_mix_specs(tm, col_a, col_b, lambda i, f: i) + [
            pl.BlockSpec((D_MODEL, tf), lambda i, f: (0, f)),
            pl.BlockSpec((D_MODEL, tf), lambda i, f: (0, f + nf)),
            pl.BlockSpec((tf, D_MODEL), lambda i, f: (f, 0)), vec, vec],
        out_specs=rows, out_shape=jax.ShapeDtypeStruct((t, D_MODEL), F32),
        scratch_shapes=[pltpu.VMEM((tm, D_MODEL), F32), pltpu.VMEM((tm, D_MODEL), F32)],
        compiler_params=_params(2), name="mix_ffn")(
            af, ab, z, bf, bb, z, h, w_out, ln1_g.reshape(1, -1), ln1_b.reshape(1, -1),
            wi, wi, wo, ln2_g.reshape(1, -1), ln2_b.reshape(1, -1))


EXPERT_TILE = 1024
DMA_UNROLL = 8
DMA_PRIORITIES = 2


def _mix_route_kernel(alpha, pad_starts, *refs):
    mix_refs = refs[:N_MIX_OPERANDS]
    router_ref, tri_ref, o_ref, meta_ref, gate_ref, cnt_ref, carry_ref = refs[N_MIX_OPERANDS:]

    @pl.when(pl.program_id(0) == 0)
    def _():
        carry_ref[...] = jnp.zeros_like(carry_ref)

    h1 = _mix_rows(alpha, False, *mix_refs)
    o_ref[...] = h1
    tm = h1.shape[0]
    row = pl.program_id(0) * tm + lax.broadcasted_iota(jnp.int32, (tm, 1), 0)
    is_pad = row < 0
    for s in pad_starts:
        is_pad = is_pad | ((row >= s) & (row < s + PAD_ROWS))
    is_token = jnp.logical_not(is_pad)

    logits = _dot_16bit(h1, router_ref[...])
    lane = lax.broadcasted_iota(jnp.int32, logits.shape, 1)
    logits = jnp.where(lane < N_EXP, logits, -jnp.inf)
    top1 = jnp.max(logits, axis=1, keepdims=True)
    idx1 = jnp.min(jnp.where(logits == top1, lane, LANES), axis=1, keepdims=True)
    rest = jnp.where(lane == idx1, -jnp.inf, logits)
    top2 = jnp.max(rest, axis=1, keepdims=True)
    idx2 = jnp.min(jnp.where(rest == top2, lane, LANES), axis=1, keepdims=True)
    g2 = jnp.exp(top2 - top1)
    denom = 1.0 + g2
    hit1 = lane == idx1
    hit2 = lane == idx2
    both = jnp.where(is_token, jnp.where(hit1, 1.0, 0.0) + jnp.where(hit2, 1.0, 0.0), 0.0)
    prefix = _dot(tri_ref[...], both) + carry_ref[...]
    rank1 = jnp.sum(jnp.where(hit1, prefix, 0.0), axis=1, keepdims=True).astype(jnp.int32)
    rank2 = jnp.sum(jnp.where(hit2, prefix, 0.0), axis=1, keepdims=True).astype(jnp.int32)
    carry_ref[...] += jnp.sum(both, axis=0, keepdims=True)
    cnt_ref[...] = carry_ref[...]
    meta_ref[...] = jnp.where(lane == 0, idx1, jnp.where(lane == 1, idx2, jnp.where(
        lane == 2, rank1, jnp.where(lane == 3, rank2, jnp.where(is_token & (lane == 4), 1, 0)))))
    gate_ref[...] = jnp.where(lane == 0, 1.0 / denom, jnp.where(lane == 1, g2 / denom, 0.0))


def _mix_route(alpha, af, ab, bf, bb, z, col_a, col_b, h, w_out, ln_g, ln_b, router, pad_starts):
    t = h.shape[0]
    tm = _pick_tile(t, 512)
    router_p = _stack_weight_terms(jnp.pad(router, ((0, 0), (0, LANES - router.shape[1]))))
    pos = np.arange(tm)
    tri = jnp.asarray(pos[:, None] > pos[None, :], BF16)
    rows = lambda w: pl.BlockSpec((tm, w), lambda i: (i, 0))
    const = lambda a: pl.BlockSpec(a.shape, lambda i: (0, 0))
    return pl.pallas_call(
        functools.partial(_mix_route_kernel, alpha, pad_starts), grid=(t // tm,),
        in_specs=_mix_specs(tm, col_a, col_b, lambda i: i) + [const(router_p), const(tri)],
        out_specs=[rows(D_MODEL), rows(LANES), rows(LANES), pl.BlockSpec((1, LANES), lambda i: (0, 0))],
        out_shape=[jax.ShapeDtypeStruct((t, D_MODEL), F32), jax.ShapeDtypeStruct((t, LANES), jnp.int32),
                   jax.ShapeDtypeStruct((t, LANES), F32), jax.ShapeDtypeStruct((1, LANES), F32)],
        scratch_shapes=[pltpu.VMEM((1, LANES), F32)],
        compiler_params=_params(1), name="mix_route")(
            af, ab, z, bf, bb, z, h, w_out, ln_g.reshape(1, -1), ln_b.reshape(1, -1), router_p, tri)


def _dispatch_plan(meta, counts, t):
    cnt = counts[0, :N_EXP].astype(jnp.int32)
    padded = ((cnt + EXPERT_TILE - 1) // EXPERT_TILE) * EXPERT_TILE
    ends = jnp.cumsum(padded)
    off = ends - padded

    def base(e):
        return sum(jnp.where(e == i, off[i], 0) for i in range(N_EXP))

    is_token = meta[:, 4] > 0
    n_tiles = -(-2 * t // EXPERT_TILE) + N_EXP
    spare = n_tiles * EXPERT_TILE + 2 * (jnp.cumsum(jnp.logical_not(is_token).astype(jnp.int32)) - 1)
    pos1 = (base(meta[:, 0]) + meta[:, 2]).astype(jnp.int32)
    pos2 = (base(meta[:, 1]) + meta[:, 3]).astype(jnp.int32)
    scatter = (jnp.where(is_token, pos1, spare), jnp.where(is_token, pos2, spare + 1))
    gather = (jnp.where(is_token, pos1, 0), jnp.where(is_token, pos2, 0))
    starts = jnp.arange(n_tiles, dtype=jnp.int32) * EXPERT_TILE
    tile_expert = jnp.minimum(jnp.sum(starts[:, None] >= ends[None, :], axis=1), N_EXP - 1).astype(jnp.int32)
    n_active = (ends[-1] // EXPERT_TILE).astype(jnp.int32).reshape(1)
    return ends.astype(jnp.int32), scatter, gather, tile_expert, n_active, n_tiles


assert D_MODEL == SUBLANES * LANES


def _store_token_tiles(ref, x):
    n = x.shape[0]
    for s in range(SUBLANES):
        ref[pl.ds(s, n, stride=SUBLANES), :] = x[:, s * LANES:(s + 1) * LANES]


def _load_token_tiles(ref, n):
    return jnp.concatenate([ref[pl.ds(s, n, stride=SUBLANES), :] for s in range(SUBLANES)], axis=1)


def _tile_rows(i):
    return pl.ds(pl.multiple_of(i * SUBLANES, SUBLANES), SUBLANES)


def _dispatch_kernel(n_tiles, ends_ref, pos1_ref, pos2_ref, h_ref, xs_ref, tok_ref, zero_ref, sem, zero_sem):
    n = h_ref.shape[0]

    @pl.when(pl.program_id(0) == 0)
    def _():
        zero_ref[...] = jnp.zeros_like(zero_ref)

        def clear_last_tile(e):
            first_row = pl.multiple_of((ends_ref[e] - EXPERT_TILE) * SUBLANES, SUBLANES)
            return pltpu.make_async_copy(zero_ref, xs_ref.at[pl.ds(first_row, EXPERT_TILE * SUBLANES)], zero_sem)

        def has_tiles(e):
            return ends_ref[e] > (ends_ref[e - 1] if e else 0)

        for e in range(N_EXP):
            pl.when(has_tiles(e))(lambda e=e: clear_last_tile(e).start())
        for e in range(N_EXP):
            pl.when(has_tiles(e))(lambda e=e: clear_last_tile(e).wait())

        def clear_tile(i):
            first_row = pl.multiple_of(i * (EXPERT_TILE * SUBLANES), SUBLANES)
            return pltpu.make_async_copy(zero_ref, xs_ref.at[pl.ds(first_row, EXPERT_TILE * SUBLANES)], zero_sem)

        used = ends_ref[N_EXP - 1] // EXPERT_TILE
        lax.fori_loop(used, n_tiles, lambda i, c: (clear_tile(i).start(), c)[1], 0)
        lax.fori_loop(used, n_tiles, lambda i, c: (clear_tile(i).wait(), c)[1], 0)

    g = pl.program_id(0)
    slot = g % 2
    _store_token_tiles(tok_ref.at[slot], h_ref[...])

    def tile_copy(r, p):
        return pltpu.make_async_copy(tok_ref.at[slot].at[_tile_rows(r)], xs_ref.at[_tile_rows(p)], sem.at[slot])

    def start(pair, c):
        for prio in range(DMA_PRIORITIES):
            r = pair * DMA_PRIORITIES + prio
            tile_copy(r, pos1_ref[0, 0, r]).start(priority=prio)
            tile_copy(r, pos2_ref[0, 0, r]).start(priority=prio)
        return c

    lax.fori_loop(0, n // DMA_PRIORITIES, start, 0, unroll=DMA_UNROLL // DMA_PRIORITIES)

    def wait_slot(s):
        all_rows = pltpu.make_async_copy(tok_ref.at[s], xs_ref.at[pl.ds(0, n * SUBLANES)], sem.at[s])
        all_rows.wait()
        all_rows.wait()

    pl.when(g > 0)(lambda: wait_slot(1 - slot))
    pl.when(g == pl.num_programs(0) - 1)(lambda: wait_slot(slot))


def _dispatch(h, ends, pos1, pos2, n_tiles, n_spare):
    t = h.shape[0]
    n_rows = n_tiles * EXPERT_TILE + n_spare
    tm = _pick_tile(t, 768)
    idx = lambda: pl.BlockSpec((1, 1, tm), lambda i, ends: (i, 0, 0), memory_space=pltpu.SMEM)
    grid_spec = pltpu.PrefetchScalarGridSpec(
        num_scalar_prefetch=1, grid=(t // tm,),
        in_specs=[idx(), idx(), pl.BlockSpec((tm, D_MODEL), lambda i, ends: (i, 0))],
        out_specs=pl.BlockSpec(memory_space=pl.ANY),
        scratch_shapes=[pltpu.VMEM((2, tm * SUBLANES, LANES), F32),
                        pltpu.VMEM((EXPERT_TILE * SUBLANES, LANES), F32),
                        pltpu.SemaphoreType.DMA((2,)), pltpu.SemaphoreType.DMA(())])
    return pl.pallas_call(
        functools.partial(_dispatch_kernel, n_tiles), grid_spec=grid_spec,
        out_shape=jax.ShapeDtypeStruct((n_rows * SUBLANES, LANES), F32),
        compiler_params=_params(1), name="dispatch")(
            ends, pos1.reshape(t // tm, 1, tm), pos2.reshape(t // tm, 1, tm), h)


def _experts_kernel(te_ref, na_ref, x_ref, wg_ref, wu_ref, wo_ref, y_ref, xb_ref, acc_ref):
    i = pl.program_id(0)
    f = pl.program_id(1)
    last = f == pl.num_programs(1) - 1
    active = i < na_ref[0]

    @pl.when(active)
    def _():
        @pl.when(f == 0)
        def _():
            acc_ref[...] = jnp.zeros_like(acc_ref)
            xb_ref[...] = _load_token_tiles(x_ref, EXPERT_TILE).astype(BF16)

        x = xb_ref[...]
        act = _silu(_dot(x, wg_ref[0])) * _dot(x, wu_ref[0])
        acc_ref[...] += _dot(act, wo_ref[0])

        @pl.when(last)
        def _():
            _store_token_tiles(y_ref, acc_ref[...])

    @pl.when(jnp.logical_not(active) & last)
    def _():
        y_ref[...] = jnp.zeros_like(y_ref)


def _experts(xs, tile_expert, n_active, wi, wo, n_tiles):
    n_rows = n_tiles * EXPERT_TILE
    e_ff = wo.shape[1]
    tf = _pick_tile(e_ff, 512)
    nf = e_ff // tf
    rows = pl.BlockSpec((EXPERT_TILE * SUBLANES, LANES), lambda i, f, te, na: (i, 0))
    rows_in = pl.BlockSpec((EXPERT_TILE * SUBLANES, LANES),
                           lambda i, f, te, na: (jnp.minimum(i, jnp.maximum(na[0] - 1, 0)), 0))
    grid_spec = pltpu.PrefetchScalarGridSpec(
        num_scalar_prefetch=2, grid=(n_rows // EXPERT_TILE, nf),
        in_specs=[rows_in, pl.BlockSpec((1, D_MODEL, tf), lambda i, f, te, na: (te[i], 0, f)),
                  pl.BlockSpec((1, D_MODEL, tf), lambda i, f, te, na: (te[i], 0, f + nf)),
                  pl.BlockSpec((1, tf, D_MODEL), lambda i, f, te, na: (te[i], f, 0))],
        out_specs=rows,
        scratch_shapes=[pltpu.VMEM((EXPERT_TILE, D_MODEL), BF16), pltpu.VMEM((EXPERT_TILE, D_MODEL), F32)])
    return pl.pallas_call(
        _experts_kernel, grid_spec=grid_spec, out_shape=jax.ShapeDtypeStruct((n_rows * SUBLANES, LANES), F32),
        compiler_params=_params(2), name="experts")(tile_expert, n_active, xs, wi, wi, wo)


def _gather_start(n, pos1_ref, pos2_ref, ys_ref, a_ref, b_ref, sems):
    def tile_copy(p, buf, r, which):
        return pltpu.make_async_copy(ys_ref.at[_tile_rows(p)], buf.at[_tile_rows(r)], sems.at[which])

    def start(pair, c):
        for prio in range(DMA_PRIORITIES):
            r = pair * DMA_PRIORITIES + prio
            tile_copy(pos1_ref[0, 0, r], a_ref, r, 0).start(priority=prio)
            tile_copy(pos2_ref[0, 0, r], b_ref, r, 1).start(priority=prio)
        return c

    lax.fori_loop(0, n // DMA_PRIORITIES, start, 0, unroll=DMA_UNROLL // DMA_PRIORITIES)


def _gather_finish(alpha, h_ref, gate_ref, ys_ref, lng_ref, lnb_ref, a_ref, b_ref, sems):
    n = h_ref.shape[0]
    for which, buf in enumerate((a_ref, b_ref)):
        pltpu.make_async_copy(ys_ref.at[pl.ds(0, n * SUBLANES)], buf, sems.at[which]).wait()
    gate = gate_ref[...]
    y = gate[:, 0:1] * _load_token_tiles(a_ref, n) + gate[:, 1:2] * _load_token_tiles(b_ref, n)
    return _layer_norm(alpha * h_ref[...] + y, lng_ref[...], lnb_ref[...])


def _combine_kernel(alpha, pos1_ref, pos2_ref, h_ref, gate_ref, ys_ref, lng_ref, lnb_ref, o_ref,
                    a_ref, b_ref, sems):
    _gather_start(h_ref.shape[0], pos1_ref, pos2_ref, ys_ref, a_ref, b_ref, sems)
    o_ref[...] = _gather_finish(alpha, h_ref, gate_ref, ys_ref, lng_ref, lnb_ref, a_ref, b_ref, sems)


def _combine_scratch(tm):
    return [pltpu.VMEM((tm * SUBLANES, LANES), F32), pltpu.VMEM((tm * SUBLANES, LANES), F32),
            pltpu.SemaphoreType.DMA((2,))]


def _combine(alpha, h, gates, ys, pos1, pos2, ln_g, ln_b):
    t = h.shape[0]
    tm = _pick_tile(t, 512)
    idx = lambda: pl.BlockSpec((1, 1, tm), lambda i: (i, 0, 0), memory_space=pltpu.SMEM)
    rows = pl.BlockSpec((tm, D_MODEL), lambda i: (i, 0))
    vec = pl.BlockSpec((1, D_MODEL), lambda i: (0, 0))
    return pl.pallas_call(
        functools.partial(_combine_kernel, alpha), grid=(t // tm,),
        in_specs=[idx(), idx(), rows, pl.BlockSpec((tm, LANES), lambda i: (i, 0)),
                  pl.BlockSpec(memory_space=pl.ANY), vec, vec],
        out_specs=rows, out_shape=jax.ShapeDtypeStruct((t, D_MODEL), F32),
        scratch_shapes=_combine_scratch(tm),
        compiler_params=_params(1), name="combine")(
            pos1.reshape(t // tm, 1, tm), pos2.reshape(t // tm, 1, tm), h, gates, ys,
            ln_g.reshape(1, -1), ln_b.reshape(1, -1))


def _output_tables(seq_shapes):
    kind, live = [], []
    window = [[], []]
    recent = [(0, 0), (0, 0)]
    for grp, (b, l) in enumerate(seq_shapes):
        for i in range(b):
            for c in range(l // CH + 1):
                kind.append(grp)
                live.append(int(c > 0))
                recent[grp] = (i, max(c - 1, 0))
                for g2 in range(2):
                    window[g2].append(recent[g2])
    tables = [kind, live] + [[w[j] for w in window[g2]] for g2 in range(2) for j in range(2)]
    return [jnp.asarray(np.asarray(x, np.int32)) for x in tables]


def _combine_out_kernel(alpha, kind_ref, live_ref, ab_ref, ac_ref, bb_ref, bc_ref,
                        pos1_ref, pos2_ref, nxt1_ref, nxt2_ref, h_ref, gate_ref, ys_ref, lng_ref, lnb_ref,
                        outa_ref, outb_ref, a_ref, b_ref, sems):
    g = pl.program_id(0)
    last = pl.num_programs(0) - 1
    slot = g % 2
    nxt = jnp.minimum(g + 1, last)

    @pl.when((g == 0) & (live_ref[0] == 1))
    def _():
        _gather_start(CH, pos1_ref, pos2_ref, ys_ref, a_ref.at[0], b_ref.at[0], sems.at[0])

    @pl.when((g < last) & (live_ref[nxt] == 1))
    def _():
        _gather_start(CH, nxt1_ref, nxt2_ref, ys_ref, a_ref.at[1 - slot], b_ref.at[1 - slot], sems.at[1 - slot])

    @pl.when(live_ref[g] == 1)
    def _():
        res = _gather_finish(alpha, h_ref, gate_ref, ys_ref, lng_ref, lnb_ref,
                             a_ref.at[slot], b_ref.at[slot], sems.at[slot])
        for grp, out_ref in enumerate((outa_ref, outb_ref)):
            @pl.when(kind_ref[g] == grp)
            def _(out_ref=out_ref):
                out_ref[0] = res


def _combine_out(alpha, h, gates, ys, pos1, pos2, ln_g, ln_b, seq_shapes):
    t = h.shape[0]
    n = t // CH
    tables = _output_tables(seq_shapes)
    idx = lambda: pl.BlockSpec((1, 1, CH), lambda g, *tb: (g, 0, 0), memory_space=pltpu.SMEM)
    idx_next = lambda: pl.BlockSpec((1, 1, CH), lambda g, *tb: (jnp.minimum(g + 1, n - 1), 0, 0),
                                    memory_space=pltpu.SMEM)
    rows = lambda w: pl.BlockSpec((CH, w), lambda g, *tb: (g, 0))
    vec = pl.BlockSpec((1, D_MODEL), lambda g, *tb: (0, 0))
    out = lambda grp: pl.BlockSpec((1, CH, D_MODEL),
                                   lambda g, *tb: (tb[2 + 2 * grp][g], tb[3 + 2 * grp][g], 0))
    fetched = pltpu.VMEM((2, CH * SUBLANES, LANES), F32)
    grid_spec = pltpu.PrefetchScalarGridSpec(
        num_scalar_prefetch=len(tables), grid=(n,),
        in_specs=[idx(), idx(), idx_next(), idx_next(), rows(D_MODEL), rows(LANES),
                  pl.BlockSpec(memory_space=pl.ANY), vec, vec],
        out_specs=[out(0), out(1)], scratch_shapes=[fetched, fetched, pltpu.SemaphoreType.DMA((2, 2))])
    pos1, pos2 = pos1.reshape(n, 1, CH), pos2.reshape(n, 1, CH)
    return pl.pallas_call(
        functools.partial(_combine_out_kernel, alpha), grid_spec=grid_spec,
        out_shape=[jax.ShapeDtypeStruct((b, l, D_MODEL), F32) for b, l in seq_shapes],
        compiler_params=_params(1), name="combine_out")(
            *tables, pos1, pos2, pos1, pos2, h, gates, ys, ln_g.reshape(1, -1), ln_b.reshape(1, -1))


def _moe(alpha, h, meta, gates, counts, wi, wo, ln_g, ln_b, n_pad_rows, out_shapes=None):
    t = h.shape[0]
    ends, scatter, gather, tile_expert, n_active, n_tiles = _dispatch_plan(meta, counts, t)
    n_spare = 2 * n_pad_rows
    xs = _dispatch(h, ends, scatter[0], scatter[1], n_tiles, n_spare)
    ys = _experts(xs, tile_expert, n_active, wi, wo, n_tiles)
    if out_shapes is None:
        return _combine(alpha, h, gates, ys, gather[0], gather[1], ln_g, ln_b)
    return _combine_out(alpha, h, gates, ys, gather[0], gather[1], ln_g, ln_b, out_shapes)


def kernel(x_prompt, x_sample, meta, e_w_in, e_conv_w, e_conv_b, e_lru_wa, e_lru_ba, e_lru_wx, e_lru_bx,
           e_lru_lambda, e_w_out, e_ffn_wi, e_ffn_wo, o_w_in, o_hg_lower, o_ml_bi, o_ml_bf, o_w_out, o_router,
           o_exp_wi, o_exp_wo, ln_g, ln_b):
    groups = (x_prompt, x_sample)
    depth = ln_g.shape[0]
    alpha = (2.0 * depth) ** 0.25
    seq_shapes = [(x.shape[0], x.shape[1]) for x in groups]
    for _, l in seq_shapes:
        assert l % CH == 0
    cpos_np, clast_np = _chunk_tables(seq_shapes)
    cpos, clast = jnp.asarray(cpos_np), jnp.asarray(clast_np)
    max_rows = max(l for _, l in seq_shapes) + CH
    pad_starts = tuple(int(i) * CH for i in np.flatnonzero(cpos_np == 0))

    head = jnp.concatenate([jnp.zeros((PAD_ROWS, D_MODEL), F32), meta.astype(F32)], axis=0)
    parts = []
    for x in groups:
        full = jnp.concatenate([jnp.broadcast_to(head[None], (x.shape[0], CH, D_MODEL)), x], axis=1)
        parts.append(full.reshape(-1, D_MODEL))
    h = jnp.concatenate(parts, axis=0)
    t = h.shape[0]

    n_odd_cols = 9 * GROUP_W
    for layer in range(depth):
        p = layer // 2
        if layer % 2 == 0:
            z = _project(h, e_w_in[p].astype(BF16))
            (ret_f, ret_b), (lru_f, lru_b) = _chunk_walk(
                [_retention(z, max_rows),
                 _rglru(z, 4, e_conv_w[p], e_conv_b[p], e_lru_wa[p], e_lru_ba[p],
                        e_lru_wx[p], e_lru_bx[p], e_lru_lambda[p])], cpos, clast, t, "even_mixers")
            h = _mix_ffn(alpha, ret_f, ret_b, lru_f, lru_b, z, 3, 5, h, e_w_out[p].astype(BF16),
                         ln_g[layer, 0], ln_b[layer, 0], e_ffn_wi[p].astype(BF16), e_ffn_wo[p].astype(BF16),
                         ln_g[layer, 1], ln_b[layer, 1])
        else:
            w_in = o_w_in[p]
            n_gate = 2 * N_HEADS
            gate_pad = ((0, 0), (0, LANES - n_gate))
            w_gates = jnp.concatenate([jnp.pad(w_in[:, n_odd_cols:n_odd_cols + n_gate], gate_pad),
                                       jnp.pad(w_in[:, n_odd_cols + n_gate:], gate_pad)], axis=1)
            z, gates = _project(h, w_in[:, :n_odd_cols].astype(BF16), w_gates)
            cps = ODD_CHUNKS_PER_STEP if (t // CH) % ODD_CHUNKS_PER_STEP == 0 else 1
            (hg_f, hg_b), (ml_f, ml_b) = _chunk_walk(
                [_hgrn2(z, o_hg_lower, layer, cps), _mlstm(z, gates, o_ml_bi[p], o_ml_bf[p], cps)],
                cpos, clast, t, "odd_mixers", cps)
            h, meta_r, gates_r, counts = _mix_route(
                alpha, hg_f, hg_b, ml_f, ml_b, z, 4, 8, h, o_w_out[p].astype(BF16),
                ln_g[layer, 0], ln_b[layer, 0], o_router[p], pad_starts)
            h = _moe(alpha, h, meta_r, gates_r, counts, o_exp_wi[p].astype(BF16), o_exp_wo[p].astype(BF16),
                     ln_g[layer, 1], ln_b[layer, 1], len(pad_starts) * PAD_ROWS,
                     seq_shapes if layer == depth - 1 else None)
            if layer == depth - 1:
                return tuple(h)

    outs = []
    row = 0
    for b, l in seq_shapes:
        n = b * (l + CH)
        outs.append(h[row:row + n].reshape(b, l + CH, D_MODEL)[:, CH:])
        row += n
    return tuple(outs)
```

```python
import functools
import math

import numpy as np
import jax
import jax.numpy as jnp
from jax import lax
from jax.experimental import pallas as pl
from jax.experimental.pallas import tpu as pltpu

F32 = jnp.float32
BF16 = jnp.bfloat16

D_MODEL = 1024
GROUP_W = D_MODEL // 2
N_HEADS = 4
HEAD_D = GROUP_W // N_HEADS
N_META = 16
ROPE_BASE = 10000.0
LRU_BLOCKS = 8
LRU_BW = GROUP_W // LRU_BLOCKS
LRU_C = 8.0
N_EXP = 8
EPS = 1e-5

LANES = 128
SUBLANES = 8
CH = 128
PAD_ROWS = CH - N_META
SUB = SUBLANES
GRP = 4 * SUB
ODD_CHUNKS_PER_STEP = 2
NEG_BIG = -1e30
LOG2_E = math.log2(math.e)
VMEM_LIMIT = 56 * 1024 * 1024


def _dot(a, b):
    return jnp.dot(a.astype(BF16), b.astype(BF16), preferred_element_type=F32)


def _dot_nt(a, b):
    return lax.dot_general(a.astype(BF16), b.astype(BF16), (((1,), (1,)), ((), ())),
                           preferred_element_type=F32)


def _dot_tn(a, b):
    return _dot(a.T, b)


def _bf16_terms(x, n_terms):
    terms = []
    for _ in range(n_terms):
        t = x.astype(BF16)
        terms.append(t)
        x = x - t.astype(F32)
    return terms


def _stack_weight_terms(w):
    hi, lo = _bf16_terms(w, 2)
    return jnp.concatenate([hi, lo, hi], axis=0)


def _dot_16bit(x, w_terms):
    hi, lo = _bf16_terms(x, 2)
    return jnp.dot(jnp.concatenate([hi, hi, lo], axis=1), w_terms, preferred_element_type=F32)


PREFIX_TERMS = 3
SPREAD_TERMS = 2


def _prefix_dot(op_tiled, x):
    return jnp.dot(op_tiled, jnp.concatenate(_bf16_terms(x, PREFIX_TERMS), axis=0),
                   preferred_element_type=F32)


def _spread_dot(x, op_tiled):
    return jnp.dot(jnp.concatenate(_bf16_terms(x, SPREAD_TERMS), axis=1), op_tiled,
                   preferred_element_type=F32)


def _sigmoid(x):
    return 1.0 / (1.0 + jnp.exp(-x))


def _silu(x):
    return x * _sigmoid(x)


def _gelu_tanh(x):
    return 0.5 * x * (1.0 + jnp.tanh(math.sqrt(2.0 / math.pi) * (x + 0.044715 * (x * x * x))))


def _softplus(x):
    return jnp.maximum(x, 0.0) + jnp.log(1.0 + jnp.exp(-jnp.abs(x)))


def _pick_tile(total, target):
    best = LANES
    for t in range(LANES, min(total, target) + 1, LANES):
        if total % t == 0:
            best = t
    return best


def _params(n_axes, sem="arbitrary"):
    return pltpu.CompilerParams(dimension_semantics=(sem,) * n_axes, vmem_limit_bytes=VMEM_LIMIT)


def _proj_kernel(x_ref, w_ref, o_ref):
    o_ref[...] = _dot(x_ref[...], w_ref[...])


def _proj_gates_kernel(x_ref, w_ref, wg_ref, o_ref, g_ref):
    g_ref[...] = _dot_16bit(x_ref[...], wg_ref[...])
    o_ref[...] = _dot(x_ref[...], w_ref[...])


PROJ_OUT_BLOCK_BYTES = 8 * 1024 * 1024


def _project(x, w, w_gates=None):
    t, k = x.shape
    n = w.shape[1]
    tm = _pick_tile(t, PROJ_OUT_BLOCK_BYTES // (4 * n))
    x_spec = pl.BlockSpec((tm, k), lambda i: (i, 0))
    w_spec = pl.BlockSpec((k, n), lambda i: (0, 0))
    o_spec = pl.BlockSpec((tm, n), lambda i: (i, 0))
    if w_gates is None:
        return pl.pallas_call(
            _proj_kernel, grid=(t // tm,), in_specs=[x_spec, w_spec], out_specs=o_spec,
            out_shape=jax.ShapeDtypeStruct((t, n), F32), compiler_params=_params(1),
            name="proj")(x, w)
    ng = w_gates.shape[1]
    w_gates = _stack_weight_terms(w_gates)
    return pl.pallas_call(
        _proj_gates_kernel, grid=(t // tm,),
        in_specs=[x_spec, w_spec, pl.BlockSpec(w_gates.shape, lambda i: (0, 0))],
        out_specs=[o_spec, pl.BlockSpec((tm, ng), lambda i: (i, 0))],
        out_shape=[jax.ShapeDtypeStruct((t, n), F32), jax.ShapeDtypeStruct((t, ng), F32)],
        compiler_params=_params(1), name="proj_gates")(x, w, w_gates)


def _chunk_tables(seq_shapes):
    cpos, clast = [], []
    for b, l in seq_shapes:
        n = l // CH + 1
        for _ in range(b):
            cpos += list(range(n))
            clast += [0] * (n - 1) + [1]
    return np.asarray(cpos, np.int32), np.asarray(clast, np.int32)


def _fwd_map(col):
    return lambda g, cp, cl: (g, col)


def _bwd_map(n_chunks, col):
    return lambda g, cp, cl: (n_chunks - 1 - g, col)


def _walk_kernel(bodies, tags, n_scratch, cps, cpos_ref, clast_ref, *refs):
    refs = list(refs)
    ins = [[refs.pop(0) for _ in t] for t in tags]
    outs = [[refs.pop(0) for _ in range(2)] for _ in bodies]
    scratch = [[refs.pop(0) for _ in range(k)] for k in n_scratch]

    @pl.when(pl.program_id(0) == 0)
    def _():
        for part in scratch:
            for r in part:
                r[...] = jnp.zeros_like(r)

    n_chunks = pl.num_programs(0) * cps
    for j in range(cps):
        g = pl.program_id(0) * cps + j
        gb = n_chunks - 1 - g
        rows = {"f": pl.ds(j * CH, CH), "b": pl.ds((cps - 1 - j) * CH, CH)}

        def view(ref, tag):
            return ref if tag == "c" or cps == 1 else ref.at[rows[tag]]

        for body, i, tg, o, s in zip(bodies, ins, tags, outs, scratch):
            body(cpos_ref, clast_ref, g, gb, *[view(r, t) for r, t in zip(i, tg)],
                 view(o[0], "f"), view(o[1], "b"), *s)


def _chunk_walk(parts, cpos, clast, t, name, cps=1):
    n = t // (CH * cps)
    blk = (CH * cps, GROUP_W)
    bodies = [p[0] for p in parts]
    grid_spec = pltpu.PrefetchScalarGridSpec(
        num_scalar_prefetch=2, grid=(n,), in_specs=[s for p in parts for s in p[1]],
        out_specs=[pl.BlockSpec(blk, _fwd_map(0)), pl.BlockSpec(blk, _bwd_map(n, 0))] * len(parts),
        scratch_shapes=[s for p in parts for s in p[3]])
    outs = pl.pallas_call(
        functools.partial(_walk_kernel, bodies, [p[4] for p in parts], [len(p[3]) for p in parts], cps),
        grid_spec=grid_spec, out_shape=[jax.ShapeDtypeStruct((t, GROUP_W), BF16)] * (2 * len(parts)),
        compiler_params=_params(1), name=name)(cpos, clast, *[a for p in parts for a in p[2]])
    return [outs[2 * i:2 * i + 2] for i in range(len(parts))]


def _row_valid(first):
    row = lax.broadcasted_iota(jnp.int32, (CH, 1), 0)
    return row >= jnp.where(first, PAD_ROWS, 0)


def _ret_log_gamma():
    return np.log1p(-np.exp2(-5.0 - np.arange(N_HEADS, dtype=np.float64)))


def _ret_consts():
    lg = _ret_log_gamma()
    pos = np.arange(CH, dtype=np.float64)
    rel = pos[:, None] - pos[None, :]
    dmat = np.zeros((2, N_HEADS, CH, CH), np.float64)
    rows = np.zeros((4, CH, GROUP_W), np.float64)
    for h in range(N_HEADS):
        dmat[0, h] = np.where(rel >= 0, np.exp(np.maximum(rel, 0.0) * lg[h]), 0.0)
        dmat[1, h] = np.where(rel < 0, np.exp(np.maximum(-rel, 0.0) * lg[h]), 0.0)
        sl = slice(h * HEAD_D, (h + 1) * HEAD_D)
        rows[0, :, sl] = np.exp((pos + 1.0) * lg[h])[:, None]
        rows[1, :, sl] = np.exp((CH - 1.0 - pos) * lg[h])[:, None]
        rows[2, :, sl] = np.exp((CH - pos) * lg[h])[:, None]
        rows[3, :, sl] = np.exp(pos * lg[h])[:, None]
    return jnp.asarray(dmat, F32), jnp.asarray(rows, F32)


def _ret_direction(q_ref, k_ref, v_ref, cos_ref, sin_ref, dmat_ref, rin_ref, rout_ref, s_ref, o_ref,
                   reset, first):
    valid = _row_valid(first)
    cos = cos_ref[...]
    sin = sin_ref[...]
    chunk_decay = np.exp(CH * _ret_log_gamma())
    for h in range(N_HEADS):
        sl = slice(h * HEAD_D, (h + 1) * HEAD_D)
        q = q_ref[:, sl]
        k = k_ref[:, sl]
        v = v_ref[:, sl]
        q = q * cos + pltpu.roll(q, HEAD_D // 2, 1) * sin
        k = (k * cos + pltpu.roll(k, HEAD_D // 2, 1) * sin) * (HEAD_D ** -0.5)
        k = jnp.where(valid, k, 0.0)
        scores = _dot_nt(q, k) * dmat_ref[h]
        state = jnp.where(reset, 0.0, s_ref[h])
        o_ref[:, sl] = _dot(jnp.concatenate([scores, q * rin_ref[:, sl]], axis=1),
                            jnp.concatenate([v, state], axis=0)).astype(o_ref.dtype)
        s_ref[h] = float(chunk_decay[h]) * state + _dot_tn(k * rout_ref[:, sl], v)


def _ret_kernel(cpos_ref, clast_ref, g, gb, qf, kf, vf, cosf, sinf, qb, kb, vb, cosb, sinb, dmat_ref, rows_ref,
                of_ref, ob_ref, sf_ref, sb_ref):
    _ret_direction(qf, kf, vf, cosf, sinf, dmat_ref.at[0], rows_ref.at[0], rows_ref.at[1], sf_ref, of_ref,
                   cpos_ref[g] == 0, cpos_ref[g] == 0)
    _ret_direction(qb, kb, vb, cosb, sinb, dmat_ref.at[1], rows_ref.at[2], rows_ref.at[3], sb_ref, ob_ref,
                   clast_ref[gb] == 1, cpos_ref[gb] == 0)


def _rope_tables(n_rows):
    inv = ROPE_BASE ** (-jnp.arange(0, HEAD_D, 2, dtype=jnp.float32) / HEAD_D)
    pos = jnp.maximum(jnp.arange(n_rows, dtype=jnp.float32) - PAD_ROWS, 0.0)
    ang = pos[:, None] * inv[None, :]
    cos, sin = jnp.cos(ang), jnp.sin(ang)
    return jnp.concatenate([cos, cos], axis=1), jnp.concatenate([-sin, sin], axis=1)


def _retention(z, max_rows):
    t = z.shape[0]
    n = t // CH
    cos2, sin2 = _rope_tables(max_rows)
    dmat, rows = _ret_consts()
    blk = (CH, GROUP_W)
    in_specs = []
    for mk, pm in ((_fwd_map, lambda g, cp, cl: (cp[g], 0)),
                   (functools.partial(_bwd_map, n), lambda g, cp, cl: (cp[n - 1 - g], 0))):
        in_specs += [pl.BlockSpec(blk, mk(0)), pl.BlockSpec(blk, mk(1)), pl.BlockSpec(blk, mk(2)),
                     pl.BlockSpec((CH, HEAD_D), pm), pl.BlockSpec((CH, HEAD_D), pm)]
    in_specs += [pl.BlockSpec(dmat.shape, lambda g, cp, cl: (0, 0, 0, 0)),
                 pl.BlockSpec(rows.shape, lambda g, cp, cl: (0, 0, 0))]
    return (_ret_kernel, in_specs, [z, z, z, cos2, sin2, z, z, z, cos2, sin2, dmat, rows],
            [pltpu.VMEM((N_HEADS, HEAD_D, HEAD_D), F32)] * 2, "fffffbbbbbcc")


HALO = SUBLANES


def _shift_rows(x, s, fill, reverse):
    row = lax.broadcasted_iota(jnp.int32, (CH, 1), 0)
    if reverse:
        return jnp.where(row < CH - s, pltpu.roll(x, CH - s, 0), fill)
    return jnp.where(row >= s, pltpu.roll(x, s, 0), fill)


def _lru_direction(x_ref, prev_ref, next_ref, convw_ref, convb_ref, wg_ref, bias_ref, lam_ref, ext_ref,
                   carry_ref, o_ref, reset, first, last, reverse):
    valid = _row_valid(first)
    ext_ref[HALO:HALO + CH, :] = jnp.where(valid, x_ref[...], 0.0)
    ext_ref[0:HALO, :] = jnp.where(first, 0.0, prev_ref[...])
    ext_ref[HALO + CH:, :] = jnp.where(last, 0.0, next_ref[...])
    xc = convb_ref[...] + ext_ref[HALO - 2:HALO - 2 + CH, :] * convw_ref[0:1, :]
    for tap in range(1, 4):
        xc = xc + ext_ref[HALO - 2 + tap:HALO - 2 + tap + CH, :] * convw_ref[tap:tap + 1, :]

    log_sig_lam = -_softplus(-lam_ref[...])
    parts_a, parts_u = [], []
    for grp in range(GROUP_W // LANES):
        sl = slice(grp * LANES, (grp + 1) * LANES)
        xg = xc[:, sl]
        pre = _dot(xg, wg_ref[grp])
        r = _sigmoid(pre[:, :LANES] + bias_ref[0:1, sl])
        i = _sigmoid(pre[:, LANES:] + bias_ref[1:2, sl])
        a = jnp.exp(LRU_C * log_sig_lam[:, sl] * r)
        u = jnp.sqrt(1.0 - a * a) * (i * xg)
        parts_a.append(a)
        parts_u.append(jnp.where(valid, u, 0.0))
    a = jnp.concatenate(parts_a, axis=1)
    u = jnp.concatenate(parts_u, axis=1)

    n_blk = CH // SUBLANES
    a = a.reshape(n_blk, SUBLANES, GROUP_W)
    u = u.reshape(n_blk, SUBLANES, GROUP_W)
    row_in_blk = lax.broadcasted_iota(jnp.int32, (1, SUBLANES, 1), 1)
    s = 1
    while s < SUBLANES:
        keep = (row_in_blk < SUBLANES - s) if reverse else (row_in_blk >= s)
        shift = SUBLANES - s if reverse else s
        u = u + a * jnp.where(keep, pltpu.roll(u, shift, 1), 0.0)
        a = a * jnp.where(keep, pltpu.roll(a, shift, 1), 1.0)
        s *= 2
    edge = 0 if reverse else SUBLANES - 1
    carry = jnp.where(reset, 0.0, carry_ref[...])
    blocks = [None] * n_blk
    for blk in (reversed(range(n_blk)) if reverse else range(n_blk)):
        blocks[blk] = u[blk] + a[blk] * carry
        carry = blocks[blk][edge:edge + 1, :]
    o_ref[...] = jnp.concatenate(blocks, axis=0).astype(o_ref.dtype)
    carry_ref[...] = carry


def _lru_kernel(cpos_ref, clast_ref, g, gb, xf, pf, nf, xb, pb, nb, convw_ref, convb_ref, wg_ref, bias_ref,
                lam_ref, of_ref, ob_ref, extf_ref, extb_ref, cf_ref, cb_ref):
    _lru_direction(xf, pf, nf, convw_ref, convb_ref, wg_ref.at[0], bias_ref.at[0], lam_ref.at[0], extf_ref,
                   cf_ref, of_ref, cpos_ref[g] == 0, cpos_ref[g] == 0, clast_ref[g] == 1, False)
    _lru_direction(xb, pb, nb, convw_ref, convb_ref, wg_ref.at[1], bias_ref.at[1], lam_ref.at[1], extb_ref,
                   cb_ref, ob_ref, clast_ref[gb] == 1, cpos_ref[gb] == 0, clast_ref[gb] == 1, True)


def _lru_gate_weights(wa, wx):
    per = LANES // LRU_BW

    def block_diag(w):
        w = w.reshape(2, LRU_BLOCKS // per, per, LRU_BW, LRU_BW)
        eye = jnp.eye(per, dtype=w.dtype)
        return jnp.einsum("dgpij,pq->dgpiqj", w, eye).reshape(2, LRU_BLOCKS // per, LANES, LANES)

    return jnp.concatenate([block_diag(wa), block_diag(wx)], axis=-1).astype(BF16)


def _rglru(z, col, conv_w, conv_b, wa, ba, wx, bx, lam):
    t = z.shape[0]
    n = t // CH
    per = CH // HALO
    n_halo = t // HALO
    blk = (CH, GROUP_W)
    hblk = (HALO, GROUP_W)
    wg = _lru_gate_weights(wa, wx)
    bias = jnp.stack([ba, bx], axis=1)
    lam = lam.reshape(2, 1, GROUP_W)

    def prev_f(g, cp, cl):
        return (jnp.maximum(g * per - 1, 0), col)

    def next_f(g, cp, cl):
        return (jnp.minimum((g + 1) * per, n_halo - 1), col)

    def prev_b(g, cp, cl):
        return (jnp.maximum((n - 1 - g) * per - 1, 0), col)

    def next_b(g, cp, cl):
        return (jnp.minimum((n - g) * per, n_halo - 1), col)

    full = lambda a: pl.BlockSpec(a.shape, lambda g, cp, cl: (0,) * a.ndim)
    conv_b2 = conv_b.reshape(1, GROUP_W)
    in_specs = [pl.BlockSpec(blk, _fwd_map(col)), pl.BlockSpec(hblk, prev_f), pl.BlockSpec(hblk, next_f),
                pl.BlockSpec(blk, _bwd_map(n, col)), pl.BlockSpec(hblk, prev_b), pl.BlockSpec(hblk, next_b),
                full(conv_w), full(conv_b2), full(wg), full(bias), full(lam)]
    return (_lru_kernel, in_specs, [z, z, z, z, z, z, conv_w, conv_b2, wg, bias, lam],
            [pltpu.VMEM((CH + 2 * HALO, GROUP_W), F32)] * 2 + [pltpu.VMEM((1, GROUP_W), F32)] * 2,
            "fffbbbccccc")


def _tri_consts():
    pos = np.arange(CH)
    lower = (pos[:, None] >= pos[None, :]).astype(np.float32)
    ops = np.stack([lower, lower.T])
    return jnp.asarray(np.tile(ops, (1, 1, PREFIX_TERMS)), BF16)


def _hg_masks():
    pos = np.arange(CH)
    same_blk = pos[:, None] // SUB == pos[None, :] // SUB
    same_grp = pos[:, None] // GRP == pos[None, :] // GRP
    return jnp.asarray(np.stack([same_blk, same_grp]).astype(np.float32))


def _hg_direction(q_ref, f_ref, v_ref, lb, cum_ref, mask_ref, st_ref, o_ref, reset, first, reverse):
    valid = _row_valid(first)
    sub_i =lax.broadcasted_iota(jnp.int32, (SUB, HEAD_D), 0)
    lane_j = lax.broadcasted_iota(jnp.int32, (SUB, HEAD_D), 1) & (SUB - 1)
    causal = (sub_i <= lane_j) if reverse else (sub_i >= lane_j)
    pick = [causal & (lane_j == j) for j in range(SUB)]
    cum_op = cum_ref[...]
    same_blk = mask_ref[0]
    same_grp = mask_ref[1]
    zero = jnp.zeros((SUB, HEAD_D), F32)
    n_blk = CH // SUB
    per = GRP // SUB
    n_grp = CH // GRP

    def edge(unit, idx):
        return unit * idx if reverse else unit * (idx + 1) - 1

    def split_product(b, qs, ks, bs, pieces):
        q_slabs, k_slabs = [], []
        for ref, q_active, k_active in pieces:
            qp, kp = [], []
            for i in range(n_blk):
                r = ref(i)
                qp.append(qs[i] * jnp.exp2(bs[i] - b[r:r + 1, :]) if q_active(i) else zero)
                kp.append(ks[i] * jnp.exp2(b[r:r + 1, :] - bs[i]) if k_active(i) else zero)
            q_slabs.append(jnp.concatenate(qp, axis=0))
            k_slabs.append(jnp.concatenate(kp, axis=0))
        return _dot_nt(jnp.concatenate(q_slabs, axis=1), jnp.concatenate(k_slabs, axis=1))

    level1 = []
    for c in (range(1, per) if reverse else range(per - 1)):
        level1.append((lambda i, c=c: edge(SUB, (i // per) * per + c),
                       (lambda i, c=c: i % per < c) if reverse else (lambda i, c=c: i % per > c),
                       lambda i, c=c: i % per == c))
    level2 = []
    for gc in (range(1, n_grp) if reverse else range(n_grp - 1)):
        level2.append((lambda i, gc=gc: edge(GRP, gc),
                       (lambda i, gc=gc: i // per < gc) if reverse else (lambda i, gc=gc: i // per > gc),
                       lambda i, gc=gc: i // per == gc))

    for h in range(N_HEADS):
        sl = slice(h * HEAD_D, (h + 1) * HEAD_D)
        lbh = lb[:, sl]
        q = _silu(q_ref[:, sl])
        f = lbh + (1.0 - lbh) * _sigmoid(f_ref[:, sl])
        k = jnp.where(valid, 1.0 - f, 0.0)
        v = v_ref[:, sl]
        b = _prefix_dot(cum_op, jnp.log(f)) * LOG2_E
        qs = [q[i * SUB:(i + 1) * SUB] for i in range(n_blk)]
        ks = [k[i * SUB:(i + 1) * SUB] for i in range(n_blk)]
        bs = [b[i * SUB:(i + 1) * SUB] for i in range(n_blk)]

        tiles = [qs[blk] * jnp.exp2(jnp.minimum(bs[blk] - bs[blk][j:j + 1, :], 0.0))
                 for blk in range(n_blk) for j in range(SUB)]
        pair = _dot_nt(jnp.concatenate(tiles, axis=0), k)
        rows = []
        for blk in range(n_blk):
            base = blk * SUB * SUB
            a_blk = jnp.where(pick[0], pair[base:base + SUB], 0.0)
            for j in range(1, SUB):
                a_blk = a_blk + jnp.where(pick[j], pair[base + j * SUB:base + (j + 1) * SUB], 0.0)
            rows.append(a_blk)
        scores = jnp.concatenate(rows, axis=0) * same_blk
        scores = scores + split_product(b, qs, ks, bs, level1) * same_grp
        scores = scores + split_product(b, qs, ks, bs, level2)

        state = jnp.where(reset, 0.0, st_ref[h])
        o_ref[:, sl] = (_dot(scores, v) + _dot_nt(q * jnp.exp2(b), state)).astype(o_ref.dtype)
        b_tot = b[0:1, :] if reverse else b[CH - 1:CH, :]
        st_ref[h] = state * jnp.exp2(b_tot) + _dot_tn(v, k * jnp.exp2(b_tot - b))


def _hg_lower_bound(lower_ref, layer):
    low = lower_ref[...]
    e = jnp.exp(low - jnp.max(low, axis=0, keepdims=True))
    soft = e / jnp.sum(e, axis=0, keepdims=True)
    lb = jnp.zeros((1, GROUP_W), F32)
    for l in range(1, layer + 1):
        lb = lb + soft[l:l + 1, :]
    return lb


def _hg_kernel(layer, cpos_ref, clast_ref, g, gb, qf, ff, vf, qb, fb, vb, lower_ref, cum_ref, mask_ref,
               of_ref, ob_ref, sf_ref, sb_ref):
    lb =_hg_lower_bound(lower_ref, layer)
    _hg_direction(qf, ff, vf, lb, cum_ref.at[0], mask_ref, sf_ref, of_ref,
                  cpos_ref[g] == 0, cpos_ref[g] == 0, False)
    _hg_direction(qb, fb, vb, lb, cum_ref.at[1], mask_ref, sb_ref, ob_ref,
                  clast_ref[gb] == 1, cpos_ref[gb] == 0, True)


def _hgrn2(z, lower, layer, cps):
    t = z.shape[0]
    n = t // (CH * cps)
    blk = (CH * cps, GROUP_W)
    cum = _tri_consts()
    masks = _hg_masks()
    full = lambda a: pl.BlockSpec(a.shape, lambda g, cp, cl: (0,) * a.ndim)
    in_specs = [pl.BlockSpec(blk, _fwd_map(0)), pl.BlockSpec(blk, _fwd_map(1)), pl.BlockSpec(blk, _fwd_map(3)),
                pl.BlockSpec(blk, _bwd_map(n, 0)), pl.BlockSpec(blk, _bwd_map(n, 2)),
                pl.BlockSpec(blk, _bwd_map(n, 3)), full(lower), full(cum), full(masks)]
    return (functools.partial(_hg_kernel, layer), in_specs, [z, z, z, z, z, z, lower, cum, masks],
            [pltpu.VMEM((N_HEADS, HEAD_D, HEAD_D), F32)] * 2, "fffbbbccc")


def _log_sigmoid(x):
    return jnp.minimum(x, 0.0) - jnp.log(1.0 + jnp.exp(-jnp.abs(x)))


def _ml_direction(q_ref, k_ref, v_ref, gi_ref, gf_ref, bias_ref, cum_op, spread_ref, s_ref, m_ref, o_ref,
                  reset, first, d, reverse):
    valid = _row_valid(first)
    cum = _prefix_dot(cum_op, _log_sigmoid(gf_ref[...] + bias_ref[1]))
    a = jnp.where(valid, gi_ref[...] + bias_ref[0] - cum, NEG_BIG)
    run = a
    s = 1
    while s < CH:
        run = jnp.maximum(run, _shift_rows(run, s, NEG_BIG, reverse))
        s *= 2
    m_st = jnp.where(reset, NEG_BIG, m_ref[...])
    mx = jnp.maximum(m_st, run)
    edge = 0 if reverse else CH - 1
    mx_last = mx[edge:edge + 1, :]
    m_ref[...] = cum[edge:edge + 1, :] + mx_last
    decay = jnp.exp(m_st - mx_last)

    stacked = jnp.concatenate([mx, m_st - mx, -(cum + mx), a - mx_last], axis=0) * LOG2_E
    spread_b = _spread_dot(stacked, spread_ref[...])
    mx_b, inter_b, floor_b, end_b = (spread_b[i * CH:(i + 1) * CH] for i in range(4))
    a_t = (a * LOG2_E).T

    row = lax.broadcasted_iota(jnp.int32, (CH, CH), 0)
    col = lax.broadcasted_iota(jnp.int32, (CH, CH), 1)
    causal = (col >= row) if reverse else (col <= row)
    ones = jnp.ones((CH, HEAD_D), F32)
    for h in range(N_HEADS):
        sl = slice(h * HEAD_D, (h + 1) * HEAD_D)
        x = d * N_HEADS + h
        q = q_ref[:, sl]
        k = jnp.where(valid, k_ref[:, sl] * (HEAD_D ** -0.5), 0.0)
        v_ext = jnp.concatenate([v_ref[:, sl], ones], axis=1)
        w = jnp.where(causal, jnp.exp2(jnp.minimum(a_t[x:x + 1, :] - mx_b[:, sl], 0.0)), 0.0)
        qk = _dot_nt(q, k) * w
        s_inter = jnp.exp2(jnp.minimum(inter_b[:, sl], 0.0))
        state = jnp.where(reset, 0.0, s_ref[h])
        ext = _dot(jnp.concatenate([qk, q * s_inter], axis=1), jnp.concatenate([v_ext, state], axis=0))
        o_ref[:, sl] = (ext[:, :HEAD_D] / jnp.maximum(jnp.abs(ext[:, HEAD_D:]), jnp.exp2(floor_b[:, sl]))
                        ).astype(o_ref.dtype)
        kw = k * jnp.exp2(jnp.minimum(end_b[:, sl], 0.0))
        s_ref[h] = decay[:, x:x + 1] * state + _dot_tn(kw, v_ext)


def _ml_kernel(cpos_ref, clast_ref, g, gb, qf, kf, vf, gif, gff, qb, kb, vb, gib, gfb, bias_ref, cum_ref,
               spread_ref, of_ref, ob_ref, sfs, mfs, sbs, mbs):
    _ml_direction(qf, kf, vf, gif, gff, bias_ref, cum_ref[0], spread_ref.at[0], sfs, mfs, of_ref,
                  cpos_ref[g] == 0, cpos_ref[g] == 0, 0, False)
    _ml_direction(qb, kb, vb, gib, gfb, bias_ref, cum_ref[1], spread_ref.at[1], sbs, mbs, ob_ref,
                  clast_ref[gb] == 1, cpos_ref[gb] == 0, 1, True)


def _ml_spread():
    spread = np.zeros((2, LANES, GROUP_W), np.float32)
    for d in range(2):
        for h in range(N_HEADS):
            spread[d, d * N_HEADS + h, h * HEAD_D:(h + 1) * HEAD_D] = 1.0
    return jnp.asarray(np.tile(spread, (1, SPREAD_TERMS, 1)), BF16)


def _mlstm(z, gates, ml_bi, ml_bf, cps):
    t = z.shape[0]
    n = t // (CH * cps)
    blk = (CH * cps, GROUP_W)
    gblk = (CH * cps, LANES)
    cum = _tri_consts()
    spread = _ml_spread()
    pad = jnp.zeros((LANES - 2 * N_HEADS,), F32)
    bias = jnp.stack([jnp.concatenate([ml_bi.reshape(-1), pad]),
                      jnp.concatenate([ml_bf.reshape(-1), pad])]).reshape(2, 1, LANES)
    full = lambda a: pl.BlockSpec(a.shape, lambda g, cp, cl: (0,) * a.ndim)
    in_specs = [pl.BlockSpec(blk, _fwd_map(5)), pl.BlockSpec(blk, _fwd_map(6)), pl.BlockSpec(blk, _fwd_map(7)),
                pl.BlockSpec(gblk, _fwd_map(0)), pl.BlockSpec(gblk, _fwd_map(1)),
                pl.BlockSpec(blk, _bwd_map(n, 5)), pl.BlockSpec(blk, _bwd_map(n, 6)),
                pl.BlockSpec(blk, _bwd_map(n, 7)),
                pl.BlockSpec(gblk, _bwd_map(n, 0)), pl.BlockSpec(gblk, _bwd_map(n, 1)),
                full(bias), full(cum), full(spread)]
    state = [pltpu.VMEM((N_HEADS, HEAD_D, 2 * HEAD_D), F32), pltpu.VMEM((1, LANES), F32)]
    return (_ml_kernel, in_specs, [z, z, z, gates, gates, z, z, z, gates, gates, bias, cum, spread], state * 2,
            "fffffbbbbbccc")


def _head_norm(x, center):
    outs = []
    for h in range(N_HEADS):
        xh = x[:, h * HEAD_D:(h + 1) * HEAD_D]
        if center:
            xh = xh - jnp.mean(xh, axis=1, keepdims=True)
        outs.append(xh * lax.rsqrt(jnp.mean(xh * xh, axis=1, keepdims=True) + EPS))
    return jnp.concatenate(outs, axis=1)


def _layer_norm(x, g, b):
    xc = x - jnp.mean(x, axis=1, keepdims=True)
    return xc * lax.rsqrt(jnp.mean(xc * xc, axis=1, keepdims=True) + EPS) * g + b


def _mix_rows(alpha, even, af, ab, ag, bf, bb, bg, h_ref, w_ref, lng_ref, lnb_ref):
    a = af[...].astype(F32) + ab[...].astype(F32)
    b = bf[...].astype(F32) + bb[...].astype(F32)
    if even:
        a = _head_norm(a, False) * _silu(ag[...])
        b = b * _gelu_tanh(bg[...])
    else:
        a = _head_norm(a, False) * _silu(ag[...])
        b = _head_norm(b, True) * _sigmoid(bg[...])
    mix = _dot(a, w_ref[0:GROUP_W, :]) + _dot(b, w_ref[GROUP_W:, :])
    return _layer_norm(alpha * h_ref[...] + mix, lng_ref[...], lnb_ref[...])


N_MIX_OPERANDS = 10


def _mix_specs(tm, col_a, col_b, index):
    half = lambda c: pl.BlockSpec((tm, GROUP_W), lambda *g: (index(*g), c))
    rows = pl.BlockSpec((tm, D_MODEL), lambda *g: (index(*g), 0))
    vec = pl.BlockSpec((1, D_MODEL), lambda *g: (0, 0))
    return [half(0), half(0), half(col_a), half(0), half(0), half(col_b), rows,
            pl.BlockSpec((D_MODEL, D_MODEL), lambda *g: (0, 0)), vec, vec]


def _mix_ffn_kernel(alpha, *refs):
    mix_refs = refs[:N_MIX_OPERANDS]
    wg_ref, wu_ref, wo_ref, lng_ref, lnb_ref, o_ref, h1_ref, acc_ref = refs[N_MIX_OPERANDS:]
    f = pl.program_id(1)

    @pl.when(f == 0)
    def _():
        h1_ref[...] = _mix_rows(alpha, True, *mix_refs)
        acc_ref[...] = jnp.zeros_like(acc_ref)

    x = h1_ref[...].astype(BF16)
    act = _silu(_dot(x, wg_ref[...])) * _dot(x, wu_ref[...])
    acc_ref[...] += _dot(act, wo_ref[...])

    @pl.when(f == pl.num_programs(1) - 1)
    def _():
        o_ref[...] = _layer_norm(alpha * h1_ref[...] + acc_ref[...], lng_ref[...], lnb_ref[...])


def _mix_ffn(alpha, af, ab, bf, bb, z, col_a, col_b, h, w_out, ln1_g, ln1_b, wi, wo, ln2_g, ln2_b):
    t = h.shape[0]
    d_ff = wo.shape[0]
    tm = _pick_tile(t, 640)
    tf = _pick_tile(d_ff, 1536)
    nf = d_ff // tf
    rows = pl.BlockSpec((tm, D_MODEL), lambda i, f: (i, 0))
    vec = pl.BlockSpec((1, D_MODEL), lambda i, f: (0, 0))
    return pl.pallas_call(
        functools.partial(_mix_ffn_kernel, alpha), grid=(t // tm, nf),
        in_specs=_mix_specs(tm, col_a, col_b, lambda i, f: i) + [
            pl.BlockSpec((D_MODEL, tf), lambda i, f: (0, f)),
            pl.BlockSpec((D_MODEL, tf), lambda i, f: (0, f + nf)),
            pl.BlockSpec((tf, D_MODEL), lambda i, f: (f, 0)), vec, vec],
        out_specs=rows, out_shape=jax.ShapeDtypeStruct((t, D_MODEL), F32),
        scratch_shapes=[pltpu.VMEM((tm, D_MODEL), F32), pltpu.VMEM((tm, D_MODEL), F32)],
        compiler_params=_params(2), name="mix_ffn")(
            af, ab, z, bf, bb, z, h, w_out, ln1_g.reshape(1, -1), ln1_b.reshape(1, -1),
            wi, wi, wo, ln2_g.reshape(1, -1), ln2_b.reshape(1, -1))


EXPERT_TILE = 1024
DMA_UNROLL = 8
DMA_PRIORITIES = 2


def _mix_route_kernel(alpha, pad_starts, *refs):
    mix_refs = refs[:N_MIX_OPERANDS]
    router_ref, tri_ref, o_ref, meta_ref, gate_ref, cnt_ref, carry_ref = refs[N_MIX_OPERANDS:]

    @pl.when(pl.program_id(0) == 0)
    def _():
        carry_ref[...] = jnp.zeros_like(carry_ref)

    h1 = _mix_rows(alpha, False, *mix_refs)
    o_ref[...] = h1
    tm = h1.shape[0]
    row = pl.program_id(0) * tm + lax.broadcasted_iota(jnp.int32, (tm, 1), 0)
    is_pad = row < 0
    for s in pad_starts:
        is_pad = is_pad | ((row >= s) & (row < s + PAD_ROWS))
    is_token = jnp.logical_not(is_pad)

    logits = _dot_16bit(h1, router_ref[...])
    lane = lax.broadcasted_iota(jnp.int32, logits.shape, 1)
    logits = jnp.where(lane < N_EXP, logits, -jnp.inf)
    top1 = jnp.max(logits, axis=1, keepdims=True)
    idx1 = jnp.min(jnp.where(logits == top1, lane, LANES), axis=1, keepdims=True)
    rest = jnp.where(lane == idx1, -jnp.inf, logits)
    top2 = jnp.max(rest, axis=1, keepdims=True)
    idx2 = jnp.min(jnp.where(rest == top2, lane, LANES), axis=1, keepdims=True)
    g2 = jnp.exp(top2 - top1)
    denom = 1.0 + g2
    hit1 = lane == idx1
    hit2 = lane == idx2
    both = jnp.where(is_token, jnp.where(hit1, 1.0, 0.0) + jnp.where(hit2, 1.0, 0.0), 0.0)
    prefix = _dot(tri_ref[...], both) + carry_ref[...]
    rank1 = jnp.sum(jnp.where(hit1, prefix, 0.0), axis=1, keepdims=True).astype(jnp.int32)
    rank2 = jnp.sum(jnp.where(hit2, prefix, 0.0), axis=1, keepdims=True).astype(jnp.int32)
    carry_ref[...] += jnp.sum(both, axis=0, keepdims=True)
    cnt_ref[...] = carry_ref[...]
    meta_ref[...] = jnp.where(lane == 0, idx1, jnp.where(lane == 1, idx2, jnp.where(
        lane == 2, rank1, jnp.where(lane == 3, rank2, jnp.where(is_token & (lane == 4), 1, 0)))))
    gate_ref[...] = jnp.where(lane == 0, 1.0 / denom, jnp.where(lane == 1, g2 / denom, 0.0))


def _mix_route(alpha, af, ab, bf, bb, z, col_a, col_b, h, w_out, ln_g, ln_b, router, pad_starts):
    t = h.shape[0]
    tm = _pick_tile(t, 640)
    router_p = _stack_weight_terms(jnp.pad(router, ((0, 0), (0, LANES - router.shape[1]))))
    pos = np.arange(tm)
    tri = jnp.asarray(pos[:, None] > pos[None, :], BF16)
    rows = lambda w: pl.BlockSpec((tm, w), lambda i: (i, 0))
    const = lambda a: pl.BlockSpec(a.shape, lambda i: (0, 0))
    return pl.pallas_call(
        functools.partial(_mix_route_kernel, alpha, pad_starts), grid=(t // tm,),
        in_specs=_mix_specs(tm, col_a, col_b, lambda i: i) + [const(router_p), const(tri)],
        out_specs=[rows(D_MODEL), rows(LANES), rows(LANES), pl.BlockSpec((1, LANES), lambda i: (0, 0))],
        out_shape=[jax.ShapeDtypeStruct((t, D_MODEL), F32), jax.ShapeDtypeStruct((t, LANES), jnp.int32),
                   jax.ShapeDtypeStruct((t, LANES), F32), jax.ShapeDtypeStruct((1, LANES), F32)],
        scratch_shapes=[pltpu.VMEM((1, LANES), F32)],
        compiler_params=_params(1), name="mix_route")(
            af, ab, z, bf, bb, z, h, w_out, ln_g.reshape(1, -1), ln_b.reshape(1, -1), router_p, tri)


def _dispatch_plan(meta, counts, t):
    cnt = counts[0, :N_EXP].astype(jnp.int32)
    padded = ((cnt + EXPERT_TILE - 1) // EXPERT_TILE) * EXPERT_TILE
    ends = jnp.cumsum(padded)
    off = ends - padded

    def base(e):
        return sum(jnp.where(e == i, off[i], 0) for i in range(N_EXP))

    is_token = meta[:, 4] > 0
    n_tiles = -(-2 * t // EXPERT_TILE) + N_EXP
    spare = n_tiles * EXPERT_TILE + 2 * (jnp.cumsum(jnp.logical_not(is_token).astype(jnp.int32)) - 1)
    pos1 = (base(meta[:, 0]) + meta[:, 2]).astype(jnp.int32)
    pos2 = (base(meta[:, 1]) + meta[:, 3]).astype(jnp.int32)
    scatter = (jnp.where(is_token, pos1, spare), jnp.where(is_token, pos2, spare + 1))
    gather = (jnp.where(is_token, pos1, 0), jnp.where(is_token, pos2, 0))
    starts = jnp.arange(n_tiles, dtype=jnp.int32) * EXPERT_TILE
    tile_expert = jnp.minimum(jnp.sum(starts[:, None] >= ends[None, :], axis=1), N_EXP - 1).astype(jnp.int32)
    n_active = (ends[-1] // EXPERT_TILE).astype(jnp.int32).reshape(1)
    return ends.astype(jnp.int32), scatter, gather, tile_expert, n_active, n_tiles


assert D_MODEL == SUBLANES * LANES


def _store_token_tiles(ref, x):
    n = x.shape[0]
    for s in range(SUBLANES):
        ref[pl.ds(s, n, stride=SUBLANES), :] = x[:, s * LANES:(s + 1) * LANES]


def _load_token_tiles(ref, n):
    return jnp.concatenate([ref[pl.ds(s, n, stride=SUBLANES), :] for s in range(SUBLANES)], axis=1)


def _tile_rows(i):
    return pl.ds(pl.multiple_of(i * SUBLANES, SUBLANES), SUBLANES)


def _dispatch_kernel(n_tiles, ends_ref, pos1_ref, pos2_ref, h_ref, xs_ref, tok_ref, zero_ref, sem, zero_sem):
    n = h_ref.shape[0]

    @pl.when(pl.program_id(0) == 0)
    def _():
        zero_ref[...] = jnp.zeros_like(zero_ref)

        def clear_last_tile(e):
            first_row = pl.multiple_of((ends_ref[e] - EXPERT_TILE) * SUBLANES, SUBLANES)
            return pltpu.make_async_copy(zero_ref, xs_ref.at[pl.ds(first_row, EXPERT_TILE * SUBLANES)], zero_sem)

        def has_tiles(e):
            return ends_ref[e] > (ends_ref[e - 1] if e else 0)

        for e in range(N_EXP):
            pl.when(has_tiles(e))(lambda e=e: clear_last_tile(e).start())
        for e in range(N_EXP):
            pl.when(has_tiles(e))(lambda e=e: clear_last_tile(e).wait())

        def clear_tile(i):
            first_row = pl.multiple_of(i * (EXPERT_TILE * SUBLANES), SUBLANES)
            return pltpu.make_async_copy(zero_ref, xs_ref.at[pl.ds(first_row, EXPERT_TILE * SUBLANES)], zero_sem)

        used = ends_ref[N_EXP - 1] // EXPERT_TILE
        lax.fori_loop(used, n_tiles, lambda i, c: (clear_tile(i).start(), c)[1], 0)
        lax.fori_loop(used, n_tiles, lambda i, c: (clear_tile(i).wait(), c)[1], 0)

    g = pl.program_id(0)
    slot = g % 2
    _store_token_tiles(tok_ref.at[slot], h_ref[...])

    def tile_copy(r, p):
        return pltpu.make_async_copy(tok_ref.at[slot].at[_tile_rows(r)], xs_ref.at[_tile_rows(p)], sem.at[slot])

    def start(pair, c):
        for prio in range(DMA_PRIORITIES):
            r = pair * DMA_PRIORITIES + prio
            tile_copy(r, pos1_ref[0, 0, r]).start(priority=prio)
            tile_copy(r, pos2_ref[0, 0, r]).start(priority=prio)
        return c

    lax.fori_loop(0, n // DMA_PRIORITIES, start, 0, unroll=DMA_UNROLL // DMA_PRIORITIES)

    def wait_slot(s):
        all_rows = pltpu.make_async_copy(tok_ref.at[s], xs_ref.at[pl.ds(0, n * SUBLANES)], sem.at[s])
        all_rows.wait()
        all_rows.wait()

    pl.when(g > 0)(lambda: wait_slot(1 - slot))
    pl.when(g == pl.num_programs(0) - 1)(lambda: wait_slot(slot))


def _dispatch(h, ends, pos1, pos2, n_tiles, n_spare):
    t = h.shape[0]
    n_rows = n_tiles * EXPERT_TILE + n_spare
    tm = _pick_tile(t, 768)
    idx = lambda: pl.BlockSpec((1, 1, tm), lambda i, ends: (i, 0, 0), memory_space=pltpu.SMEM)
    grid_spec = pltpu.PrefetchScalarGridSpec(
        num_scalar_prefetch=1, grid=(t // tm,),
        in_specs=[idx(), idx(), pl.BlockSpec((tm, D_MODEL), lambda i, ends: (i, 0))],
        out_specs=pl.BlockSpec(memory_space=pl.ANY),
        scratch_shapes=[pltpu.VMEM((2, tm * SUBLANES, LANES), F32),
                        pltpu.VMEM((EXPERT_TILE * SUBLANES, LANES), F32),
                        pltpu.SemaphoreType.DMA((2,)), pltpu.SemaphoreType.DMA(())])
    return pl.pallas_call(
        functools.partial(_dispatch_kernel, n_tiles), grid_spec=grid_spec,
        out_shape=jax.ShapeDtypeStruct((n_rows * SUBLANES, LANES), F32),
        compiler_params=_params(1), name="dispatch")(
            ends, pos1.reshape(t // tm, 1, tm), pos2.reshape(t // tm, 1, tm), h)


def _experts_kernel(te_ref, na_ref, x_ref, wg_ref, wu_ref, wo_ref, y_ref, xb_ref, acc_ref):
    i = pl.program_id(0)
    f = pl.program_id(1)
    last = f == pl.num_programs(1) - 1
    active = i < na_ref[0]

    @pl.when(active)
    def _():
        @pl.when(f == 0)
        def _():
            acc_ref[...] = jnp.zeros_like(acc_ref)
            xb_ref[...] = _load_token_tiles(x_ref, EXPERT_TILE).astype(BF16)

        x = xb_ref[...]
        act = _silu(_dot(x, wg_ref[0])) * _dot(x, wu_ref[0])
        acc_ref[...] += _dot(act, wo_ref[0])

        @pl.when(last)
        def _():
            _store_token_tiles(y_ref, acc_ref[...])

    @pl.when(jnp.logical_not(active) & last)
    def _():
        y_ref[...] = jnp.zeros_like(y_ref)


def _experts(xs, tile_expert, n_active, wi, wo, n_tiles):
    n_rows = n_tiles * EXPERT_TILE
    e_ff = wo.shape[1]
    tf = _pick_tile(e_ff, 512)
    nf = e_ff // tf
    rows = pl.BlockSpec((EXPERT_TILE * SUBLANES, LANES), lambda i, f, te, na: (i, 0))
    rows_in = pl.BlockSpec((EXPERT_TILE * SUBLANES, LANES),
                           lambda i, f, te, na: (jnp.minimum(i, jnp.maximum(na[0] - 1, 0)), 0))
    def held(i, f, na):
        return jnp.where(i < na[0], f, nf - 1)

    grid_spec = pltpu.PrefetchScalarGridSpec(
        num_scalar_prefetch=2, grid=(n_rows // EXPERT_TILE, nf),
        in_specs=[rows_in, pl.BlockSpec((1, D_MODEL, tf), lambda i, f, te, na: (te[i], 0, held(i, f, na))),
                  pl.BlockSpec((1, D_MODEL, tf), lambda i, f, te, na: (te[i], 0, held(i, f, na) + nf)),
                  pl.BlockSpec((1, tf, D_MODEL), lambda i, f, te, na: (te[i], held(i, f, na), 0))],
        out_specs=rows,
        scratch_shapes=[pltpu.VMEM((EXPERT_TILE, D_MODEL), BF16), pltpu.VMEM((EXPERT_TILE, D_MODEL), F32)])
    return pl.pallas_call(
        _experts_kernel, grid_spec=grid_spec, out_shape=jax.ShapeDtypeStruct((n_rows * SUBLANES, LANES), F32),
        compiler_params=_params(2), name="experts")(tile_expert, n_active, xs, wi, wi, wo)


def _gather_start(n, pos1_ref, pos2_ref, ys_ref, a_ref, b_ref, sems):
    def tile_copy(p, buf, r, which):
        return pltpu.make_async_copy(ys_ref.at[_tile_rows(p)], buf.at[_tile_rows(r)], sems.at[which])

    def start(pair, c):
        for prio in range(DMA_PRIORITIES):
            r = pair * DMA_PRIORITIES + prio
            tile_copy(pos1_ref[0, 0, r], a_ref, r, 0).start(priority=prio)
            tile_copy(pos2_ref[0, 0, r], b_ref, r, 1).start(priority=prio)
        return c

    lax.fori_loop(0, n // DMA_PRIORITIES, start, 0, unroll=DMA_UNROLL // DMA_PRIORITIES)


def _gather_finish(alpha, h_ref, gate_ref, ys_ref, lng_ref, lnb_ref, a_ref, b_ref, sems):
    n = h_ref.shape[0]
    for which, buf in enumerate((a_ref, b_ref)):
        pltpu.make_async_copy(ys_ref.at[pl.ds(0, n * SUBLANES)], buf, sems.at[which]).wait()
    gate = gate_ref[...]
    y = gate[:, 0:1] * _load_token_tiles(a_ref, n) + gate[:, 1:2] * _load_token_tiles(b_ref, n)
    return _layer_norm(alpha * h_ref[...] + y, lng_ref[...], lnb_ref[...])


def _combine_kernel(alpha, pos1_ref, pos2_ref, h_ref, gate_ref, ys_ref, lng_ref, lnb_ref, o_ref,
                    a_ref, b_ref, sems):
    _gather_start(h_ref.shape[0], pos1_ref, pos2_ref, ys_ref, a_ref, b_ref, sems)
    o_ref[...] = _gather_finish(alpha, h_ref, gate_ref, ys_ref, lng_ref, lnb_ref, a_ref, b_ref, sems)


def _combine_scratch(tm):
    return [pltpu.VMEM((tm * SUBLANES, LANES), F32), pltpu.VMEM((tm * SUBLANES, LANES), F32),
            pltpu.SemaphoreType.DMA((2,))]


def _combine(alpha, h, gates, ys, pos1, pos2, ln_g, ln_b):
    t = h.shape[0]
    tm = _pick_tile(t, 512)
    idx = lambda: pl.BlockSpec((1, 1, tm), lambda i: (i, 0, 0), memory_space=pltpu.SMEM)
    rows = pl.BlockSpec((tm, D_MODEL), lambda i: (i, 0))
    vec = pl.BlockSpec((1, D_MODEL), lambda i: (0, 0))
    return pl.pallas_call(
        functools.partial(_combine_kernel, alpha), grid=(t // tm,),
        in_specs=[idx(), idx(), rows, pl.BlockSpec((tm, LANES), lambda i: (i, 0)),
                  pl.BlockSpec(memory_space=pl.ANY), vec, vec],
        out_specs=rows, out_shape=jax.ShapeDtypeStruct((t, D_MODEL), F32),
        scratch_shapes=_combine_scratch(tm),
        compiler_params=_params(1), name="combine")(
            pos1.reshape(t // tm, 1, tm), pos2.reshape(t // tm, 1, tm), h, gates, ys,
            ln_g.reshape(1, -1), ln_b.reshape(1, -1))


def _output_tables(seq_shapes):
    kind, live = [], []
    window = [[], []]
    recent = [(0, 0), (0, 0)]
    for grp, (b, l) in enumerate(seq_shapes):
        for i in range(b):
            for c in range(l // CH + 1):
                kind.append(grp)
                live.append(int(c > 0))
                recent[grp] = (i, max(c - 1, 0))
                for g2 in range(2):
                    window[g2].append(recent[g2])
    tables = [kind, live] + [[w[j] for w in window[g2]] for g2 in range(2) for j in range(2)]
    return [jnp.asarray(np.asarray(x, np.int32)) for x in tables]


def _combine_out_kernel(alpha, kind_ref, live_ref, ab_ref, ac_ref, bb_ref, bc_ref,
                        pos1_ref, pos2_ref, nxt1_ref, nxt2_ref, h_ref, gate_ref, ys_ref, lng_ref, lnb_ref,
                        outa_ref, outb_ref, a_ref, b_ref, sems):
    g = pl.program_id(0)
    last = pl.num_programs(0) - 1
    slot = g % 2
    nxt = jnp.minimum(g + 1, last)

    @pl.when((g == 0) & (live_ref[0] == 1))
    def _():
        _gather_start(CH, pos1_ref, pos2_ref, ys_ref, a_ref.at[0], b_ref.at[0], sems.at[0])

    @pl.when((g < last) & (live_ref[nxt] == 1))
    def _():
        _gather_start(CH, nxt1_ref, nxt2_ref, ys_ref, a_ref.at[1 - slot], b_ref.at[1 - slot], sems.at[1 - slot])

    @pl.when(live_ref[g] == 1)
    def _():
        res = _gather_finish(alpha, h_ref, gate_ref, ys_ref, lng_ref, lnb_ref,
                             a_ref.at[slot], b_ref.at[slot], sems.at[slot])
        for grp, out_ref in enumerate((outa_ref, outb_ref)):
            @pl.when(kind_ref[g] == grp)
            def _(out_ref=out_ref):
                out_ref[0] = res


def _combine_out(alpha, h, gates, ys, pos1, pos2, ln_g, ln_b, seq_shapes):
    t = h.shape[0]
    n = t // CH
    tables = _output_tables(seq_shapes)
    idx = lambda: pl.BlockSpec((1, 1, CH), lambda g, *tb: (g, 0, 0), memory_space=pltpu.SMEM)
    idx_next = lambda: pl.BlockSpec((1, 1, CH), lambda g, *tb: (jnp.minimum(g + 1, n - 1), 0, 0),
                                    memory_space=pltpu.SMEM)
    rows = lambda w: pl.BlockSpec((CH, w), lambda g, *tb: (g, 0))
    vec = pl.BlockSpec((1, D_MODEL), lambda g, *tb: (0, 0))
    out = lambda grp: pl.BlockSpec((1, CH, D_MODEL),
                                   lambda g, *tb: (tb[2 + 2 * grp][g], tb[3 + 2 * grp][g], 0))
    fetched = pltpu.VMEM((2, CH * SUBLANES, LANES), F32)
    grid_spec = pltpu.PrefetchScalarGridSpec(
        num_scalar_prefetch=len(tables), grid=(n,),
        in_specs=[idx(), idx(), idx_next(), idx_next(), rows(D_MODEL), rows(LANES),
                  pl.BlockSpec(memory_space=pl.ANY), vec, vec],
        out_specs=[out(0), out(1)], scratch_shapes=[fetched, fetched, pltpu.SemaphoreType.DMA((2, 2))])
    pos1, pos2 = pos1.reshape(n, 1, CH), pos2.reshape(n, 1, CH)
    return pl.pallas_call(
        functools.partial(_combine_out_kernel, alpha), grid_spec=grid_spec,
        out_shape=[jax.ShapeDtypeStruct((b, l, D_MODEL), F32) for b, l in seq_shapes],
        compiler_params=_params(1), name="combine_out")(
            *tables, pos1, pos2, pos1, pos2, h, gates, ys, ln_g.reshape(1, -1), ln_b.reshape(1, -1))


def _moe(alpha, h, meta, gates, counts, wi, wo, ln_g, ln_b, n_pad_rows, out_shapes=None):
    t = h.shape[0]
    ends, scatter, gather, tile_expert, n_active, n_tiles = _dispatch_plan(meta, counts, t)
    n_spare = 2 * n_pad_rows
    xs = _dispatch(h, ends, scatter[0], scatter[1], n_tiles, n_spare)
    ys = _experts(xs, tile_expert, n_active, wi, wo, n_tiles)
    if out_shapes is None:
        return _combine(alpha, h, gates, ys, gather[0], gather[1], ln_g, ln_b)
    return _combine_out(alpha, h, gates, ys, gather[0], gather[1], ln_g, ln_b, out_shapes)


def kernel(x_prompt, x_sample, meta, e_w_in, e_conv_w, e_conv_b, e_lru_wa, e_lru_ba, e_lru_wx, e_lru_bx,
           e_lru_lambda, e_w_out, e_ffn_wi, e_ffn_wo, o_w_in, o_hg_lower, o_ml_bi, o_ml_bf, o_w_out, o_router,
           o_exp_wi, o_exp_wo, ln_g, ln_b):
    groups = (x_prompt, x_sample)
    depth = ln_g.shape[0]
    alpha = (2.0 * depth) ** 0.25
    seq_shapes = [(x.shape[0], x.shape[1]) for x in groups]
    for _, l in seq_shapes:
        assert l % CH == 0
    cpos_np, clast_np = _chunk_tables(seq_shapes)
    cpos, clast = jnp.asarray(cpos_np), jnp.asarray(clast_np)
    max_rows = max(l for _, l in seq_shapes) + CH
    pad_starts = tuple(int(i) * CH for i in np.flatnonzero(cpos_np == 0))

    head = jnp.concatenate([jnp.zeros((PAD_ROWS, D_MODEL), F32), meta.astype(F32)], axis=0)
    parts = []
    for x in groups:
        full = jnp.concatenate([jnp.broadcast_to(head[None], (x.shape[0], CH, D_MODEL)), x], axis=1)
        parts.append(full.reshape(-1, D_MODEL))
    h = jnp.concatenate(parts, axis=0)
    t = h.shape[0]

    n_odd_cols = 9 * GROUP_W
    for layer in range(depth):
        p = layer // 2
        if layer % 2 == 0:
            z = _project(h, e_w_in[p].astype(BF16))
            (ret_f, ret_b), (lru_f, lru_b) = _chunk_walk(
                [_retention(z, max_rows),
                 _rglru(z, 4, e_conv_w[p], e_conv_b[p], e_lru_wa[p], e_lru_ba[p],
                        e_lru_wx[p], e_lru_bx[p], e_lru_lambda[p])], cpos, clast, t, "even_mixers")
            h = _mix_ffn(alpha, ret_f, ret_b, lru_f, lru_b, z, 3, 5, h, e_w_out[p].astype(BF16),
                         ln_g[layer, 0], ln_b[layer, 0], e_ffn_wi[p].astype(BF16), e_ffn_wo[p].astype(BF16),
                         ln_g[layer, 1], ln_b[layer, 1])
        else:
            w_in = o_w_in[p]
            n_gate = 2 * N_HEADS
            gate_pad = ((0, 0), (0, LANES - n_gate))
            w_gates = jnp.concatenate([jnp.pad(w_in[:, n_odd_cols:n_odd_cols + n_gate], gate_pad),
                                       jnp.pad(w_in[:, n_odd_cols + n_gate:], gate_pad)], axis=1)
            z, gates = _project(h, w_in[:, :n_odd_cols].astype(BF16), w_gates)
            cps = ODD_CHUNKS_PER_STEP if (t // CH) % ODD_CHUNKS_PER_STEP == 0 else 1
            (hg_f, hg_b), (ml_f, ml_b) = _chunk_walk(
                [_hgrn2(z, o_hg_lower, layer, cps), _mlstm(z, gates, o_ml_bi[p], o_ml_bf[p], cps)],
                cpos, clast, t, "odd_mixers", cps)
            h, meta_r, gates_r, counts = _mix_route(
                alpha, hg_f, hg_b, ml_f, ml_b, z, 4, 8, h, o_w_out[p].astype(BF16),
                ln_g[layer, 0], ln_b[layer, 0], o_router[p], pad_starts)
            h = _moe(alpha, h, meta_r, gates_r, counts, o_exp_wi[p].astype(BF16), o_exp_wo[p].astype(BF16),
                     ln_g[layer, 1], ln_b[layer, 1], len(pad_starts) * PAD_ROWS,
                     seq_shapes if layer == depth - 1 else None)
            if layer == depth - 1:
                return tuple(h)

    outs = []
    row = 0
    for b, l in seq_shapes:
        n = b * (l + CH)
        outs.append(h[row:row + n].reshape(b, l + CH, D_MODEL)[:, CH:])
        row += n
    return tuple(outs)
```

```python
import functools
import math

import numpy as np
import jax
import jax.numpy as jnp
from jax import lax
from jax.experimental import pallas as pl
from jax.experimental.pallas import tpu as pltpu

F32 = jnp.float32
BF16 = jnp.bfloat16

D_MODEL = 1024
GROUP_W = D_MODEL // 2
N_HEADS = 4
HEAD_D = GROUP_W // N_HEADS
N_META = 16
ROPE_BASE = 10000.0
LRU_BLOCKS = 8
LRU_BW = GROUP_W // LRU_BLOCKS
LRU_C = 8.0
N_EXP = 8
EPS = 1e-5

LANES = 128
SUBLANES = 8
CH = 128
PAD_ROWS = CH - N_META
SUB = SUBLANES
GRP = 4 * SUB
ODD_CHUNKS_PER_STEP = 2
NEG_BIG = -1e30
LOG2_E = math.log2(math.e)
VMEM_LIMIT = 56 * 1024 * 1024


def _dot(a, b):
    return jnp.dot(a.astype(BF16), b.astype(BF16), preferred_element_type=F32)


def _dot_nt(a, b):
    return lax.dot_general(a.astype(BF16), b.astype(BF16), (((1,), (1,)), ((), ())),
                           preferred_element_type=F32)


def _dot_tn(a, b):
    return _dot(a.T, b)


def _bf16_terms(x, n_terms):
    terms = []
    for _ in range(n_terms):
        t = x.astype(BF16)
        terms.append(t)
        x = x - t.astype(F32)
    return terms


def _stack_weight_terms(w):
    hi, lo = _bf16_terms(w, 2)
    return jnp.concatenate([hi, lo, hi], axis=0)


def _dot_16bit(x, w_terms):
    hi, lo = _bf16_terms(x, 2)
    return jnp.dot(jnp.concatenate([hi, hi, lo], axis=1), w_terms, preferred_element_type=F32)


PREFIX_TERMS = 3
SPREAD_TERMS = 2


def _prefix_dot(op_tiled, x):
    return jnp.dot(op_tiled, jnp.concatenate(_bf16_terms(x, PREFIX_TERMS), axis=0),
                   preferred_element_type=F32)


def _spread_dot(x, op_tiled):
    return jnp.dot(jnp.concatenate(_bf16_terms(x, SPREAD_TERMS), axis=1), op_tiled,
                   preferred_element_type=F32)


def _sigmoid(x):
    return 1.0 / (1.0 + jnp.exp(-x))


def _silu(x):
    return x * _sigmoid(x)


def _gelu_tanh(x):
    return 0.5 * x * (1.0 + jnp.tanh(math.sqrt(2.0 / math.pi) * (x + 0.044715 * (x * x * x))))


def _softplus(x):
    return jnp.maximum(x, 0.0) + jnp.log(1.0 + jnp.exp(-jnp.abs(x)))


def _pick_tile(total, target):
    best = LANES
    for t in range(LANES, min(total, target) + 1, LANES):
        if total % t == 0:
            best = t
    return best


def _params(n_axes, sem="arbitrary"):
    return pltpu.CompilerParams(dimension_semantics=(sem,) * n_axes, vmem_limit_bytes=VMEM_LIMIT)


def _proj_kernel(x_ref, w_ref, o_ref):
    o_ref[...] = _dot(x_ref[...], w_ref[...])


def _proj_gates_kernel(x_ref, w_ref, wg_ref, o_ref, g_ref):
    g_ref[...] = _dot_16bit(x_ref[...], wg_ref[...])
    o_ref[...] = _dot(x_ref[...], w_ref[...])


PROJ_OUT_BLOCK_BYTES = 12 * 1024 * 1024


def _project(x, w, w_gates=None):
    t, k = x.shape
    n = w.shape[1]
    tm = _pick_tile(t, PROJ_OUT_BLOCK_BYTES // (4 * n))
    x_spec = pl.BlockSpec((tm, k), lambda i: (i, 0))
    w_spec = pl.BlockSpec((k, n), lambda i: (0, 0))
    o_spec = pl.BlockSpec((tm, n), lambda i: (i, 0))
    if w_gates is None:
        return pl.pallas_call(
            _proj_kernel, grid=(t // tm,), in_specs=[x_spec, w_spec], out_specs=o_spec,
            out_shape=jax.ShapeDtypeStruct((t, n), F32), compiler_params=_params(1),
            name="proj")(x, w)
    ng = w_gates.shape[1]
    w_gates = _stack_weight_terms(w_gates)
    return pl.pallas_call(
        _proj_gates_kernel, grid=(t // tm,),
        in_specs=[x_spec, w_spec, pl.BlockSpec(w_gates.shape, lambda i: (0, 0))],
        out_specs=[o_spec, pl.BlockSpec((tm, ng), lambda i: (i, 0))],
        out_shape=[jax.ShapeDtypeStruct((t, n), F32), jax.ShapeDtypeStruct((t, ng), F32)],
        compiler_params=_params(1), name="proj_gates")(x, w, w_gates)


def _chunk_tables(seq_shapes):
    cpos, clast = [], []
    for b, l in seq_shapes:
        n = l // CH + 1
        for _ in range(b):
            cpos += list(range(n))
            clast += [0] * (n - 1) + [1]
    return np.asarray(cpos, np.int32), np.asarray(clast, np.int32)


def _fwd_map(col):
    return lambda g, cp, cl: (g, col)


def _bwd_map(n_chunks, col):
    return lambda g, cp, cl: (n_chunks - 1 - g, col)


def _walk_kernel(bodies, tags, n_scratch, cps, cpos_ref, clast_ref, *refs):
    refs = list(refs)
    ins = [[refs.pop(0) for _ in t] for t in tags]
    outs = [[refs.pop(0) for _ in range(2)] for _ in bodies]
    scratch = [[refs.pop(0) for _ in range(k)] for k in n_scratch]

    @pl.when(pl.program_id(0) == 0)
    def _():
        for part in scratch:
            for r in part:
                r[...] = jnp.zeros_like(r)

    n_chunks = pl.num_programs(0) * cps
    for j in range(cps):
        g = pl.program_id(0) * cps + j
        gb = n_chunks - 1 - g
        rows = {"f": pl.ds(j * CH, CH), "b": pl.ds((cps - 1 - j) * CH, CH)}

        def view(ref, tag):
            return ref if tag == "c" or cps == 1 else ref.at[rows[tag]]

        for body, i, tg, o, s in zip(bodies, ins, tags, outs, scratch):
            body(cpos_ref, clast_ref, g, gb, *[view(r, t) for r, t in zip(i, tg)],
                 view(o[0], "f"), view(o[1], "b"), *s)


def _chunk_walk(parts, cpos, clast, t, name, cps=1):
    n = t // (CH * cps)
    blk = (CH * cps, GROUP_W)
    bodies = [p[0] for p in parts]
    grid_spec = pltpu.PrefetchScalarGridSpec(
        num_scalar_prefetch=2, grid=(n,), in_specs=[s for p in parts for s in p[1]],
        out_specs=[pl.BlockSpec(blk, _fwd_map(0)), pl.BlockSpec(blk, _bwd_map(n, 0))] * len(parts),
        scratch_shapes=[s for p in parts for s in p[3]])
    outs = pl.pallas_call(
        functools.partial(_walk_kernel, bodies, [p[4] for p in parts], [len(p[3]) for p in parts], cps),
        grid_spec=grid_spec, out_shape=[jax.ShapeDtypeStruct((t, GROUP_W), BF16)] * (2 * len(parts)),
        compiler_params=_params(1), name=name)(cpos, clast, *[a for p in parts for a in p[2]])
    return [outs[2 * i:2 * i + 2] for i in range(len(parts))]


def _row_valid(first):
    row = lax.broadcasted_iota(jnp.int32, (CH, 1), 0)
    return row >= jnp.where(first, PAD_ROWS, 0)


def _ret_log_gamma():
    return np.log1p(-np.exp2(-5.0 - np.arange(N_HEADS, dtype=np.float64)))


def _ret_consts():
    lg = _ret_log_gamma()
    pos = np.arange(CH, dtype=np.float64)
    rel = pos[:, None] - pos[None, :]
    dmat = np.zeros((2, N_HEADS, CH, CH), np.float64)
    rows = np.zeros((4, CH, GROUP_W), np.float64)
    for h in range(N_HEADS):
        dmat[0, h] = np.where(rel >= 0, np.exp(np.maximum(rel, 0.0) * lg[h]), 0.0)
        dmat[1, h] = np.where(rel < 0, np.exp(np.maximum(-rel, 0.0) * lg[h]), 0.0)
        sl = slice(h * HEAD_D, (h + 1) * HEAD_D)
        rows[0, :, sl] = np.exp((pos + 1.0) * lg[h])[:, None]
        rows[1, :, sl] = np.exp((CH - 1.0 - pos) * lg[h])[:, None]
        rows[2, :, sl] = np.exp((CH - pos) * lg[h])[:, None]
        rows[3, :, sl] = np.exp(pos * lg[h])[:, None]
    return jnp.asarray(dmat, F32), jnp.asarray(rows, F32)


def _ret_direction(q_ref, k_ref, v_ref, cos_ref, sin_ref, dmat_ref, rin_ref, rout_ref, s_ref, o_ref,
                   reset, first):
    valid = _row_valid(first)
    cos = cos_ref[...]
    sin = sin_ref[...]
    chunk_decay = np.exp(CH * _ret_log_gamma())
    for h in range(N_HEADS):
        sl = slice(h * HEAD_D, (h + 1) * HEAD_D)
        q = q_ref[:, sl]
        k = k_ref[:, sl]
        v = v_ref[:, sl]
        q = q * cos + pltpu.roll(q, HEAD_D // 2, 1) * sin
        k = (k * cos + pltpu.roll(k, HEAD_D // 2, 1) * sin) * (HEAD_D ** -0.5)
        k = jnp.where(valid, k, 0.0)
        scores = _dot_nt(q, k) * dmat_ref[h]
        state = jnp.where(reset, 0.0, s_ref[h])
        o_ref[:, sl] = _dot(jnp.concatenate([scores, q * rin_ref[:, sl]], axis=1),
                            jnp.concatenate([v, state], axis=0)).astype(o_ref.dtype)
        s_ref[h] = float(chunk_decay[h]) * state + _dot_tn(k * rout_ref[:, sl], v)


def _ret_kernel(cpos_ref, clast_ref, g, gb, qf, kf, vf, cosf, sinf, qb, kb, vb, cosb, sinb, dmat_ref, rows_ref,
                of_ref, ob_ref, sf_ref, sb_ref):
    _ret_direction(qf, kf, vf, cosf, sinf, dmat_ref.at[0], rows_ref.at[0], rows_ref.at[1], sf_ref, of_ref,
                   cpos_ref[g] == 0, cpos_ref[g] == 0)
    _ret_direction(qb, kb, vb, cosb, sinb, dmat_ref.at[1], rows_ref.at[2], rows_ref.at[3], sb_ref, ob_ref,
                   clast_ref[gb] == 1, cpos_ref[gb] == 0)


def _rope_tables(n_rows):
    inv = ROPE_BASE ** (-jnp.arange(0, HEAD_D, 2, dtype=jnp.float32) / HEAD_D)
    pos = jnp.maximum(jnp.arange(n_rows, dtype=jnp.float32) - PAD_ROWS, 0.0)
    ang = pos[:, None] * inv[None, :]
    cos, sin = jnp.cos(ang), jnp.sin(ang)
    return jnp.concatenate([cos, cos], axis=1), jnp.concatenate([-sin, sin], axis=1)


def _retention(z, max_rows):
    t = z.shape[0]
    n = t // CH
    cos2, sin2 = _rope_tables(max_rows)
    dmat, rows = _ret_consts()
    blk = (CH, GROUP_W)
    in_specs = []
    for mk, pm in ((_fwd_map, lambda g, cp, cl: (cp[g], 0)),
                   (functools.partial(_bwd_map, n), lambda g, cp, cl: (cp[n - 1 - g], 0))):
        in_specs += [pl.BlockSpec(blk, mk(0)), pl.BlockSpec(blk, mk(1)), pl.BlockSpec(blk, mk(2)),
                     pl.BlockSpec((CH, HEAD_D), pm), pl.BlockSpec((CH, HEAD_D), pm)]
    in_specs += [pl.BlockSpec(dmat.shape, lambda g, cp, cl: (0, 0, 0, 0)),
                 pl.BlockSpec(rows.shape, lambda g, cp, cl: (0, 0, 0))]
    return (_ret_kernel, in_specs, [z, z, z, cos2, sin2, z, z, z, cos2, sin2, dmat, rows],
            [pltpu.VMEM((N_HEADS, HEAD_D, HEAD_D), F32)] * 2, "fffffbbbbbcc")


HALO = SUBLANES


def _shift_rows(x, s, fill, reverse):
    row = lax.broadcasted_iota(jnp.int32, (CH, 1), 0)
    if reverse:
        return jnp.where(row < CH - s, pltpu.roll(x, CH - s, 0), fill)
    return jnp.where(row >= s, pltpu.roll(x, s, 0), fill)


def _lru_direction(x_ref, prev_ref, next_ref, convw_ref, convb_ref, wg_ref, bias_ref, lam_ref, ext_ref,
                   carry_ref, o_ref, reset, first, last, reverse):
    valid = _row_valid(first)
    ext_ref[HALO:HALO + CH, :] = jnp.where(valid, x_ref[...], 0.0)
    ext_ref[0:HALO, :] = jnp.where(first, 0.0, prev_ref[...])
    ext_ref[HALO + CH:, :] = jnp.where(last, 0.0, next_ref[...])
    xc = convb_ref[...] + ext_ref[HALO - 2:HALO - 2 + CH, :] * convw_ref[0:1, :]
    for tap in range(1, 4):
        xc = xc + ext_ref[HALO - 2 + tap:HALO - 2 + tap + CH, :] * convw_ref[tap:tap + 1, :]

    log_sig_lam = -_softplus(-lam_ref[...])
    parts_a, parts_u = [], []
    for grp in range(GROUP_W // LANES):
        sl = slice(grp * LANES, (grp + 1) * LANES)
        xg = xc[:, sl]
        pre = _dot(xg, wg_ref[grp])
        r = _sigmoid(pre[:, :LANES] + bias_ref[0:1, sl])
        i = _sigmoid(pre[:, LANES:] + bias_ref[1:2, sl])
        a = jnp.exp(LRU_C * log_sig_lam[:, sl] * r)
        u = jnp.sqrt(1.0 - a * a) * (i * xg)
        parts_a.append(a)
        parts_u.append(jnp.where(valid, u, 0.0))
    a = jnp.concatenate(parts_a, axis=1)
    u = jnp.concatenate(parts_u, axis=1)

    n_blk = CH // SUBLANES
    a = a.reshape(n_blk, SUBLANES, GROUP_W)
    u = u.reshape(n_blk, SUBLANES, GROUP_W)
    row_in_blk = lax.broadcasted_iota(jnp.int32, (1, SUBLANES, 1), 1)
    s = 1
    while s < SUBLANES:
        keep = (row_in_blk < SUBLANES - s) if reverse else (row_in_blk >= s)
        shift = SUBLANES - s if reverse else s
        u = u + a * jnp.where(keep, pltpu.roll(u, shift, 1), 0.0)
        a = a * jnp.where(keep, pltpu.roll(a, shift, 1), 1.0)
        s *= 2
    edge = 0 if reverse else SUBLANES - 1
    carry = jnp.where(reset, 0.0, carry_ref[...])
    blocks = [None] * n_blk
    for blk in (reversed(range(n_blk)) if reverse else range(n_blk)):
        blocks[blk] = u[blk] + a[blk] * carry
        carry = blocks[blk][edge:edge + 1, :]
    o_ref[...] = jnp.concatenate(blocks, axis=0).astype(o_ref.dtype)
    carry_ref[...] = carry


def _lru_kernel(cpos_ref, clast_ref, g, gb, xf, pf, nf, xb, pb, nb, convw_ref, convb_ref, wg_ref, bias_ref,
                lam_ref, of_ref, ob_ref, extf_ref, extb_ref, cf_ref, cb_ref):
    _lru_direction(xf, pf, nf, convw_ref, convb_ref, wg_ref.at[0], bias_ref.at[0], lam_ref.at[0], extf_ref,
                   cf_ref, of_ref, cpos_ref[g] == 0, cpos_ref[g] == 0, clast_ref[g] == 1, False)
    _lru_direction(xb, pb, nb, convw_ref, convb_ref, wg_ref.at[1], bias_ref.at[1], lam_ref.at[1], extb_ref,
                   cb_ref, ob_ref, clast_ref[gb] == 1, cpos_ref[gb] == 0, clast_ref[gb] == 1, True)


def _lru_gate_weights(wa, wx):
    per = LANES // LRU_BW

    def block_diag(w):
        w = w.reshape(2, LRU_BLOCKS // per, per, LRU_BW, LRU_BW)
        eye = jnp.eye(per, dtype=w.dtype)
        return jnp.einsum("dgpij,pq->dgpiqj", w, eye).reshape(2, LRU_BLOCKS // per, LANES, LANES)

    return jnp.concatenate([block_diag(wa), block_diag(wx)], axis=-1).astype(BF16)


def _rglru(z, col, conv_w, conv_b, wa, ba, wx, bx, lam):
    t = z.shape[0]
    n = t // CH
    per = CH // HALO
    n_halo = t // HALO
    blk = (CH, GROUP_W)
    hblk = (HALO, GROUP_W)
    wg = _lru_gate_weights(wa, wx)
    bias = jnp.stack([ba, bx], axis=1)
    lam = lam.reshape(2, 1, GROUP_W)

    def prev_f(g, cp, cl):
        return (jnp.maximum(g * per - 1, 0), col)

    def next_f(g, cp, cl):
        return (jnp.minimum((g + 1) * per, n_halo - 1), col)

    def prev_b(g, cp, cl):
        return (jnp.maximum((n - 1 - g) * per - 1, 0), col)

    def next_b(g, cp, cl):
        return (jnp.minimum((n - g) * per, n_halo - 1), col)

    full = lambda a: pl.BlockSpec(a.shape, lambda g, cp, cl: (0,) * a.ndim)
    conv_b2 = conv_b.reshape(1, GROUP_W)
    in_specs = [pl.BlockSpec(blk, _fwd_map(col)), pl.BlockSpec(hblk, prev_f), pl.BlockSpec(hblk, next_f),
                pl.BlockSpec(blk, _bwd_map(n, col)), pl.BlockSpec(hblk, prev_b), pl.BlockSpec(hblk, next_b),
                full(conv_w), full(conv_b2), full(wg), full(bias), full(lam)]
    return (_lru_kernel, in_specs, [z, z, z, z, z, z, conv_w, conv_b2, wg, bias, lam],
            [pltpu.VMEM((CH + 2 * HALO, GROUP_W), F32)] * 2 + [pltpu.VMEM((1, GROUP_W), F32)] * 2,
            "fffbbbccccc")


def _tri_consts():
    pos = np.arange(CH)
    lower = (pos[:, None] >= pos[None, :]).astype(np.float32)
    ops = np.stack([lower, lower.T])
    return jnp.asarray(np.tile(ops, (1, 1, PREFIX_TERMS)), BF16)


def _hg_masks():
    pos = np.arange(CH)
    same_blk = pos[:, None] // SUB == pos[None, :] // SUB
    same_grp = pos[:, None] // GRP == pos[None, :] // GRP
    return jnp.asarray(np.stack([same_blk, same_grp]).astype(np.float32))


def _hg_direction(q_ref, f_ref, v_ref, lb, cum_ref, mask_ref, st_ref, o_ref, reset, first, reverse):
    valid = _row_valid(first)
    sub_i =lax.broadcasted_iota(jnp.int32, (SUB, HEAD_D), 0)
    lane_j = lax.broadcasted_iota(jnp.int32, (SUB, HEAD_D), 1) & (SUB - 1)
    causal = (sub_i <= lane_j) if reverse else (sub_i >= lane_j)
    pick = [causal & (lane_j == j) for j in range(SUB)]
    cum_op = cum_ref[...]
    same_blk = mask_ref[0]
    same_grp = mask_ref[1]
    zero = jnp.zeros((SUB, HEAD_D), F32)
    n_blk = CH // SUB
    per = GRP // SUB
    n_grp = CH // GRP

    def edge(unit, idx):
        return unit * idx if reverse else unit * (idx + 1) - 1

    def split_product(b, qs, ks, bs, pieces):
        q_slabs, k_slabs = [], []
        for ref, q_active, k_active in pieces:
            qp, kp = [], []
            for i in range(n_blk):
                r = ref(i)
                qp.append(qs[i] * jnp.exp2(bs[i] - b[r:r + 1, :]) if q_active(i) else zero)
                kp.append(ks[i] * jnp.exp2(b[r:r + 1, :] - bs[i]) if k_active(i) else zero)
            q_slabs.append(jnp.concatenate(qp, axis=0))
            k_slabs.append(jnp.concatenate(kp, axis=0))
        return _dot_nt(jnp.concatenate(q_slabs, axis=1), jnp.concatenate(k_slabs, axis=1))

    level1 = []
    for c in (range(1, per) if reverse else range(per - 1)):
        level1.append((lambda i, c=c: edge(SUB, (i // per) * per + c),
                       (lambda i, c=c: i % per < c) if reverse else (lambda i, c=c: i % per > c),
                       lambda i, c=c: i % per == c))
    level2 = []
    for gc in (range(1, n_grp) if reverse else range(n_grp - 1)):
        level2.append((lambda i, gc=gc: edge(GRP, gc),
                       (lambda i, gc=gc: i // per < gc) if reverse else (lambda i, gc=gc: i // per > gc),
                       lambda i, gc=gc: i // per == gc))

    for h in range(N_HEADS):
        sl = slice(h * HEAD_D, (h + 1) * HEAD_D)
        lbh = lb[:, sl]
        q = _silu(q_ref[:, sl])
        f = lbh + (1.0 - lbh) * _sigmoid(f_ref[:, sl])
        k = jnp.where(valid, 1.0 - f, 0.0)
        v = v_ref[:, sl]
        b = _prefix_dot(cum_op, jnp.log(f)) * LOG2_E
        qs = [q[i * SUB:(i + 1) * SUB] for i in range(n_blk)]
        ks = [k[i * SUB:(i + 1) * SUB] for i in range(n_blk)]
        bs = [b[i * SUB:(i + 1) * SUB] for i in range(n_blk)]

        tiles = [qs[blk] * jnp.exp2(jnp.minimum(bs[blk] - bs[blk][j:j + 1, :], 0.0))
                 for blk in range(n_blk) for j in range(SUB)]
        pair = _dot_nt(jnp.concatenate(tiles, axis=0), k)
        rows = []
        for blk in range(n_blk):
            base = blk * SUB * SUB
            a_blk = jnp.where(pick[0], pair[base:base + SUB], 0.0)
            for j in range(1, SUB):
                a_blk = a_blk + jnp.where(pick[j], pair[base + j * SUB:base + (j + 1) * SUB], 0.0)
            rows.append(a_blk)
        scores = jnp.concatenate(rows, axis=0) * same_blk
        scores = scores + split_product(b, qs, ks, bs, level1) * same_grp
        scores = scores + split_product(b, qs, ks, bs, level2)

        state = jnp.where(reset, 0.0, st_ref[h])
        o_ref[:, sl] = (_dot(scores, v) + _dot_nt(q * jnp.exp2(b), state)).astype(o_ref.dtype)
        b_tot = b[0:1, :] if reverse else b[CH - 1:CH, :]
        st_ref[h] = state * jnp.exp2(b_tot) + _dot_tn(v, k * jnp.exp2(b_tot - b))


def _hg_lower_bound(lower_ref, layer):
    low = lower_ref[...]
    e = jnp.exp(low - jnp.max(low, axis=0, keepdims=True))
    soft = e / jnp.sum(e, axis=0, keepdims=True)
    lb = jnp.zeros((1, GROUP_W), F32)
    for l in range(1, layer + 1):
        lb = lb + soft[l:l + 1, :]
    return lb


def _hg_kernel(layer, cpos_ref, clast_ref, g, gb, qf, ff, vf, qb, fb, vb, lower_ref, cum_ref, mask_ref,
               of_ref, ob_ref, sf_ref, sb_ref):
    lb =_hg_lower_bound(lower_ref, layer)
    _hg_direction(qf, ff, vf, lb, cum_ref.at[0], mask_ref, sf_ref, of_ref,
                  cpos_ref[g] == 0, cpos_ref[g] == 0, False)
    _hg_direction(qb, fb, vb, lb, cum_ref.at[1], mask_ref, sb_ref, ob_ref,
                  clast_ref[gb] == 1, cpos_ref[gb] == 0, True)


def _hgrn2(z, lower, layer, cps):
    t = z.shape[0]
    n = t // (CH * cps)
    blk = (CH * cps, GROUP_W)
    cum = _tri_consts()
    masks = _hg_masks()
    full = lambda a: pl.BlockSpec(a.shape, lambda g, cp, cl: (0,) * a.ndim)
    in_specs = [pl.BlockSpec(blk, _fwd_map(0)), pl.BlockSpec(blk, _fwd_map(1)), pl.BlockSpec(blk, _fwd_map(3)),
                pl.BlockSpec(blk, _bwd_map(n, 0)), pl.BlockSpec(blk, _bwd_map(n, 2)),
                pl.BlockSpec(blk, _bwd_map(n, 3)), full(lower), full(cum), full(masks)]
    return (functools.partial(_hg_kernel, layer), in_specs, [z, z, z, z, z, z, lower, cum, masks],
            [pltpu.VMEM((N_HEADS, HEAD_D, HEAD_D), F32)] * 2, "fffbbbccc")


def _log_sigmoid(x):
    return jnp.minimum(x, 0.0) - jnp.log(1.0 + jnp.exp(-jnp.abs(x)))


def _ml_direction(q_ref, k_ref, v_ref, gi_ref, gf_ref, bias_ref, cum_op, spread_ref, s_ref, m_ref, o_ref,
                  reset, first, d, reverse):
    valid = _row_valid(first)
    cum = _prefix_dot(cum_op, _log_sigmoid(gf_ref[...] + bias_ref[1]))
    a = jnp.where(valid, gi_ref[...] + bias_ref[0] - cum, NEG_BIG)
    run = a
    s = 1
    while s < CH:
        run = jnp.maximum(run, _shift_rows(run, s, NEG_BIG, reverse))
        s *= 2
    m_st = jnp.where(reset, NEG_BIG, m_ref[...])
    mx = jnp.maximum(m_st, run)
    edge = 0 if reverse else CH - 1
    mx_last = mx[edge:edge + 1, :]
    m_ref[...] = cum[edge:edge + 1, :] + mx_last
    decay = jnp.exp(m_st - mx_last)

    stacked = jnp.concatenate([mx, m_st - mx, -(cum + mx), a - mx_last], axis=0) * LOG2_E
    spread_b = _spread_dot(stacked, spread_ref[...])
    mx_b, inter_b, floor_b, end_b = (spread_b[i * CH:(i + 1) * CH] for i in range(4))
    a_t = (a * LOG2_E).T

    row = lax.broadcasted_iota(jnp.int32, (CH, CH), 0)
    col = lax.broadcasted_iota(jnp.int32, (CH, CH), 1)
    causal = (col >= row) if reverse else (col <= row)
    ones = jnp.ones((CH, HEAD_D), F32)
    for h in range(N_HEADS):
        sl = slice(h * HEAD_D, (h + 1) * HEAD_D)
        x = d * N_HEADS + h
        q = q_ref[:, sl]
        k = jnp.where(valid, k_ref[:, sl] * (HEAD_D ** -0.5), 0.0)
        v_ext = jnp.concatenate([v_ref[:, sl], ones], axis=1)
        w = jnp.where(causal, jnp.exp2(jnp.minimum(a_t[x:x + 1, :] - mx_b[:, sl], 0.0)), 0.0)
        qk = _dot_nt(q, k) * w
        s_inter = jnp.exp2(jnp.minimum(inter_b[:, sl], 0.0))
        state = jnp.where(reset, 0.0, s_ref[h])
        ext = _dot(jnp.concatenate([qk, q * s_inter], axis=1), jnp.concatenate([v_ext, state], axis=0))
        o_ref[:, sl] = (ext[:, :HEAD_D] / jnp.maximum(jnp.abs(ext[:, HEAD_D:]), jnp.exp2(floor_b[:, sl]))
                        ).astype(o_ref.dtype)
        kw = k * jnp.exp2(jnp.minimum(end_b[:, sl], 0.0))
        s_ref[h] = decay[:, x:x + 1] * state + _dot_tn(kw, v_ext)


def _ml_kernel(cpos_ref, clast_ref, g, gb, qf, kf, vf, gif, gff, qb, kb, vb, gib, gfb, bias_ref, cum_ref,
               spread_ref, of_ref, ob_ref, sfs, mfs, sbs, mbs):
    _ml_direction(qf, kf, vf, gif, gff, bias_ref, cum_ref[0], spread_ref.at[0], sfs, mfs, of_ref,
                  cpos_ref[g] == 0, cpos_ref[g] == 0, 0, False)
    _ml_direction(qb, kb, vb, gib, gfb, bias_ref, cum_ref[1], spread_ref.at[1], sbs, mbs, ob_ref,
                  clast_ref[gb] == 1, cpos_ref[gb] == 0, 1, True)


def _ml_spread():
    spread = np.zeros((2, LANES, GROUP_W), np.float32)
    for d in range(2):
        for h in range(N_HEADS):
            spread[d, d * N_HEADS + h, h * HEAD_D:(h + 1) * HEAD_D] = 1.0
    return jnp.asarray(np.tile(spread, (1, SPREAD_TERMS, 1)), BF16)


def _mlstm(z, gates, ml_bi, ml_bf, cps):
    t = z.shape[0]
    n = t // (CH * cps)
    blk = (CH * cps, GROUP_W)
    gblk = (CH * cps, LANES)
    cum = _tri_consts()
    spread = _ml_spread()
    pad = jnp.zeros((LANES - 2 * N_HEADS,), F32)
    bias = jnp.stack([jnp.concatenate([ml_bi.reshape(-1), pad]),
                      jnp.concatenate([ml_bf.reshape(-1), pad])]).reshape(2, 1, LANES)
    full = lambda a: pl.BlockSpec(a.shape, lambda g, cp, cl: (0,) * a.ndim)
    in_specs = [pl.BlockSpec(blk, _fwd_map(5)), pl.BlockSpec(blk, _fwd_map(6)), pl.BlockSpec(blk, _fwd_map(7)),
                pl.BlockSpec(gblk, _fwd_map(0)), pl.BlockSpec(gblk, _fwd_map(1)),
                pl.BlockSpec(blk, _bwd_map(n, 5)), pl.BlockSpec(blk, _bwd_map(n, 6)),
                pl.BlockSpec(blk, _bwd_map(n, 7)),
                pl.BlockSpec(gblk, _bwd_map(n, 0)), pl.BlockSpec(gblk, _bwd_map(n, 1)),
                full(bias), full(cum), full(spread)]
    state = [pltpu.VMEM((N_HEADS, HEAD_D, 2 * HEAD_D), F32), pltpu.VMEM((1, LANES), F32)]
    return (_ml_kernel, in_specs, [z, z, z, gates, gates, z, z, z, gates, gates, bias, cum, spread], state * 2,
            "fffffbbbbbccc")


def _head_norm(x, center):
    outs = []
    for h in range(N_HEADS):
        xh = x[:, h * HEAD_D:(h + 1) * HEAD_D]
        if center:
            xh = xh - jnp.mean(xh, axis=1, keepdims=True)
        outs.append(xh * lax.rsqrt(jnp.mean(xh * xh, axis=1, keepdims=True) + EPS))
    return jnp.concatenate(outs, axis=1)


def _layer_norm(x, g, b):
    xc = x - jnp.mean(x, axis=1, keepdims=True)
    return xc * lax.rsqrt(jnp.mean(xc * xc, axis=1, keepdims=True) + EPS) * g + b


def _mix_rows(alpha, even, af, ab, ag, bf, bb, bg, h_ref, w_ref, lng_ref, lnb_ref):
    a = af[...].astype(F32) + ab[...].astype(F32)
    b = bf[...].astype(F32) + bb[...].astype(F32)
    if even:
        a = _head_norm(a, False) * _silu(ag[...])
        b = b * _gelu_tanh(bg[...])
    else:
        a = _head_norm(a, False) * _silu(ag[...])
        b = _head_norm(b, True) * _sigmoid(bg[...])
    mix = _dot(a, w_ref[0:GROUP_W, :]) + _dot(b, w_ref[GROUP_W:, :])
    return _layer_norm(alpha * h_ref[...] + mix, lng_ref[...], lnb_ref[...])


N_MIX_OPERANDS = 10


def _mix_specs(tm, col_a, col_b, index):
    half = lambda c: pl.BlockSpec((tm, GROUP_W), lambda *g: (index(*g), c))
    rows = pl.BlockSpec((tm, D_MODEL), lambda *g: (index(*g), 0))
    vec = pl.BlockSpec((1, D_MODEL), lambda *g: (0, 0))
    return [half(0), half(0), half(col_a), half(0), half(0), half(col_b), rows,
            pl.BlockSpec((D_MODEL, D_MODEL), lambda *g: (0, 0)), vec, vec]


def _mix_ffn_kernel(alpha, *refs):
    mix_refs = refs[:N_MIX_OPERANDS]
    wg_ref, wu_ref, wo_ref, lng_ref, lnb_ref, o_ref, h1_ref, acc_ref = refs[N_MIX_OPERANDS:]
    f = pl.program_id(1)

    @pl.when(f == 0)
    def _():
        h1_ref[...] = _mix_rows(alpha, True, *mix_refs)
        acc_ref[...] = jnp.zeros_like(acc_ref)

    x = h1_ref[...].astype(BF16)
    act = _silu(_dot(x, wg_ref[...])) * _dot(x, wu_ref[...])
    acc_ref[...] += _dot(act, wo_ref[...])

    @pl.when(f == pl.num_programs(1) - 1)
    def _():
        o_ref[...] = _layer_norm(alpha * h1_ref[...] + acc_ref[...], lng_ref[...], lnb_ref[...])


def _mix_ffn(alpha, af, ab, bf, bb, z, col_a, col_b, h, w_out, ln1_g, ln1_b, wi, wo, ln2_g, ln2_b):
    t = h.shape[0]
    d_ff = wo.shape[0]
    tm = _pick_tile(t, 640)
    tf = _pick_tile(d_ff, 1536)
    nf = d_ff // tf
    rows = pl.BlockSpec((tm, D_MODEL), lambda i, f: (i, 0))
    vec = pl.BlockSpec((1, D_MODEL), lambda i, f: (0, 0))
    return pl.pallas_call(
        functools.partial(_mix_ffn_kernel, alpha), grid=(t // tm, nf),
        in_specs=_mix_specs(tm, col_a, col_b, lambda i, f: i) + [
            pl.BlockSpec((D_MODEL, tf), lambda i, f: (0, f)),
            pl.BlockSpec((D_MODEL, tf), lambda i, f: (0, f + nf)),
            pl.BlockSpec((tf, D_MODEL), lambda i, f: (f, 0)), vec, vec],
        out_specs=rows, out_shape=jax.ShapeDtypeStruct((t, D_MODEL), F32),
        scratch_shapes=[pltpu.VMEM((tm, D_MODEL), F32), pltpu.VMEM((tm, D_MODEL), F32)],
        compiler_params=_params(2), name="mix_ffn")(
            af, ab, z, bf, bb, z, h, w_out, ln1_g.reshape(1, -1), ln1_b.reshape(1, -1),
            wi, wi, wo, ln2_g.reshape(1, -1), ln2_b.reshape(1, -1))


EXPERT_TILE = 1024
DMA_UNROLL = 8
DMA_PRIORITIES = 2


def _mix_route_kernel(alpha, pad_starts, *refs):
    mix_refs = refs[:N_MIX_OPERANDS]
    router_ref, tri_ref, o_ref, meta_ref, gate_ref, cnt_ref, carry_ref = refs[N_MIX_OPERANDS:]

    @pl.when(pl.program_id(0) == 0)
    def _():
        carry_ref[...] = jnp.zeros_like(carry_ref)

    h1 = _mix_rows(alpha, False, *mix_refs)
    o_ref[...] = h1
    tm = h1.shape[0]
    row = pl.program_id(0) * tm + lax.broadcasted_iota(jnp.int32, (tm, 1), 0)
    is_pad = row < 0
    for s in pad_starts:
        is_pad = is_pad | ((row >= s) & (row < s + PAD_ROWS))
    is_token = jnp.logical_not(is_pad)

    logits = _dot_16bit(h1, router_ref[...])
    lane = lax.broadcasted_iota(jnp.int32, logits.shape, 1)
    logits = jnp.where(lane < N_EXP, logits, -jnp.inf)
    top1 = jnp.max(logits, axis=1, keepdims=True)
    idx1 = jnp.min(jnp.where(logits == top1, lane, LANES), axis=1, keepdims=True)
    rest = jnp.where(lane == idx1, -jnp.inf, logits)
    top2 = jnp.max(rest, axis=1, keepdims=True)
    idx2 = jnp.min(jnp.where(rest == top2, lane, LANES), axis=1, keepdims=True)
    g2 = jnp.exp(top2 - top1)
    denom = 1.0 + g2
    hit1 = lane == idx1
    hit2 = lane == idx2
    both = jnp.where(is_token, jnp.where(hit1, 1.0, 0.0) + jnp.where(hit2, 1.0, 0.0), 0.0)
    prefix = _dot(tri_ref[...], both) + carry_ref[...]
    rank1 = jnp.sum(jnp.where(hit1, prefix, 0.0), axis=1, keepdims=True).astype(jnp.int32)
    rank2 = jnp.sum(jnp.where(hit2, prefix, 0.0), axis=1, keepdims=True).astype(jnp.int32)
    carry_ref[...] += jnp.sum(both, axis=0, keepdims=True)
    cnt_ref[...] = carry_ref[...]
    meta_ref[...] = jnp.where(lane == 0, idx1, jnp.where(lane == 1, idx2, jnp.where(
        lane == 2, rank1, jnp.where(lane == 3, rank2, jnp.where(is_token & (lane == 4), 1, 0)))))
    gate_ref[...] = jnp.where(lane == 0, 1.0 / denom, jnp.where(lane == 1, g2 / denom, 0.0))


def _mix_route(alpha, af, ab, bf, bb, z, col_a, col_b, h, w_out, ln_g, ln_b, router, pad_starts):
    t = h.shape[0]
    tm = _pick_tile(t, 640)
    router_p = _stack_weight_terms(jnp.pad(router, ((0, 0), (0, LANES - router.shape[1]))))
    pos = np.arange(tm)
    tri = jnp.asarray(pos[:, None] > pos[None, :], BF16)
    rows = lambda w: pl.BlockSpec((tm, w), lambda i: (i, 0))
    const = lambda a: pl.BlockSpec(a.shape, lambda i: (0, 0))
    return pl.pallas_call(
        functools.partial(_mix_route_kernel, alpha, pad_starts), grid=(t // tm,),
        in_specs=_mix_specs(tm, col_a, col_b, lambda i: i) + [const(router_p), const(tri)],
        out_specs=[rows(D_MODEL), rows(LANES), rows(LANES), pl.BlockSpec((1, LANES), lambda i: (0, 0))],
        out_shape=[jax.ShapeDtypeStruct((t, D_MODEL), F32), jax.ShapeDtypeStruct((t, LANES), jnp.int32),
                   jax.ShapeDtypeStruct((t, LANES), F32), jax.ShapeDtypeStruct((1, LANES), F32)],
        scratch_shapes=[pltpu.VMEM((1, LANES), F32)],
        compiler_params=_params(1), name="mix_route")(
            af, ab, z, bf, bb, z, h, w_out, ln_g.reshape(1, -1), ln_b.reshape(1, -1), router_p, tri)


def _dispatch_plan(meta, counts, t):
    cnt = counts[0, :N_EXP].astype(jnp.int32)
    padded = ((cnt + EXPERT_TILE - 1) // EXPERT_TILE) * EXPERT_TILE
    ends = jnp.cumsum(padded)
    off = ends - padded

    def base(e):
        return sum(jnp.where(e == i, off[i], 0) for i in range(N_EXP))

    is_token = meta[:, 4] > 0
    n_tiles = -(-2 * t // EXPERT_TILE) + N_EXP
    spare = n_tiles * EXPERT_TILE + 2 * (jnp.cumsum(jnp.logical_not(is_token).astype(jnp.int32)) - 1)
    pos1 = (base(meta[:, 0]) + meta[:, 2]).astype(jnp.int32)
    pos2 = (base(meta[:, 1]) + meta[:, 3]).astype(jnp.int32)
    scatter = (jnp.where(is_token, pos1, spare), jnp.where(is_token, pos2, spare + 1))
    gather = (jnp.where(is_token, pos1, 0), jnp.where(is_token, pos2, 0))
    starts = jnp.arange(n_tiles, dtype=jnp.int32) * EXPERT_TILE
    tile_expert = jnp.minimum(jnp.sum(starts[:, None] >= ends[None, :], axis=1), N_EXP - 1).astype(jnp.int32)
    n_active = (ends[-1] // EXPERT_TILE).astype(jnp.int32).reshape(1)
    return ends.astype(jnp.int32), scatter, gather, tile_expert, n_active, n_tiles


assert D_MODEL == SUBLANES * LANES


def _store_token_tiles(ref, x):
    n = x.shape[0]
    for s in range(SUBLANES):
        ref[pl.ds(s, n, stride=SUBLANES), :] = x[:, s * LANES:(s + 1) * LANES]


def _load_token_tiles(ref, n):
    return jnp.concatenate([ref[pl.ds(s, n, stride=SUBLANES), :] for s in range(SUBLANES)], axis=1)


def _tile_rows(i):
    return pl.ds(pl.multiple_of(i * SUBLANES, SUBLANES), SUBLANES)


def _dispatch_kernel(n_tiles, ends_ref, pos1_ref, pos2_ref, h_ref, xs_ref, tok_ref, zero_ref, sem, zero_sem):
    n = h_ref.shape[0]

    @pl.when(pl.program_id(0) == 0)
    def _():
        zero_ref[...] = jnp.zeros_like(zero_ref)

        def clear_last_tile(e):
            first_row = pl.multiple_of((ends_ref[e] - EXPERT_TILE) * SUBLANES, SUBLANES)
            return pltpu.make_async_copy(zero_ref, xs_ref.at[pl.ds(first_row, EXPERT_TILE * SUBLANES)], zero_sem)

        def has_tiles(e):
            return ends_ref[e] > (ends_ref[e - 1] if e else 0)

        for e in range(N_EXP):
            pl.when(has_tiles(e))(lambda e=e: clear_last_tile(e).start())
        for e in range(N_EXP):
            pl.when(has_tiles(e))(lambda e=e: clear_last_tile(e).wait())

        def clear_tile(i):
            first_row = pl.multiple_of(i * (EXPERT_TILE * SUBLANES), SUBLANES)
            return pltpu.make_async_copy(zero_ref, xs_ref.at[pl.ds(first_row, EXPERT_TILE * SUBLANES)], zero_sem)

        used = ends_ref[N_EXP - 1] // EXPERT_TILE
        lax.fori_loop(used, n_tiles, lambda i, c: (clear_tile(i).start(), c)[1], 0)
        lax.fori_loop(used, n_tiles, lambda i, c: (clear_tile(i).wait(), c)[1], 0)

    g = pl.program_id(0)
    slot = g % 2
    _store_token_tiles(tok_ref.at[slot], h_ref[...])

    def tile_copy(r, p):
        return pltpu.make_async_copy(tok_ref.at[slot].at[_tile_rows(r)], xs_ref.at[_tile_rows(p)], sem.at[slot])

    def start(pair, c):
        for prio in range(DMA_PRIORITIES):
            r = pair * DMA_PRIORITIES + prio
            tile_copy(r, pos1_ref[0, 0, r]).start(priority=prio)
            tile_copy(r, pos2_ref[0, 0, r]).start(priority=prio)
        return c

    lax.fori_loop(0, n // DMA_PRIORITIES, start, 0, unroll=DMA_UNROLL // DMA_PRIORITIES)

    def wait_slot(s):
        all_rows = pltpu.make_async_copy(tok_ref.at[s], xs_ref.at[pl.ds(0, n * SUBLANES)], sem.at[s])
        all_rows.wait()
        all_rows.wait()

    pl.when(g > 0)(lambda: wait_slot(1 - slot))
    pl.when(g == pl.num_programs(0) - 1)(lambda: wait_slot(slot))


def _dispatch(h, ends, pos1, pos2, n_tiles, n_spare):
    t = h.shape[0]
    n_rows = n_tiles * EXPERT_TILE + n_spare
    tm = _pick_tile(t, 768)
    idx = lambda: pl.BlockSpec((1, 1, tm), lambda i, ends: (i, 0, 0), memory_space=pltpu.SMEM)
    grid_spec = pltpu.PrefetchScalarGridSpec(
        num_scalar_prefetch=1, grid=(t // tm,),
        in_specs=[idx(), idx(), pl.BlockSpec((tm, D_MODEL), lambda i, ends: (i, 0))],
        out_specs=pl.BlockSpec(memory_space=pl.ANY),
        scratch_shapes=[pltpu.VMEM((2, tm * SUBLANES, LANES), F32),
                        pltpu.VMEM((EXPERT_TILE * SUBLANES, LANES), F32),
                        pltpu.SemaphoreType.DMA((2,)), pltpu.SemaphoreType.DMA(())])
    return pl.pallas_call(
        functools.partial(_dispatch_kernel, n_tiles), grid_spec=grid_spec,
        out_shape=jax.ShapeDtypeStruct((n_rows * SUBLANES, LANES), F32),
        compiler_params=_params(1), name="dispatch")(
            ends, pos1.reshape(t // tm, 1, tm), pos2.reshape(t // tm, 1, tm), h)


def _experts_kernel(te_ref, na_ref, x_ref, wg_ref, wu_ref, wo_ref, y_ref, xb_ref, acc_ref):
    i = pl.program_id(0)
    f = pl.program_id(1)
    last = f == pl.num_programs(1) - 1
    active = i < na_ref[0]

    @pl.when(active)
    def _():
        @pl.when(f == 0)
        def _():
            acc_ref[...] = jnp.zeros_like(acc_ref)
            xb_ref[...] = _load_token_tiles(x_ref, EXPERT_TILE).astype(BF16)

        x = xb_ref[...]
        act = _silu(_dot(x, wg_ref[0])) * _dot(x, wu_ref[0])
        acc_ref[...] += _dot(act, wo_ref[0])

        @pl.when(last)
        def _():
            _store_token_tiles(y_ref, acc_ref[...])

    @pl.when(jnp.logical_not(active) & last)
    def _():
        y_ref[...] = jnp.zeros_like(y_ref)


def _experts(xs, tile_expert, n_active, wi, wo, n_tiles):
    n_rows = n_tiles * EXPERT_TILE
    e_ff = wo.shape[1]
    tf = _pick_tile(e_ff, 512)
    nf = e_ff // tf
    rows = pl.BlockSpec((EXPERT_TILE * SUBLANES, LANES), lambda i, f, te, na: (i, 0))
    rows_in = pl.BlockSpec((EXPERT_TILE * SUBLANES, LANES),
                           lambda i, f, te, na: (jnp.minimum(i, jnp.maximum(na[0] - 1, 0)), 0))
    def held(i, f, na):
        return jnp.where(i < na[0], f, nf - 1)

    grid_spec = pltpu.PrefetchScalarGridSpec(
        num_scalar_prefetch=2, grid=(n_rows // EXPERT_TILE, nf),
        in_specs=[rows_in, pl.BlockSpec((1, D_MODEL, tf), lambda i, f, te, na: (te[i], 0, held(i, f, na))),
                  pl.BlockSpec((1, D_MODEL, tf), lambda i, f, te, na: (te[i], 0, held(i, f, na) + nf)),
                  pl.BlockSpec((1, tf, D_MODEL), lambda i, f, te, na: (te[i], held(i, f, na), 0))],
        out_specs=rows,
        scratch_shapes=[pltpu.VMEM((EXPERT_TILE, D_MODEL), BF16), pltpu.VMEM((EXPERT_TILE, D_MODEL), F32)])
    return pl.pallas_call(
        _experts_kernel, grid_spec=grid_spec, out_shape=jax.ShapeDtypeStruct((n_rows * SUBLANES, LANES), F32),
        compiler_params=_params(2), name="experts")(tile_expert, n_active, xs, wi, wi, wo)


def _gather_start(n, pos1_ref, pos2_ref, ys_ref, a_ref, b_ref, sems):
    def tile_copy(p, buf, r, which):
        return pltpu.make_async_copy(ys_ref.at[_tile_rows(p)], buf.at[_tile_rows(r)], sems.at[which])

    def start(pair, c):
        for prio in range(DMA_PRIORITIES):
            r = pair * DMA_PRIORITIES + prio
            tile_copy(pos1_ref[0, 0, r], a_ref, r, 0).start(priority=prio)
            tile_copy(pos2_ref[0, 0, r], b_ref, r, 1).start(priority=prio)
        return c

    lax.fori_loop(0, n // DMA_PRIORITIES, start, 0, unroll=DMA_UNROLL // DMA_PRIORITIES)


def _gather_finish(alpha, h_ref, gate_ref, ys_ref, lng_ref, lnb_ref, a_ref, b_ref, sems):
    n = h_ref.shape[0]
    for which, buf in enumerate((a_ref, b_ref)):
        pltpu.make_async_copy(ys_ref.at[pl.ds(0, n * SUBLANES)], buf, sems.at[which]).wait()
    gate = gate_ref[...]
    y = gate[:, 0:1] * _load_token_tiles(a_ref, n) + gate[:, 1:2] * _load_token_tiles(b_ref, n)
    return _layer_norm(alpha * h_ref[...] + y, lng_ref[...], lnb_ref[...])


def _combine_kernel(alpha, pos1_ref, pos2_ref, h_ref, gate_ref, ys_ref, lng_ref, lnb_ref, o_ref,
                    a_ref, b_ref, sems):
    _gather_start(h_ref.shape[0], pos1_ref, pos2_ref, ys_ref, a_ref, b_ref, sems)
    o_ref[...] = _gather_finish(alpha, h_ref, gate_ref, ys_ref, lng_ref, lnb_ref, a_ref, b_ref, sems)


def _combine_scratch(tm):
    return [pltpu.VMEM((tm * SUBLANES, LANES), F32), pltpu.VMEM((tm * SUBLANES, LANES), F32),
            pltpu.SemaphoreType.DMA((2,))]


def _combine(alpha, h, gates, ys, pos1, pos2, ln_g, ln_b):
    t = h.shape[0]
    tm = _pick_tile(t, 512)
    idx = lambda: pl.BlockSpec((1, 1, tm), lambda i: (i, 0, 0), memory_space=pltpu.SMEM)
    rows = pl.BlockSpec((tm, D_MODEL), lambda i: (i, 0))
    vec = pl.BlockSpec((1, D_MODEL), lambda i: (0, 0))
    return pl.pallas_call(
        functools.partial(_combine_kernel, alpha), grid=(t // tm,),
        in_specs=[idx(), idx(), rows, pl.BlockSpec((tm, LANES), lambda i: (i, 0)),
                  pl.BlockSpec(memory_space=pl.ANY), vec, vec],
        out_specs=rows, out_shape=jax.ShapeDtypeStruct((t, D_MODEL), F32),
        scratch_shapes=_combine_scratch(tm),
        compiler_params=_params(1), name="combine")(
            pos1.reshape(t // tm, 1, tm), pos2.reshape(t // tm, 1, tm), h, gates, ys,
            ln_g.reshape(1, -1), ln_b.reshape(1, -1))


def _output_tables(seq_shapes):
    kind, live = [], []
    window = [[], []]
    recent = [(0, 0), (0, 0)]
    for grp, (b, l) in enumerate(seq_shapes):
        for i in range(b):
            for c in range(l // CH + 1):
                kind.append(grp)
                live.append(int(c > 0))
                recent[grp] = (i, max(c - 1, 0))
                for g2 in range(2):
                    window[g2].append(recent[g2])
    tables = [kind, live] + [[w[j] for w in window[g2]] for g2 in range(2) for j in range(2)]
    return [jnp.asarray(np.asarray(x, np.int32)) for x in tables]


def _combine_out_kernel(alpha, kind_ref, live_ref, ab_ref, ac_ref, bb_ref, bc_ref,
                        pos1_ref, pos2_ref, nxt1_ref, nxt2_ref, h_ref, gate_ref, ys_ref, lng_ref, lnb_ref,
                        outa_ref, outb_ref, a_ref, b_ref, sems):
    g = pl.program_id(0)
    last = pl.num_programs(0) - 1
    slot = g % 2
    nxt = jnp.minimum(g + 1, last)

    @pl.when((g == 0) & (live_ref[0] == 1))
    def _():
        _gather_start(CH, pos1_ref, pos2_ref, ys_ref, a_ref.at[0], b_ref.at[0], sems.at[0])

    @pl.when((g < last) & (live_ref[nxt] == 1))
    def _():
        _gather_start(CH, nxt1_ref, nxt2_ref, ys_ref, a_ref.at[1 - slot], b_ref.at[1 - slot], sems.at[1 - slot])

    @pl.when(live_ref[g] == 1)
    def _():
        res = _gather_finish(alpha, h_ref, gate_ref, ys_ref, lng_ref, lnb_ref,
                             a_ref.at[slot], b_ref.at[slot], sems.at[slot])
        for grp, out_ref in enumerate((outa_ref, outb_ref)):
            @pl.when(kind_ref[g] == grp)
            def _(out_ref=out_ref):
                out_ref[0] = res


def _combine_out(alpha, h, gates, ys, pos1, pos2, ln_g, ln_b, seq_shapes):
    t = h.shape[0]
    n = t // CH
    tables = _output_tables(seq_shapes)
    idx = lambda: pl.BlockSpec((1, 1, CH), lambda g, *tb: (g, 0, 0), memory_space=pltpu.SMEM)
    idx_next = lambda: pl.BlockSpec((1, 1, CH), lambda g, *tb: (jnp.minimum(g + 1, n - 1), 0, 0),
                                    memory_space=pltpu.SMEM)
    rows = lambda w: pl.BlockSpec((CH, w), lambda g, *tb: (g, 0))
    vec = pl.BlockSpec((1, D_MODEL), lambda g, *tb: (0, 0))
    out = lambda grp: pl.BlockSpec((1, CH, D_MODEL),
                                   lambda g, *tb: (tb[2 + 2 * grp][g], tb[3 + 2 * grp][g], 0))
    fetched = pltpu.VMEM((2, CH * SUBLANES, LANES), F32)
    grid_spec = pltpu.PrefetchScalarGridSpec(
        num_scalar_prefetch=len(tables), grid=(n,),
        in_specs=[idx(), idx(), idx_next(), idx_next(), rows(D_MODEL), rows(LANES),
                  pl.BlockSpec(memory_space=pl.ANY), vec, vec],
        out_specs=[out(0), out(1)], scratch_shapes=[fetched, fetched, pltpu.SemaphoreType.DMA((2, 2))])
    pos1, pos2 = pos1.reshape(n, 1, CH), pos2.reshape(n, 1, CH)
    return pl.pallas_call(
        functools.partial(_combine_out_kernel, alpha), grid_spec=grid_spec,
        out_shape=[jax.ShapeDtypeStruct((b, l, D_MODEL), F32) for b, l in seq_shapes],
        compiler_params=_params(1), name="combine_out")(
            *tables, pos1, pos2, pos1, pos2, h, gates, ys, ln_g.reshape(1, -1), ln_b.reshape(1, -1))


def _moe(alpha, h, meta, gates, counts, wi, wo, ln_g, ln_b, n_pad_rows, out_shapes=None):
    t = h.shape[0]
    ends, scatter, gather, tile_expert, n_active, n_tiles = _dispatch_plan(meta, counts, t)
    n_spare = 2 * n_pad_rows
    xs = _dispatch(h, ends, scatter[0], scatter[1], n_tiles, n_spare)
    ys = _experts(xs, tile_expert, n_active, wi, wo, n_tiles)
    if out_shapes is None:
        return _combine(alpha, h, gates, ys, gather[0], gather[1], ln_g, ln_b)
    return _combine_out(alpha, h, gates, ys, gather[0], gather[1], ln_g, ln_b, out_shapes)


def kernel(x_prompt, x_sample, meta, e_w_in, e_conv_w, e_conv_b, e_lru_wa, e_lru_ba, e_lru_wx, e_lru_bx,
           e_lru_lambda, e_w_out, e_ffn_wi, e_ffn_wo, o_w_in, o_hg_lower, o_ml_bi, o_ml_bf, o_w_out, o_router,
           o_exp_wi, o_exp_wo, ln_g, ln_b):
    groups = (x_prompt, x_sample)
    depth = ln_g.shape[0]
    alpha = (2.0 * depth) ** 0.25
    seq_shapes = [(x.shape[0], x.shape[1]) for x in groups]
    for _, l in seq_shapes:
        assert l % CH == 0
    cpos_np, clast_np = _chunk_tables(seq_shapes)
    cpos, clast = jnp.asarray(cpos_np), jnp.asarray(clast_np)
    max_rows = max(l for _, l in seq_shapes) + CH
    pad_starts = tuple(int(i) * CH for i in np.flatnonzero(cpos_np == 0))

    head = jnp.concatenate([jnp.zeros((PAD_ROWS, D_MODEL), F32), meta.astype(F32)], axis=0)
    parts = []
    for x in groups:
        full = jnp.concatenate([jnp.broadcast_to(head[None], (x.shape[0], CH, D_MODEL)), x], axis=1)
        parts.append(full.reshape(-1, D_MODEL))
    h = jnp.concatenate(parts, axis=0)
    t = h.shape[0]

    n_odd_cols = 9 * GROUP_W
    for layer in range(depth):
        p = layer // 2
        if layer % 2 == 0:
            z = _project(h, e_w_in[p].astype(BF16))
            (ret_f, ret_b), (lru_f, lru_b) = _chunk_walk(
                [_retention(z, max_rows),
                 _rglru(z, 4, e_conv_w[p], e_conv_b[p], e_lru_wa[p], e_lru_ba[p],
                        e_lru_wx[p], e_lru_bx[p], e_lru_lambda[p])], cpos, clast, t, "even_mixers")
            h = _mix_ffn(alpha, ret_f, ret_b, lru_f, lru_b, z, 3, 5, h, e_w_out[p].astype(BF16),
                         ln_g[layer, 0], ln_b[layer, 0], e_ffn_wi[p].astype(BF16), e_ffn_wo[p].astype(BF16),
                         ln_g[layer, 1], ln_b[layer, 1])
        else:
            w_in = o_w_in[p]
            n_gate = 2 * N_HEADS
            gate_pad = ((0, 0), (0, LANES - n_gate))
            w_gates = jnp.concatenate([jnp.pad(w_in[:, n_odd_cols:n_odd_cols + n_gate], gate_pad),
                                       jnp.pad(w_in[:, n_odd_cols + n_gate:], gate_pad)], axis=1)
            z, gates = _project(h, w_in[:, :n_odd_cols].astype(BF16), w_gates)
            cps = ODD_CHUNKS_PER_STEP if (t // CH) % ODD_CHUNKS_PER_STEP == 0 else 1
            (hg_f, hg_b), (ml_f, ml_b) = _chunk_walk(
                [_hgrn2(z, o_hg_lower, layer, cps), _mlstm(z, gates, o_ml_bi[p], o_ml_bf[p], cps)],
                cpos, clast, t, "odd_mixers", cps)
            h, meta_r, gates_r, counts = _mix_route(
                alpha, hg_f, hg_b, ml_f, ml_b, z, 4, 8, h, o_w_out[p].astype(BF16),
                ln_g[layer, 0], ln_b[layer, 0], o_router[p], pad_starts)
            h = _moe(alpha, h, meta_r, gates_r, counts, o_exp_wi[p].astype(BF16), o_exp_wo[p].astype(BF16),
                     ln_g[layer, 1], ln_b[layer, 1], len(pad_starts) * PAD_ROWS,
                     seq_shapes if layer == depth - 1 else None)
            if layer == depth - 1:
                return tuple(h)

    outs = []
    row = 0
    for b, l in seq_shapes:
        n = b * (l + CH)
        outs.append(h[row:row + n].reshape(b, l + CH, D_MODEL)[:, CH:])
        row += n
    return tuple(outs)
```

```python
import functools
import math

import numpy as np
import jax
import jax.numpy as jnp
from jax import lax
from jax.experimental import pallas as pl
from jax.experimental.pallas import tpu as pltpu

F32 = jnp.float32
BF16 = jnp.bfloat16

D_MODEL = 1024
GROUP_W = D_MODEL // 2
N_HEADS = 4
HEAD_D = GROUP_W // N_HEADS
N_META = 16
ROPE_BASE = 10000.0
LRU_BLOCKS = 8
LRU_BW = GROUP_W // LRU_BLOCKS
LRU_C = 8.0
N_EXP = 8
EPS = 1e-5

LANES = 128
SUBLANES = 8
CH = 128
PAD_ROWS = CH - N_META
SUB = SUBLANES
GRP = 4 * SUB
ODD_CHUNKS_PER_STEP = 2
NEG_BIG = -1e30
LOG2_E = math.log2(math.e)
VMEM_LIMIT = 56 * 1024 * 1024


def _dot(a, b):
    return jnp.dot(a.astype(BF16), b.astype(BF16), preferred_element_type=F32)


def _dot_nt(a, b):
    return lax.dot_general(a.astype(BF16), b.astype(BF16), (((1,), (1,)), ((), ())),
                           preferred_element_type=F32)


def _dot_tn(a, b):
    return _dot(a.T, b)


def _bf16_terms(x, n_terms):
    terms = []
    for _ in range(n_terms):
        t = x.astype(BF16)
        terms.append(t)
        x = x - t.astype(F32)
    return terms


def _stack_weight_terms(w):
    hi, lo = _bf16_terms(w, 2)
    return jnp.concatenate([hi, lo, hi], axis=0)


def _dot_16bit(x, w_terms):
    hi, lo = _bf16_terms(x, 2)
    return jnp.dot(jnp.concatenate([hi, hi, lo], axis=1), w_terms, preferred_element_type=F32)


PREFIX_TERMS = 3
SPREAD_TERMS = 2


def _prefix_dot(op_tiled, x):
    return jnp.dot(op_tiled, jnp.concatenate(_bf16_terms(x, PREFIX_TERMS), axis=0),
                   preferred_element_type=F32)


def _spread_dot(x, op_tiled):
    return jnp.dot(jnp.concatenate(_bf16_terms(x, SPREAD_TERMS), axis=1), op_tiled,
                   preferred_element_type=F32)


def _sigmoid(x):
    return 1.0 / (1.0 + jnp.exp(-x))


def _silu(x):
    return x * _sigmoid(x)


def _gelu_tanh(x):
    return 0.5 * x * (1.0 + jnp.tanh(math.sqrt(2.0 / math.pi) * (x + 0.044715 * (x * x * x))))


def _softplus(x):
    return jnp.maximum(x, 0.0) + jnp.log(1.0 + jnp.exp(-jnp.abs(x)))


def _pick_tile(total, target):
    best = LANES
    for t in range(LANES, min(total, target) + 1, LANES):
        if total % t == 0:
            best = t
    return best


def _params(n_axes, sem="arbitrary"):
    return pltpu.CompilerParams(dimension_semantics=(sem,) * n_axes, vmem_limit_bytes=VMEM_LIMIT)


def _proj_kernel(x_ref, w_ref, o_ref):
    o_ref[...] = _dot(x_ref[...], w_ref[...])


def _proj_gates_kernel(x_ref, w_ref, wg_ref, o_ref, g_ref):
    g_ref[...] = _dot_16bit(x_ref[...], wg_ref[...])
    o_ref[...] = _dot(x_ref[...], w_ref[...])


PROJ_OUT_BLOCK_BYTES = 8 * 1024 * 1024


def _project(x, w, w_gates=None):
    t, k = x.shape
    n = w.shape[1]
    tm = _pick_tile(t, PROJ_OUT_BLOCK_BYTES // (4 * n))
    x_spec = pl.BlockSpec((tm, k), lambda i: (i, 0))
    w_spec = pl.BlockSpec((k, n), lambda i: (0, 0))
    o_spec = pl.BlockSpec((tm, n), lambda i: (i, 0))
    if w_gates is None:
        return pl.pallas_call(
            _proj_kernel, grid=(t // tm,), in_specs=[x_spec, w_spec], out_specs=o_spec,
            out_shape=jax.ShapeDtypeStruct((t, n), F32), compiler_params=_params(1),
            name="proj")(x, w)
    ng = w_gates.shape[1]
    w_gates = _stack_weight_terms(w_gates)
    return pl.pallas_call(
        _proj_gates_kernel, grid=(t // tm,),
        in_specs=[x_spec, w_spec, pl.BlockSpec(w_gates.shape, lambda i: (0, 0))],
        out_specs=[o_spec, pl.BlockSpec((tm, ng), lambda i: (i, 0))],
        out_shape=[jax.ShapeDtypeStruct((t, n), F32), jax.ShapeDtypeStruct((t, ng), F32)],
        compiler_params=_params(1), name="proj_gates")(x, w, w_gates)


def _chunk_tables(seq_shapes):
    cpos, clast = [], []
    for b, l in seq_shapes:
        n = l // CH + 1
        for _ in range(b):
            cpos += list(range(n))
            clast += [0] * (n - 1) + [1]
    return np.asarray(cpos, np.int32), np.asarray(clast, np.int32)


def _fwd_map(col):
    return lambda g, cp, cl: (g, col)


def _bwd_map(n_chunks, col):
    return lambda g, cp, cl: (n_chunks - 1 - g, col)


def _walk_kernel(bodies, tags, n_scratch, cps, cpos_ref, clast_ref, *refs):
    refs = list(refs)
    ins = [[refs.pop(0) for _ in t] for t in tags]
    outs = [[refs.pop(0) for _ in range(2)] for _ in bodies]
    scratch = [[refs.pop(0) for _ in range(k)] for k in n_scratch]

    @pl.when(pl.program_id(0) == 0)
    def _():
        for part in scratch:
            for r in part:
                r[...] = jnp.zeros_like(r)

    n_chunks = pl.num_programs(0) * cps
    for j in range(cps):
        g = pl.program_id(0) * cps + j
        gb = n_chunks - 1 - g
        rows = {"f": pl.ds(j * CH, CH), "b": pl.ds((cps - 1 - j) * CH, CH)}

        def view(ref, tag):
            return ref if tag == "c" or cps == 1 else ref.at[rows[tag]]

        for body, i, tg, o, s in zip(bodies, ins, tags, outs, scratch):
            body(cpos_ref, clast_ref, g, gb, *[view(r, t) for r, t in zip(i, tg)],
                 view(o[0], "f"), view(o[1], "b"), *s)


def _chunk_walk(parts, cpos, clast, t, name, cps=1):
    n = t // (CH * cps)
    blk = (CH * cps, GROUP_W)
    bodies = [p[0] for p in parts]
    grid_spec = pltpu.PrefetchScalarGridSpec(
        num_scalar_prefetch=2, grid=(n,), in_specs=[s for p in parts for s in p[1]],
        out_specs=[pl.BlockSpec(blk, _fwd_map(0)), pl.BlockSpec(blk, _bwd_map(n, 0))] * len(parts),
        scratch_shapes=[s for p in parts for s in p[3]])
    outs = pl.pallas_call(
        functools.partial(_walk_kernel, bodies, [p[4] for p in parts], [len(p[3]) for p in parts], cps),
        grid_spec=grid_spec, out_shape=[jax.ShapeDtypeStruct((t, GROUP_W), BF16)] * (2 * len(parts)),
        compiler_params=_params(1), name=name)(cpos, clast, *[a for p in parts for a in p[2]])
    return [outs[2 * i:2 * i + 2] for i in range(len(parts))]


def _row_valid(first):
    row = lax.broadcasted_iota(jnp.int32, (CH, 1), 0)
    return row >= jnp.where(first, PAD_ROWS, 0)


def _ret_log_gamma():
    return np.log1p(-np.exp2(-5.0 - np.arange(N_HEADS, dtype=np.float64)))


def _ret_consts():
    lg = _ret_log_gamma()
    pos = np.arange(CH, dtype=np.float64)
    rel = pos[:, None] - pos[None, :]
    dmat = np.zeros((2, N_HEADS, CH, CH), np.float64)
    rows = np.zeros((4, CH, GROUP_W), np.float64)
    for h in range(N_HEADS):
        dmat[0, h] = np.where(rel >= 0, np.exp(np.maximum(rel, 0.0) * lg[h]), 0.0)
        dmat[1, h] = np.where(rel < 0, np.exp(np.maximum(-rel, 0.0) * lg[h]), 0.0)
        sl = slice(h * HEAD_D, (h + 1) * HEAD_D)
        rows[0, :, sl] = np.exp((pos + 1.0) * lg[h])[:, None]
        rows[1, :, sl] = np.exp((CH - 1.0 - pos) * lg[h])[:, None]
        rows[2, :, sl] = np.exp((CH - pos) * lg[h])[:, None]
        rows[3, :, sl] = np.exp(pos * lg[h])[:, None]
    return jnp.asarray(dmat, F32), jnp.asarray(rows, F32)


def _ret_direction(q_ref, k_ref, v_ref, cos_ref, sin_ref, dmat_ref, rin_ref, rout_ref, s_ref, o_ref,
                   reset, first):
    valid = _row_valid(first)
    cos = cos_ref[...]
    sin = sin_ref[...]
    chunk_decay = np.exp(CH * _ret_log_gamma())
    for h in range(N_HEADS):
        sl = slice(h * HEAD_D, (h + 1) * HEAD_D)
        q = q_ref[:, sl]
        k = k_ref[:, sl]
        v = v_ref[:, sl]
        q = q * cos + pltpu.roll(q, HEAD_D // 2, 1) * sin
        k = (k * cos + pltpu.roll(k, HEAD_D // 2, 1) * sin) * (HEAD_D ** -0.5)
        k = jnp.where(valid, k, 0.0)
        scores = _dot_nt(q, k) * dmat_ref[h]
        state = jnp.where(reset, 0.0, s_ref[h])
        o_ref[:, sl] = _dot(jnp.concatenate([scores, q * rin_ref[:, sl]], axis=1),
                            jnp.concatenate([v, state], axis=0)).astype(o_ref.dtype)
        s_ref[h] = float(chunk_decay[h]) * state + _dot_tn(k * rout_ref[:, sl], v)


def _ret_kernel(cpos_ref, clast_ref, g, gb, qf, kf, vf, cosf, sinf, qb, kb, vb, cosb, sinb, dmat_ref, rows_ref,
                of_ref, ob_ref, sf_ref, sb_ref):
    _ret_direction(qf, kf, vf, cosf, sinf, dmat_ref.at[0], rows_ref.at[0], rows_ref.at[1], sf_ref, of_ref,
                   cpos_ref[g] == 0, cpos_ref[g] == 0)
    _ret_direction(qb, kb, vb, cosb, sinb, dmat_ref.at[1], rows_ref.at[2], rows_ref.at[3], sb_ref, ob_ref,
                   clast_ref[gb] == 1, cpos_ref[gb] == 0)


def _rope_tables(n_rows):
    inv = ROPE_BASE ** (-jnp.arange(0, HEAD_D, 2, dtype=jnp.float32) / HEAD_D)
    pos = jnp.maximum(jnp.arange(n_rows, dtype=jnp.float32) - PAD_ROWS, 0.0)
    ang = pos[:, None] * inv[None, :]
    cos, sin = jnp.cos(ang), jnp.sin(ang)
    return jnp.concatenate([cos, cos], axis=1), jnp.concatenate([-sin, sin], axis=1)


def _retention(z, max_rows):
    t = z.shape[0]
    n = t // CH
    cos2, sin2 = _rope_tables(max_rows)
    dmat, rows = _ret_consts()
    blk = (CH, GROUP_W)
    in_specs = []
    for mk, pm in ((_fwd_map, lambda g, cp, cl: (cp[g], 0)),
                   (functools.partial(_bwd_map, n), lambda g, cp, cl: (cp[n - 1 - g], 0))):
        in_specs += [pl.BlockSpec(blk, mk(0)), pl.BlockSpec(blk, mk(1)), pl.BlockSpec(blk, mk(2)),
                     pl.BlockSpec((CH, HEAD_D), pm), pl.BlockSpec((CH, HEAD_D), pm)]
    in_specs += [pl.BlockSpec(dmat.shape, lambda g, cp, cl: (0, 0, 0, 0)),
                 pl.BlockSpec(rows.shape, lambda g, cp, cl: (0, 0, 0))]
    return (_ret_kernel, in_specs, [z, z, z, cos2, sin2, z, z, z, cos2, sin2, dmat, rows],
            [pltpu.VMEM((N_HEADS, HEAD_D, HEAD_D), F32)] * 2, "fffffbbbbbcc")


HALO = SUBLANES


def _shift_rows(x, s, fill, reverse):
    row = lax.broadcasted_iota(jnp.int32, (CH, 1), 0)
    if reverse:
        return jnp.where(row < CH - s, pltpu.roll(x, CH - s, 0), fill)
    return jnp.where(row >= s, pltpu.roll(x, s, 0), fill)


def _lru_direction(x_ref, prev_ref, next_ref, convw_ref, convb_ref, wg_ref, bias_ref, lam_ref, ext_ref,
                   carry_ref, o_ref, reset, first, last, reverse):
    valid = _row_valid(first)
    ext_ref[HALO:HALO + CH, :] = jnp.where(valid, x_ref[...], 0.0)
    ext_ref[0:HALO, :] = jnp.where(first, 0.0, prev_ref[...])
    ext_ref[HALO + CH:, :] = jnp.where(last, 0.0, next_ref[...])
    xc = convb_ref[...] + ext_ref[HALO - 2:HALO - 2 + CH, :] * convw_ref[0:1, :]
    for tap in range(1, 4):
        xc = xc + ext_ref[HALO - 2 + tap:HALO - 2 + tap + CH, :] * convw_ref[tap:tap + 1, :]

    log_sig_lam = -_softplus(-lam_ref[...])
    parts_a, parts_u = [], []
    for grp in range(GROUP_W // LANES):
        sl = slice(grp * LANES, (grp + 1) * LANES)
        xg = xc[:, sl]
        pre = _dot(xg, wg_ref[grp])
        r = _sigmoid(pre[:, :LANES] + bias_ref[0:1, sl])
        i = _sigmoid(pre[:, LANES:] + bias_ref[1:2, sl])
        a = jnp.exp(LRU_C * log_sig_lam[:, sl] * r)
        u = jnp.sqrt(1.0 - a * a) * (i * xg)
        parts_a.append(a)
        parts_u.append(jnp.where(valid, u, 0.0))
    a = jnp.concatenate(parts_a, axis=1)
    u = jnp.concatenate(parts_u, axis=1)

    n_blk = CH // SUBLANES
    a = a.reshape(n_blk, SUBLANES, GROUP_W)
    u = u.reshape(n_blk, SUBLANES, GROUP_W)
    row_in_blk = lax.broadcasted_iota(jnp.int32, (1, SUBLANES, 1), 1)
    s = 1
    while s < SUBLANES:
        keep = (row_in_blk < SUBLANES - s) if reverse else (row_in_blk >= s)
        shift = SUBLANES - s if reverse else s
        u = u + a * jnp.where(keep, pltpu.roll(u, shift, 1), 0.0)
        a = a * jnp.where(keep, pltpu.roll(a, shift, 1), 1.0)
        s *= 2
    edge = 0 if reverse else SUBLANES - 1
    carry = jnp.where(reset, 0.0, carry_ref[...])
    blocks = [None] * n_blk
    for blk in (reversed(range(n_blk)) if reverse else range(n_blk)):
        blocks[blk] = u[blk] + a[blk] * carry
        carry = blocks[blk][edge:edge + 1, :]
    o_ref[...] = jnp.concatenate(blocks, axis=0).astype(o_ref.dtype)
    carry_ref[...] = carry


def _lru_kernel(cpos_ref, clast_ref, g, gb, xf, pf, nf, xb, pb, nb, convw_ref, convb_ref, wg_ref, bias_ref,
                lam_ref, of_ref, ob_ref, extf_ref, extb_ref, cf_ref, cb_ref):
    _lru_direction(xf, pf, nf, convw_ref, convb_ref, wg_ref.at[0], bias_ref.at[0], lam_ref.at[0], extf_ref,
                   cf_ref, of_ref, cpos_ref[g] == 0, cpos_ref[g] == 0, clast_ref[g] == 1, False)
    _lru_direction(xb, pb, nb, convw_ref, convb_ref, wg_ref.at[1], bias_ref.at[1], lam_ref.at[1], extb_ref,
                   cb_ref, ob_ref, clast_ref[gb] == 1, cpos_ref[gb] == 0, clast_ref[gb] == 1, True)


def _lru_gate_weights(wa, wx):
    per = LANES // LRU_BW

    def block_diag(w):
        w = w.reshape(2, LRU_BLOCKS // per, per, LRU_BW, LRU_BW)
        eye = jnp.eye(per, dtype=w.dtype)
        return jnp.einsum("dgpij,pq->dgpiqj", w, eye).reshape(2, LRU_BLOCKS // per, LANES, LANES)

    return jnp.concatenate([block_diag(wa), block_diag(wx)], axis=-1).astype(BF16)


def _rglru(z, col, conv_w, conv_b, wa, ba, wx, bx, lam):
    t = z.shape[0]
    n = t // CH
    per = CH // HALO
    n_halo = t // HALO
    blk = (CH, GROUP_W)
    hblk = (HALO, GROUP_W)
    wg = _lru_gate_weights(wa, wx)
    bias = jnp.stack([ba, bx], axis=1)
    lam = lam.reshape(2, 1, GROUP_W)

    def prev_f(g, cp, cl):
        return (jnp.maximum(g * per - 1, 0), col)

    def next_f(g, cp, cl):
        return (jnp.minimum((g + 1) * per, n_halo - 1), col)

    def prev_b(g, cp, cl):
        return (jnp.maximum((n - 1 - g) * per - 1, 0), col)

    def next_b(g, cp, cl):
        return (jnp.minimum((n - g) * per, n_halo - 1), col)

    full = lambda a: pl.BlockSpec(a.shape, lambda g, cp, cl: (0,) * a.ndim)
    conv_b2 = conv_b.reshape(1, GROUP_W)
    in_specs = [pl.BlockSpec(blk, _fwd_map(col)), pl.BlockSpec(hblk, prev_f), pl.BlockSpec(hblk, next_f),
                pl.BlockSpec(blk, _bwd_map(n, col)), pl.BlockSpec(hblk, prev_b), pl.BlockSpec(hblk, next_b),
                full(conv_w), full(conv_b2), full(wg), full(bias), full(lam)]
    return (_lru_kernel, in_specs, [z, z, z, z, z, z, conv_w, conv_b2, wg, bias, lam],
            [pltpu.VMEM((CH + 2 * HALO, GROUP_W), F32)] * 2 + [pltpu.VMEM((1, GROUP_W), F32)] * 2,
            "fffbbbccccc")


def _tri_consts():
    pos = np.arange(CH)
    lower = (pos[:, None] >= pos[None, :]).astype(np.float32)
    ops = np.stack([lower, lower.T])
    return jnp.asarray(np.tile(ops, (1, 1, PREFIX_TERMS)), BF16)


def _hg_masks():
    pos = np.arange(CH)
    same_blk = pos[:, None] // SUB == pos[None, :] // SUB
    same_grp = pos[:, None] // GRP == pos[None, :] // GRP
    return jnp.asarray(np.stack([same_blk, same_grp]).astype(np.float32))


def _hg_direction(q_ref, f_ref, v_ref, lb, cum_ref, mask_ref, st_ref, o_ref, reset, first, reverse):
    valid = _row_valid(first)
    sub_i =lax.broadcasted_iota(jnp.int32, (SUB, HEAD_D), 0)
    lane_j = lax.broadcasted_iota(jnp.int32, (SUB, HEAD_D), 1) & (SUB - 1)
    causal = (sub_i <= lane_j) if reverse else (sub_i >= lane_j)
    pick = [causal & (lane_j == j) for j in range(SUB)]
    cum_op = cum_ref[...]
    same_blk = mask_ref[0]
    same_grp = mask_ref[1]
    zero = jnp.zeros((SUB, HEAD_D), F32)
    n_blk = CH // SUB
    per = GRP // SUB
    n_grp = CH // GRP

    def edge(unit, idx):
        return unit * idx if reverse else unit * (idx + 1) - 1

    def split_product(b, qs, ks, bs, pieces):
        q_slabs, k_slabs = [], []
        for ref, q_active, k_active in pieces:
            qp, kp = [], []
            for i in range(n_blk):
                r = ref(i)
                qp.append(qs[i] * jnp.exp2(bs[i] - b[r:r + 1, :]) if q_active(i) else zero)
                kp.append(ks[i] * jnp.exp2(b[r:r + 1, :] - bs[i]) if k_active(i) else zero)
            q_slabs.append(jnp.concatenate(qp, axis=0))
            k_slabs.append(jnp.concatenate(kp, axis=0))
        return _dot_nt(jnp.concatenate(q_slabs, axis=1), jnp.concatenate(k_slabs, axis=1))

    level1 = []
    for c in (range(1, per) if reverse else range(per - 1)):
        level1.append((lambda i, c=c: edge(SUB, (i // per) * per + c),
                       (lambda i, c=c: i % per < c) if reverse else (lambda i, c=c: i % per > c),
                       lambda i, c=c: i % per == c))
    level2 = []
    for gc in (range(1, n_grp) if reverse else range(n_grp - 1)):
        level2.append((lambda i, gc=gc: edge(GRP, gc),
                       (lambda i, gc=gc: i // per < gc) if reverse else (lambda i, gc=gc: i // per > gc),
                       lambda i, gc=gc: i // per == gc))

    for h in range(N_HEADS):
        sl = slice(h * HEAD_D, (h + 1) * HEAD_D)
        lbh = lb[:, sl]
        q = _silu(q_ref[:, sl])
        f = lbh + (1.0 - lbh) * _sigmoid(f_ref[:, sl])
        k = jnp.where(valid, 1.0 - f, 0.0)
        v = v_ref[:, sl]
        b = _prefix_dot(cum_op, jnp.log(f)) * LOG2_E
        qs = [q[i * SUB:(i + 1) * SUB] for i in range(n_blk)]
        ks = [k[i * SUB:(i + 1) * SUB] for i in range(n_blk)]
        bs = [b[i * SUB:(i + 1) * SUB] for i in range(n_blk)]

        tiles = [qs[blk] * jnp.exp2(jnp.minimum(bs[blk] - bs[blk][j:j + 1, :], 0.0))
                 for blk in range(n_blk) for j in range(SUB)]
        pair = _dot_nt(jnp.concatenate(tiles, axis=0), k)
        rows = []
        for blk in range(n_blk):
            base = blk * SUB * SUB
            a_blk = jnp.where(pick[0], pair[base:base + SUB], 0.0)
            for j in range(1, SUB):
                a_blk = a_blk + jnp.where(pick[j], pair[base + j * SUB:base + (j + 1) * SUB], 0.0)
            rows.append(a_blk)
        scores = jnp.concatenate(rows, axis=0) * same_blk
        scores = scores + split_product(b, qs, ks, bs, level1) * same_grp
        scores = scores + split_product(b, qs, ks, bs, level2)

        state = jnp.where(reset, 0.0, st_ref[h])
        o_ref[:, sl] = (_dot(scores, v) + _dot_nt(q * jnp.exp2(b), state)).astype(o_ref.dtype)
        b_tot = b[0:1, :] if reverse else b[CH - 1:CH, :]
        st_ref[h] = state * jnp.exp2(b_tot) + _dot_tn(v, k * jnp.exp2(b_tot - b))


def _hg_lower_bound(lower_ref, layer):
    low = lower_ref[...]
    e = jnp.exp(low - jnp.max(low, axis=0, keepdims=True))
    soft = e / jnp.sum(e, axis=0, keepdims=True)
    lb = jnp.zeros((1, GROUP_W), F32)
    for l in range(1, layer + 1):
        lb = lb + soft[l:l + 1, :]
    return lb


def _hg_kernel(layer, cpos_ref, clast_ref, g, gb, qf, ff, vf, qb, fb, vb, lower_ref, cum_ref, mask_ref,
               of_ref, ob_ref, sf_ref, sb_ref):
    lb =_hg_lower_bound(lower_ref, layer)
    _hg_direction(qf, ff, vf, lb, cum_ref.at[0], mask_ref, sf_ref, of_ref,
                  cpos_ref[g] == 0, cpos_ref[g] == 0, False)
    _hg_direction(qb, fb, vb, lb, cum_ref.at[1], mask_ref, sb_ref, ob_ref,
                  clast_ref[gb] == 1, cpos_ref[gb] == 0, True)


def _hgrn2(z, lower, layer, cps):
    t = z.shape[0]
    n = t // (CH * cps)
    blk = (CH * cps, GROUP_W)
    cum = _tri_consts()
    masks = _hg_masks()
    full = lambda a: pl.BlockSpec(a.shape, lambda g, cp, cl: (0,) * a.ndim)
    in_specs = [pl.BlockSpec(blk, _fwd_map(0)), pl.BlockSpec(blk, _fwd_map(1)), pl.BlockSpec(blk, _fwd_map(3)),
                pl.BlockSpec(blk, _bwd_map(n, 0)), pl.BlockSpec(blk, _bwd_map(n, 2)),
                pl.BlockSpec(blk, _bwd_map(n, 3)), full(lower), full(cum), full(masks)]
    return (functools.partial(_hg_kernel, layer), in_specs, [z, z, z, z, z, z, lower, cum, masks],
            [pltpu.VMEM((N_HEADS, HEAD_D, HEAD_D), F32)] * 2, "fffbbbccc")


def _log_sigmoid(x):
    return jnp.minimum(x, 0.0) - jnp.log(1.0 + jnp.exp(-jnp.abs(x)))


def _ml_direction(q_ref, k_ref, v_ref, gi_ref, gf_ref, bias_ref, cum_op, spread_ref, s_ref, m_ref, o_ref,
                  reset, first, d, reverse):
    valid = _row_valid(first)
    cum = _prefix_dot(cum_op, _log_sigmoid(gf_ref[...] + bias_ref[1]))
    a = jnp.where(valid, gi_ref[...] + bias_ref[0] - cum, NEG_BIG)
    run = a
    s = 1
    while s < CH:
        run = jnp.maximum(run, _shift_rows(run, s, NEG_BIG, reverse))
        s *= 2
    m_st = jnp.where(reset, NEG_BIG, m_ref[...])
    mx = jnp.maximum(m_st, run)
    edge = 0 if reverse else CH - 1
    mx_last = mx[edge:edge + 1, :]
    m_ref[...] = cum[edge:edge + 1, :] + mx_last
    decay = jnp.exp(m_st - mx_last)

    stacked = jnp.concatenate([mx, m_st - mx, -(cum + mx), a - mx_last], axis=0) * LOG2_E
    spread_b = _spread_dot(stacked, spread_ref[...])
    mx_b, inter_b, floor_b, end_b = (spread_b[i * CH:(i + 1) * CH] for i in range(4))
    a_t = (a * LOG2_E).T

    row = lax.broadcasted_iota(jnp.int32, (CH, CH), 0)
    col = lax.broadcasted_iota(jnp.int32, (CH, CH), 1)
    causal = (col >= row) if reverse else (col <= row)
    ones = jnp.ones((CH, HEAD_D), F32)
    for h in range(N_HEADS):
        sl = slice(h * HEAD_D, (h + 1) * HEAD_D)
        x = d * N_HEADS + h
        q = q_ref[:, sl]
        k = jnp.where(valid, k_ref[:, sl] * (HEAD_D ** -0.5), 0.0)
        v_ext = jnp.concatenate([v_ref[:, sl], ones], axis=1)
        w = jnp.where(causal, jnp.exp2(jnp.minimum(a_t[x:x + 1, :] - mx_b[:, sl], 0.0)), 0.0)
        qk = _dot_nt(q, k) * w
        s_inter = jnp.exp2(jnp.minimum(inter_b[:, sl], 0.0))
        state = jnp.where(reset, 0.0, s_ref[h])
        ext = _dot(jnp.concatenate([qk, q * s_inter], axis=1), jnp.concatenate([v_ext, state], axis=0))
        o_ref[:, sl] = (ext[:, :HEAD_D] / jnp.maximum(jnp.abs(ext[:, HEAD_D:]), jnp.exp2(floor_b[:, sl]))
                        ).astype(o_ref.dtype)
        kw = k * jnp.exp2(jnp.minimum(end_b[:, sl], 0.0))
        s_ref[h] = decay[:, x:x + 1] * state + _dot_tn(kw, v_ext)


def _ml_kernel(cpos_ref, clast_ref, g, gb, qf, kf, vf, gif, gff, qb, kb, vb, gib, gfb, bias_ref, cum_ref,
               spread_ref, of_ref, ob_ref, sfs, mfs, sbs, mbs):
    _ml_direction(qf, kf, vf, gif, gff, bias_ref, cum_ref[0], spread_ref.at[0], sfs, mfs, of_ref,
                  cpos_ref[g] == 0, cpos_ref[g] == 0, 0, False)
    _ml_direction(qb, kb, vb, gib, gfb, bias_ref, cum_ref[1], spread_ref.at[1], sbs, mbs, ob_ref,
                  clast_ref[gb] == 1, cpos_ref[gb] == 0, 1, True)


def _ml_spread():
    spread = np.zeros((2, LANES, GROUP_W), np.float32)
    for d in range(2):
        for h in range(N_HEADS):
            spread[d, d * N_HEADS + h, h * HEAD_D:(h + 1) * HEAD_D] = 1.0
    return jnp.asarray(np.tile(spread, (1, SPREAD_TERMS, 1)), BF16)


def _mlstm(z, gates, ml_bi, ml_bf, cps):
    t = z.shape[0]
    n = t // (CH * cps)
    blk = (CH * cps, GROUP_W)
    gblk = (CH * cps, LANES)
    cum = _tri_consts()
    spread = _ml_spread()
    pad = jnp.zeros((LANES - 2 * N_HEADS,), F32)
    bias = jnp.stack([jnp.concatenate([ml_bi.reshape(-1), pad]),
                      jnp.concatenate([ml_bf.reshape(-1), pad])]).reshape(2, 1, LANES)
    full = lambda a: pl.BlockSpec(a.shape, lambda g, cp, cl: (0,) * a.ndim)
    in_specs = [pl.BlockSpec(blk, _fwd_map(5)), pl.BlockSpec(blk, _fwd_map(6)), pl.BlockSpec(blk, _fwd_map(7)),
                pl.BlockSpec(gblk, _fwd_map(0)), pl.BlockSpec(gblk, _fwd_map(1)),
                pl.BlockSpec(blk, _bwd_map(n, 5)), pl.BlockSpec(blk, _bwd_map(n, 6)),
                pl.BlockSpec(blk, _bwd_map(n, 7)),
                pl.BlockSpec(gblk, _bwd_map(n, 0)), pl.BlockSpec(gblk, _bwd_map(n, 1)),
                full(bias), full(cum), full(spread)]
    state = [pltpu.VMEM((N_HEADS, HEAD_D, 2 * HEAD_D), F32), pltpu.VMEM((1, LANES), F32)]
    return (_ml_kernel, in_specs, [z, z, z, gates, gates, z, z, z, gates, gates, bias, cum, spread], state * 2,
            "fffffbbbbbccc")


def _head_norm(x, center):
    outs = []
    for h in range(N_HEADS):
        xh = x[:, h * HEAD_D:(h + 1) * HEAD_D]
        if center:
            xh = xh - jnp.mean(xh, axis=1, keepdims=True)
        outs.append(xh * lax.rsqrt(jnp.mean(xh * xh, axis=1, keepdims=True) + EPS))
    return jnp.concatenate(outs, axis=1)


def _layer_norm(x, g, b):
    xc = x - jnp.mean(x, axis=1, keepdims=True)
    return xc * lax.rsqrt(jnp.mean(xc * xc, axis=1, keepdims=True) + EPS) * g + b


def _mix_rows(alpha, even, af, ab, ag, bf, bb, bg, h_ref, w_ref, lng_ref, lnb_ref):
    a = af[...].astype(F32) + ab[...].astype(F32)
    b = bf[...].astype(F32) + bb[...].astype(F32)
    if even:
        a = _head_norm(a, False) * _silu(ag[...])
        b = b * _gelu_tanh(bg[...])
    else:
        a = _head_norm(a, False) * _silu(ag[...])
        b = _head_norm(b, True) * _sigmoid(bg[...])
    mix = _dot(a, w_ref[0:GROUP_W, :]) + _dot(b, w_ref[GROUP_W:, :])
    return _layer_norm(alpha * h_ref[...] + mix, lng_ref[...], lnb_ref[...])


N_MIX_OPERANDS = 10


def _mix_specs(tm, col_a, col_b, index):
    half = lambda c: pl.BlockSpec((tm, GROUP_W), lambda *g: (index(*g), c))
    rows = pl.BlockSpec((tm, D_MODEL), lambda *g: (index(*g), 0))
    vec = pl.BlockSpec((1, D_MODEL), lambda *g: (0, 0))
    return [half(0), half(0), half(col_a), half(0), half(0), half(col_b), rows,
            pl.BlockSpec((D_MODEL, D_MODEL), lambda *g: (0, 0)), vec, vec]


def _mix_ffn_kernel(alpha, *refs):
    mix_refs = refs[:N_MIX_OPERANDS]
    wg_ref, wu_ref, wo_ref, lng_ref, lnb_ref, o_ref, h1_ref, acc_ref = refs[N_MIX_OPERANDS:]
    f = pl.program_id(1)

    @pl.when(f == 0)
    def _():
        h1_ref[...] = _mix_rows(alpha, True, *mix_refs)
        acc_ref[...] = jnp.zeros_like(acc_ref)

    x = h1_ref[...].astype(BF16)
    act = _silu(_dot(x, wg_ref[...])) * _dot(x, wu_ref[...])
    acc_ref[...] += _dot(act, wo_ref[...])

    @pl.when(f == pl.num_programs(1) - 1)
    def _():
        o_ref[...] = _layer_norm(alpha * h1_ref[...] + acc_ref[...], lng_ref[...], lnb_ref[...])


def _mix_ffn(alpha, af, ab, bf, bb, z, col_a, col_b, h, w_out, ln1_g, ln1_b, wi, wo, ln2_g, ln2_b):
    t = h.shape[0]
    d_ff = wo.shape[0]
    tm = _pick_tile(t, 640)
    tf = _pick_tile(d_ff, 1536)
    nf = d_ff // tf
    rows = pl.BlockSpec((tm, D_MODEL), lambda i, f: (i, 0))
    vec = pl.BlockSpec((1, D_MODEL), lambda i, f: (0, 0))
    return pl.pallas_call(
        functools.partial(_mix_ffn_kernel, alpha), grid=(t // tm, nf),
        in_specs=_mix_specs(tm, col_a, col_b, lambda i, f: i) + [
            pl.BlockSpec((D_MODEL, tf), lambda i, f: (0, f)),
            pl.BlockSpec((D_MODEL, tf), lambda i, f: (0, f + nf)),
            pl.BlockSpec((tf, D_MODEL), lambda i, f: (f, 0)), vec, vec],
        out_specs=rows, out_shape=jax.ShapeDtypeStruct((t, D_MODEL), F32),
        scratch_shapes=[pltpu.VMEM((tm, D_MODEL), F32), pltpu.VMEM((tm, D_MODEL), F32)],
        compiler_params=_params(2), name="mix_ffn")(
            af, ab, z, bf, bb, z, h, w_out, ln1_g.reshape(1, -1), ln1_b.reshape(1, -1),
            wi, wi, wo, ln2_g.reshape(1, -1), ln2_b.reshape(1, -1))


EXPERT_TILE = 1024
DMA_UNROLL = 8
DMA_PRIORITIES = 2


def _mix_route_kernel(alpha, pad_starts, *refs):
    mix_refs = refs[:N_MIX_OPERANDS]
    router_ref, tri_ref, o_ref, meta_ref, gate_ref, cnt_ref, carry_ref = refs[N_MIX_OPERANDS:]

    @pl.when(pl.program_id(0) == 0)
    def _():
        carry_ref[...] = jnp.zeros_like(carry_ref)

    h1 = _mix_rows(alpha, False, *mix_refs)
    o_ref[...] = h1
    tm = h1.shape[0]
    row = pl.program_id(0) * tm + lax.broadcasted_iota(jnp.int32, (tm, 1), 0)
    is_pad = row < 0
    for s in pad_starts:
        is_pad = is_pad | ((row >= s) & (row < s + PAD_ROWS))
    is_token = jnp.logical_not(is_pad)

    logits = _dot_16bit(h1, router_ref[...])
    lane = lax.broadcasted_iota(jnp.int32, logits.shape, 1)
    logits = jnp.where(lane < N_EXP, logits, -jnp.inf)
    top1 = jnp.max(logits, axis=1, keepdims=True)
    idx1 = jnp.min(jnp.where(logits == top1, lane, LANES), axis=1, keepdims=True)
    rest = jnp.where(lane == idx1, -jnp.inf, logits)
    top2 = jnp.max(rest, axis=1, keepdims=True)
    idx2 = jnp.min(jnp.where(rest == top2, lane, LANES), axis=1, keepdims=True)
    g2 = jnp.exp(top2 - top1)
    denom = 1.0 + g2
    hit1 = lane == idx1
    hit2 = lane == idx2
    both = jnp.where(is_token, jnp.where(hit1, 1.0, 0.0) + jnp.where(hit2, 1.0, 0.0), 0.0)
    prefix = _dot(tri_ref[...], both) + carry_ref[...]
    rank1 = jnp.sum(jnp.where(hit1, prefix, 0.0), axis=1, keepdims=True).astype(jnp.int32)
    rank2 = jnp.sum(jnp.where(hit2, prefix, 0.0), axis=1, keepdims=True).astype(jnp.int32)
    carry_ref[...] += jnp.sum(both, axis=0, keepdims=True)
    cnt_ref[...] = carry_ref[...]
    meta_ref[...] = jnp.where(lane == 0, idx1, jnp.where(lane == 1, idx2, jnp.where(
        lane == 2, rank1, jnp.where(lane == 3, rank2, jnp.where(is_token & (lane == 4), 1, 0)))))
    gate_ref[...] = jnp.where(lane == 0, 1.0 / denom, jnp.where(lane == 1, g2 / denom, 0.0))


def _mix_route(alpha, af, ab, bf, bb, z, col_a, col_b, h, w_out, ln_g, ln_b, router, pad_starts):
    t = h.shape[0]
    tm = _pick_tile(t, 640)
    router_p = _stack_weight_terms(jnp.pad(router, ((0, 0), (0, LANES - router.shape[1]))))
    pos = np.arange(tm)
    tri = jnp.asarray(pos[:, None] > pos[None, :], BF16)
    rows = lambda w: pl.BlockSpec((tm, w), lambda i: (i, 0))
    const = lambda a: pl.BlockSpec(a.shape, lambda i: (0, 0))
    return pl.pallas_call(
        functools.partial(_mix_route_kernel, alpha, pad_starts), grid=(t // tm,),
        in_specs=_mix_specs(tm, col_a, col_b, lambda i: i) + [const(router_p), const(tri)],
        out_specs=[rows(D_MODEL), rows(LANES), rows(LANES), pl.BlockSpec((1, LANES), lambda i: (0, 0))],
        out_shape=[jax.ShapeDtypeStruct((t, D_MODEL), F32), jax.ShapeDtypeStruct((t, LANES), jnp.int32),
                   jax.ShapeDtypeStruct((t, LANES), F32), jax.ShapeDtypeStruct((1, LANES), F32)],
        scratch_shapes=[pltpu.VMEM((1, LANES), F32)],
        compiler_params=_params(1), name="mix_route")(
            af, ab, z, bf, bb, z, h, w_out, ln_g.reshape(1, -1), ln_b.reshape(1, -1), router_p, tri)


def _dispatch_plan(meta, counts, t):
    cnt = counts[0, :N_EXP].astype(jnp.int32)
    padded = ((cnt + EXPERT_TILE - 1) // EXPERT_TILE) * EXPERT_TILE
    ends = jnp.cumsum(padded)
    off = ends - padded

    def base(e):
        return sum(jnp.where(e == i, off[i], 0) for i in range(N_EXP))

    is_token = meta[:, 4] > 0
    n_tiles = -(-2 * t // EXPERT_TILE) + N_EXP
    spare = n_tiles * EXPERT_TILE + 2 * (jnp.cumsum(jnp.logical_not(is_token).astype(jnp.int32)) - 1)
    pos1 = (base(meta[:, 0]) + meta[:, 2]).astype(jnp.int32)
    pos2 = (base(meta[:, 1]) + meta[:, 3]).astype(jnp.int32)
    scatter = (jnp.where(is_token, pos1, spare) * SUBLANES, jnp.where(is_token, pos2, spare + 1) * SUBLANES)
    gather = (jnp.where(is_token, pos1, 0) * SUBLANES, jnp.where(is_token, pos2, 0) * SUBLANES)
    starts = jnp.arange(n_tiles, dtype=jnp.int32) * EXPERT_TILE
    tile_expert = jnp.minimum(jnp.sum(starts[:, None] >= ends[None, :], axis=1), N_EXP - 1).astype(jnp.int32)
    n_active = (ends[-1] // EXPERT_TILE).astype(jnp.int32).reshape(1)
    return ends.astype(jnp.int32), scatter, gather, tile_expert, n_active, n_tiles


assert D_MODEL == SUBLANES * LANES


def _store_token_tiles(ref, x):
    n = x.shape[0]
    for s in range(SUBLANES):
        ref[pl.ds(s, n, stride=SUBLANES), :] = x[:, s * LANES:(s + 1) * LANES]


def _load_token_tiles(ref, n):
    return jnp.concatenate([ref[pl.ds(s, n, stride=SUBLANES), :] for s in range(SUBLANES)], axis=1)


def _tile_rows(i):
    return pl.ds(pl.multiple_of(i * SUBLANES, SUBLANES), SUBLANES)


def _tile_at(first_row):
    return pl.ds(pl.multiple_of(first_row, SUBLANES), SUBLANES)


def _dispatch_kernel(n_tiles, ends_ref, pos1_ref, pos2_ref, h_ref, xs_ref, tok_ref, zero_ref, sem, zero_sem):
    n = h_ref.shape[0]

    @pl.when(pl.program_id(0) == 0)
    def _():
        zero_ref[...] = jnp.zeros_like(zero_ref)

        def clear_last_tile(e):
            first_row = pl.multiple_of((ends_ref[e] - EXPERT_TILE) * SUBLANES, SUBLANES)
            return pltpu.make_async_copy(zero_ref, xs_ref.at[pl.ds(first_row, EXPERT_TILE * SUBLANES)], zero_sem)

        def has_tiles(e):
            return ends_ref[e] > (ends_ref[e - 1] if e else 0)

        for e in range(N_EXP):
            pl.when(has_tiles(e))(lambda e=e: clear_last_tile(e).start())
        for e in range(N_EXP):
            pl.when(has_tiles(e))(lambda e=e: clear_last_tile(e).wait())

        def clear_tile(i):
            first_row = pl.multiple_of(i * (EXPERT_TILE * SUBLANES), SUBLANES)
            return pltpu.make_async_copy(zero_ref, xs_ref.at[pl.ds(first_row, EXPERT_TILE * SUBLANES)], zero_sem)

        used = ends_ref[N_EXP - 1] // EXPERT_TILE
        lax.fori_loop(used, n_tiles, lambda i, c: (clear_tile(i).start(), c)[1], 0)
        lax.fori_loop(used, n_tiles, lambda i, c: (clear_tile(i).wait(), c)[1], 0)

    g = pl.program_id(0)
    slot = g % 2
    _store_token_tiles(tok_ref.at[slot], h_ref[...])

    def tile_copy(r, p):
        return pltpu.make_async_copy(tok_ref.at[slot].at[_tile_rows(r)], xs_ref.at[_tile_at(p)], sem.at[slot])

    def start(pair, c):
        for prio in range(DMA_PRIORITIES):
            r = pair * DMA_PRIORITIES + prio
            tile_copy(r, pos1_ref[0, 0, r]).start(priority=prio)
            tile_copy(r, pos2_ref[0, 0, r]).start(priority=prio)
        return c

    lax.fori_loop(0, n // DMA_PRIORITIES, start, 0, unroll=DMA_UNROLL // DMA_PRIORITIES)

    def wait_slot(s):
        all_rows = pltpu.make_async_copy(tok_ref.at[s], xs_ref.at[pl.ds(0, n * SUBLANES)], sem.at[s])
        all_rows.wait()
        all_rows.wait()

    pl.when(g > 0)(lambda: wait_slot(1 - slot))
    pl.when(g == pl.num_programs(0) - 1)(lambda: wait_slot(slot))


def _dispatch(h, ends, pos1, pos2, n_tiles, n_spare):
    t = h.shape[0]
    n_rows = n_tiles * EXPERT_TILE + n_spare
    tm = _pick_tile(t, 768)
    idx = lambda: pl.BlockSpec((1, 1, tm), lambda i, ends: (i, 0, 0), memory_space=pltpu.SMEM)
    grid_spec = pltpu.PrefetchScalarGridSpec(
        num_scalar_prefetch=1, grid=(t // tm,),
        in_specs=[idx(), idx(), pl.BlockSpec((tm, D_MODEL), lambda i, ends: (i, 0))],
        out_specs=pl.BlockSpec(memory_space=pl.ANY),
        scratch_shapes=[pltpu.VMEM((2, tm * SUBLANES, LANES), F32),
                        pltpu.VMEM((EXPERT_TILE * SUBLANES, LANES), F32),
                        pltpu.SemaphoreType.DMA((2,)), pltpu.SemaphoreType.DMA(())])
    return pl.pallas_call(
        functools.partial(_dispatch_kernel, n_tiles), grid_spec=grid_spec,
        out_shape=jax.ShapeDtypeStruct((n_rows * SUBLANES, LANES), F32),
        compiler_params=_params(1), name="dispatch")(
            ends, pos1.reshape(t // tm, 1, tm), pos2.reshape(t // tm, 1, tm), h)


def _experts_kernel(te_ref, na_ref, x_ref, wg_ref, wu_ref, wo_ref, y_ref, xb_ref, acc_ref):
    i = pl.program_id(0)
    f = pl.program_id(1)
    last = f == pl.num_programs(1) - 1
    active = i < na_ref[0]

    @pl.when(active)
    def _():
        @pl.when(f == 0)
        def _():
            acc_ref[...] = jnp.zeros_like(acc_ref)
            xb_ref[...] = _load_token_tiles(x_ref, EXPERT_TILE).astype(BF16)

        x = xb_ref[...]
        act = _silu(_dot(x, wg_ref[0])) * _dot(x, wu_ref[0])
        acc_ref[...] += _dot(act, wo_ref[0])

        @pl.when(last)
        def _():
            _store_token_tiles(y_ref, acc_ref[...])

    @pl.when(jnp.logical_not(active) & last)
    def _():
        y_ref[...] = jnp.zeros_like(y_ref)


def _experts(xs, tile_expert, n_active, wi, wo, n_tiles):
    n_rows = n_tiles * EXPERT_TILE
    e_ff = wo.shape[1]
    tf = _pick_tile(e_ff, 512)
    nf = e_ff // tf
    rows = pl.BlockSpec((EXPERT_TILE * SUBLANES, LANES), lambda i, f, te, na: (i, 0))
    rows_in = pl.BlockSpec((EXPERT_TILE * SUBLANES, LANES),
                           lambda i, f, te, na: (jnp.minimum(i, jnp.maximum(na[0] - 1, 0)), 0))
    def held(i, f, na):
        return jnp.where(i < na[0], f, nf - 1)

    grid_spec = pltpu.PrefetchScalarGridSpec(
        num_scalar_prefetch=2, grid=(n_rows // EXPERT_TILE, nf),
        in_specs=[rows_in, pl.BlockSpec((1, D_MODEL, tf), lambda i, f, te, na: (te[i], 0, held(i, f, na))),
                  pl.BlockSpec((1, D_MODEL, tf), lambda i, f, te, na: (te[i], 0, held(i, f, na) + nf)),
                  pl.BlockSpec((1, tf, D_MODEL), lambda i, f, te, na: (te[i], held(i, f, na), 0))],
        out_specs=rows,
        scratch_shapes=[pltpu.VMEM((EXPERT_TILE, D_MODEL), BF16), pltpu.VMEM((EXPERT_TILE, D_MODEL), F32)])
    return pl.pallas_call(
        _experts_kernel, grid_spec=grid_spec, out_shape=jax.ShapeDtypeStruct((n_rows * SUBLANES, LANES), F32),
        compiler_params=_params(2), name="experts")(tile_expert, n_active, xs, wi, wi, wo)


def _gather_start(n, pos1_ref, pos2_ref, ys_ref, a_ref, b_ref, sems):
    def tile_copy(p, buf, r, which):
        return pltpu.make_async_copy(ys_ref.at[_tile_at(p)], buf.at[_tile_rows(r)], sems.at[which])

    def start(pair, c):
        for prio in range(DMA_PRIORITIES):
            r = pair * DMA_PRIORITIES + prio
            tile_copy(pos1_ref[0, 0, r], a_ref, r, 0).start(priority=prio)
            tile_copy(pos2_ref[0, 0, r], b_ref, r, 1).start(priority=prio)
        return c

    lax.fori_loop(0, n // DMA_PRIORITIES, start, 0, unroll=DMA_UNROLL // DMA_PRIORITIES)


def _gather_finish(alpha, h_ref, gate_ref, ys_ref, lng_ref, lnb_ref, a_ref, b_ref, sems):
    n = h_ref.shape[0]
    for which, buf in enumerate((a_ref, b_ref)):
        pltpu.make_async_copy(ys_ref.at[pl.ds(0, n * SUBLANES)], buf, sems.at[which]).wait()
    gate = gate_ref[...]
    y = gate[:, 0:1] * _load_token_tiles(a_ref, n) + gate[:, 1:2] * _load_token_tiles(b_ref, n)
    return _layer_norm(alpha * h_ref[...] + y, lng_ref[...], lnb_ref[...])


def _combine_kernel(alpha, pos1_ref, pos2_ref, h_ref, gate_ref, ys_ref, lng_ref, lnb_ref, o_ref,
                    a_ref, b_ref, sems):
    _gather_start(h_ref.shape[0], pos1_ref, pos2_ref, ys_ref, a_ref, b_ref, sems)
    o_ref[...] = _gather_finish(alpha, h_ref, gate_ref, ys_ref, lng_ref, lnb_ref, a_ref, b_ref, sems)


def _combine_scratch(tm):
    return [pltpu.VMEM((tm * SUBLANES, LANES), F32), pltpu.VMEM((tm * SUBLANES, LANES), F32),
            pltpu.SemaphoreType.DMA((2,))]


def _combine(alpha, h, gates, ys, pos1, pos2, ln_g, ln_b):
    t = h.shape[0]
    tm = _pick_tile(t, 512)
    idx = lambda: pl.BlockSpec((1, 1, tm), lambda i: (i, 0, 0), memory_space=pltpu.SMEM)
    rows = pl.BlockSpec((tm, D_MODEL), lambda i: (i, 0))
    vec = pl.BlockSpec((1, D_MODEL), lambda i: (0, 0))
    return pl.pallas_call(
        functools.partial(_combine_kernel, alpha), grid=(t // tm,),
        in_specs=[idx(), idx(), rows, pl.BlockSpec((tm, LANES), lambda i: (i, 0)),
                  pl.BlockSpec(memory_space=pl.ANY), vec, vec],
        out_specs=rows, out_shape=jax.ShapeDtypeStruct((t, D_MODEL), F32),
        scratch_shapes=_combine_scratch(tm),
        compiler_params=_params(1), name="combine")(
            pos1.reshape(t // tm, 1, tm), pos2.reshape(t // tm, 1, tm), h, gates, ys,
            ln_g.reshape(1, -1), ln_b.reshape(1, -1))


def _output_tables(seq_shapes):
    kind, live = [], []
    window = [[], []]
    recent = [(0, 0), (0, 0)]
    for grp, (b, l) in enumerate(seq_shapes):
        for i in range(b):
            for c in range(l // CH + 1):
                kind.append(grp)
                live.append(int(c > 0))
                recent[grp] = (i, max(c - 1, 0))
                for g2 in range(2):
                    window[g2].append(recent[g2])
    tables = [kind, live] + [[w[j] for w in window[g2]] for g2 in range(2) for j in range(2)]
    return [jnp.asarray(np.asarray(x, np.int32)) for x in tables]


def _combine_out_kernel(alpha, kind_ref, live_ref, ab_ref, ac_ref, bb_ref, bc_ref,
                        pos1_ref, pos2_ref, nxt1_ref, nxt2_ref, h_ref, gate_ref, ys_ref, lng_ref, lnb_ref,
                        outa_ref, outb_ref, a_ref, b_ref, sems):
    g = pl.program_id(0)
    last = pl.num_programs(0) - 1
    slot = g % 2
    nxt = jnp.minimum(g + 1, last)

    @pl.when((g == 0) & (live_ref[0] == 1))
    def _():
        _gather_start(CH, pos1_ref, pos2_ref, ys_ref, a_ref.at[0], b_ref.at[0], sems.at[0])

    @pl.when((g < last) & (live_ref[nxt] == 1))
    def _():
        _gather_start(CH, nxt1_ref, nxt2_ref, ys_ref, a_ref.at[1 - slot], b_ref.at[1 - slot], sems.at[1 - slot])

    @pl.when(live_ref[g] == 1)
    def _():
        res = _gather_finish(alpha, h_ref, gate_ref, ys_ref, lng_ref, lnb_ref,
                             a_ref.at[slot], b_ref.at[slot], sems.at[slot])
        for grp, out_ref in enumerate((outa_ref, outb_ref)):
            @pl.when(kind_ref[g] == grp)
            def _(out_ref=out_ref):
                out_ref[0] = res


def _combine_out(alpha, h, gates, ys, pos1, pos2, ln_g, ln_b, seq_shapes):
    t = h.shape[0]
    n = t // CH
    tables = _output_tables(seq_shapes)
    idx = lambda: pl.BlockSpec((1, 1, CH), lambda g, *tb: (g, 0, 0), memory_space=pltpu.SMEM)
    idx_next = lambda: pl.BlockSpec((1, 1, CH), lambda g, *tb: (jnp.minimum(g + 1, n - 1), 0, 0),
                                    memory_space=pltpu.SMEM)
    rows = lambda w: pl.BlockSpec((CH, w), lambda g, *tb: (g, 0))
    vec = pl.BlockSpec((1, D_MODEL), lambda g, *tb: (0, 0))
    out = lambda grp: pl.BlockSpec((1, CH, D_MODEL),
                                   lambda g, *tb: (tb[2 + 2 * grp][g], tb[3 + 2 * grp][g], 0))
    fetched = pltpu.VMEM((2, CH * SUBLANES, LANES), F32)
    grid_spec = pltpu.PrefetchScalarGridSpec(
        num_scalar_prefetch=len(tables), grid=(n,),
        in_specs=[idx(), idx(), idx_next(), idx_next(), rows(D_MODEL), rows(LANES),
                  pl.BlockSpec(memory_space=pl.ANY), vec, vec],
        out_specs=[out(0), out(1)], scratch_shapes=[fetched, fetched, pltpu.SemaphoreType.DMA((2, 2))])
    pos1, pos2 = pos1.reshape(n, 1, CH), pos2.reshape(n, 1, CH)
    return pl.pallas_call(
        functools.partial(_combine_out_kernel, alpha), grid_spec=grid_spec,
        out_shape=[jax.ShapeDtypeStruct((b, l, D_MODEL), F32) for b, l in seq_shapes],
        compiler_params=_params(1), name="combine_out")(
            *tables, pos1, pos2, pos1, pos2, h, gates, ys, ln_g.reshape(1, -1), ln_b.reshape(1, -1))


def _moe(alpha, h, meta, gates, counts, wi, wo, ln_g, ln_b, n_pad_rows, out_shapes=None):
    t = h.shape[0]
    ends, scatter, gather, tile_expert, n_active, n_tiles = _dispatch_plan(meta, counts, t)
    n_spare = 2 * n_pad_rows
    xs = _dispatch(h, ends, scatter[0], scatter[1], n_tiles, n_spare)
    ys = _experts(xs, tile_expert, n_active, wi, wo, n_tiles)
    if out_shapes is None:
        return _combine(alpha, h, gates, ys, gather[0], gather[1], ln_g, ln_b)
    return _combine_out(alpha, h, gates, ys, gather[0], gather[1], ln_g, ln_b, out_shapes)


def kernel(x_prompt, x_sample, meta, e_w_in, e_conv_w, e_conv_b, e_lru_wa, e_lru_ba, e_lru_wx, e_lru_bx,
           e_lru_lambda, e_w_out, e_ffn_wi, e_ffn_wo, o_w_in, o_hg_lower, o_ml_bi, o_ml_bf, o_w_out, o_router,
           o_exp_wi, o_exp_wo, ln_g, ln_b):
    groups = (x_prompt, x_sample)
    depth = ln_g.shape[0]
    alpha = (2.0 * depth) ** 0.25
    seq_shapes = [(x.shape[0], x.shape[1]) for x in groups]
    for _, l in seq_shapes:
        assert l % CH == 0
    cpos_np, clast_np = _chunk_tables(seq_shapes)
    cpos, clast = jnp.asarray(cpos_np), jnp.asarray(clast_np)
    max_rows = max(l for _, l in seq_shapes) + CH
    pad_starts = tuple(int(i) * CH for i in np.flatnonzero(cpos_np == 0))

    head = jnp.concatenate([jnp.zeros((PAD_ROWS, D_MODEL), F32), meta.astype(F32)], axis=0)
    parts = []
    for x in groups:
        full = jnp.concatenate([jnp.broadcast_to(head[None], (x.shape[0], CH, D_MODEL)), x], axis=1)
        parts.append(full.reshape(-1, D_MODEL))
    h = jnp.concatenate(parts, axis=0)
    t = h.shape[0]

    n_odd_cols = 9 * GROUP_W
    for layer in range(depth):
        p = layer // 2
        if layer % 2 == 0:
            z = _project(h, e_w_in[p].astype(BF16))
            (ret_f, ret_b), (lru_f, lru_b) = _chunk_walk(
                [_retention(z, max_rows),
                 _rglru(z, 4, e_conv_w[p], e_conv_b[p], e_lru_wa[p], e_lru_ba[p],
                        e_lru_wx[p], e_lru_bx[p], e_lru_lambda[p])], cpos, clast, t, "even_mixers")
            h = _mix_ffn(alpha, ret_f, ret_b, lru_f, lru_b, z, 3, 5, h, e_w_out[p].astype(BF16),
                         ln_g[layer, 0], ln_b[layer, 0], e_ffn_wi[p].astype(BF16), e_ffn_wo[p].astype(BF16),
                         ln_g[layer, 1], ln_b[layer, 1])
        else:
            w_in = o_w_in[p]
            n_gate = 2 * N_HEADS
            gate_pad = ((0, 0), (0, LANES - n_gate))
            w_gates = jnp.concatenate([jnp.pad(w_in[:, n_odd_cols:n_odd_cols + n_gate], gate_pad),
                                       jnp.pad(w_in[:, n_odd_cols + n_gate:], gate_pad)], axis=1)
            z, gates = _project(h, w_in[:, :n_odd_cols].astype(BF16), w_gates)
            cps = ODD_CHUNKS_PER_STEP if (t // CH) % ODD_CHUNKS_PER_STEP == 0 else 1
            (hg_f, hg_b), (ml_f, ml_b) = _chunk_walk(
                [_hgrn2(z, o_hg_lower, layer, cps), _mlstm(z, gates, o_ml_bi[p], o_ml_bf[p], cps)],
                cpos, clast, t, "odd_mixers", cps)
            h, meta_r, gates_r, counts = _mix_route(
                alpha, hg_f, hg_b, ml_f, ml_b, z, 4, 8, h, o_w_out[p].astype(BF16),
                ln_g[layer, 0], ln_b[layer, 0], o_router[p], pad_starts)
            h = _moe(alpha, h, meta_r, gates_r, counts, o_exp_wi[p].astype(BF16), o_exp_wo[p].astype(BF16),
                     ln_g[layer, 1], ln_b[layer, 1], len(pad_starts) * PAD_ROWS,
                     seq_shapes if layer == depth - 1 else None)
            if layer == depth - 1:
                return tuple(h)

    outs = []
    row = 0
    for b, l in seq_shapes:
        n = b * (l + CH)
        outs.append(h[row:row + n].reshape(b, l + CH, D_MODEL)[:, CH:])
        row += n
    return tuple(outs)
```
